```python
import jax, jax.numpy as jnp
from jax import lax
import numpy as np

D_MODEL = 1024
BATCH = 4
SEQ = 8192
DEPTH = 1
DEC_BATCH = 32
DEC_SEQ = 2048
PAST_LEN = 128

HEAD_DIM = 64
ATTN_HEADS = 8
RET_HEADS = 8
ATTN_WIDTH = ATTN_HEADS * HEAD_DIM
RET_WIDTH = RET_HEADS * HEAD_DIM
MIX_WIDTH = ATTN_WIDTH + RET_WIDTH
IN_PROJ_WIDTH = 3 * ATTN_WIDTH + 4 * RET_WIDTH
DILATED_PATTERNS = ((128, 1), (512, 4), (2048, 16))
ROPE_THETA = 10000.0
RET_CHUNK = 128
RET_DECAY_BASE_FWD = 5.0
RET_DECAY_BASE_BWD = 5.5
N_EXPERTS = 32
TOP_K = 4
EXPERT_FF = D_MODEL
SWIGLU_LIMIT = 7.0
SWIGLU_ALPHA = 1.702
EXPERT_BLOCK = 128
NORM_EPS = 1e-6
NEG_INF = -1e30

kernel_name = 'hybrid_dilated_retention_moe_encoder'


def rms_norm(x, g):
    xf = x.astype(jnp.float32)
    y = xf * lax.rsqrt(jnp.mean(xf * xf, axis=-1, keepdims=True) + NORM_EPS)
    return (y * g.astype(jnp.float32)).astype(x.dtype)


def rotary(x):
    s, hd = x.shape[1], x.shape[-1]
    inv_freq = ROPE_THETA ** (-jnp.arange(0, hd, 2, dtype=jnp.float32) / hd)
    ang = jnp.arange(s, dtype=jnp.float32)[:, None] * inv_freq[None, :]
    cos = jnp.cos(ang)[None, :, None, :]
    sin = jnp.sin(ang)[None, :, None, :]
    xf = x.astype(jnp.float32)
    x1, x2 = xf[..., : hd // 2], xf[..., hd // 2:]
    return jnp.concatenate([x1 * cos - x2 * sin, x2 * cos + x1 * sin], axis=-1)


def dilated_branch(q, k, v, dilation, half_span):
    b, s, h, hd = q.shape
    L = s // dilation
    blk = half_span
    nb = -(-L // blk)
    Lp = nb * blk
    bd = b * dilation

    def to_sub(t):
        return t.reshape(b, L, dilation, h, hd).transpose(0, 2, 1, 3, 4).reshape(bd, L, h, hd)

    qs = jnp.pad(to_sub(q), ((0, 0), (0, Lp - L), (0, 0), (0, 0))).reshape(bd, nb, blk, h, hd)

    def windows(t):
        tp = jnp.pad(to_sub(t), ((0, 0), (blk, Lp - L + blk), (0, 0), (0, 0)))
        tp = tp.reshape(bd, nb + 2, blk, h, hd)
        return jnp.concatenate([tp[:, :-2], tp[:, 1:-1], tp[:, 2:]], axis=2)

    kw, vw = windows(k), windows(v)
    qpos = jnp.arange(Lp).reshape(nb, blk)
    kpos = jnp.arange(nb)[:, None] * blk - blk + jnp.arange(3 * blk)[None, :]
    valid = ((jnp.abs(qpos[:, :, None] - kpos[:, None, :]) <= half_span)
             & (kpos >= 0)[:, None, :] & (kpos < L)[:, None, :])
    sc = jnp.einsum('bnqhd,bnkhd->bnhqk', qs, kw)
    sc = jnp.where(valid[None, :, None], sc, NEG_INF)
    m = jnp.max(sc, axis=-1, keepdims=True)
    p = jnp.exp(sc - m)
    den = jnp.sum(p, axis=-1)
    o = jnp.einsum('bnhqk,bnkhd->bnqhd', p, vw) / jnp.transpose(den, (0, 1, 3, 2))[..., None]
    lse = m[..., 0] + jnp.log(den)
    o = o.reshape(bd, Lp, h, hd)[:, :L]
    o = o.reshape(b, dilation, L, h, hd).transpose(0, 2, 1, 3, 4).reshape(b, s, h, hd)
    lse = jnp.transpose(lse, (0, 1, 3, 2)).reshape(bd, Lp, h)[:, :L]
    lse = lse.reshape(b, dilation, L, h).transpose(0, 2, 1, 3).reshape(b, s, h)
    return o, lse


def dilated_attention(q, k, v):
    q = rotary(q) * (HEAD_DIM ** -0.5)
    k = rotary(k)
    v = v.astype(jnp.float32)
    outs, lses = [], []
    for window, dilation in DILATED_PATTERNS:
        o, lse = dilated_branch(q, k, v, dilation, window // (2 * dilation))
        outs.append(o)
        lses.append(lse)
    w = jax.nn.softmax(jnp.stack(lses, axis=0), axis=0)
    return jnp.sum(w[..., None] * jnp.stack(outs, axis=0), axis=0)


def retention_decays(base):
    hidx = jnp.arange(RET_HEADS, dtype=jnp.float32)
    return jnp.log1p(-jnp.exp2(-(base + hidx)))


def retention_scan(q, k, v, log_gamma, include_diag):
    b, s, h, dk = q.shape
    dv = v.shape[-1]
    c = RET_CHUNK
    n = s // c
    qc = q.reshape(b, n, c, h, dk)
    kc = k.reshape(b, n, c, h, dk)
    vc = v.reshape(b, n, c, h, dv)
    pos = jnp.arange(c, dtype=jnp.float32)
    diff = pos[:, None] - pos[None, :]
    mask = (diff >= 0) if include_diag else (diff > 0)
    decay = jnp.where(mask[None], jnp.exp(jnp.where(mask, diff, 0.0)[None] * log_gamma[:, None, None]), 0.0)
    sc = jnp.einsum('bnihd,bnjhd->bnhij', qc, kc) * decay[None, None]
    intra = jnp.einsum('bnhij,bnjhe->bnihe', sc, vc)
    k_w = jnp.exp((c - 1.0 - pos)[None, :] * log_gamma[:, None])
    kv = jnp.einsum('bnjhd,hj,bnjhe->nbhde', kc, k_w, vc)
    chunk_decay = jnp.exp(c * log_gamma)[None, :, None, None]

    def step(state, kv_n):
        return state * chunk_decay + kv_n, state

    _, state_prev = lax.scan(step, jnp.zeros((b, h, dk, dv), jnp.float32), kv)
    q_w = jnp.exp((pos + 1.0)[None, :] * log_gamma[:, None])
    cross = jnp.einsum('bnihd,hi,nbhde->bnihe', qc, q_w, state_prev)
    return (intra + cross).reshape(b, s, h, dv)


def bidirectional_retention(q, k, v):
    q = rotary(q)
    k = rotary(k) * (HEAD_DIM ** -0.5)
    v = v.astype(jnp.float32)
    fwd = retention_scan(q, k, v, retention_decays(RET_DECAY_BASE_FWD), True)
    bwd = retention_scan(jnp.flip(q, 1), jnp.flip(k, 1), jnp.flip(v, 1),
                         retention_decays(RET_DECAY_BASE_BWD), False)
    return fwd + jnp.flip(bwd, 1)


def clamped_swiglu(gu):
    gu = gu.astype(jnp.float32)
    gate, up = gu[..., :EXPERT_FF], gu[..., EXPERT_FF:]
    gate = jnp.minimum(gate, SWIGLU_LIMIT)
    up = jnp.clip(up, -SWIGLU_LIMIT, SWIGLU_LIMIT)
    return gate * jax.nn.sigmoid(SWIGLU_ALPHA * gate) * (up + 1.0)


def moe_ffn(h, w_router, b_router, w_gate_up, b_gate_up, w_down, b_down):
    b, s, d = h.shape
    t = b * s
    a = t * TOP_K
    hf = h.reshape(t, d)
    logits = hf.astype(jnp.float32) @ w_router.astype(jnp.float32) + b_router.astype(jnp.float32)
    top_val, top_idx = lax.top_k(logits, TOP_K)
    gates = jax.nn.softmax(top_val, axis=-1)
    flat_e = top_idx.reshape(a).astype(jnp.int32)
    flat_tok = jnp.arange(a, dtype=jnp.int32) // TOP_K
    flat_g = gates.reshape(a)
    order = jnp.argsort(flat_e)
    e_s, tok_s, g_s = flat_e[order], flat_tok[order], flat_g[order]
    counts = jnp.bincount(flat_e, length=N_EXPERTS).astype(jnp.int32)
    padded = ((counts + EXPERT_BLOCK - 1) // EXPERT_BLOCK) * EXPERT_BLOCK
    start = jnp.cumsum(counts) - counts
    pend = jnp.cumsum(padded)
    pstart = pend - padded
    dest = pstart[e_s] + jnp.arange(a, dtype=jnp.int32) - start[e_s]
    cap = a + N_EXPERTS * EXPERT_BLOCK
    nblk = cap // EXPERT_BLOCK
    buf_tok = jnp.full((cap,), t, jnp.int32).at[dest].set(tok_s)
    buf_g = jnp.zeros((cap,), jnp.float32).at[dest].set(g_s)
    blk_e = jnp.minimum(jnp.searchsorted(pend, jnp.arange(nblk, dtype=jnp.int32) * EXPERT_BLOCK, side='right'),
                        N_EXPERTS - 1).astype(jnp.int32)
    x_pad = jnp.concatenate([hf, jnp.zeros((1, d), hf.dtype)], axis=0)
    xb = x_pad[buf_tok].reshape(nblk, EXPERT_BLOCK, d)

    def expert_block(args):
        xblk, e = args
        gu = xblk @ w_gate_up[e] + b_gate_up[e]
        act = clamped_swiglu(gu).astype(w_down.dtype)
        return (act @ w_down[e] + b_down[e]).astype(jnp.float32)

    yb = lax.map(expert_block, (xb, blk_e)).reshape(cap, d)
    y = jnp.zeros((t + 1, d), jnp.float32).at[buf_tok].add(yb * buf_g[:, None])[:t]
    return y.reshape(b, s, d).astype(h.dtype)


def encoder_layer(x, norm_mix_g, w_in, ret_norm_g, w_out, norm_ffn_g,
                  w_router, b_router, w_gate_up, b_gate_up, w_down, b_down):
    b, s, _ = x.shape
    h = rms_norm(x, norm_mix_g)
    proj = jnp.einsum('bsd,dc->bsc', h, w_in)
    A, R = ATTN_WIDTH, RET_WIDTH
    qa, ka, va, qr, kr, vr, gr = jnp.split(
        proj, [A, 2 * A, 3 * A, 3 * A + R, 3 * A + 2 * R, 3 * A + 3 * R], axis=-1)

    def heads(z):
        return z.reshape(b, s, -1, HEAD_DIM)

    attn = dilated_attention(heads(qa), heads(ka), heads(va)).reshape(b, s, ATTN_WIDTH)
    ret = bidirectional_retention(heads(qr), heads(kr), heads(vr))
    mu = jnp.mean(ret, axis=-1, keepdims=True)
    var = jnp.mean(jnp.square(ret - mu), axis=-1, keepdims=True)
    ret = ((ret - mu) * lax.rsqrt(var + NORM_EPS)).reshape(b, s, RET_WIDTH)
    ret = ret * ret_norm_g.astype(jnp.float32) * jax.nn.silu(gr.astype(jnp.float32))
    mixed = jnp.concatenate([attn, ret], axis=-1).astype(x.dtype)
    x = x + jnp.einsum('bsc,cd->bsd', mixed, w_out)
    x = x + moe_ffn(rms_norm(x, norm_ffn_g), w_router, b_router, w_gate_up, b_gate_up, w_down, b_down)
    return x


def encoder(x, norm_mix_g, w_in, ret_norm_g, w_out, norm_ffn_g, w_router, b_router,
            w_gate_up, b_gate_up, w_down, b_down, norm_final_g):
    for i in range(DEPTH):
        x = encoder_layer(x, norm_mix_g[i], w_in[i], ret_norm_g[i], w_out[i], norm_ffn_g[i],
                          w_router[i], b_router[i], w_gate_up[i], b_gate_up[i], w_down[i], b_down[i])
    return rms_norm(x, norm_final_g)


def setup_inputs(seed: int = 0) -> dict:
    key = jax.random.key(seed)
    ks = jax.random.split(key, 14)
    f32 = jnp.float32
    nrm = jax.random.normal
    return {
        'x_prompt': nrm(ks[0], (BATCH, SEQ, D_MODEL), f32),
        'x_sample': nrm(ks[1], (DEC_BATCH, DEC_SEQ, D_MODEL), f32),
        'norm_mix_g': 1.0 + 0.02 * nrm(ks[2], (DEPTH, D_MODEL), f32),
        'w_in': nrm(ks[3], (DEPTH, D_MODEL, IN_PROJ_WIDTH), f32) * D_MODEL ** -0.5,
        'ret_norm_g': 1.0 + 0.02 * nrm(ks[4], (DEPTH, RET_WIDTH), f32),
        'w_out': nrm(ks[5], (DEPTH, MIX_WIDTH, D_MODEL), f32) * MIX_WIDTH ** -0.5,
        'norm_ffn_g': 1.0 + 0.02 * nrm(ks[6], (DEPTH, D_MODEL), f32),
        'w_router': nrm(ks[7], (DEPTH, D_MODEL, N_EXPERTS), f32) * D_MODEL ** -0.5,
        'b_router': 0.01 * nrm(ks[8], (DEPTH, N_EXPERTS), f32),
        'w_gate_up': nrm(ks[9], (DEPTH, N_EXPERTS, D_MODEL, 2 * EXPERT_FF), f32) * D_MODEL ** -0.5,
        'b_gate_up': 0.01 * nrm(ks[10], (DEPTH, N_EXPERTS, 2 * EXPERT_FF), f32),
        'w_down': nrm(ks[11], (DEPTH, N_EXPERTS, EXPERT_FF, D_MODEL), f32) * EXPERT_FF ** -0.5,
        'b_down': 0.01 * nrm(ks[12], (DEPTH, N_EXPERTS, D_MODEL), f32),
        'norm_final_g': 1.0 + 0.02 * nrm(ks[13], (D_MODEL,), f32),
    }


def reference(x_prompt, x_sample, norm_mix_g, w_in, ret_norm_g, w_out, norm_ffn_g,
              w_router, b_router, w_gate_up, b_gate_up, w_down, b_down, norm_final_g):
    y_prompt = encoder(x_prompt, norm_mix_g, w_in, ret_norm_g, w_out, norm_ffn_g, w_router, b_router,
                       w_gate_up, b_gate_up, w_down, b_down, norm_final_g)
    y_sample = encoder(x_sample, norm_mix_g, w_in, ret_norm_g, w_out, norm_ffn_g, w_router, b_router,
                       w_gate_up, b_gate_up, w_down, b_down, norm_final_g)
    return (y_prompt, y_sample)
```

```python
import functools

import jax
import jax.numpy as jnp
from jax import lax
from jax.experimental import pallas as pl
from jax.experimental.pallas import tpu as pltpu

D_MODEL = 1024
HEAD_DIM = 64
N_HEADS = 8
WIDTH = N_HEADS * HEAD_DIM
N_SLABS = 7
DILATIONS = (1, 4, 16)
HALF_SPAN = 64
ROPE_THETA = 10000.0
RET_DECAY_BASE_FWD = 5.0
RET_DECAY_BASE_BWD = 5.5
N_EXPERTS = 32
TOP_K = 4
EXPERT_FF = D_MODEL
SWIGLU_LIMIT = 7.0
SWIGLU_ALPHA = 1.702
NORM_EPS = 1e-6
NEG_INF = -1e30

LANES = 128
VMEM_LIMIT_BYTES = 48 * 1024 * 1024

TM_INPROJ = 512
TQ = 128
TL_ATTN = 512
RET_CHUNK = 256
TM_OUT = 512
TG = 512
BM = 512
TC = 256

F32 = jnp.float32
BF16 = jnp.bfloat16


def _cparams(*sem):
    return pltpu.CompilerParams(dimension_semantics=sem, vmem_limit_bytes=VMEM_LIMIT_BYTES)


_ROTATE = (True, True, False, True, True, False, False)
_SCALE = (HEAD_DIM ** -0.5, 1.0, 1.0, 1.0, HEAD_DIM ** -0.5, 1.0, 1.0)


def _inproj_kernel(x_ref, g_ref, w_ref, cos_ref, sin_ref, *out_refs):
    x = x_ref[...]
    h = (x * lax.rsqrt(jnp.mean(x * x, axis=-1, keepdims=True) + NORM_EPS) * g_ref[...]).astype(BF16)
    cos = cos_ref[...]
    sin = sin_ref[...]
    lane = lax.broadcasted_iota(jnp.int32, cos.shape, 1)
    first_half = (lane & (HEAD_DIM - 1)) < HEAD_DIM // 2
    for j, o_ref in enumerate(out_refs):
        p = jnp.dot(h, w_ref[:, j * WIDTH:(j + 1) * WIDTH], preferred_element_type=F32)
        if _ROTATE[j]:
            for c in range(WIDTH // LANES):
                blk = p[:, c * LANES:(c + 1) * LANES]
                partner = jnp.where(first_half, pltpu.roll(blk, LANES - HEAD_DIM // 2, 1),
                                    pltpu.roll(blk, HEAD_DIM // 2, 1))
                r = blk * cos + partner * sin
                if _SCALE[j] != 1.0:
                    r = r * _SCALE[j]
                o_ref[:, c * LANES:(c + 1) * LANES] = r.astype(BF16)
        else:
            o_ref[...] = p.astype(BF16)


def _inproj(x2d, g, w_bf16, cos_t, sin_t, seq):
    t = x2d.shape[0]
    tm = min(TM_INPROJ, seq)
    pos_blocks = seq // tm
    out = jax.ShapeDtypeStruct((t, WIDTH), BF16)
    row = lambda i: (i, 0)
    return pl.pallas_call(
        _inproj_kernel,
        grid=(t // tm,),
        in_specs=[
            pl.BlockSpec((tm, D_MODEL), row),
            pl.BlockSpec((1, D_MODEL), lambda i: (0, 0)),
            pl.BlockSpec((D_MODEL, N_SLABS * WIDTH), lambda i: (0, 0)),
            pl.BlockSpec((tm, LANES), lambda i: (i % pos_blocks, 0)),
            pl.BlockSpec((tm, LANES), lambda i: (i % pos_blocks, 0)),
        ],
        out_specs=[pl.BlockSpec((tm, WIDTH), row)] * N_SLABS,
        out_shape=[out] * N_SLABS,
        compiler_params=_cparams("arbitrary"),
        name="inproj",
    )(x2d, g.reshape(1, D_MODEL), w_bf16, cos_t, sin_t)


def _rope_tables(seq):
    half = HEAD_DIM // 2
    inv_freq = ROPE_THETA ** (-jnp.arange(0, HEAD_DIM, 2, dtype=F32) / HEAD_DIM)
    ang = jnp.arange(seq, dtype=F32)[:, None] * inv_freq[None, :]
    cos, sin = jnp.cos(ang), jnp.sin(ang)
    reps = LANES // HEAD_DIM
    cos_t = jnp.tile(jnp.concatenate([cos, cos], axis=1), (1, reps))
    sin_t = jnp.tile(jnp.concatenate([-sin, sin], axis=1), (1, reps))
    assert cos_t.shape == (seq, LANES) and half * 2 == HEAD_DIM
    return cos_t, sin_t


def _attn_kernel(q_ref, kp_ref, kc_ref, kn_ref, vp_ref, vc_ref, vn_ref, o_ref, st_ref, kbuf, vbuf, *, tl, sub_len):
    i = pl.program_id(1)
    hs = HALF_SPAN
    kbuf[0:hs] = kp_ref[0]
    kbuf[hs:hs + tl] = kc_ref[0]
    kbuf[hs + tl:hs + tl + hs] = kn_ref[0]
    vbuf[0:hs] = vp_ref[0]
    vbuf[hs:hs + tl] = vc_ref[0]
    vbuf[hs + tl:hs + tl + hs] = vn_ref[0]
    tk = TQ + 2 * hs
    qi = lax.broadcasted_iota(jnp.int32, (TQ, tk), 0)
    kj = lax.broadcasted_iota(jnp.int32, (TQ, tk), 1)
    band = (kj >= qi) & (kj - qi <= 2 * hs)
    lane = lax.broadcasted_iota(jnp.int32, (TQ, LANES), 1)
    for sub in range(tl // TQ):
        a = sub * TQ
        kpos = kj + (i * tl + a - hs)
        bias = jnp.where(band & (kpos >= 0) & (kpos < sub_len), 0.0, NEG_INF).astype(F32)
        outs = []
        st = jnp.zeros((TQ, LANES), F32)
        for h in range(N_HEADS):
            cols = slice(h * HEAD_DIM, (h + 1) * HEAD_DIM)
            qh = q_ref[0, a:a + TQ, cols]
            kh = kbuf[a:a + tk, cols]
            vh = vbuf[a:a + tk, cols]
            s = lax.dot_general(qh, kh, (((1,), (1,)), ((), ())), preferred_element_type=F32) + bias
            m = jnp.max(s, axis=-1, keepdims=True)
            p = jnp.exp(s - m)
            l = jnp.sum(p, axis=-1, keepdims=True)
            o = jnp.dot(p.astype(BF16), vh, preferred_element_type=F32) / l
            outs.append(o)
            st = jnp.where(lane == h, m + jnp.log(l), st)
        o_ref[0, a:a + TQ, :] = jnp.concatenate(outs, axis=1).astype(BF16)
        st_ref[0, a:a + TQ, :] = st


def _banded_attention(q, k, v):
    g, sub_len, _ = q.shape
    tl = min(TL_ATTN, sub_len)
    hs = HALF_SPAN
    per = tl // hs
    last = sub_len // hs - 1
    cur = pl.BlockSpec((1, tl, WIDTH), lambda b, i: (b, i, 0))
    prev = pl.BlockSpec((1, hs, WIDTH), lambda b, i: (b, jnp.maximum(i * per - 1, 0), 0))
    nxt = pl.BlockSpec((1, hs, WIDTH), lambda b, i: (b, jnp.minimum((i + 1) * per, last), 0))
    return pl.pallas_call(
        functools.partial(_attn_kernel, tl=tl, sub_len=sub_len),
        grid=(g, sub_len // tl),
        in_specs=[cur, prev, cur, nxt, prev, cur, nxt],
        out_specs=[cur, pl.BlockSpec((1, tl, LANES), lambda b, i: (b, i, 0))],
        out_shape=[jax.ShapeDtypeStruct((g, sub_len, WIDTH), BF16),
                   jax.ShapeDtypeStruct((g, sub_len, LANES), F32)],
        scratch_shapes=[pltpu.VMEM((tl + 2 * hs, WIDTH), BF16), pltpu.VMEM((tl + 2 * hs, WIDTH), BF16)],
        compiler_params=_cparams("arbitrary", "arbitrary"),
        name="banded_attention",
    )(q, k, k, k, v, v, v)


def _to_residues(z, b, s, d):
    w = z.shape[-1]
    if d == 1:
        return z.reshape(b, s, w)
    return z.reshape(b, s // d, d, w).transpose(0, 2, 1, 3).reshape(b * d, s // d, w)


def _from_residues(z, b, s, d):
    w = z.shape[-1]
    if d == 1:
        return z.reshape(b * s, w)
    return z.reshape(b, d, s // d, w).transpose(0, 2, 1, 3).reshape(b * s, w)


def _ret_kernel(q_ref, k_ref, v_ref, gate_ref, dmat_ref, wq_ref, wk_ref, dec_ref, gn_ref, o_ref, sb_scr, *, c, n_chunks):
    pair = 2 * HEAD_DIM
    lane = lax.broadcasted_iota(jnp.int32, (c, pair), 1)
    head0 = lane < HEAD_DIM
    blk_r = lax.broadcasted_iota(jnp.int32, (pair, pair), 0) // HEAD_DIM
    blk_c = lax.broadcasted_iota(jnp.int32, (pair, pair), 1) // HEAD_DIM
    same_head = blk_r == blk_c
    dec_f = dec_ref[0:1, :]
    dec_b = dec_ref[1:2, :]
    tn = (((0,), (0,)), ((), ()))
    nt = (((1,), (1,)), ((), ()))

    def rows_of(n):
        return pl.ds(pl.multiple_of(n * c, c), c)

    def bwd_body(t, sb):
        n = n_chunks - 1 - t
        sb_scr[n] = sb
        rows = rows_of(n)
        kb = (k_ref[0, rows, :].astype(F32) * wk_ref[:, pair:]).astype(BF16)
        kv = lax.dot_general(kb, v_ref[0, rows, :], tn, preferred_element_type=F32)
        return sb * dec_b + jnp.where(same_head, kv, 0.0)

    lax.fori_loop(0, n_chunks, bwd_body, jnp.zeros((pair, pair), F32))

    def fwd_body(n, sf):
        rows = rows_of(n)
        q = q_ref[0, rows, :]
        k = k_ref[0, rows, :]
        v = v_ref[0, rows, :]
        qf32 = q.astype(F32)
        kf32 = k.astype(F32)
        intra = []
        for hh in range(2):
            mask = head0 if hh == 0 else jnp.logical_not(head0)
            kh = jnp.where(mask, k, jnp.zeros_like(k))
            s = lax.dot_general(q, kh, nt, preferred_element_type=F32)
            a = (s * dmat_ref[hh]).astype(BF16)
            intra.append(jnp.dot(a, v, preferred_element_type=F32))
        qw = jnp.concatenate([(qf32 * wq_ref[:, :pair]).astype(BF16), (qf32 * wq_ref[:, pair:]).astype(BF16)], axis=1)
        states = jnp.concatenate([sf.astype(BF16), sb_scr[n].astype(BF16)], axis=0)
        tot = jnp.where(head0, intra[0], intra[1]) + jnp.dot(qw, states, preferred_element_type=F32)
        inv = 1.0 / HEAD_DIM
        s0 = jnp.sum(jnp.where(head0, tot, 0.0), axis=-1, keepdims=True)
        s1 = jnp.sum(jnp.where(head0, 0.0, tot), axis=-1, keepdims=True)
        xc = tot - jnp.where(head0, s0, s1) * inv
        sq = xc * xc
        v0 = jnp.sum(jnp.where(head0, sq, 0.0), axis=-1, keepdims=True)
        v1 = jnp.sum(jnp.where(head0, 0.0, sq), axis=-1, keepdims=True)
        y = xc * lax.rsqrt(jnp.where(head0, v0, v1) * inv + NORM_EPS)
        gt = gate_ref[0, rows, :].astype(F32)
        y = y * gn_ref[...] * (gt / (1.0 + jnp.exp(-gt)))
        o_ref[0, rows, :] = y.astype(BF16)
        kfw = (kf32 * wk_ref[:, :pair]).astype(BF16)
        kv = lax.dot_general(kfw, v, tn, preferred_element_type=F32)
        return sf * dec_f + jnp.where(same_head, kv, 0.0)

    lax.fori_loop(0, n_chunks, fwd_body, jnp.zeros((pair, pair), F32))


def _retention_tables(c):
    hidx = jnp.arange(N_HEADS, dtype=F32)
    lg_f = jnp.log1p(-jnp.exp2(-(RET_DECAY_BASE_FWD + hidx)))
    lg_b = jnp.log1p(-jnp.exp2(-(RET_DECAY_BASE_BWD + hidx)))
    pos = jnp.arange(c, dtype=F32)
    diff = pos[:, None] - pos[None, :]
    dm_f = jnp.exp(jnp.maximum(diff, 0.0)[None] * lg_f[:, None, None])
    dm_b = jnp.exp(jnp.maximum(-diff, 0.0)[None] * lg_b[:, None, None])
    dmat = jnp.where((diff >= 0)[None], dm_f, dm_b)

    def per_lane(tab):
        t = jnp.repeat(tab[:, :, None], HEAD_DIM, axis=2)
        return t.reshape(N_HEADS // 2, 2, c, HEAD_DIM).transpose(0, 2, 1, 3).reshape(N_HEADS // 2, c, 2 * HEAD_DIM)

    wq_f = per_lane(jnp.exp((pos + 1.0)[None, :] * lg_f[:, None]))
    wq_b = per_lane(jnp.exp((c - pos)[None, :] * lg_b[:, None]))
    wk_f = per_lane(jnp.exp((c - 1.0 - pos)[None, :] * lg_f[:, None]))
    wk_b = per_lane(jnp.exp(pos[None, :] * lg_b[:, None]))
    wq = jnp.concatenate([wq_f, wq_b], axis=2)
    wk = jnp.concatenate([wk_f, wk_b], axis=2)
    dec = jnp.stack([jnp.repeat(jnp.exp(c * lg_f), HEAD_DIM), jnp.repeat(jnp.exp(c * lg_b), HEAD_DIM)], axis=0)
    dec = dec.reshape(2, N_HEADS // 2, 2 * HEAD_DIM).transpose(1, 0, 2)
    return dmat, wq, wk, dec


def _retention(q, k, v, gate, ret_norm_g, b, s):
    c = min(RET_CHUNK, s)
    n_chunks = s // c
    pair = 2 * HEAD_DIM
    dmat, wq, wk, dec = _retention_tables(c)
    seq_blk = pl.BlockSpec((1, s, pair), lambda bi, hp: (bi, 0, hp))
    r3 = lambda z: z.reshape(b, s, WIDTH)
    return pl.pallas_call(
        functools.partial(_ret_kernel, c=c, n_chunks=n_chunks),
        grid=(b, N_HEADS // 2),
        in_specs=[seq_blk, seq_blk, seq_blk, seq_blk,
                  pl.BlockSpec((2, c, c), lambda bi, hp: (hp, 0, 0)),
                  pl.BlockSpec((None, c, 2 * pair), lambda bi, hp: (hp, 0, 0)),
                  pl.BlockSpec((None, c, 2 * pair), lambda bi, hp: (hp, 0, 0)),
                  pl.BlockSpec((None, 2, pair), lambda bi, hp: (hp, 0, 0)),
                  pl.BlockSpec((1, pair), lambda bi, hp: (0, hp))],
        out_specs=seq_blk,
        out_shape=jax.ShapeDtypeStruct((b, s, WIDTH), BF16),
        scratch_shapes=[pltpu.VMEM((n_chunks, pair, pair), F32)],
        compiler_params=_cparams("arbitrary", "arbitrary"),
        name="retention",
    )(r3(q), r3(k), r3(v), r3(gate), dmat, wq, wk, dec, ret_norm_g.reshape(1, WIDTH)).reshape(b * s, WIDTH)


def _split_bf16(x):
    hi = x.astype(BF16)
    return hi, (x - hi.astype(F32)).astype(BF16)


def _outproj_kernel(x_ref, o1_ref, o2_ref, o3_ref, s1_ref, s2_ref, s3_ref, ret_ref, wout_ref, expand_ref,
                    gffn_ref, wr_ref, br_ref, tri_ref,
                    x1_ref, h2p_ref, ri_ref, gates_ref, cnt_ref, base_scr):
    i = pl.program_id(0)

    @pl.when(i == 0)
    def _():
        base_scr[...] = jnp.zeros_like(base_scr)

    sts = [s1_ref[...], s2_ref[...], s3_ref[...]]
    mx = jnp.maximum(jnp.maximum(sts[0], sts[1]), sts[2])
    es = [jnp.exp(st - mx) for st in sts]
    den = es[0] + es[1] + es[2]
    attn = None
    for e, o_ref in zip(es, (o1_ref, o2_ref, o3_ref)):
        hi, lo = _split_bf16(e / den)
        w_full = jnp.dot(jnp.concatenate([hi, lo], axis=1), expand_ref[...], preferred_element_type=F32)
        term = w_full * o_ref[...].astype(F32)
        attn = term if attn is None else attn + term
    mixed = jnp.concatenate([attn.astype(BF16), ret_ref[...]], axis=1)
    x1 = x_ref[...] + jnp.dot(mixed, wout_ref[...], preferred_element_type=F32)
    x1_ref[...] = x1

    h2 = x1 * lax.rsqrt(jnp.mean(x1 * x1, axis=-1, keepdims=True) + NORM_EPS) * gffn_ref[...]
    half = D_MODEL // 2
    ra = h2[:, :half].astype(BF16)
    rb = h2[:, half:].astype(BF16)
    ua = pltpu.bitcast(ra.astype(F32), jnp.uint32)
    ub = pltpu.bitcast(rb.astype(F32), jnp.uint32)
    h2p_ref[...] = ua | (ub >> 16)

    hi, lo = _split_bf16(h2)
    logits = jnp.dot(jnp.concatenate([hi, lo, hi], axis=1), wr_ref[...], preferred_element_type=F32) + br_ref[...]
    tm = logits.shape[0]
    lane = lax.broadcasted_iota(jnp.int32, (tm, LANES), 1).astype(F32)
    work = logits
    vals, idxs = [], []
    onehot = jnp.zeros((tm, LANES), F32)
    for _k in range(TOP_K):
        mk = jnp.max(work, axis=-1, keepdims=True)
        ik = jnp.min(jnp.where(work == mk, lane, float(LANES)), axis=-1, keepdims=True)
        sel = lane == ik
        onehot = jnp.where(sel, 1.0, onehot)
        work = jnp.where(sel, -jnp.inf, work)
        vals.append(mk)
        idxs.append(ik)
    ex = [jnp.exp(vk - vals[0]) for vk in vals]
    tot = ex[0] + ex[1] + ex[2] + ex[3]
    before = jnp.dot(tri_ref[...], onehot.astype(BF16), preferred_element_type=F32) + base_scr[...]
    ri = jnp.zeros((tm, LANES), F32)
    gt = jnp.zeros((tm, LANES), F32)
    for kk in range(TOP_K):
        rank = jnp.sum(jnp.where(lane == idxs[kk], before, 0.0), axis=-1, keepdims=True)
        ri = jnp.where(lane == float(kk), idxs[kk], ri)
        ri = jnp.where(lane == float(TOP_K + kk), rank, ri)
        gt = jnp.where(lane == float(kk), ex[kk] / tot, gt)
    ri_ref[...] = ri.astype(jnp.int32)
    gates_ref[...] = gt
    new_base = base_scr[...] + jnp.sum(onehot, axis=0, keepdims=True)
    base_scr[...] = new_base
    cnt_ref[...] = new_base


def _outproj_router(x2d, outs, stats, ret, wout_bf16, norm_ffn_g, w_router, b_router):
    t = x2d.shape[0]
    tm = TM_OUT
    row = lambda i: (i, 0)
    const = lambda i: (0, 0)
    head_of_col = jnp.arange(WIDTH) // HEAD_DIM
    expand = (jnp.arange(LANES)[:, None] == head_of_col[None, :]).astype(BF16)
    expand2 = jnp.concatenate([expand, expand], axis=0)
    wr = jnp.zeros((D_MODEL, LANES), F32).at[:, :N_EXPERTS].set(w_router)
    wr_hi, wr_lo = _split_bf16(wr)
    wr3 = jnp.concatenate([wr_hi, wr_hi, wr_lo], axis=0)
    br = jnp.full((1, LANES), NEG_INF, F32).at[0, :N_EXPERTS].set(b_router)
    tri = (jnp.arange(tm)[:, None] > jnp.arange(tm)[None, :]).astype(BF16)
    o_spec = pl.BlockSpec((tm, WIDTH), row)
    s_spec = pl.BlockSpec((tm, LANES), row)
    return pl.pallas_call(
        _outproj_kernel,
        grid=(t // tm,),
        in_specs=[pl.BlockSpec((tm, D_MODEL), row), o_spec, o_spec, o_spec, s_spec, s_spec, s_spec, o_spec,
                  pl.BlockSpec((D_MODEL, D_MODEL), const), pl.BlockSpec((2 * LANES, WIDTH), const),
                  pl.BlockSpec((1, D_MODEL), const), pl.BlockSpec((3 * D_MODEL, LANES), const),
                  pl.BlockSpec((1, LANES), const), pl.BlockSpec((tm, tm), const)],
        out_specs=[pl.BlockSpec((tm, D_MODEL), row), pl.BlockSpec((tm, D_MODEL // 2), row),
                   s_spec, s_spec, pl.BlockSpec((1, LANES), const)],
        out_shape=[jax.ShapeDtypeStruct((t, D_MODEL), F32), jax.ShapeDtypeStruct((t, D_MODEL // 2), jnp.uint32),
                   jax.ShapeDtypeStruct((t, LANES), jnp.int32), jax.ShapeDtypeStruct((t, LANES), F32),
                   jax.ShapeDtypeStruct((1, LANES), F32)],
        scratch_shapes=[pltpu.VMEM((1, LANES), F32)],
        compiler_params=_cparams("arbitrary"),
        name="outproj_router",
    )(x2d, *outs, *stats, ret, wout_bf16, expand2, norm_ffn_g.reshape(1, D_MODEL), wr3, br, tri)


def _dispatch_kernel(pstart_ref, cnt_ref, pend_ref, ri_ref, h2p_hbm, xs_hbm, zrow, sem, zsem, *, tg):
    i = pl.program_id(0)

    def zero_copy(dst):
        return pltpu.make_async_copy(zrow, xs_hbm.at[pl.ds(dst, 1)], zsem)

    @pl.when(i == 0)
    def _():
        zrow[...] = jnp.zeros_like(zrow)
        for e in range(N_EXPERTS):
            lo = pstart_ref[e] + cnt_ref[e]
            hi = pend_ref[e]

            def start(r, carry):
                zero_copy(r).start()
                return carry

            def wait(r, carry):
                zero_copy(r).wait()
                return carry

            lax.fori_loop(lo, hi, start, 0)
            lax.fori_loop(lo, hi, wait, 0)

    def row_copy(src, dst):
        return pltpu.make_async_copy(h2p_hbm.at[pl.ds(src, 1)], xs_hbm.at[pl.ds(dst, 1)], sem)

    def start(t, carry):
        for kk in range(TOP_K):
            e = ri_ref[0, 0, t * 2 * TOP_K + kk]
            r = ri_ref[0, 0, t * 2 * TOP_K + TOP_K + kk]
            row_copy(i * tg + t, pstart_ref[e] + r).start()
        return carry

    def wait(t, carry):
        for kk in range(TOP_K):
            row_copy(0, 0).wait()
        return carry

    lax.fori_loop(0, tg, start, 0)
    lax.fori_loop(0, tg, wait, 0)


def _dispatch(h2p, ri3, pstart, counts, pend, cap):
    t = h2p.shape[0]
    tg = TG
    return pl.pallas_call(
        functools.partial(_dispatch_kernel, tg=tg),
        grid_spec=pltpu.PrefetchScalarGridSpec(
            num_scalar_prefetch=3,
            grid=(t // tg,),
            in_specs=[pl.BlockSpec((1, 1, tg * 2 * TOP_K), lambda i, *_: (i, 0, 0), memory_space=pltpu.SMEM),
                      pl.BlockSpec(memory_space=pl.ANY)],
            out_specs=pl.BlockSpec(memory_space=pl.ANY),
            scratch_shapes=[pltpu.VMEM((1, D_MODEL // 2), jnp.uint32), pltpu.SemaphoreType.DMA(()),
                            pltpu.SemaphoreType.DMA(())],
        ),
        out_shape=jax.ShapeDtypeStruct((cap, D_MODEL // 2), jnp.uint32),
        compiler_params=_cparams("arbitrary"),
        name="moe_dispatch",
    )(pstart, counts, pend, ri3, h2p)


def _expert_kernel(blk_e_ref, nvalid_ref, xs_ref, wgu_ref, bgu_ref, wd_ref, bd_ref, ys_ref):
    i = pl.program_id(0)

    @pl.when(i < nvalid_ref[0])
    def _():
        u = xs_ref[...]
        xa = pltpu.bitcast(u & jnp.uint32(0xFFFF0000), F32).astype(BF16)
        xb = pltpu.bitcast(u << 16, F32).astype(BF16)
        x = jnp.concatenate([xa, xb], axis=1)
        gu = jnp.dot(x, wgu_ref[0], preferred_element_type=F32) + bgu_ref[0]
        gate = jnp.minimum(gu[:, :EXPERT_FF], SWIGLU_LIMIT)
        up = jnp.clip(gu[:, EXPERT_FF:], -SWIGLU_LIMIT, SWIGLU_LIMIT)
        act = gate * (1.0 / (1.0 + jnp.exp(-SWIGLU_ALPHA * gate))) * (up + 1.0)
        ys_ref[...] = jnp.dot(act.astype(BF16), wd_ref[0], preferred_element_type=F32) + bd_ref[0]


def _experts(xs, blk_e, nvalid, wgu_bf16, bgu, wd_bf16, bd):
    cap = xs.shape[0]
    nblk = cap // BM

    def blk(i, be, nv):
        return (jnp.minimum(i, nv[0] - 1), 0)

    def by_expert(i, be, nv):
        return (be[i], 0, 0)

    return pl.pallas_call(
        _expert_kernel,
        grid_spec=pltpu.PrefetchScalarGridSpec(
            num_scalar_prefetch=2,
            grid=(nblk,),
            in_specs=[pl.BlockSpec((BM, D_MODEL // 2), blk),
                      pl.BlockSpec((1, D_MODEL, 2 * EXPERT_FF), by_expert),
                      pl.BlockSpec((1, 1, 2 * EXPERT_FF), by_expert),
                      pl.BlockSpec((1, EXPERT_FF, D_MODEL), by_expert),
                      pl.BlockSpec((1, 1, D_MODEL), by_expert)],
            out_specs=pl.BlockSpec((BM, D_MODEL), blk),
        ),
        out_shape=jax.ShapeDtypeStruct((cap, D_MODEL), F32),
        compiler_params=_cparams("arbitrary"),
        name="moe_experts",
    )(blk_e, nvalid, xs, wgu_bf16, bgu.reshape(N_EXPERTS, 1, 2 * EXPERT_FF), wd_bf16, bd.reshape(N_EXPERTS, 1, D_MODEL))


def _combine_kernel(pstart_ref, ri_ref, ri_next_ref, x1_ref, gates_ref, gfin_ref, ys_hbm, o_ref, buf, sem, *, tc):
    i = pl.program_id(0)
    n = pl.num_programs(0)

    def row_copy(src, slot, kk, t):
        return pltpu.make_async_copy(ys_hbm.at[pl.ds(src, 1)], buf.at[slot, kk, pl.ds(t, 1)], sem.at[slot])

    def issue(idx_ref, slot):
        def body(t, carry):
            for kk in range(TOP_K):
                e = idx_ref[0, 0, t * 2 * TOP_K + kk]
                r = idx_ref[0, 0, t * 2 * TOP_K + TOP_K + kk]
                row_copy(pstart_ref[e] + r, slot, kk, t).start()
            return carry

        lax.fori_loop(0, tc, body, 0)

    @pl.when(i == 0)
    def _():
        issue(ri_ref, 0)

    @pl.when(i + 1 < n)
    def _():
        issue(ri_next_ref, (i + 1) % 2)

    slot = i % 2

    def wait(t, carry):
        for kk in range(TOP_K):
            row_copy(0, slot, kk, 0).wait()
        return carry

    lax.fori_loop(0, tc, wait, 0)
    z = x1_ref[...]
    g = gates_ref[...]
    for kk in range(TOP_K):
        z = z + g[:, kk:kk + 1] * buf[slot, kk]
    o_ref[...] = z * lax.rsqrt(jnp.mean(z * z, axis=-1, keepdims=True) + NORM_EPS) * gfin_ref[...]


def _combine(x1, gates, ri3, pstart, ys, norm_final_g):
    t = x1.shape[0]
    tc = TC
    n = t // tc
    row = lambda i, *_: (i, 0)
    return pl.pallas_call(
        functools.partial(_combine_kernel, tc=tc),
        grid_spec=pltpu.PrefetchScalarGridSpec(
            num_scalar_prefetch=1,
            grid=(n,),
            in_specs=[pl.BlockSpec((1, 1, tc * 2 * TOP_K), lambda i, *_: (i, 0, 0), memory_space=pltpu.SMEM),
                      pl.BlockSpec((1, 1, tc * 2 * TOP_K), lambda i, *_: (jnp.minimum(i + 1, n - 1), 0, 0),
                                   memory_space=pltpu.SMEM),
                      pl.BlockSpec((tc, D_MODEL), row), pl.BlockSpec((tc, LANES), row),
                      pl.BlockSpec((1, D_MODEL), lambda i, *_: (0, 0)),
                      pl.BlockSpec(memory_space=pl.ANY)],
            out_specs=pl.BlockSpec((tc, D_MODEL), row),
            scratch_shapes=[pltpu.VMEM((2, TOP_K, tc, D_MODEL), F32), pltpu.SemaphoreType.DMA((2,))],
        ),
        out_shape=jax.ShapeDtypeStruct((t, D_MODEL), F32),
        compiler_params=_cparams("arbitrary"),
        name="moe_combine",
    )(pstart, ri3, ri3, x1, gates, norm_final_g.reshape(1, D_MODEL), ys)


def _moe(x1, h2p, ri, gates, counts_f, wgu_bf16, bgu, wd_bf16, bd, norm_final_g):
    t = x1.shape[0]
    a = t * TOP_K
    cap = a + N_EXPERTS * BM
    nblk = cap // BM
    counts = counts_f[0, :N_EXPERTS].astype(jnp.int32)
    padded = ((counts + BM - 1) // BM) * BM
    pend = jnp.cumsum(padded)
    pstart = pend - padded
    nvalid = (pend[-1] // BM).reshape(1)
    blk_e = jnp.searchsorted(pend, jnp.arange(nblk, dtype=jnp.int32) * BM, side="right")
    last_e = jnp.searchsorted(pend, pend[-1] - 1, side="right")
    blk_e = jnp.minimum(blk_e, last_e).astype(jnp.int32)
    ri8 = ri[:, :2 * TOP_K]
    xs = _dispatch(h2p, ri8.reshape(t // TG, 1, TG * 2 * TOP_K), pstart, counts, pend, cap)
    ys = _experts(xs, blk_e, nvalid, wgu_bf16, bgu, wd_bf16, bd)
    return _combine(x1, gates, ri8.reshape(t // TC, 1, TC * 2 * TOP_K), pstart, ys, norm_final_g)


def _encoder(x, p):
    b, s, _ = x.shape
    x2d = x.reshape(b * s, D_MODEL)
    cos_t, sin_t = _rope_tables(s)
    qa, ka, va, qr, kr, vr, gr = _inproj(x2d, p["norm_mix_g"], p["w_in"], cos_t, sin_t, s)
    outs, stats = [], []
    for d in DILATIONS:
        o, st = _banded_attention(*[_to_residues(z, b, s, d) for z in (qa, ka, va)])
        outs.append(_from_residues(o, b, s, d))
        stats.append(_from_residues(st, b, s, d))
    ret = _retention(qr, kr, vr, gr, p["ret_norm_g"], b, s)
    x1, h2p, ri, gates, counts = _outproj_router(x2d, outs, stats, ret, p["w_out"], p["norm_ffn_g"],
                                                 p["w_router"], p["b_router"])
    y = _moe(x1, h2p, ri, gates, counts, p["w_gate_up"], p["b_gate_up"], p["w_down"], p["b_down"], p["norm_final_g"])
    return y.reshape(b, s, D_MODEL)


def kernel(x_prompt, x_sample, norm_mix_g, w_in, ret_norm_g, w_out, norm_ffn_g, w_router, b_router, w_gate_up, b_gate_up, w_down, b_down, norm_final_g):
    assert norm_mix_g.shape[0] == 1, "single layer"
    p = dict(norm_mix_g=norm_mix_g[0], w_in=w_in[0].astype(BF16), ret_norm_g=ret_norm_g[0],
             w_out=w_out[0].astype(BF16), norm_ffn_g=norm_ffn_g[0], w_router=w_router[0], b_router=b_router[0],
             w_gate_up=w_gate_up[0].astype(BF16), b_gate_up=b_gate_up[0], w_down=w_down[0].astype(BF16),
             b_down=b_down[0], norm_final_g=norm_final_g)
    return (_encoder(x_prompt, p), _encoder(x_sample, p))
```

```python
import functools

import jax
import jax.numpy as jnp
from jax import lax
from jax.experimental import pallas as pl
from jax.experimental.pallas import tpu as pltpu

D_MODEL = 1024
HEAD_DIM = 64
N_HEADS = 8
WIDTH = N_HEADS * HEAD_DIM
N_SLABS = 7
DILATIONS = (1, 4, 16)
HALF_SPAN = 64
ROPE_THETA = 10000.0
RET_DECAY_BASE_FWD = 5.0
RET_DECAY_BASE_BWD = 5.5
N_EXPERTS = 32
TOP_K = 4
EXPERT_FF = D_MODEL
SWIGLU_LIMIT = 7.0
SWIGLU_ALPHA = 1.702
NORM_EPS = 1e-6
NEG_INF = -1e30

LANES = 128
VMEM_LIMIT_BYTES = 48 * 1024 * 1024

TM_INPROJ = 512
TQ = 128
TL_ATTN = 512
RET_CHUNK = 256
TM_OUT = 512
TG = 512
BM = 512
TC = 256

F32 = jnp.float32
BF16 = jnp.bfloat16


def _cparams(*sem):
    return pltpu.CompilerParams(dimension_semantics=sem, vmem_limit_bytes=VMEM_LIMIT_BYTES)


_ROTATE = (True, True, False, True, True, False, False)
_SCALE = (HEAD_DIM ** -0.5, 1.0, 1.0, 1.0, HEAD_DIM ** -0.5, 1.0, 1.0)


def _inproj_kernel(x_ref, g_ref, w_ref, cos_ref, sin_ref, *out_refs):
    x = x_ref[...]
    h = (x * lax.rsqrt(jnp.mean(x * x, axis=-1, keepdims=True) + NORM_EPS) * g_ref[...]).astype(BF16)
    cos = cos_ref[...]
    sin = sin_ref[...]
    lane = lax.broadcasted_iota(jnp.int32, cos.shape, 1)
    first_half = (lane & (HEAD_DIM - 1)) < HEAD_DIM // 2
    for j, o_ref in enumerate(out_refs):
        p = jnp.dot(h, w_ref[:, j * WIDTH:(j + 1) * WIDTH], preferred_element_type=F32)
        if _ROTATE[j]:
            for c in range(WIDTH // LANES):
                blk = p[:, c * LANES:(c + 1) * LANES]
                partner = jnp.where(first_half, pltpu.roll(blk, LANES - HEAD_DIM // 2, 1),
                                    pltpu.roll(blk, HEAD_DIM // 2, 1))
                r = blk * cos + partner * sin
                if _SCALE[j] != 1.0:
                    r = r * _SCALE[j]
                o_ref[:, c * LANES:(c + 1) * LANES] = r.astype(BF16)
        else:
            o_ref[...] = p.astype(BF16)


def _inproj(x2d, g, w_bf16, cos_t, sin_t, seq):
    t = x2d.shape[0]
    tm = min(TM_INPROJ, seq)
    pos_blocks = seq // tm
    out = jax.ShapeDtypeStruct((t, WIDTH), BF16)
    row = lambda i: (i, 0)
    return pl.pallas_call(
        _inproj_kernel,
        grid=(t // tm,),
        in_specs=[
            pl.BlockSpec((tm, D_MODEL), row),
            pl.BlockSpec((1, D_MODEL), lambda i: (0, 0)),
            pl.BlockSpec((D_MODEL, N_SLABS * WIDTH), lambda i: (0, 0)),
            pl.BlockSpec((tm, LANES), lambda i: (i % pos_blocks, 0)),
            pl.BlockSpec((tm, LANES), lambda i: (i % pos_blocks, 0)),
        ],
        out_specs=[pl.BlockSpec((tm, WIDTH), row)] * N_SLABS,
        out_shape=[out] * N_SLABS,
        compiler_params=_cparams("arbitrary"),
        name="inproj",
    )(x2d, g.reshape(1, D_MODEL), w_bf16, cos_t, sin_t)


def _rope_tables(seq):
    half = HEAD_DIM // 2
    inv_freq = ROPE_THETA ** (-jnp.arange(0, HEAD_DIM, 2, dtype=F32) / HEAD_DIM)
    ang = jnp.arange(seq, dtype=F32)[:, None] * inv_freq[None, :]
    cos, sin = jnp.cos(ang), jnp.sin(ang)
    reps = LANES // HEAD_DIM
    cos_t = jnp.tile(jnp.concatenate([cos, cos], axis=1), (1, reps))
    sin_t = jnp.tile(jnp.concatenate([-sin, sin], axis=1), (1, reps))
    assert cos_t.shape == (seq, LANES) and half * 2 == HEAD_DIM
    return cos_t, sin_t


def _attn_kernel(q_ref, kp_ref, kc_ref, kn_ref, vp_ref, vc_ref, vn_ref, o_ref, st_ref, kbuf, vbuf, *, tl, sub_len):
    i = pl.program_id(1)
    hs = HALF_SPAN
    kbuf[0:hs] = kp_ref[0]
    kbuf[hs:hs + tl] = kc_ref[0]
    kbuf[hs + tl:hs + tl + hs] = kn_ref[0]
    vbuf[0:hs] = vp_ref[0]
    vbuf[hs:hs + tl] = vc_ref[0]
    vbuf[hs + tl:hs + tl + hs] = vn_ref[0]
    tk = TQ + 2 * hs
    qi = lax.broadcasted_iota(jnp.int32, (TQ, tk), 0)
    kj = lax.broadcasted_iota(jnp.int32, (TQ, tk), 1)
    band = (kj >= qi) & (kj - qi <= 2 * hs)
    lane = lax.broadcasted_iota(jnp.int32, (TQ, LANES), 1)
    for sub in range(tl // TQ):
        a = sub * TQ
        kpos = kj + (i * tl + a - hs)
        bias = jnp.where(band & (kpos >= 0) & (kpos < sub_len), 0.0, NEG_INF).astype(F32)
        outs = []
        st = jnp.zeros((TQ, LANES), F32)
        for h in range(N_HEADS):
            cols = slice(h * HEAD_DIM, (h + 1) * HEAD_DIM)
            qh = q_ref[0, a:a + TQ, cols]
            kh = kbuf[a:a + tk, cols]
            vh = vbuf[a:a + tk, cols]
            s = lax.dot_general(qh, kh, (((1,), (1,)), ((), ())), preferred_element_type=F32) + bias
            m = jnp.max(s, axis=-1, keepdims=True)
            p = jnp.exp(s - m)
            l = jnp.sum(p, axis=-1, keepdims=True)
            o = jnp.dot(p.astype(BF16), vh, preferred_element_type=F32) / l
            outs.append(o)
            st = jnp.where(lane == h, m + jnp.log(l), st)
        o_ref[0, a:a + TQ, :] = jnp.concatenate(outs, axis=1).astype(BF16)
        st_ref[0, a:a + TQ, :] = st


def _banded_attention(q, k, v):
    g, sub_len, _ = q.shape
    tl = min(TL_ATTN, sub_len)
    hs = HALF_SPAN
    per = tl // hs
    last = sub_len // hs - 1
    cur = pl.BlockSpec((1, tl, WIDTH), lambda b, i: (b, i, 0))
    prev = pl.BlockSpec((1, hs, WIDTH), lambda b, i: (b, jnp.maximum(i * per - 1, 0), 0))
    nxt = pl.BlockSpec((1, hs, WIDTH), lambda b, i: (b, jnp.minimum((i + 1) * per, last), 0))
    return pl.pallas_call(
        functools.partial(_attn_kernel, tl=tl, sub_len=sub_len),
        grid=(g, sub_len // tl),
        in_specs=[cur, prev, cur, nxt, prev, cur, nxt],
        out_specs=[cur, pl.BlockSpec((1, tl, LANES), lambda b, i: (b, i, 0))],
        out_shape=[jax.ShapeDtypeStruct((g, sub_len, WIDTH), BF16),
                   jax.ShapeDtypeStruct((g, sub_len, LANES), F32)],
        scratch_shapes=[pltpu.VMEM((tl + 2 * hs, WIDTH), BF16), pltpu.VMEM((tl + 2 * hs, WIDTH), BF16)],
        compiler_params=_cparams("arbitrary", "arbitrary"),
        name="banded_attention",
    )(q, k, k, k, v, v, v)


def _to_residues(z, b, s, d):
    w = z.shape[-1]
    if d == 1:
        return z.reshape(b, s, w)
    return z.reshape(b, s // d, d, w).transpose(0, 2, 1, 3).reshape(b * d, s // d, w)


def _from_residues(z, b, s, d):
    w = z.shape[-1]
    if d == 1:
        return z.reshape(b * s, w)
    return z.reshape(b, d, s // d, w).transpose(0, 2, 1, 3).reshape(b * s, w)


def _ret_kernel(q_ref, k_ref, v_ref, gate_ref, dmat_ref, wq_ref, wk_ref, dec_ref, gn_ref, o_ref, sb_scr, *, c, n_chunks):
    pair = 2 * HEAD_DIM
    lane = lax.broadcasted_iota(jnp.int32, (c, pair), 1)
    head0 = lane < HEAD_DIM
    blk_r = lax.broadcasted_iota(jnp.int32, (pair, pair), 0) // HEAD_DIM
    blk_c = lax.broadcasted_iota(jnp.int32, (pair, pair), 1) // HEAD_DIM
    same_head = blk_r == blk_c
    dec_f = dec_ref[0:1, :]
    dec_b = dec_ref[1:2, :]
    tn = (((0,), (0,)), ((), ()))
    nt = (((1,), (1,)), ((), ()))

    def rows_of(n):
        return pl.ds(pl.multiple_of(n * c, c), c)

    def bwd_body(t, sb):
        n = n_chunks - 1 - t
        sb_scr[n] = sb
        rows = rows_of(n)
        kb = (k_ref[0, rows, :].astype(F32) * wk_ref[:, pair:]).astype(BF16)
        kv = lax.dot_general(kb, v_ref[0, rows, :], tn, preferred_element_type=F32)
        return sb * dec_b + jnp.where(same_head, kv, 0.0)

    lax.fori_loop(0, n_chunks, bwd_body, jnp.zeros((pair, pair), F32))

    def fwd_body(n, sf):
        rows = rows_of(n)
        q = q_ref[0, rows, :]
        k = k_ref[0, rows, :]
        v = v_ref[0, rows, :]
        qf32 = q.astype(F32)
        kf32 = k.astype(F32)
        intra = []
        for hh in range(2):
            mask = head0 if hh == 0 else jnp.logical_not(head0)
            kh = jnp.where(mask, k, jnp.zeros_like(k))
            s = lax.dot_general(q, kh, nt, preferred_element_type=F32)
            a = (s * dmat_ref[hh]).astype(BF16)
            intra.append(jnp.dot(a, v, preferred_element_type=F32))
        qw = jnp.concatenate([(qf32 * wq_ref[:, :pair]).astype(BF16), (qf32 * wq_ref[:, pair:]).astype(BF16)], axis=1)
        states = jnp.concatenate([sf.astype(BF16), sb_scr[n].astype(BF16)], axis=0)
        tot = jnp.where(head0, intra[0], intra[1]) + jnp.dot(qw, states, preferred_element_type=F32)
        inv = 1.0 / HEAD_DIM
        s0 = jnp.sum(jnp.where(head0, tot, 0.0), axis=-1, keepdims=True)
        s1 = jnp.sum(jnp.where(head0, 0.0, tot), axis=-1, keepdims=True)
        xc = tot - jnp.where(head0, s0, s1) * inv
        sq = xc * xc
        v0 = jnp.sum(jnp.where(head0, sq, 0.0), axis=-1, keepdims=True)
        v1 = jnp.sum(jnp.where(head0, 0.0, sq), axis=-1, keepdims=True)
        y = xc * lax.rsqrt(jnp.where(head0, v0, v1) * inv + NORM_EPS)
        gt = gate_ref[0, rows, :].astype(F32)
        y = y * gn_ref[...] * (gt / (1.0 + jnp.exp(-gt)))
        o_ref[0, rows, :] = y.astype(BF16)
        kfw = (kf32 * wk_ref[:, :pair]).astype(BF16)
        kv = lax.dot_general(kfw, v, tn, preferred_element_type=F32)
        return sf * dec_f + jnp.where(same_head, kv, 0.0)

    lax.fori_loop(0, n_chunks, fwd_body, jnp.zeros((pair, pair), F32))


def _retention_tables(c):
    hidx = jnp.arange(N_HEADS, dtype=F32)
    lg_f = jnp.log1p(-jnp.exp2(-(RET_DECAY_BASE_FWD + hidx)))
    lg_b = jnp.log1p(-jnp.exp2(-(RET_DECAY_BASE_BWD + hidx)))
    pos = jnp.arange(c, dtype=F32)
    diff = pos[:, None] - pos[None, :]
    dm_f = jnp.exp(jnp.maximum(diff, 0.0)[None] * lg_f[:, None, None])
    dm_b = jnp.exp(jnp.maximum(-diff, 0.0)[None] * lg_b[:, None, None])
    dmat = jnp.where((diff >= 0)[None], dm_f, dm_b)

    def per_lane(tab):
        t = jnp.repeat(tab[:, :, None], HEAD_DIM, axis=2)
        return t.reshape(N_HEADS // 2, 2, c, HEAD_DIM).transpose(0, 2, 1, 3).reshape(N_HEADS // 2, c, 2 * HEAD_DIM)

    wq_f = per_lane(jnp.exp((pos + 1.0)[None, :] * lg_f[:, None]))
    wq_b = per_lane(jnp.exp((c - pos)[None, :] * lg_b[:, None]))
    wk_f = per_lane(jnp.exp((c - 1.0 - pos)[None, :] * lg_f[:, None]))
    wk_b = per_lane(jnp.exp(pos[None, :] * lg_b[:, None]))
    wq = jnp.concatenate([wq_f, wq_b], axis=2)
    wk = jnp.concatenate([wk_f, wk_b], axis=2)
    dec = jnp.stack([jnp.repeat(jnp.exp(c * lg_f), HEAD_DIM), jnp.repeat(jnp.exp(c * lg_b), HEAD_DIM)], axis=0)
    dec = dec.reshape(2, N_HEADS // 2, 2 * HEAD_DIM).transpose(1, 0, 2)
    return dmat, wq, wk, dec


def _retention(q, k, v, gate, ret_norm_g, b, s):
    c = min(RET_CHUNK, s)
    n_chunks = s // c
    pair = 2 * HEAD_DIM
    dmat, wq, wk, dec = _retention_tables(c)
    seq_blk = pl.BlockSpec((1, s, pair), lambda bi, hp: (bi, 0, hp))
    r3 = lambda z: z.reshape(b, s, WIDTH)
    return pl.pallas_call(
        functools.partial(_ret_kernel, c=c, n_chunks=n_chunks),
        grid=(b, N_HEADS // 2),
        in_specs=[seq_blk, seq_blk, seq_blk, seq_blk,
                  pl.BlockSpec((2, c, c), lambda bi, hp: (hp, 0, 0)),
                  pl.BlockSpec((None, c, 2 * pair), lambda bi, hp: (hp, 0, 0)),
                  pl.BlockSpec((None, c, 2 * pair), lambda bi, hp: (hp, 0, 0)),
                  pl.BlockSpec((None, 2, pair), lambda bi, hp: (hp, 0, 0)),
                  pl.BlockSpec((1, pair), lambda bi, hp: (0, hp))],
        out_specs=seq_blk,
        out_shape=jax.ShapeDtypeStruct((b, s, WIDTH), BF16),
        scratch_shapes=[pltpu.VMEM((n_chunks, pair, pair), F32)],
        compiler_params=_cparams("arbitrary", "arbitrary"),
        name="retention",
    )(r3(q), r3(k), r3(v), r3(gate), dmat, wq, wk, dec, ret_norm_g.reshape(1, WIDTH)).reshape(b * s, WIDTH)


ROW_TILE = 8
assert ROW_TILE * LANES == D_MODEL


def _store_row_tiles(ref, val):
    n = val.shape[0]
    for s in range(ROW_TILE):
        ref[pl.ds(s, n, stride=ROW_TILE), :] = val[:, s * LANES:(s + 1) * LANES]


def _load_row_tiles(ref, n, dtype=None):
    parts = [ref[pl.ds(s, n, stride=ROW_TILE), :] for s in range(ROW_TILE)]
    if dtype is not None:
        parts = [p.astype(dtype) for p in parts]
    return jnp.concatenate(parts, axis=1)


def _split_bf16(x):
    hi = x.astype(BF16)
    return hi, (x - hi.astype(F32)).astype(BF16)


def _outproj_kernel(x_ref, o1_ref, o2_ref, o3_ref, s1_ref, s2_ref, s3_ref, ret_ref, wout_ref, expand_ref,
                    gffn_ref, wr_ref, br_ref, tri_ref,
                    x1_ref, h2t_ref, ri_ref, gates_ref, cnt_ref, base_scr):
    i = pl.program_id(0)

    @pl.when(i == 0)
    def _():
        base_scr[...] = jnp.zeros_like(base_scr)

    sts = [s1_ref[...], s2_ref[...], s3_ref[...]]
    mx = jnp.maximum(jnp.maximum(sts[0], sts[1]), sts[2])
    es = [jnp.exp(st - mx) for st in sts]
    den = es[0] + es[1] + es[2]
    attn = None
    for e, o_ref in zip(es, (o1_ref, o2_ref, o3_ref)):
        hi, lo = _split_bf16(e / den)
        w_full = jnp.dot(jnp.concatenate([hi, lo], axis=1), expand_ref[...], preferred_element_type=F32)
        term = w_full * o_ref[...].astype(F32)
        attn = term if attn is None else attn + term
    mixed = jnp.concatenate([attn.astype(BF16), ret_ref[...]], axis=1)
    x1 = x_ref[...] + jnp.dot(mixed, wout_ref[...], preferred_element_type=F32)
    x1_ref[...] = x1

    h2 = x1 * lax.rsqrt(jnp.mean(x1 * x1, axis=-1, keepdims=True) + NORM_EPS) * gffn_ref[...]
    _store_row_tiles(h2t_ref, h2)

    hi, lo = _split_bf16(h2)
    logits = jnp.dot(jnp.concatenate([hi, lo, hi], axis=1), wr_ref[...], preferred_element_type=F32) + br_ref[...]
    tm = logits.shape[0]
    lane = lax.broadcasted_iota(jnp.int32, (tm, LANES), 1).astype(F32)
    work = logits
    vals, idxs = [], []
    onehot = jnp.zeros((tm, LANES), F32)
    for _k in range(TOP_K):
        mk = jnp.max(work, axis=-1, keepdims=True)
        ik = jnp.min(jnp.where(work == mk, lane, float(LANES)), axis=-1, keepdims=True)
        sel = lane == ik
        onehot = jnp.where(sel, 1.0, onehot)
        work = jnp.where(sel, -jnp.inf, work)
        vals.append(mk)
        idxs.append(ik)
    ex = [jnp.exp(vk - vals[0]) for vk in vals]
    tot = ex[0] + ex[1] + ex[2] + ex[3]
    before = jnp.dot(tri_ref[...], onehot.astype(BF16), preferred_element_type=F32) + base_scr[...]
    ri = jnp.zeros((tm, LANES), F32)
    gt = jnp.zeros((tm, LANES), F32)
    for kk in range(TOP_K):
        rank = jnp.sum(jnp.where(lane == idxs[kk], before, 0.0), axis=-1, keepdims=True)
        ri = jnp.where(lane == float(kk), idxs[kk], ri)
        ri = jnp.where(lane == float(TOP_K + kk), rank, ri)
        gt = jnp.where(lane == float(kk), ex[kk] / tot, gt)
    ri_ref[...] = ri.astype(jnp.int32)
    gates_ref[...] = gt
    new_base = base_scr[...] + jnp.sum(onehot, axis=0, keepdims=True)
    base_scr[...] = new_base
    cnt_ref[...] = new_base


def _outproj_router(x2d, outs, stats, ret, wout_bf16, norm_ffn_g, w_router, b_router):
    t = x2d.shape[0]
    tm = TM_OUT
    row = lambda i: (i, 0)
    const = lambda i: (0, 0)
    head_of_col = jnp.arange(WIDTH) // HEAD_DIM
    expand = (jnp.arange(LANES)[:, None] == head_of_col[None, :]).astype(BF16)
    expand2 = jnp.concatenate([expand, expand], axis=0)
    wr = jnp.zeros((D_MODEL, LANES), F32).at[:, :N_EXPERTS].set(w_router)
    wr_hi, wr_lo = _split_bf16(wr)
    wr3 = jnp.concatenate([wr_hi, wr_hi, wr_lo], axis=0)
    br = jnp.full((1, LANES), NEG_INF, F32).at[0, :N_EXPERTS].set(b_router)
    tri = (jnp.arange(tm)[:, None] > jnp.arange(tm)[None, :]).astype(BF16)
    o_spec = pl.BlockSpec((tm, WIDTH), row)
    s_spec = pl.BlockSpec((tm, LANES), row)
    return pl.pallas_call(
        _outproj_kernel,
        grid=(t // tm,),
        in_specs=[pl.BlockSpec((tm, D_MODEL), row), o_spec, o_spec, o_spec, s_spec, s_spec, s_spec, o_spec,
                  pl.BlockSpec((D_MODEL, D_MODEL), const), pl.BlockSpec((2 * LANES, WIDTH), const),
                  pl.BlockSpec((1, D_MODEL), const), pl.BlockSpec((3 * D_MODEL, LANES), const),
                  pl.BlockSpec((1, LANES), const), pl.BlockSpec((tm, tm), const)],
        out_specs=[pl.BlockSpec((tm, D_MODEL), row), pl.BlockSpec((tm * ROW_TILE, LANES), row),
                   s_spec, s_spec, pl.BlockSpec((1, LANES), const)],
        out_shape=[jax.ShapeDtypeStruct((t, D_MODEL), F32), jax.ShapeDtypeStruct((t * ROW_TILE, LANES), F32),
                   jax.ShapeDtypeStruct((t, LANES), jnp.int32), jax.ShapeDtypeStruct((t, LANES), F32),
                   jax.ShapeDtypeStruct((1, LANES), F32)],
        scratch_shapes=[pltpu.VMEM((1, LANES), F32)],
        compiler_params=_cparams("arbitrary"),
        name="outproj_router",
    )(x2d, *outs, *stats, ret, wout_bf16, expand2, norm_ffn_g.reshape(1, D_MODEL), wr3, br, tri)


def _tile_rows(r):
    return pl.ds(pl.multiple_of(r * ROW_TILE, ROW_TILE), ROW_TILE)


def _dispatch_kernel(pstart_ref, cnt_ref, pend_ref, ri_ref, h2t_ref, xs_hbm, zrow, sem, zsem, *, tg):
    i = pl.program_id(0)

    def zero_copy(dst):
        return pltpu.make_async_copy(zrow, xs_hbm.at[_tile_rows(dst)], zsem)

    @pl.when(i == 0)
    def _():
        zrow[...] = jnp.zeros_like(zrow)
        for e in range(N_EXPERTS):
            lo = pstart_ref[e] + cnt_ref[e]
            hi = pend_ref[e]

            def start(r, carry):
                zero_copy(r).start()
                return carry

            def wait(r, carry):
                zero_copy(r).wait()
                return carry

            lax.fori_loop(lo, hi, start, 0)
            lax.fori_loop(lo, hi, wait, 0)

    def row_copy(src, dst):
        return pltpu.make_async_copy(h2t_ref.at[_tile_rows(src)], xs_hbm.at[_tile_rows(dst)], sem)

    def start(t, carry):
        for kk in range(TOP_K):
            e = ri_ref[0, 0, t * 2 * TOP_K + kk]
            r = ri_ref[0, 0, t * 2 * TOP_K + TOP_K + kk]
            row_copy(t, pstart_ref[e] + r).start()
        return carry

    def wait(t, carry):
        for kk in range(TOP_K):
            row_copy(0, 0).wait()
        return carry

    lax.fori_loop(0, tg, start, 0)
    lax.fori_loop(0, tg, wait, 0)


def _dispatch(h2t, ri3, pstart, counts, pend, cap):
    t = h2t.shape[0] // ROW_TILE
    tg = TG
    return pl.pallas_call(
        functools.partial(_dispatch_kernel, tg=tg),
        grid_spec=pltpu.PrefetchScalarGridSpec(
            num_scalar_prefetch=3,
            grid=(t // tg,),
            in_specs=[pl.BlockSpec((1, 1, tg * 2 * TOP_K), lambda i, *_: (i, 0, 0), memory_space=pltpu.SMEM),
                      pl.BlockSpec((tg * ROW_TILE, LANES), lambda i, *_: (i, 0))],
            out_specs=pl.BlockSpec(memory_space=pl.ANY),
            scratch_shapes=[pltpu.VMEM((ROW_TILE, LANES), F32), pltpu.SemaphoreType.DMA(()),
                            pltpu.SemaphoreType.DMA(())],
        ),
        out_shape=jax.ShapeDtypeStruct((cap * ROW_TILE, LANES), F32),
        compiler_params=_cparams("arbitrary"),
        name="moe_dispatch",
    )(pstart, counts, pend, ri3, h2t)


def _expert_kernel(blk_e_ref, nvalid_ref, xs_ref, wgu_ref, bgu_ref, wd_ref, bd_ref, ys_ref):
    i = pl.program_id(0)

    @pl.when(i < nvalid_ref[0])
    def _():
        x = _load_row_tiles(xs_ref, BM, BF16)
        gu = jnp.dot(x, wgu_ref[0], preferred_element_type=F32) + bgu_ref[0]
        gate = jnp.minimum(gu[:, :EXPERT_FF], SWIGLU_LIMIT)
        up = jnp.clip(gu[:, EXPERT_FF:], -SWIGLU_LIMIT, SWIGLU_LIMIT)
        act = gate * (1.0 / (1.0 + jnp.exp(-SWIGLU_ALPHA * gate))) * (up + 1.0)
        y = jnp.dot(act.astype(BF16), wd_ref[0], preferred_element_type=F32) + bd_ref[0]
        _store_row_tiles(ys_ref, y)


def _experts(xs, blk_e, nvalid, wgu_bf16, bgu, wd_bf16, bd):
    cap = xs.shape[0] // ROW_TILE
    nblk = cap // BM

    def blk(i, be, nv):
        return (jnp.minimum(i, nv[0] - 1), 0)

    def by_expert(i, be, nv):
        return (be[i], 0, 0)

    return pl.pallas_call(
        _expert_kernel,
        grid_spec=pltpu.PrefetchScalarGridSpec(
            num_scalar_prefetch=2,
            grid=(nblk,),
            in_specs=[pl.BlockSpec((BM * ROW_TILE, LANES), blk),
                      pl.BlockSpec((1, D_MODEL, 2 * EXPERT_FF), by_expert),
                      pl.BlockSpec((1, 1, 2 * EXPERT_FF), by_expert),
                      pl.BlockSpec((1, EXPERT_FF, D_MODEL), by_expert),
                      pl.BlockSpec((1, 1, D_MODEL), by_expert)],
            out_specs=pl.BlockSpec((BM * ROW_TILE, LANES), blk),
        ),
        out_shape=jax.ShapeDtypeStruct((cap * ROW_TILE, LANES), F32),
        compiler_params=_cparams("arbitrary"),
        name="moe_experts",
    )(blk_e, nvalid, xs, wgu_bf16, bgu.reshape(N_EXPERTS, 1, 2 * EXPERT_FF), wd_bf16, bd.reshape(N_EXPERTS, 1, D_MODEL))


def _combine_kernel(pstart_ref, ri_ref, ri_next_ref, x1_ref, gates_ref, gfin_ref, ys_hbm, o_ref, buf, sem, *, tc):
    i = pl.program_id(0)
    n = pl.num_programs(0)

    def row_copy(src, slot, kk, t):
        return pltpu.make_async_copy(ys_hbm.at[_tile_rows(src)], buf.at[slot, kk, _tile_rows(t)], sem.at[slot])

    def issue(idx_ref, slot):
        def body(t, carry):
            for kk in range(TOP_K):
                e = idx_ref[0, 0, t * 2 * TOP_K + kk]
                r = idx_ref[0, 0, t * 2 * TOP_K + TOP_K + kk]
                row_copy(pstart_ref[e] + r, slot, kk, t).start()
            return carry

        lax.fori_loop(0, tc, body, 0)

    @pl.when(i == 0)
    def _():
        issue(ri_ref, 0)

    @pl.when(i + 1 < n)
    def _():
        issue(ri_next_ref, (i + 1) % 2)

    slot = i % 2

    def wait(t, carry):
        for kk in range(TOP_K):
            row_copy(0, slot, kk, 0).wait()
        return carry

    lax.fori_loop(0, tc, wait, 0)
    g = gates_ref[...]
    gk = [jnp.broadcast_to(g[:, kk:kk + 1], (tc, LANES)) for kk in range(TOP_K)]
    zs = []
    ssq = jnp.zeros((tc, 1), F32)
    for s in range(ROW_TILE):
        z = x1_ref[:, s * LANES:(s + 1) * LANES]
        for kk in range(TOP_K):
            z = z + gk[kk] * buf[slot, kk, pl.ds(s, tc, stride=ROW_TILE), :]
        zs.append(z)
        ssq = ssq + jnp.sum(z * z, axis=-1, keepdims=True)
    inv = lax.rsqrt(ssq * (1.0 / D_MODEL) + NORM_EPS)
    for s in range(ROW_TILE):
        cols = slice(s * LANES, (s + 1) * LANES)
        o_ref[:, cols] = zs[s] * inv * gfin_ref[:, cols]


def _combine(x1, gates, ri3, pstart, ys, norm_final_g):
    t = x1.shape[0]
    tc = TC
    n = t // tc
    row = lambda i, *_: (i, 0)
    return pl.pallas_call(
        functools.partial(_combine_kernel, tc=tc),
        grid_spec=pltpu.PrefetchScalarGridSpec(
            num_scalar_prefetch=1,
            grid=(n,),
            in_specs=[pl.BlockSpec((1, 1, tc * 2 * TOP_K), lambda i, *_: (i, 0, 0), memory_space=pltpu.SMEM),
                      pl.BlockSpec((1, 1, tc * 2 * TOP_K), lambda i, *_: (jnp.minimum(i + 1, n - 1), 0, 0),
                                   memory_space=pltpu.SMEM),
                      pl.BlockSpec((tc, D_MODEL), row), pl.BlockSpec((tc, LANES), row),
                      pl.BlockSpec((1, D_MODEL), lambda i, *_: (0, 0)),
                      pl.BlockSpec(memory_space=pl.ANY)],
            out_specs=pl.BlockSpec((tc, D_MODEL), row),
            scratch_shapes=[pltpu.VMEM((2, TOP_K, tc * ROW_TILE, LANES), F32), pltpu.SemaphoreType.DMA((2,))],
        ),
        out_shape=jax.ShapeDtypeStruct((t, D_MODEL), F32),
        compiler_params=_cparams("arbitrary"),
        name="moe_combine",
    )(pstart, ri3, ri3, x1, gates, norm_final_g.reshape(1, D_MODEL), ys)


def _moe(x1, h2t, ri, gates, counts_f, wgu_bf16, bgu, wd_bf16, bd, norm_final_g):
    t = x1.shape[0]
    a = t * TOP_K
    cap = a + N_EXPERTS * BM
    nblk = cap // BM
    counts = counts_f[0, :N_EXPERTS].astype(jnp.int32)
    padded = ((counts + BM - 1) // BM) * BM
    pend = jnp.cumsum(padded)
    pstart = pend - padded
    nvalid = (pend[-1] // BM).reshape(1)
    blk_e = jnp.searchsorted(pend, jnp.arange(nblk, dtype=jnp.int32) * BM, side="right")
    last_e = jnp.searchsorted(pend, pend[-1] - 1, side="right")
    blk_e = jnp.minimum(blk_e, last_e).astype(jnp.int32)
    ri8 = ri[:, :2 * TOP_K]
    xs = _dispatch(h2t, ri8.reshape(t // TG, 1, TG * 2 * TOP_K), pstart, counts, pend, cap)
    ys = _experts(xs, blk_e, nvalid, wgu_bf16, bgu, wd_bf16, bd)
    return _combine(x1, gates, ri8.reshape(t // TC, 1, TC * 2 * TOP_K), pstart, ys, norm_final_g)


def _encoder(x, p):
    b, s, _ = x.shape
    x2d = x.reshape(b * s, D_MODEL)
    cos_t, sin_t = _rope_tables(s)
    qa, ka, va, qr, kr, vr, gr = _inproj(x2d, p["norm_mix_g"], p["w_in"], cos_t, sin_t, s)
    outs, stats = [], []
    for d in DILATIONS:
        o, st = _banded_attention(*[_to_residues(z, b, s, d) for z in (qa, ka, va)])
        outs.append(_from_residues(o, b, s, d))
        stats.append(_from_residues(st, b, s, d))
    ret = _retention(qr, kr, vr, gr, p["ret_norm_g"], b, s)
    x1, h2t, ri, gates, counts = _outproj_router(x2d, outs, stats, ret, p["w_out"], p["norm_ffn_g"],
                                                 p["w_router"], p["b_router"])
    y = _moe(x1, h2t, ri, gates, counts, p["w_gate_up"], p["b_gate_up"], p["w_down"], p["b_down"], p["norm_final_g"])
    return y.reshape(b, s, D_MODEL)


def kernel(x_prompt, x_sample, norm_mix_g, w_in, ret_norm_g, w_out, norm_ffn_g, w_router, b_router, w_gate_up, b_gate_up, w_down, b_down, norm_final_g):
    assert norm_mix_g.shape[0] == 1, "single layer"
    p = dict(norm_mix_g=norm_mix_g[0], w_in=w_in[0].astype(BF16), ret_norm_g=ret_norm_g[0],
             w_out=w_out[0].astype(BF16), norm_ffn_g=norm_ffn_g[0], w_router=w_router[0], b_router=b_router[0],
             w_gate_up=w_gate_up[0].astype(BF16), b_gate_up=b_gate_up[0], w_down=w_down[0].astype(BF16),
             b_down=b_down[0], norm_final_g=norm_final_g)
    return (_encoder(x_prompt, p), _encoder(x_sample, p))
```

```python
import functools

import jax
import jax.numpy as jnp
from jax import lax
from jax.experimental import pallas as pl
from jax.experimental.pallas import tpu as pltpu

D_MODEL = 1024
HEAD_DIM = 64
N_HEADS = 8
WIDTH = N_HEADS * HEAD_DIM
N_SLABS = 7
DILATIONS = (1, 4, 16)
HALF_SPAN = 64
ROPE_THETA = 10000.0
RET_DECAY_BASE_FWD = 5.0
RET_DECAY_BASE_BWD = 5.5
N_EXPERTS = 32
TOP_K = 4
EXPERT_FF = D_MODEL
SWIGLU_LIMIT = 7.0
SWIGLU_ALPHA = 1.702
NORM_EPS = 1e-6
NEG_INF = -1e30

LANES = 128
VMEM_LIMIT_BYTES = 48 * 1024 * 1024

TM_INPROJ = 512
TQ = 128
TL_ATTN = 512
HEAD_GROUP = 4
RET_CHUNK = 256
TM_OUT = 512
TG = 512
BM = 512
TC = 256
DMA_UNROLL = 4

F32 = jnp.float32
BF16 = jnp.bfloat16


def _cparams(*sem):
    return pltpu.CompilerParams(dimension_semantics=sem, vmem_limit_bytes=VMEM_LIMIT_BYTES)


_ROTATE = (True, True, False, True, True, False, False)
_SCALE = (HEAD_DIM ** -0.5, 1.0, 1.0, 1.0, HEAD_DIM ** -0.5, 1.0, 1.0)


def _inproj_kernel(x_ref, g_ref, w_ref, cos_ref, sin_ref, *out_refs):
    x = x_ref[...]
    h = (x * lax.rsqrt(jnp.mean(x * x, axis=-1, keepdims=True) + NORM_EPS) * g_ref[...]).astype(BF16)
    cos = cos_ref[...]
    sin = sin_ref[...]
    lane = lax.broadcasted_iota(jnp.int32, cos.shape, 1)
    first_half = (lane & (HEAD_DIM - 1)) < HEAD_DIM // 2
    for j, o_ref in enumerate(out_refs):
        p = jnp.dot(h, w_ref[:, j * WIDTH:(j + 1) * WIDTH], preferred_element_type=F32)
        if _ROTATE[j]:
            for c in range(WIDTH // LANES):
                blk = p[:, c * LANES:(c + 1) * LANES]
                partner = jnp.where(first_half, pltpu.roll(blk, LANES - HEAD_DIM // 2, 1),
                                    pltpu.roll(blk, HEAD_DIM // 2, 1))
                r = blk * cos + partner * sin
                if _SCALE[j] != 1.0:
                    r = r * _SCALE[j]
                o_ref[:, c * LANES:(c + 1) * LANES] = r.astype(BF16)
        else:
            o_ref[...] = p.astype(BF16)


def _inproj(x2d, g, w_bf16, cos_t, sin_t, seq):
    t = x2d.shape[0]
    tm = min(TM_INPROJ, seq)
    pos_blocks = seq // tm
    out = jax.ShapeDtypeStruct((t, WIDTH), BF16)
    row = lambda i: (i, 0)
    return pl.pallas_call(
        _inproj_kernel,
        grid=(t // tm,),
        in_specs=[
            pl.BlockSpec((tm, D_MODEL), row),
            pl.BlockSpec((1, D_MODEL), lambda i: (0, 0)),
            pl.BlockSpec((D_MODEL, N_SLABS * WIDTH), lambda i: (0, 0)),
            pl.BlockSpec((tm, LANES), lambda i: (i % pos_blocks, 0)),
            pl.BlockSpec((tm, LANES), lambda i: (i % pos_blocks, 0)),
        ],
        out_specs=[pl.BlockSpec((tm, WIDTH), row)] * N_SLABS,
        out_shape=[out] * N_SLABS,
        compiler_params=_cparams("arbitrary"),
        name="inproj",
    )(x2d, g.reshape(1, D_MODEL), w_bf16, cos_t, sin_t)


def _rope_tables(seq):
    half = HEAD_DIM // 2
    inv_freq = ROPE_THETA ** (-jnp.arange(0, HEAD_DIM, 2, dtype=F32) / HEAD_DIM)
    ang = jnp.arange(seq, dtype=F32)[:, None] * inv_freq[None, :]
    cos, sin = jnp.cos(ang), jnp.sin(ang)
    reps = LANES // HEAD_DIM
    cos_t = jnp.tile(jnp.concatenate([cos, cos], axis=1), (1, reps))
    sin_t = jnp.tile(jnp.concatenate([-sin, sin], axis=1), (1, reps))
    assert cos_t.shape == (seq, LANES) and half * 2 == HEAD_DIM
    return cos_t, sin_t


def _attn_kernel(q_ref, kp_ref, kc_ref, kn_ref, vp_ref, vc_ref, vn_ref, o_ref, st_ref, kbuf, vbuf, *, tl, sub_len):
    i = pl.program_id(1)
    hs = HALF_SPAN
    kbuf[0:hs] = kp_ref[0]
    kbuf[hs:hs + tl] = kc_ref[0]
    kbuf[hs + tl:hs + tl + hs] = kn_ref[0]
    vbuf[0:hs] = vp_ref[0]
    vbuf[hs:hs + tl] = vc_ref[0]
    vbuf[hs + tl:hs + tl + hs] = vn_ref[0]
    tk = TQ + 2 * hs
    gw = HEAD_GROUP * HEAD_DIM
    qi = lax.broadcasted_iota(jnp.int32, (TQ, tk), 0)
    kj = lax.broadcasted_iota(jnp.int32, (TQ, tk), 1)
    band = (kj >= qi) & (kj - qi <= 2 * hs)
    lane = lax.broadcasted_iota(jnp.int32, (TQ, LANES), 1)
    head_of_lane = lax.broadcasted_iota(jnp.int32, (TQ, gw), 1) // HEAD_DIM
    for sub in range(tl // TQ):
        a = sub * TQ
        kpos = kj + (i * tl + a - hs)
        bias = jnp.where(band & (kpos >= 0) & (kpos < sub_len), 0.0, NEG_INF).astype(F32)
        bias = jnp.concatenate([bias] * HEAD_GROUP, axis=0)
        st = jnp.zeros((TQ, LANES), F32)
        for g in range(N_HEADS // HEAD_GROUP):
            cols = slice(g * gw, (g + 1) * gw)
            q4 = q_ref[0, a:a + TQ, cols]
            k4 = kbuf[a:a + tk, cols]
            v4 = vbuf[a:a + tk, cols]
            lhs = jnp.concatenate([jnp.where(head_of_lane == h, q4, jnp.zeros_like(q4)) for h in range(HEAD_GROUP)], axis=0)
            s = lax.dot_general(lhs, k4, (((1,), (1,)), ((), ())), preferred_element_type=F32) + bias
            m = jnp.max(s, axis=-1, keepdims=True)
            p = jnp.exp(s - m)
            l = jnp.sum(p, axis=-1, keepdims=True)
            o_all = jnp.dot(p.astype(BF16), v4, preferred_element_type=F32) / l
            lse = m + jnp.log(l)
            o = o_all[0:TQ]
            for h in range(HEAD_GROUP):
                rows = slice(h * TQ, (h + 1) * TQ)
                if h:
                    o = jnp.where(head_of_lane == h, o_all[rows], o)
                st = jnp.where(lane == g * HEAD_GROUP + h, lse[rows], st)
            o_ref[0, a:a + TQ, cols] = o.astype(BF16)
        st_ref[0, a:a + TQ, :] = st


def _banded_attention(q, k, v):
    g, sub_len, _ = q.shape
    tl = min(TL_ATTN, sub_len)
    hs = HALF_SPAN
    per = tl // hs
    last = sub_len // hs - 1
    cur = pl.BlockSpec((1, tl, WIDTH), lambda b, i: (b, i, 0))
    prev = pl.BlockSpec((1, hs, WIDTH), lambda b, i: (b, jnp.maximum(i * per - 1, 0), 0))
    nxt = pl.BlockSpec((1, hs, WIDTH), lambda b, i: (b, jnp.minimum((i + 1) * per, last), 0))
    return pl.pallas_call(
        functools.partial(_attn_kernel, tl=tl, sub_len=sub_len),
        grid=(g, sub_len // tl),
        in_specs=[cur, prev, cur, nxt, prev, cur, nxt],
        out_specs=[cur, pl.BlockSpec((1, tl, LANES), lambda b, i: (b, i, 0))],
        out_shape=[jax.ShapeDtypeStruct((g, sub_len, WIDTH), BF16),
                   jax.ShapeDtypeStruct((g, sub_len, LANES), F32)],
        scratch_shapes=[pltpu.VMEM((tl + 2 * hs, WIDTH), BF16), pltpu.VMEM((tl + 2 * hs, WIDTH), BF16)],
        compiler_params=_cparams("arbitrary", "arbitrary"),
        name="banded_attention",
    )(q, k, k, k, v, v, v)


def _to_residues(z, b, s, d):
    w = z.shape[-1]
    if d == 1:
        return z.reshape(b, s, w)
    return z.reshape(b, s // d, d, w).transpose(0, 2, 1, 3).reshape(b * d, s // d, w)


def _from_residues(z, b, s, d):
    w = z.shape[-1]
    if d == 1:
        return z.reshape(b * s, w)
    return z.reshape(b, d, s // d, w).transpose(0, 2, 1, 3).reshape(b * s, w)


def _ret_kernel(q_ref, k_ref, v_ref, gate_ref, dmat_ref, wq_ref, wk_ref, dec_ref, gn_ref, o_ref, sb_scr, *, c, n_chunks):
    pair = 2 * HEAD_DIM
    lane = lax.broadcasted_iota(jnp.int32, (c, pair), 1)
    head0 = lane < HEAD_DIM
    blk_r = lax.broadcasted_iota(jnp.int32, (pair, pair), 0) // HEAD_DIM
    blk_c = lax.broadcasted_iota(jnp.int32, (pair, pair), 1) // HEAD_DIM
    same_head = blk_r == blk_c
    dec_f = dec_ref[0:1, :]
    dec_b = dec_ref[1:2, :]
    tn = (((0,), (0,)), ((), ()))
    nt = (((1,), (1,)), ((), ()))

    def rows_of(n):
        return pl.ds(pl.multiple_of(n * c, c), c)

    def bwd_body(t, sb):
        n = n_chunks - 1 - t
        sb_scr[n] = sb
        rows = rows_of(n)
        kb = (k_ref[0, rows, :].astype(F32) * wk_ref[:, pair:]).astype(BF16)
        kv = lax.dot_general(kb, v_ref[0, rows, :], tn, preferred_element_type=F32)
        return sb * dec_b + jnp.where(same_head, kv, 0.0)

    lax.fori_loop(0, n_chunks, bwd_body, jnp.zeros((pair, pair), F32))

    def fwd_body(n, sf):
        rows = rows_of(n)
        q = q_ref[0, rows, :]
        k = k_ref[0, rows, :]
        v = v_ref[0, rows, :]
        qf32 = q.astype(F32)
        kf32 = k.astype(F32)
        intra = []
        for hh in range(2):
            mask = head0 if hh == 0 else jnp.logical_not(head0)
            kh = jnp.where(mask, k, jnp.zeros_like(k))
            s = lax.dot_general(q, kh, nt, preferred_element_type=F32)
            a = (s * dmat_ref[hh]).astype(BF16)
            intra.append(jnp.dot(a, v, preferred_element_type=F32))
        qw = jnp.concatenate([(qf32 * wq_ref[:, :pair]).astype(BF16), (qf32 * wq_ref[:, pair:]).astype(BF16)], axis=1)
        states = jnp.concatenate([sf.astype(BF16), sb_scr[n].astype(BF16)], axis=0)
        tot = jnp.where(head0, intra[0], intra[1]) + jnp.dot(qw, states, preferred_element_type=F32)
        inv = 1.0 / HEAD_DIM
        s0 = jnp.sum(jnp.where(head0, tot, 0.0), axis=-1, keepdims=True)
        s1 = jnp.sum(jnp.where(head0, 0.0, tot), axis=-1, keepdims=True)
        xc = tot - jnp.where(head0, s0, s1) * inv
        sq = xc * xc
        v0 = jnp.sum(jnp.where(head0, sq, 0.0), axis=-1, keepdims=True)
        v1 = jnp.sum(jnp.where(head0, 0.0, sq), axis=-1, keepdims=True)
        y = xc * lax.rsqrt(jnp.where(head0, v0, v1) * inv + NORM_EPS)
        gt = gate_ref[0, rows, :].astype(F32)
        y = y * gn_ref[...] * (gt / (1.0 + jnp.exp(-gt)))
        o_ref[0, rows, :] = y.astype(BF16)
        kfw = (kf32 * wk_ref[:, :pair]).astype(BF16)
        kv = lax.dot_general(kfw, v, tn, preferred_element_type=F32)
        return sf * dec_f + jnp.where(same_head, kv, 0.0)

    lax.fori_loop(0, n_chunks, fwd_body, jnp.zeros((pair, pair), F32))


def _retention_tables(c):
    hidx = jnp.arange(N_HEADS, dtype=F32)
    lg_f = jnp.log1p(-jnp.exp2(-(RET_DECAY_BASE_FWD + hidx)))
    lg_b = jnp.log1p(-jnp.exp2(-(RET_DECAY_BASE_BWD + hidx)))
    pos = jnp.arange(c, dtype=F32)
    diff = pos[:, None] - pos[None, :]
    dm_f = jnp.exp(jnp.maximum(diff, 0.0)[None] * lg_f[:, None, None])
    dm_b = jnp.exp(jnp.maximum(-diff, 0.0)[None] * lg_b[:, None, None])
    dmat = jnp.where((diff >= 0)[None], dm_f, dm_b)

    def per_lane(tab):
        t = jnp.repeat(tab[:, :, None], HEAD_DIM, axis=2)
        return t.reshape(N_HEADS // 2, 2, c, HEAD_DIM).transpose(0, 2, 1, 3).reshape(N_HEADS // 2, c, 2 * HEAD_DIM)

    wq_f = per_lane(jnp.exp((pos + 1.0)[None, :] * lg_f[:, None]))
    wq_b = per_lane(jnp.exp((c - pos)[None, :] * lg_b[:, None]))
    wk_f = per_lane(jnp.exp((c - 1.0 - pos)[None, :] * lg_f[:, None]))
    wk_b = per_lane(jnp.exp(pos[None, :] * lg_b[:, None]))
    wq = jnp.concatenate([wq_f, wq_b], axis=2)
    wk = jnp.concatenate([wk_f, wk_b], axis=2)
    dec = jnp.stack([jnp.repeat(jnp.exp(c * lg_f), HEAD_DIM), jnp.repeat(jnp.exp(c * lg_b), HEAD_DIM)], axis=0)
    dec = dec.reshape(2, N_HEADS // 2, 2 * HEAD_DIM).transpose(1, 0, 2)
    return dmat, wq, wk, dec


def _retention(q, k, v, gate, ret_norm_g, b, s):
    c = min(RET_CHUNK, s)
    n_chunks = s // c
    pair = 2 * HEAD_DIM
    dmat, wq, wk, dec = _retention_tables(c)
    seq_blk = pl.BlockSpec((1, s, pair), lambda bi, hp: (bi, 0, hp))
    r3 = lambda z: z.reshape(b, s, WIDTH)
    return pl.pallas_call(
        functools.partial(_ret_kernel, c=c, n_chunks=n_chunks),
        grid=(b, N_HEADS // 2),
        in_specs=[seq_blk, seq_blk, seq_blk, seq_blk,
                  pl.BlockSpec((2, c, c), lambda bi, hp: (hp, 0, 0)),
                  pl.BlockSpec((None, c, 2 * pair), lambda bi, hp: (hp, 0, 0)),
                  pl.BlockSpec((None, c, 2 * pair), lambda bi, hp: (hp, 0, 0)),
                  pl.BlockSpec((None, 2, pair), lambda bi, hp: (hp, 0, 0)),
                  pl.BlockSpec((1, pair), lambda bi, hp: (0, hp))],
        out_specs=seq_blk,
        out_shape=jax.ShapeDtypeStruct((b, s, WIDTH), BF16),
        scratch_shapes=[pltpu.VMEM((n_chunks, pair, pair), F32)],
        compiler_params=_cparams("arbitrary", "arbitrary"),
        name="retention",
    )(r3(q), r3(k), r3(v), r3(gate), dmat, wq, wk, dec, ret_norm_g.reshape(1, WIDTH)).reshape(b * s, WIDTH)


ROW_TILE = 8
assert ROW_TILE * LANES == D_MODEL


def _store_row_tiles(ref, val):
    n = val.shape[0]
    for s in range(ROW_TILE):
        ref[pl.ds(s, n, stride=ROW_TILE), :] = val[:, s * LANES:(s + 1) * LANES]


def _load_row_tiles(ref, n, dtype=None):
    parts = [ref[pl.ds(s, n, stride=ROW_TILE), :] for s in range(ROW_TILE)]
    if dtype is not None:
        parts = [p.astype(dtype) for p in parts]
    return jnp.concatenate(parts, axis=1)


def _split_bf16(x):
    hi = x.astype(BF16)
    return hi, (x - hi.astype(F32)).astype(BF16)


def _outproj_kernel(x_ref, o1_ref, o2_ref, o3_ref, s1_ref, s2_ref, s3_ref, ret_ref, wout_ref, expand_ref,
                    gffn_ref, wr_ref, br_ref, tri_ref,
                    x1_ref, h2t_ref, ri_ref, gates_ref, cnt_ref, base_scr):
    i = pl.program_id(0)

    @pl.when(i == 0)
    def _():
        base_scr[...] = jnp.zeros_like(base_scr)

    sts = [s1_ref[...], s2_ref[...], s3_ref[...]]
    mx = jnp.maximum(jnp.maximum(sts[0], sts[1]), sts[2])
    es = [jnp.exp(st - mx) for st in sts]
    den = es[0] + es[1] + es[2]
    attn = None
    for e, o_ref in zip(es, (o1_ref, o2_ref, o3_ref)):
        hi, lo = _split_bf16(e / den)
        w_full = jnp.dot(jnp.concatenate([hi, lo], axis=1), expand_ref[...], preferred_element_type=F32)
        term = w_full * o_ref[...].astype(F32)
        attn = term if attn is None else attn + term
    mixed = jnp.concatenate([attn.astype(BF16), ret_ref[...]], axis=1)
    x1 = x_ref[...] + jnp.dot(mixed, wout_ref[...], preferred_element_type=F32)
    x1_ref[...] = x1

    h2 = x1 * lax.rsqrt(jnp.mean(x1 * x1, axis=-1, keepdims=True) + NORM_EPS) * gffn_ref[...]
    _store_row_tiles(h2t_ref, h2)

    hi, lo = _split_bf16(h2)
    logits = jnp.dot(jnp.concatenate([hi, lo, hi], axis=1), wr_ref[...], preferred_element_type=F32) + br_ref[...]
    tm = logits.shape[0]
    lane = lax.broadcasted_iota(jnp.int32, (tm, LANES), 1).astype(F32)
    work = logits
    vals, idxs = [], []
    onehot = jnp.zeros((tm, LANES), F32)
    for _k in range(TOP_K):
        mk = jnp.max(work, axis=-1, keepdims=True)
        ik = jnp.min(jnp.where(work == mk, lane, float(LANES)), axis=-1, keepdims=True)
        sel = lane == ik
        onehot = jnp.where(sel, 1.0, onehot)
        work = jnp.where(sel, -jnp.inf, work)
        vals.append(mk)
        idxs.append(ik)
    ex = [jnp.exp(vk - vals[0]) for vk in vals]
    tot = ex[0] + ex[1] + ex[2] + ex[3]
    before = jnp.dot(tri_ref[...], onehot.astype(BF16), preferred_element_type=F32) + base_scr[...]
    ri = jnp.zeros((tm, LANES), F32)
    gt = jnp.zeros((tm, LANES), F32)
    for kk in range(TOP_K):
        rank = jnp.sum(jnp.where(lane == idxs[kk], before, 0.0), axis=-1, keepdims=True)
        ri = jnp.where(lane == float(kk), idxs[kk], ri)
        ri = jnp.where(lane == float(TOP_K + kk), rank, ri)
        gt = jnp.where(lane == float(kk), ex[kk] / tot, gt)
    ri_ref[...] = ri.astype(jnp.int32)
    gates_ref[...] = gt
    new_base = base_scr[...] + jnp.sum(onehot, axis=0, keepdims=True)
    base_scr[...] = new_base
    cnt_ref[...] = new_base


def _outproj_router(x2d, outs, stats, ret, wout_bf16, norm_ffn_g, w_router, b_router):
    t = x2d.shape[0]
    tm = TM_OUT
    row = lambda i: (i, 0)
    const = lambda i: (0, 0)
    head_of_col = jnp.arange(WIDTH) // HEAD_DIM
    expand = (jnp.arange(LANES)[:, None] == head_of_col[None, :]).astype(BF16)
    expand2 = jnp.concatenate([expand, expand], axis=0)
    wr = jnp.zeros((D_MODEL, LANES), F32).at[:, :N_EXPERTS].set(w_router)
    wr_hi, wr_lo = _split_bf16(wr)
    wr3 = jnp.concatenate([wr_hi, wr_hi, wr_lo], axis=0)
    br = jnp.full((1, LANES), NEG_INF, F32).at[0, :N_EXPERTS].set(b_router)
    tri = (jnp.arange(tm)[:, None] > jnp.arange(tm)[None, :]).astype(BF16)
    o_spec = pl.BlockSpec((tm, WIDTH), row)
    s_spec = pl.BlockSpec((tm, LANES), row)
    return pl.pallas_call(
        _outproj_kernel,
        grid=(t // tm,),
        in_specs=[pl.BlockSpec((tm, D_MODEL), row), o_spec, o_spec, o_spec, s_spec, s_spec, s_spec, o_spec,
                  pl.BlockSpec((D_MODEL, D_MODEL), const), pl.BlockSpec((2 * LANES, WIDTH), const),
                  pl.BlockSpec((1, D_MODEL), const), pl.BlockSpec((3 * D_MODEL, LANES), const),
                  pl.BlockSpec((1, LANES), const), pl.BlockSpec((tm, tm), const)],
        out_specs=[pl.BlockSpec((tm, D_MODEL), row), pl.BlockSpec((tm * ROW_TILE, LANES), row),
                   s_spec, s_spec, pl.BlockSpec((1, LANES), const)],
        out_shape=[jax.ShapeDtypeStruct((t, D_MODEL), F32), jax.ShapeDtypeStruct((t * ROW_TILE, LANES), F32),
                   jax.ShapeDtypeStruct((t, LANES), jnp.int32), jax.ShapeDtypeStruct((t, LANES), F32),
                   jax.ShapeDtypeStruct((1, LANES), F32)],
        scratch_shapes=[pltpu.VMEM((1, LANES), F32)],
        compiler_params=_cparams("arbitrary"),
        name="outproj_router",
    )(x2d, *outs, *stats, ret, wout_bf16, expand2, norm_ffn_g.reshape(1, D_MODEL), wr3, br, tri)


def _tile_rows(r):
    return pl.ds(pl.multiple_of(r * ROW_TILE, ROW_TILE), ROW_TILE)


def _dispatch_kernel(pstart_ref, cnt_ref, pend_ref, dest_ref, h2t_ref, xs_hbm, zrow, sem, zsem, *, tg):
    i = pl.program_id(0)

    def zero_copy(dst):
        return pltpu.make_async_copy(zrow, xs_hbm.at[_tile_rows(dst)], zsem)

    @pl.when(i == 0)
    def _():
        zrow[...] = jnp.zeros_like(zrow)
        for e in range(N_EXPERTS):
            lo = pstart_ref[e] + cnt_ref[e]
            hi = pend_ref[e]

            def start(r, carry):
                zero_copy(r).start()
                return carry

            def wait(r, carry):
                zero_copy(r).wait()
                return carry

            lax.fori_loop(lo, hi, start, 0)
            lax.fori_loop(lo, hi, wait, 0)

    def row_copy(src, dst):
        return pltpu.make_async_copy(h2t_ref.at[_tile_rows(src)], xs_hbm.at[_tile_rows(dst)], sem)

    def start(t, carry):
        for kk in range(TOP_K):
            row_copy(t, dest_ref[0, 0, t * TOP_K + kk]).start(priority=kk % 2)
        return carry

    def wait(t, carry):
        for kk in range(TOP_K):
            row_copy(0, 0).wait()
        return carry

    lax.fori_loop(0, tg, start, 0, unroll=DMA_UNROLL)
    lax.fori_loop(0, tg, wait, 0, unroll=DMA_UNROLL)


def _dispatch(h2t, dest3, pstart, counts, pend, cap):
    t = h2t.shape[0] // ROW_TILE
    tg = TG
    return pl.pallas_call(
        functools.partial(_dispatch_kernel, tg=tg),
        grid_spec=pltpu.PrefetchScalarGridSpec(
            num_scalar_prefetch=3,
            grid=(t // tg,),
            in_specs=[pl.BlockSpec((1, 1, tg * TOP_K), lambda i, *_: (i, 0, 0), memory_space=pltpu.SMEM),
                      pl.BlockSpec((tg * ROW_TILE, LANES), lambda i, *_: (i, 0))],
            out_specs=pl.BlockSpec(memory_space=pl.ANY),
            scratch_shapes=[pltpu.VMEM((ROW_TILE, LANES), F32), pltpu.SemaphoreType.DMA(()),
                            pltpu.SemaphoreType.DMA(())],
        ),
        out_shape=jax.ShapeDtypeStruct((cap * ROW_TILE, LANES), F32),
        compiler_params=_cparams("arbitrary"),
        name="moe_dispatch",
    )(pstart, counts, pend, dest3, h2t)


def _expert_kernel(blk_e_ref, nvalid_ref, xs_ref, wgu_ref, bgu_ref, wd_ref, bd_ref, ys_ref):
    i = pl.program_id(0)

    @pl.when(i < nvalid_ref[0])
    def _():
        x = _load_row_tiles(xs_ref, BM, BF16)
        gu = jnp.dot(x, wgu_ref[0], preferred_element_type=F32) + bgu_ref[0]
        gate = jnp.minimum(gu[:, :EXPERT_FF], SWIGLU_LIMIT)
        up = jnp.clip(gu[:, EXPERT_FF:], -SWIGLU_LIMIT, SWIGLU_LIMIT)
        act = gate * (1.0 / (1.0 + jnp.exp(-SWIGLU_ALPHA * gate))) * (up + 1.0)
        y = jnp.dot(act.astype(BF16), wd_ref[0], preferred_element_type=F32) + bd_ref[0]
        _store_row_tiles(ys_ref, y)


def _experts(xs, blk_e, nvalid, wgu_bf16, bgu, wd_bf16, bd):
    cap = xs.shape[0] // ROW_TILE
    nblk = cap // BM

    def blk(i, be, nv):
        return (jnp.minimum(i, nv[0] - 1), 0)

    def by_expert(i, be, nv):
        return (be[i], 0, 0)

    return pl.pallas_call(
        _expert_kernel,
        grid_spec=pltpu.PrefetchScalarGridSpec(
            num_scalar_prefetch=2,
            grid=(nblk,),
            in_specs=[pl.BlockSpec((BM * ROW_TILE, LANES), blk),
                      pl.BlockSpec((1, D_MODEL, 2 * EXPERT_FF), by_expert),
                      pl.BlockSpec((1, 1, 2 * EXPERT_FF), by_expert),
                      pl.BlockSpec((1, EXPERT_FF, D_MODEL), by_expert),
                      pl.BlockSpec((1, 1, D_MODEL), by_expert)],
            out_specs=pl.BlockSpec((BM * ROW_TILE, LANES), blk),
        ),
        out_shape=jax.ShapeDtypeStruct((cap * ROW_TILE, LANES), F32),
        compiler_params=_cparams("arbitrary"),
        name="moe_experts",
    )(blk_e, nvalid, xs, wgu_bf16, bgu.reshape(N_EXPERTS, 1, 2 * EXPERT_FF), wd_bf16, bd.reshape(N_EXPERTS, 1, D_MODEL))


def _combine_kernel(dest_ref, dest_next_ref, x1_ref, gates_ref, gfin_ref, ys_hbm, o_ref, buf, sem, *, tc):
    i = pl.program_id(0)
    n = pl.num_programs(0)

    def row_copy(src, slot, kk, t):
        return pltpu.make_async_copy(ys_hbm.at[_tile_rows(src)], buf.at[slot, kk, _tile_rows(t)], sem.at[slot])

    def issue(idx_ref, slot):
        def body(t, carry):
            for kk in range(TOP_K):
                row_copy(idx_ref[0, 0, t * TOP_K + kk], slot, kk, t).start(priority=kk % 2)
            return carry

        lax.fori_loop(0, tc, body, 0, unroll=DMA_UNROLL)

    @pl.when(i == 0)
    def _():
        issue(dest_ref, 0)

    @pl.when(i + 1 < n)
    def _():
        issue(dest_next_ref, (i + 1) % 2)

    slot = i % 2

    def wait(t, carry):
        for kk in range(TOP_K):
            row_copy(0, slot, kk, 0).wait()
        return carry

    lax.fori_loop(0, tc, wait, 0, unroll=DMA_UNROLL)
    g = gates_ref[...]
    gk = [jnp.broadcast_to(g[:, kk:kk + 1], (tc, LANES)) for kk in range(TOP_K)]
    zs = []
    ssq = jnp.zeros((tc, 1), F32)
    for s in range(ROW_TILE):
        z = x1_ref[:, s * LANES:(s + 1) * LANES]
        for kk in range(TOP_K):
            z = z + gk[kk] * buf[slot, kk, pl.ds(s, tc, stride=ROW_TILE), :]
        zs.append(z)
        ssq = ssq + jnp.sum(z * z, axis=-1, keepdims=True)
    inv = lax.rsqrt(ssq * (1.0 / D_MODEL) + NORM_EPS)
    for s in range(ROW_TILE):
        cols = slice(s * LANES, (s + 1) * LANES)
        o_ref[:, cols] = zs[s] * inv * gfin_ref[:, cols]


def _combine(x1, gates, dest3, ys, norm_final_g):
    t = x1.shape[0]
    tc = TC
    n = t // tc
    row = lambda i: (i, 0)
    return pl.pallas_call(
        functools.partial(_combine_kernel, tc=tc),
        grid=(n,),
        in_specs=[pl.BlockSpec((1, 1, tc * TOP_K), lambda i: (i, 0, 0), memory_space=pltpu.SMEM),
                  pl.BlockSpec((1, 1, tc * TOP_K), lambda i: (jnp.minimum(i + 1, n - 1), 0, 0),
                               memory_space=pltpu.SMEM),
                  pl.BlockSpec((tc, D_MODEL), row), pl.BlockSpec((tc, LANES), row),
                  pl.BlockSpec((1, D_MODEL), lambda i: (0, 0)),
                  pl.BlockSpec(memory_space=pl.ANY)],
        out_specs=pl.BlockSpec((tc, D_MODEL), row),
        scratch_shapes=[pltpu.VMEM((2, TOP_K, tc * ROW_TILE, LANES), F32), pltpu.SemaphoreType.DMA((2,))],
        out_shape=jax.ShapeDtypeStruct((t, D_MODEL), F32),
        compiler_params=_cparams("arbitrary"),
        name="moe_combine",
    )(dest3, dest3, x1, gates, norm_final_g.reshape(1, D_MODEL), ys)


def _moe(x1, h2t, ri, gates, counts_f, wgu_bf16, bgu, wd_bf16, bd, norm_final_g):
    t = x1.shape[0]
    a = t * TOP_K
    cap = a + N_EXPERTS * BM
    nblk = cap // BM
    counts = counts_f[0, :N_EXPERTS].astype(jnp.int32)
    padded = ((counts + BM - 1) // BM) * BM
    pend = jnp.cumsum(padded)
    pstart = pend - padded
    nvalid = (pend[-1] // BM).reshape(1)
    first_row = jnp.minimum(jnp.arange(nblk, dtype=jnp.int32) * BM, pend[-1] - 1)
    blk_e = jnp.sum(pend[None, :] <= first_row[:, None], axis=1).astype(jnp.int32)
    idx, rank = ri[:, :TOP_K], ri[:, TOP_K:2 * TOP_K]
    onehot = idx[:, :, None] == jnp.arange(N_EXPERTS, dtype=jnp.int32)[None, None, :]
    dest = rank + jnp.sum(jnp.where(onehot, pstart[None, None, :], 0), axis=-1)
    xs = _dispatch(h2t, dest.reshape(t // TG, 1, TG * TOP_K), pstart, counts, pend, cap)
    ys = _experts(xs, blk_e, nvalid, wgu_bf16, bgu, wd_bf16, bd)
    return _combine(x1, gates, dest.reshape(t // TC, 1, TC * TOP_K), ys, norm_final_g)


def _encoder(x, p):
    b, s, _ = x.shape
    x2d = x.reshape(b * s, D_MODEL)
    cos_t, sin_t = _rope_tables(s)
    qa, ka, va, qr, kr, vr, gr = _inproj(x2d, p["norm_mix_g"], p["w_in"], cos_t, sin_t, s)
    outs, stats = [], []
    for d in DILATIONS:
        o, st = _banded_attention(*[_to_residues(z, b, s, d) for z in (qa, ka, va)])
        outs.append(_from_residues(o, b, s, d))
        stats.append(_from_residues(st, b, s, d))
    ret = _retention(qr, kr, vr, gr, p["ret_norm_g"], b, s)
    x1, h2t, ri, gates, counts = _outproj_router(x2d, outs, stats, ret, p["w_out"], p["norm_ffn_g"],
                                                 p["w_router"], p["b_router"])
    y = _moe(x1, h2t, ri, gates, counts, p["w_gate_up"], p["b_gate_up"], p["w_down"], p["b_down"], p["norm_final_g"])
    return y.reshape(b, s, D_MODEL)


def kernel(x_prompt, x_sample, norm_mix_g, w_in, ret_norm_g, w_out, norm_ffn_g, w_router, b_router, w_gate_up, b_gate_up, w_down, b_down, norm_final_g):
    assert norm_mix_g.shape[0] == 1, "single layer"
    p = dict(norm_mix_g=norm_mix_g[0], w_in=w_in[0].astype(BF16), ret_norm_g=ret_norm_g[0],
             w_out=w_out[0].astype(BF16), norm_ffn_g=norm_ffn_g[0], w_router=w_router[0], b_router=b_router[0],
             w_gate_up=w_gate_up[0].astype(BF16), b_gate_up=b_gate_up[0], w_down=w_down[0].astype(BF16),
             b_down=b_down[0], norm_final_g=norm_final_g)
    return (_encoder(x_prompt, p), _encoder(x_sample, p))
```

```python
import functools

import jax
import jax.numpy as jnp
from jax import lax
from jax.experimental import pallas as pl
from jax.experimental.pallas import tpu as pltpu

D_MODEL = 1024
HEAD_DIM = 64
N_HEADS = 8
WIDTH = N_HEADS * HEAD_DIM
N_SLABS = 7
DILATIONS = (1, 4, 16)
HALF_SPAN = 64
ROPE_THETA = 10000.0
RET_DECAY_BASE_FWD = 5.0
RET_DECAY_BASE_BWD = 5.5
N_EXPERTS = 32
TOP_K = 4
EXPERT_FF = D_MODEL
SWIGLU_LIMIT = 7.0
SWIGLU_ALPHA = 1.702
NORM_EPS = 1e-6
NEG_INF = -1e30

LANES = 128
VMEM_LIMIT_BYTES = 48 * 1024 * 1024

TM_INPROJ = 512
TQ = 128
TL_ATTN = 512
HEAD_GROUP = 4
RET_CHUNK = 256
TM_OUT = 512
TG = 512
BM = 512
TC = 256
DMA_UNROLL = 4

F32 = jnp.float32
BF16 = jnp.bfloat16


def _cparams(*sem):
    return pltpu.CompilerParams(dimension_semantics=sem, vmem_limit_bytes=VMEM_LIMIT_BYTES)


_ROTATE = (True, True, False, True, True, False, False)
_SCALE = (HEAD_DIM ** -0.5, 1.0, 1.0, 1.0, HEAD_DIM ** -0.5, 1.0, 1.0)


N_ATTN_SLABS = 3
RESIDUE_DILATIONS = tuple(d for d in DILATIONS if d > 1)


def _inproj_kernel(x_ref, g_ref, w_ref, cos_ref, sin_ref, *refs):
    out_refs = refs[:N_SLABS]
    res_refs = refs[N_SLABS:N_SLABS + N_ATTN_SLABS * len(RESIDUE_DILATIONS)]
    stage = refs[-1]
    x = x_ref[...]
    tm = x.shape[0]
    h = (x * lax.rsqrt(jnp.mean(x * x, axis=-1, keepdims=True) + NORM_EPS) * g_ref[...]).astype(BF16)
    cos = cos_ref[...]
    sin = sin_ref[...]
    lane = lax.broadcasted_iota(jnp.int32, cos.shape, 1)
    first_half = (lane & (HEAD_DIM - 1)) < HEAD_DIM // 2
    for j, o_ref in enumerate(out_refs):
        p = jnp.dot(h, w_ref[:, j * WIDTH:(j + 1) * WIDTH], preferred_element_type=F32)
        for c in range(WIDTH // LANES):
            cols = slice(c * LANES, (c + 1) * LANES)
            r = p[:, cols]
            if _ROTATE[j]:
                partner = jnp.where(first_half, pltpu.roll(r, LANES - HEAD_DIM // 2, 1), pltpu.roll(r, HEAD_DIM // 2, 1))
                r = r * cos + partner * sin
                if _SCALE[j] != 1.0:
                    r = r * _SCALE[j]
            o_ref[:, cols] = r.astype(BF16)
            if j < N_ATTN_SLABS:
                stage[j, c] = r
                for di, d in enumerate(RESIDUE_DILATIONS):
                    dst = res_refs[di * N_ATTN_SLABS + j]
                    for res in range(d):
                        dst[0, res, :, cols] = stage[j, c, pl.ds(res, tm // d, stride=d), :].astype(BF16)


def _inproj(x2d, g, w_bf16, cos_t, sin_t, seq):
    t = x2d.shape[0]
    tm = min(TM_INPROJ, seq)
    pos_blocks = seq // tm
    b = t // seq
    out = jax.ShapeDtypeStruct((t, WIDTH), BF16)
    row = lambda i: (i, 0)
    out_specs = [pl.BlockSpec((tm, WIDTH), row)] * N_SLABS
    out_shape = [out] * N_SLABS
    for d in RESIDUE_DILATIONS:
        out_specs += [pl.BlockSpec((1, d, tm // d, WIDTH), lambda i: (i // pos_blocks, 0, i % pos_blocks, 0))] * N_ATTN_SLABS
        out_shape += [jax.ShapeDtypeStruct((b, d, seq // d, WIDTH), BF16)] * N_ATTN_SLABS
    res = pl.pallas_call(
        _inproj_kernel,
        grid=(t // tm,),
        in_specs=[
            pl.BlockSpec((tm, D_MODEL), row),
            pl.BlockSpec((1, D_MODEL), lambda i: (0, 0)),
            pl.BlockSpec((D_MODEL, N_SLABS * WIDTH), lambda i: (0, 0)),
            pl.BlockSpec((tm, LANES), lambda i: (i % pos_blocks, 0)),
            pl.BlockSpec((tm, LANES), lambda i: (i % pos_blocks, 0)),
        ],
        out_specs=out_specs,
        out_shape=out_shape,
        scratch_shapes=[pltpu.VMEM((N_ATTN_SLABS, WIDTH // LANES, tm, LANES), F32)],
        compiler_params=_cparams("arbitrary"),
        name="inproj",
    )(x2d, g.reshape(1, D_MODEL), w_bf16, cos_t, sin_t)
    natural = res[:N_SLABS]
    by_residue = {d: res[N_SLABS + di * N_ATTN_SLABS:N_SLABS + (di + 1) * N_ATTN_SLABS]
                  for di, d in enumerate(RESIDUE_DILATIONS)}
    return natural, by_residue


def _rope_tables(seq):
    half = HEAD_DIM // 2
    inv_freq = ROPE_THETA ** (-jnp.arange(0, HEAD_DIM, 2, dtype=F32) / HEAD_DIM)
    ang = jnp.arange(seq, dtype=F32)[:, None] * inv_freq[None, :]
    cos, sin = jnp.cos(ang), jnp.sin(ang)
    reps = LANES // HEAD_DIM
    cos_t = jnp.tile(jnp.concatenate([cos, cos], axis=1), (1, reps))
    sin_t = jnp.tile(jnp.concatenate([-sin, sin], axis=1), (1, reps))
    assert cos_t.shape == (seq, LANES) and half * 2 == HEAD_DIM
    return cos_t, sin_t


def _attn_kernel(q_ref, kp_ref, kc_ref, kn_ref, vp_ref, vc_ref, vn_ref, o_ref, st_ref, kbuf, vbuf, *, gb, tl, sub_len):
    i = pl.program_id(1)
    hs = HALF_SPAN
    for gi in range(gb):
        kbuf[gi, 0:hs] = kp_ref[gi]
        kbuf[gi, hs:hs + tl] = kc_ref[gi]
        kbuf[gi, hs + tl:hs + tl + hs] = kn_ref[gi]
        vbuf[gi, 0:hs] = vp_ref[gi]
        vbuf[gi, hs:hs + tl] = vc_ref[gi]
        vbuf[gi, hs + tl:hs + tl + hs] = vn_ref[gi]
    tk = TQ + 2 * hs
    gw = HEAD_GROUP * HEAD_DIM
    qi = lax.broadcasted_iota(jnp.int32, (TQ, tk), 0)
    kj = lax.broadcasted_iota(jnp.int32, (TQ, tk), 1)
    band = (kj >= qi) & (kj - qi <= 2 * hs)
    lane = lax.broadcasted_iota(jnp.int32, (TQ, LANES), 1)
    head_of_lane = lax.broadcasted_iota(jnp.int32, (TQ, gw), 1) // HEAD_DIM
    for gi, sub in [(gi, sub) for gi in range(gb) for sub in range(tl // TQ)]:
        a = sub * TQ
        kpos = kj + (i * tl + a - hs)
        bias = jnp.where(band & (kpos >= 0) & (kpos < sub_len), 0.0, NEG_INF).astype(F32)
        bias = jnp.concatenate([bias] * HEAD_GROUP, axis=0)
        st = jnp.zeros((TQ, LANES), F32)
        for g in range(N_HEADS // HEAD_GROUP):
            cols = slice(g * gw, (g + 1) * gw)
            q4 = q_ref[gi, a:a + TQ, cols]
            k4 = kbuf[gi, a:a + tk, cols]
            v4 = vbuf[gi, a:a + tk, cols]
            lhs = jnp.concatenate([jnp.where(head_of_lane == h, q4, jnp.zeros_like(q4)) for h in range(HEAD_GROUP)], axis=0)
            s = lax.dot_general(lhs, k4, (((1,), (1,)), ((), ())), preferred_element_type=F32) + bias
            m = jnp.max(s, axis=-1, keepdims=True)
            p = jnp.exp(s - m)
            l = jnp.sum(p, axis=-1, keepdims=True)
            o_all = jnp.dot(p.astype(BF16), v4, preferred_element_type=F32) / l
            lse = m + jnp.log(l)
            o = o_all[0:TQ]
            for h in range(HEAD_GROUP):
                rows = slice(h * TQ, (h + 1) * TQ)
                if h:
                    o = jnp.where(head_of_lane == h, o_all[rows], o)
                st = jnp.where(lane == g * HEAD_GROUP + h, lse[rows], st)
            o_ref[gi, a:a + TQ, cols] = o.astype(BF16)
        st_ref[gi, a:a + TQ, :] = st


def _banded_attention(q, k, v):
    g, sub_len, _ = q.shape
    tl = min(TL_ATTN, sub_len)
    gb = min(TL_ATTN // tl, g)
    hs = HALF_SPAN
    per = tl // hs
    last = sub_len // hs - 1
    cur = pl.BlockSpec((gb, tl, WIDTH), lambda b, i: (b, i, 0))
    prev = pl.BlockSpec((gb, hs, WIDTH), lambda b, i: (b, jnp.maximum(i * per - 1, 0), 0))
    nxt = pl.BlockSpec((gb, hs, WIDTH), lambda b, i: (b, jnp.minimum((i + 1) * per, last), 0))
    return pl.pallas_call(
        functools.partial(_attn_kernel, gb=gb, tl=tl, sub_len=sub_len),
        grid=(g // gb, sub_len // tl),
        in_specs=[cur, prev, cur, nxt, prev, cur, nxt],
        out_specs=[cur, pl.BlockSpec((gb, tl, LANES), lambda b, i: (b, i, 0))],
        out_shape=[jax.ShapeDtypeStruct((g, sub_len, WIDTH), BF16),
                   jax.ShapeDtypeStruct((g, sub_len, LANES), F32)],
        scratch_shapes=[pltpu.VMEM((gb, tl + 2 * hs, WIDTH), BF16), pltpu.VMEM((gb, tl + 2 * hs, WIDTH), BF16)],
        compiler_params=_cparams("arbitrary", "arbitrary"),
        name="banded_attention",
    )(q, k, k, k, v, v, v)


def _to_residues(z, b, s, d):
    w = z.shape[-1]
    if d == 1:
        return z.reshape(b, s, w)
    return z.reshape(b, s // d, d, w).transpose(0, 2, 1, 3).reshape(b * d, s // d, w)


def _from_residues(z, b, s, d):
    w = z.shape[-1]
    if d == 1:
        return z.reshape(b * s, w)
    return z.reshape(b, d, s // d, w).transpose(0, 2, 1, 3).reshape(b * s, w)


def _ret_kernel(q_ref, k_ref, v_ref, gate_ref, dmat_ref, wq_ref, wk_ref, dec_ref, gn_ref, o_ref, sf_scr, sb_scr, *, c, n_chunks):
    pair = 2 * HEAD_DIM
    lane = lax.broadcasted_iota(jnp.int32, (c, pair), 1)
    head0 = lane < HEAD_DIM
    blk_r = lax.broadcasted_iota(jnp.int32, (pair, pair), 0) // HEAD_DIM
    blk_c = lax.broadcasted_iota(jnp.int32, (pair, pair), 1) // HEAD_DIM
    same_head = blk_r == blk_c
    dec_f = dec_ref[0:1, :]
    dec_b = dec_ref[1:2, :]
    tn = (((0,), (0,)), ((), ()))
    nt = (((1,), (1,)), ((), ()))

    def rows_of(n):
        return pl.ds(pl.multiple_of(n * c, c), c)

    def kv_body(n, carry):
        rows = rows_of(n)
        kf32 = k_ref[0, rows, :].astype(F32)
        kw = jnp.concatenate([(kf32 * wk_ref[:, :pair]).astype(BF16), (kf32 * wk_ref[:, pair:]).astype(BF16)], axis=1)
        kv = lax.dot_general(kw, v_ref[0, rows, :], tn, preferred_element_type=F32)
        sf_scr[n] = jnp.where(same_head, kv[:pair], 0.0)
        sb_scr[n] = jnp.where(same_head, kv[pair:], 0.0)
        return carry

    lax.fori_loop(0, n_chunks, kv_body, 0, unroll=2)

    def scan_body(t, carry):
        sf, sb = carry
        nb = n_chunks - 1 - t
        kv_f = sf_scr[t]
        kv_b = sb_scr[nb]
        sf_scr[t] = sf
        sb_scr[nb] = sb
        return sf * dec_f + kv_f, sb * dec_b + kv_b

    zero = jnp.zeros((pair, pair), F32)
    lax.fori_loop(0, n_chunks, scan_body, (zero, zero))

    def out_body(n, carry):
        rows = rows_of(n)
        q = q_ref[0, rows, :]
        k = k_ref[0, rows, :]
        v = v_ref[0, rows, :]
        qf32 = q.astype(F32)
        intra = []
        for hh in range(2):
            mask = head0 if hh == 0 else jnp.logical_not(head0)
            kh = jnp.where(mask, k, jnp.zeros_like(k))
            s = lax.dot_general(q, kh, nt, preferred_element_type=F32)
            a = (s * dmat_ref[hh]).astype(BF16)
            intra.append(jnp.dot(a, v, preferred_element_type=F32))
        qw = jnp.concatenate([(qf32 * wq_ref[:, :pair]).astype(BF16), (qf32 * wq_ref[:, pair:]).astype(BF16)], axis=1)
        states = jnp.concatenate([sf_scr[n].astype(BF16), sb_scr[n].astype(BF16)], axis=0)
        tot = jnp.where(head0, intra[0], intra[1]) + jnp.dot(qw, states, preferred_element_type=F32)
        inv = 1.0 / HEAD_DIM
        s0 = jnp.sum(jnp.where(head0, tot, 0.0), axis=-1, keepdims=True)
        s1 = jnp.sum(jnp.where(head0, 0.0, tot), axis=-1, keepdims=True)
        xc = tot - jnp.where(head0, s0, s1) * inv
        sq = xc * xc
        v0 = jnp.sum(jnp.where(head0, sq, 0.0), axis=-1, keepdims=True)
        v1 = jnp.sum(jnp.where(head0, 0.0, sq), axis=-1, keepdims=True)
        y = xc * lax.rsqrt(jnp.where(head0, v0, v1) * inv + NORM_EPS)
        gt = gate_ref[0, rows, :].astype(F32)
        y = y * gn_ref[...] * (gt / (1.0 + jnp.exp(-gt)))
        o_ref[0, rows, :] = y.astype(BF16)
        return carry

    lax.fori_loop(0, n_chunks, out_body, 0, unroll=2)


def _retention_tables(c):
    hidx = jnp.arange(N_HEADS, dtype=F32)
    lg_f = jnp.log1p(-jnp.exp2(-(RET_DECAY_BASE_FWD + hidx)))
    lg_b = jnp.log1p(-jnp.exp2(-(RET_DECAY_BASE_BWD + hidx)))
    pos = jnp.arange(c, dtype=F32)
    diff = pos[:, None] - pos[None, :]
    dm_f = jnp.exp(jnp.maximum(diff, 0.0)[None] * lg_f[:, None, None])
    dm_b = jnp.exp(jnp.maximum(-diff, 0.0)[None] * lg_b[:, None, None])
    dmat = jnp.where((diff >= 0)[None], dm_f, dm_b)

    def per_lane(tab):
        t = jnp.repeat(tab[:, :, None], HEAD_DIM, axis=2)
        return t.reshape(N_HEADS // 2, 2, c, HEAD_DIM).transpose(0, 2, 1, 3).reshape(N_HEADS // 2, c, 2 * HEAD_DIM)

    wq_f = per_lane(jnp.exp((pos + 1.0)[None, :] * lg_f[:, None]))
    wq_b = per_lane(jnp.exp((c - pos)[None, :] * lg_b[:, None]))
    wk_f = per_lane(jnp.exp((c - 1.0 - pos)[None, :] * lg_f[:, None]))
    wk_b = per_lane(jnp.exp(pos[None, :] * lg_b[:, None]))
    wq = jnp.concatenate([wq_f, wq_b], axis=2)
    wk = jnp.concatenate([wk_f, wk_b], axis=2)
    dec = jnp.stack([jnp.repeat(jnp.exp(c * lg_f), HEAD_DIM), jnp.repeat(jnp.exp(c * lg_b), HEAD_DIM)], axis=0)
    dec = dec.reshape(2, N_HEADS // 2, 2 * HEAD_DIM).transpose(1, 0, 2)
    return dmat, wq, wk, dec


def _retention(q, k, v, gate, ret_norm_g, b, s):
    c = min(RET_CHUNK, s)
    n_chunks = s // c
    pair = 2 * HEAD_DIM
    dmat, wq, wk, dec = _retention_tables(c)
    seq_blk = pl.BlockSpec((1, s, pair), lambda bi, hp: (bi, 0, hp))
    r3 = lambda z: z.reshape(b, s, WIDTH)
    return pl.pallas_call(
        functools.partial(_ret_kernel, c=c, n_chunks=n_chunks),
        grid=(b, N_HEADS // 2),
        in_specs=[seq_blk, seq_blk, seq_blk, seq_blk,
                  pl.BlockSpec((2, c, c), lambda bi, hp: (hp, 0, 0)),
                  pl.BlockSpec((None, c, 2 * pair), lambda bi, hp: (hp, 0, 0)),
                  pl.BlockSpec((None, c, 2 * pair), lambda bi, hp: (hp, 0, 0)),
                  pl.BlockSpec((None, 2, pair), lambda bi, hp: (hp, 0, 0)),
                  pl.BlockSpec((1, pair), lambda bi, hp: (0, hp))],
        out_specs=seq_blk,
        out_shape=jax.ShapeDtypeStruct((b, s, WIDTH), BF16),
        scratch_shapes=[pltpu.VMEM((n_chunks, pair, pair), F32), pltpu.VMEM((n_chunks, pair, pair), F32)],
        compiler_params=_cparams("arbitrary", "arbitrary"),
        name="retention",
    )(r3(q), r3(k), r3(v), r3(gate), dmat, wq, wk, dec, ret_norm_g.reshape(1, WIDTH)).reshape(b * s, WIDTH)


ROW_TILE = 8
assert ROW_TILE * LANES == D_MODEL


def _store_row_tiles(ref, val):
    n = val.shape[0]
    for s in range(ROW_TILE):
        ref[pl.ds(s, n, stride=ROW_TILE), :] = val[:, s * LANES:(s + 1) * LANES]


def _load_row_tiles(ref, n, dtype=None):
    parts = [ref[pl.ds(s, n, stride=ROW_TILE), :] for s in range(ROW_TILE)]
    if dtype is not None:
        parts = [p.astype(dtype) for p in parts]
    return jnp.concatenate(parts, axis=1)


def _split_bf16(x):
    hi = x.astype(BF16)
    return hi, (x - hi.astype(F32)).astype(BF16)


def _outproj_kernel(x_ref, o1_ref, o2_ref, o3_ref, s1_ref, s2_ref, s3_ref, ret_ref, wout_ref, expand_ref,
                    gffn_ref, wr_ref, br_ref, tri_ref,
                    x1_ref, h2t_ref, ri_ref, gates_ref, cnt_ref, base_scr, nat_o, nat_s):
    i = pl.program_id(0)
    tm = x_ref.shape[0]

    @pl.when(i == 0)
    def _():
        base_scr[...] = jnp.zeros_like(base_scr)

    o_nat = [[o1_ref[:, c * LANES:(c + 1) * LANES].astype(F32) for c in range(WIDTH // LANES)]]
    sts = [s1_ref[...]]
    for di, (d, o_ref, s_ref) in enumerate(zip(RESIDUE_DILATIONS, (o2_ref, o3_ref), (s2_ref, s3_ref))):
        for res in range(d):
            rows = pl.ds(res, tm // d, stride=d)
            for c in range(WIDTH // LANES):
                nat_o[di, c, rows, :] = o_ref[0, res, :, c * LANES:(c + 1) * LANES].astype(F32)
            nat_s[di, rows, :] = s_ref[0, res]
        o_nat.append([nat_o[di, c] for c in range(WIDTH // LANES)])
        sts.append(nat_s[di])

    mx = jnp.maximum(jnp.maximum(sts[0], sts[1]), sts[2])
    es = [jnp.exp(st - mx) for st in sts]
    den = es[0] + es[1] + es[2]
    attn = None
    for e, o_cols in zip(es, o_nat):
        hi, lo = _split_bf16(e / den)
        w_full = jnp.dot(jnp.concatenate([hi, lo], axis=1), expand_ref[...], preferred_element_type=F32)
        term = w_full * jnp.concatenate(o_cols, axis=1)
        attn = term if attn is None else attn + term
    mixed = jnp.concatenate([attn.astype(BF16), ret_ref[...]], axis=1)
    x1 = x_ref[...] + jnp.dot(mixed, wout_ref[...], preferred_element_type=F32)
    x1_ref[...] = x1

    h2 = x1 * lax.rsqrt(jnp.mean(x1 * x1, axis=-1, keepdims=True) + NORM_EPS) * gffn_ref[...]
    _store_row_tiles(h2t_ref, h2)

    hi, lo = _split_bf16(h2)
    logits = jnp.dot(jnp.concatenate([hi, lo, hi], axis=1), wr_ref[...], preferred_element_type=F32) + br_ref[...]
    tm = logits.shape[0]
    lane = lax.broadcasted_iota(jnp.int32, (tm, LANES), 1).astype(F32)
    work = logits
    vals, idxs = [], []
    onehot = jnp.zeros((tm, LANES), F32)
    for _k in range(TOP_K):
        mk = jnp.max(work, axis=-1, keepdims=True)
        ik = jnp.min(jnp.where(work == mk, lane, float(LANES)), axis=-1, keepdims=True)
        sel = lane == ik
        onehot = jnp.where(sel, 1.0, onehot)
        work = jnp.where(sel, -jnp.inf, work)
        vals.append(mk)
        idxs.append(ik)
    ex = [jnp.exp(vk - vals[0]) for vk in vals]
    tot = ex[0] + ex[1] + ex[2] + ex[3]
    before = jnp.dot(tri_ref[...], onehot.astype(BF16), preferred_element_type=F32) + base_scr[...]
    ri = jnp.zeros((tm, LANES), F32)
    gt = jnp.zeros((tm, LANES), F32)
    for kk in range(TOP_K):
        rank = jnp.sum(jnp.where(lane == idxs[kk], before, 0.0), axis=-1, keepdims=True)
        ri = jnp.where(lane == float(kk), idxs[kk], ri)
        ri = jnp.where(lane == float(TOP_K + kk), rank, ri)
        gt = jnp.where(lane == float(kk), ex[kk] / tot, gt)
    ri_ref[...] = ri.astype(jnp.int32)
    gates_ref[...] = gt
    new_base = base_scr[...] + jnp.sum(onehot, axis=0, keepdims=True)
    base_scr[...] = new_base
    cnt_ref[...] = new_base


def _outproj_router(x2d, outs, stats, ret, wout_bf16, norm_ffn_g, w_router, b_router, seq):
    t = x2d.shape[0]
    tm = TM_OUT
    per_seq = seq // tm
    row = lambda i: (i, 0)

    def res_spec(d, width):
        return pl.BlockSpec((1, d, tm // d, width), lambda i: (i // per_seq, 0, i % per_seq, 0))

    const = lambda i: (0, 0)
    head_of_col = jnp.arange(WIDTH) // HEAD_DIM
    expand = (jnp.arange(LANES)[:, None] == head_of_col[None, :]).astype(BF16)
    expand2 = jnp.concatenate([expand, expand], axis=0)
    wr = jnp.zeros((D_MODEL, LANES), F32).at[:, :N_EXPERTS].set(w_router)
    wr_hi, wr_lo = _split_bf16(wr)
    wr3 = jnp.concatenate([wr_hi, wr_hi, wr_lo], axis=0)
    br = jnp.full((1, LANES), NEG_INF, F32).at[0, :N_EXPERTS].set(b_router)
    tri = (jnp.arange(tm)[:, None] > jnp.arange(tm)[None, :]).astype(BF16)
    o_spec = pl.BlockSpec((tm, WIDTH), row)
    s_spec = pl.BlockSpec((tm, LANES), row)
    return pl.pallas_call(
        _outproj_kernel,
        grid=(t // tm,),
        in_specs=[pl.BlockSpec((tm, D_MODEL), row),
                  o_spec, *[res_spec(d, WIDTH) for d in RESIDUE_DILATIONS],
                  s_spec, *[res_spec(d, LANES) for d in RESIDUE_DILATIONS], o_spec,
                  pl.BlockSpec((D_MODEL, D_MODEL), const), pl.BlockSpec((2 * LANES, WIDTH), const),
                  pl.BlockSpec((1, D_MODEL), const), pl.BlockSpec((3 * D_MODEL, LANES), const),
                  pl.BlockSpec((1, LANES), const), pl.BlockSpec((tm, tm), const)],
        out_specs=[pl.BlockSpec((tm, D_MODEL), row), pl.BlockSpec((tm * ROW_TILE, LANES), row),
                   s_spec, s_spec, pl.BlockSpec((1, LANES), const)],
        out_shape=[jax.ShapeDtypeStruct((t, D_MODEL), F32), jax.ShapeDtypeStruct((t * ROW_TILE, LANES), F32),
                   jax.ShapeDtypeStruct((t, LANES), jnp.int32), jax.ShapeDtypeStruct((t, LANES), F32),
                   jax.ShapeDtypeStruct((1, LANES), F32)],
        scratch_shapes=[pltpu.VMEM((1, LANES), F32),
                        pltpu.VMEM((len(RESIDUE_DILATIONS), WIDTH // LANES, tm, LANES), F32),
                        pltpu.VMEM((len(RESIDUE_DILATIONS), tm, LANES), F32)],
        compiler_params=_cparams("arbitrary"),
        name="outproj_router",
    )(x2d, *outs, *stats, ret, wout_bf16, expand2, norm_ffn_g.reshape(1, D_MODEL), wr3, br, tri)


def _tile_rows(r):
    return pl.ds(pl.multiple_of(r * ROW_TILE, ROW_TILE), ROW_TILE)


def _dispatch_kernel(pstart_ref, cnt_ref, pend_ref, dest_ref, h2t_ref, xs_hbm, zrow, sem, zsem, *, tg):
    i = pl.program_id(0)

    def zero_copy(dst):
        return pltpu.make_async_copy(zrow, xs_hbm.at[_tile_rows(dst)], zsem)

    @pl.when(i == 0)
    def _():
        zrow[...] = jnp.zeros_like(zrow)
        for e in range(N_EXPERTS):
            lo = pstart_ref[e] + cnt_ref[e]
            hi = pend_ref[e]

            def start(r, carry):
                zero_copy(r).start()
                return carry

            def wait(r, carry):
                zero_copy(r).wait()
                return carry

            lax.fori_loop(lo, hi, start, 0)
            lax.fori_loop(lo, hi, wait, 0)

    def row_copy(src, dst):
        return pltpu.make_async_copy(h2t_ref.at[_tile_rows(src)], xs_hbm.at[_tile_rows(dst)], sem)

    def start(t, carry):
        for kk in range(TOP_K):
            row_copy(t, dest_ref[0, 0, t * TOP_K + kk]).start(priority=kk % 2)
        return carry

    def wait(t, carry):
        for kk in range(TOP_K):
            row_copy(0, 0).wait()
        return carry

    lax.fori_loop(0, tg, start, 0, unroll=DMA_UNROLL)
    lax.fori_loop(0, tg, wait, 0, unroll=DMA_UNROLL)


def _dispatch(h2t, dest3, pstart, counts, pend, cap):
    t = h2t.shape[0] // ROW_TILE
    tg = TG
    return pl.pallas_call(
        functools.partial(_dispatch_kernel, tg=tg),
        grid_spec=pltpu.PrefetchScalarGridSpec(
            num_scalar_prefetch=3,
            grid=(t // tg,),
            in_specs=[pl.BlockSpec((1, 1, tg * TOP_K), lambda i, *_: (i, 0, 0), memory_space=pltpu.SMEM),
                      pl.BlockSpec((tg * ROW_TILE, LANES), lambda i, *_: (i, 0))],
            out_specs=pl.BlockSpec(memory_space=pl.ANY),
            scratch_shapes=[pltpu.VMEM((ROW_TILE, LANES), F32), pltpu.SemaphoreType.DMA(()),
                            pltpu.SemaphoreType.DMA(())],
        ),
        out_shape=jax.ShapeDtypeStruct((cap * ROW_TILE, LANES), F32),
        compiler_params=_cparams("arbitrary"),
        name="moe_dispatch",
    )(pstart, counts, pend, dest3, h2t)


def _expert_kernel(blk_e_ref, nvalid_ref, xs_ref, wgu_ref, bgu_ref, wd_ref, bd_ref, ys_ref):
    i = pl.program_id(0)

    @pl.when(i < nvalid_ref[0])
    def _():
        x = _load_row_tiles(xs_ref, BM, BF16)
        gu = jnp.dot(x, wgu_ref[0], preferred_element_type=F32) + bgu_ref[0]
        gate = jnp.minimum(gu[:, :EXPERT_FF], SWIGLU_LIMIT)
        up = jnp.clip(gu[:, EXPERT_FF:], -SWIGLU_LIMIT, SWIGLU_LIMIT)
        act = gate * (1.0 / (1.0 + jnp.exp(-SWIGLU_ALPHA * gate))) * (up + 1.0)
        y = jnp.dot(act.astype(BF16), wd_ref[0], preferred_element_type=F32) + bd_ref[0]
        _store_row_tiles(ys_ref, y)


def _experts(xs, blk_e, nvalid, wgu_bf16, bgu, wd_bf16, bd):
    cap = xs.shape[0] // ROW_TILE
    nblk = cap // BM

    def blk(i, be, nv):
        return (jnp.minimum(i, nv[0] - 1), 0)

    def by_expert(i, be, nv):
        return (be[i], 0, 0)

    return pl.pallas_call(
        _expert_kernel,
        grid_spec=pltpu.PrefetchScalarGridSpec(
            num_scalar_prefetch=2,
            grid=(nblk,),
            in_specs=[pl.BlockSpec((BM * ROW_TILE, LANES), blk),
                      pl.BlockSpec((1, D_MODEL, 2 * EXPERT_FF), by_expert),
                      pl.BlockSpec((1, 1, 2 * EXPERT_FF), by_expert),
                      pl.BlockSpec((1, EXPERT_FF, D_MODEL), by_expert),
                      pl.BlockSpec((1, 1, D_MODEL), by_expert)],
            out_specs=pl.BlockSpec((BM * ROW_TILE, LANES), blk),
        ),
        out_shape=jax.ShapeDtypeStruct((cap * ROW_TILE, LANES), F32),
        compiler_params=_cparams("arbitrary"),
        name="moe_experts",
    )(blk_e, nvalid, xs, wgu_bf16, bgu.reshape(N_EXPERTS, 1, 2 * EXPERT_FF), wd_bf16, bd.reshape(N_EXPERTS, 1, D_MODEL))


def _combine_kernel(dest_ref, dest_next_ref, x1_ref, gates_ref, gfin_ref, ys_hbm, o_ref, buf, sem, *, tc):
    i = pl.program_id(0)
    n = pl.num_programs(0)

    def row_copy(src, slot, kk, t):
        return pltpu.make_async_copy(ys_hbm.at[_tile_rows(src)], buf.at[slot, kk, _tile_rows(t)], sem.at[slot])

    def issue(idx_ref, slot):
        def body(t, carry):
            for kk in range(TOP_K):
                row_copy(idx_ref[0, 0, t * TOP_K + kk], slot, kk, t).start(priority=kk % 2)
            return carry

        lax.fori_loop(0, tc, body, 0, unroll=DMA_UNROLL)

    @pl.when(i == 0)
    def _():
        issue(dest_ref, 0)

    @pl.when(i + 1 < n)
    def _():
        issue(dest_next_ref, (i + 1) % 2)

    slot = i % 2

    def wait(t, carry):
        for kk in range(TOP_K):
            row_copy(0, slot, kk, 0).wait()
        return carry

    lax.fori_loop(0, tc, wait, 0, unroll=DMA_UNROLL)
    g = gates_ref[...]
    gk = [jnp.broadcast_to(g[:, kk:kk + 1], (tc, LANES)) for kk in range(TOP_K)]
    zs = []
    ssq = jnp.zeros((tc, 1), F32)
    for s in range(ROW_TILE):
        z = x1_ref[:, s * LANES:(s + 1) * LANES]
        for kk in range(TOP_K):
            z = z + gk[kk] * buf[slot, kk, pl.ds(s, tc, stride=ROW_TILE), :]
        zs.append(z)
        ssq = ssq + jnp.sum(z * z, axis=-1, keepdims=True)
    inv = lax.rsqrt(ssq * (1.0 / D_MODEL) + NORM_EPS)
    for s in range(ROW_TILE):
        cols = slice(s * LANES, (s + 1) * LANES)
        o_ref[:, cols] = zs[s] * inv * gfin_ref[:, cols]


def _combine(x1, gates, dest3, ys, norm_final_g):
    t = x1.shape[0]
    tc = TC
    n = t // tc
    row = lambda i: (i, 0)
    return pl.pallas_call(
        functools.partial(_combine_kernel, tc=tc),
        grid=(n,),
        in_specs=[pl.BlockSpec((1, 1, tc * TOP_K), lambda i: (i, 0, 0), memory_space=pltpu.SMEM),
                  pl.BlockSpec((1, 1, tc * TOP_K), lambda i: (jnp.minimum(i + 1, n - 1), 0, 0),
                               memory_space=pltpu.SMEM),
                  pl.BlockSpec((tc, D_MODEL), row), pl.BlockSpec((tc, LANES), row),
                  pl.BlockSpec((1, D_MODEL), lambda i: (0, 0)),
                  pl.BlockSpec(memory_space=pl.ANY)],
        out_specs=pl.BlockSpec((tc, D_MODEL), row),
        scratch_shapes=[pltpu.VMEM((2, TOP_K, tc * ROW_TILE, LANES), F32), pltpu.SemaphoreType.DMA((2,))],
        out_shape=jax.ShapeDtypeStruct((t, D_MODEL), F32),
        compiler_params=_cparams("arbitrary"),
        name="moe_combine",
    )(dest3, dest3, x1, gates, norm_final_g.reshape(1, D_MODEL), ys)


def _moe(x1, h2t, ri, gates, counts_f, wgu_bf16, bgu, wd_bf16, bd, norm_final_g):
    t = x1.shape[0]
    a = t * TOP_K
    cap = a + N_EXPERTS * BM
    nblk = cap // BM
    counts = counts_f[0, :N_EXPERTS].astype(jnp.int32)
    padded = ((counts + BM - 1) // BM) * BM
    pend = jnp.cumsum(padded)
    pstart = pend - padded
    nvalid = (pend[-1] // BM).reshape(1)
    first_row = jnp.minimum(jnp.arange(nblk, dtype=jnp.int32) * BM, pend[-1] - 1)
    blk_e = jnp.sum(pend[None, :] <= first_row[:, None], axis=1).astype(jnp.int32)
    idx, rank = ri[:, :TOP_K], ri[:, TOP_K:2 * TOP_K]
    onehot = idx[:, :, None] == jnp.arange(N_EXPERTS, dtype=jnp.int32)[None, None, :]
    dest = rank + jnp.sum(jnp.where(onehot, pstart[None, None, :], 0), axis=-1)
    xs = _dispatch(h2t, dest.reshape(t // TG, 1, TG * TOP_K), pstart, counts, pend, cap)
    ys = _experts(xs, blk_e, nvalid, wgu_bf16, bgu, wd_bf16, bd)
    return _combine(x1, gates, dest.reshape(t // TC, 1, TC * TOP_K), ys, norm_final_g)


def _encoder(x, p):
    b, s, _ = x.shape
    x2d = x.reshape(b * s, D_MODEL)
    cos_t, sin_t = _rope_tables(s)
    (qa, ka, va, qr, kr, vr, gr), by_residue = _inproj(x2d, p["norm_mix_g"], p["w_in"], cos_t, sin_t, s)
    outs, stats = [], []
    for d in DILATIONS:
        if d == 1:
            o, st = _banded_attention(*[z.reshape(b, s, WIDTH) for z in (qa, ka, va)])
            outs.append(o.reshape(b * s, WIDTH))
            stats.append(st.reshape(b * s, LANES))
        else:
            o, st = _banded_attention(*[z.reshape(b * d, s // d, WIDTH) for z in by_residue[d]])
            outs.append(o.reshape(b, d, s // d, WIDTH))
            stats.append(st.reshape(b, d, s // d, LANES))
    ret = _retention(qr, kr, vr, gr, p["ret_norm_g"], b, s)
    x1, h2t, ri, gates, counts = _outproj_router(x2d, outs, stats, ret, p["w_out"], p["norm_ffn_g"],
                                                 p["w_router"], p["b_router"], s)
    y = _moe(x1, h2t, ri, gates, counts, p["w_gate_up"], p["b_gate_up"], p["w_down"], p["b_down"], p["norm_final_g"])
    return y.reshape(b, s, D_MODEL)


def kernel(x_prompt, x_sample, norm_mix_g, w_in, ret_norm_g, w_out, norm_ffn_g, w_router, b_router, w_gate_up, b_gate_up, w_down, b_down, norm_final_g):
    assert norm_mix_g.shape[0] == 1, "single layer"
    p = dict(norm_mix_g=norm_mix_g[0], w_in=w_in[0].astype(BF16), ret_norm_g=ret_norm_g[0],
             w_out=w_out[0].astype(BF16), norm_ffn_g=norm_ffn_g[0], w_router=w_router[0], b_router=b_router[0],
             w_gate_up=w_gate_up[0].astype(BF16), b_gate_up=b_gate_up[0], w_down=w_down[0].astype(BF16),
             b_down=b_down[0], norm_final_g=norm_final_g)
    return (_encoder(x_prompt, p), _encoder(x_sample, p))
```

```python
import functools

import jax
import jax.numpy as jnp
from jax import lax
from jax.experimental import pallas as pl
from jax.experimental.pallas import tpu as pltpu

D_MODEL = 1024
HEAD_DIM = 64
N_HEADS = 8
WIDTH = N_HEADS * HEAD_DIM
N_SLABS = 7
DILATIONS = (1, 4, 16)
HALF_SPAN = 64
ROPE_THETA = 10000.0
RET_DECAY_BASE_FWD = 5.0
RET_DECAY_BASE_BWD = 5.5
N_EXPERTS = 32
TOP_K = 4
EXPERT_FF = D_MODEL
SWIGLU_LIMIT = 7.0
SWIGLU_ALPHA = 1.702
NORM_EPS = 1e-6
NEG_INF = -1e30

LANES = 128
VMEM_LIMIT_BYTES = 48 * 1024 * 1024

TM_INPROJ = 512
TQ = 128
TL_ATTN = 512
HEAD_GROUP = 4
RET_CHUNK = 256
TM_OUT = 512
TG = 512
BM = 512
TC = 512
DMA_UNROLL = 4

F32 = jnp.float32
BF16 = jnp.bfloat16


def _cparams(*sem):
    return pltpu.CompilerParams(dimension_semantics=sem, vmem_limit_bytes=VMEM_LIMIT_BYTES)


_ROTATE = (True, True, False, True, True, False, False)
_SCALE = (HEAD_DIM ** -0.5, 1.0, 1.0, 1.0, HEAD_DIM ** -0.5, 1.0, 1.0)


N_ATTN_SLABS = 3
RESIDUE_DILATIONS = tuple(d for d in DILATIONS if d > 1)
REGROUP = 4
assert RESIDUE_DILATIONS == (REGROUP, REGROUP ** 2)


def _inproj_kernel(x_ref, g_ref, w_ref, cos_ref, sin_ref, *refs):
    out_refs = refs[:N_SLABS]
    res_refs = refs[N_SLABS:N_SLABS + N_ATTN_SLABS * len(RESIDUE_DILATIONS)]
    stage, stage4 = refs[-2:]
    x = x_ref[...]
    tm = x.shape[0]
    h = (x * lax.rsqrt(jnp.mean(x * x, axis=-1, keepdims=True) + NORM_EPS) * g_ref[...]).astype(BF16)
    cos = cos_ref[...]
    sin = sin_ref[...]
    lane = lax.broadcasted_iota(jnp.int32, cos.shape, 1)
    first_half = (lane & (HEAD_DIM - 1)) < HEAD_DIM // 2
    for j, o_ref in enumerate(out_refs):
        p = jnp.dot(h, w_ref[:, j * WIDTH:(j + 1) * WIDTH], preferred_element_type=F32)
        for c in range(WIDTH // LANES):
            cols = slice(c * LANES, (c + 1) * LANES)
            r = p[:, cols]
            if _ROTATE[j]:
                partner = jnp.where(first_half, pltpu.roll(r, LANES - HEAD_DIM // 2, 1), pltpu.roll(r, HEAD_DIM // 2, 1))
                r = r * cos + partner * sin
                if _SCALE[j] != 1.0:
                    r = r * _SCALE[j]
            o_ref[:, cols] = r.astype(BF16)
            if j < N_ATTN_SLABS:
                stage[j, c] = r
                dst4 = res_refs[j]
                dst16 = res_refs[N_ATTN_SLABS + j]
                for bb in range(REGROUP):
                    grp = stage[j, c, pl.ds(bb, tm // REGROUP, stride=REGROUP), :]
                    dst4[0, bb, :, cols] = grp.astype(BF16)
                    stage4[j, c, bb] = grp
                    for aa in range(REGROUP):
                        sub = stage4[j, c, bb, pl.ds(aa, tm // REGROUP ** 2, stride=REGROUP), :]
                        dst16[0, bb * REGROUP + aa, :, cols] = sub.astype(BF16)


def _inproj(x2d, g, w_bf16, cos_t, sin_t, seq):
    t = x2d.shape[0]
    tm = min(TM_INPROJ, seq)
    pos_blocks = seq // tm
    b = t // seq
    out = jax.ShapeDtypeStruct((t, WIDTH), BF16)
    row = lambda i: (i, 0)
    out_specs = [pl.BlockSpec((tm, WIDTH), row)] * N_SLABS
    out_shape = [out] * N_SLABS
    for d in RESIDUE_DILATIONS:
        out_specs += [pl.BlockSpec((1, d, tm // d, WIDTH), lambda i: (i // pos_blocks, 0, i % pos_blocks, 0))] * N_ATTN_SLABS
        out_shape += [jax.ShapeDtypeStruct((b, d, seq // d, WIDTH), BF16)] * N_ATTN_SLABS
    res = pl.pallas_call(
        _inproj_kernel,
        grid=(t // tm,),
        in_specs=[
            pl.BlockSpec((tm, D_MODEL), row),
            pl.BlockSpec((1, D_MODEL), lambda i: (0, 0)),
            pl.BlockSpec((D_MODEL, N_SLABS * WIDTH), lambda i: (0, 0)),
            pl.BlockSpec((tm, LANES), lambda i: (i % pos_blocks, 0)),
            pl.BlockSpec((tm, LANES), lambda i: (i % pos_blocks, 0)),
        ],
        out_specs=out_specs,
        out_shape=out_shape,
        scratch_shapes=[pltpu.VMEM((N_ATTN_SLABS, WIDTH // LANES, tm, LANES), F32),
                        pltpu.VMEM((N_ATTN_SLABS, WIDTH // LANES, REGROUP, tm // REGROUP, LANES), F32)],
        compiler_params=_cparams("arbitrary"),
        name="inproj",
    )(x2d, g.reshape(1, D_MODEL), w_bf16, cos_t, sin_t)
    natural = res[:N_SLABS]
    by_residue = {d: res[N_SLABS + di * N_ATTN_SLABS:N_SLABS + (di + 1) * N_ATTN_SLABS]
                  for di, d in enumerate(RESIDUE_DILATIONS)}
    return natural, by_residue


def _rope_tables(seq):
    half = HEAD_DIM // 2
    inv_freq = ROPE_THETA ** (-jnp.arange(0, HEAD_DIM, 2, dtype=F32) / HEAD_DIM)
    ang = jnp.arange(seq, dtype=F32)[:, None] * inv_freq[None, :]
    cos, sin = jnp.cos(ang), jnp.sin(ang)
    reps = LANES // HEAD_DIM
    cos_t = jnp.tile(jnp.concatenate([cos, cos], axis=1), (1, reps))
    sin_t = jnp.tile(jnp.concatenate([-sin, sin], axis=1), (1, reps))
    assert cos_t.shape == (seq, LANES) and half * 2 == HEAD_DIM
    return cos_t, sin_t


def _attn_kernel(q_ref, kp_ref, kc_ref, kn_ref, vp_ref, vc_ref, vn_ref, o_ref, st_ref, kbuf, vbuf, *, gb, tl, sub_len):
    i = pl.program_id(1)
    hs = HALF_SPAN
    for gi in range(gb):
        kbuf[gi, 0:hs] = kp_ref[gi]
        kbuf[gi, hs:hs + tl] = kc_ref[gi]
        kbuf[gi, hs + tl:hs + tl + hs] = kn_ref[gi]
        vbuf[gi, 0:hs] = vp_ref[gi]
        vbuf[gi, hs:hs + tl] = vc_ref[gi]
        vbuf[gi, hs + tl:hs + tl + hs] = vn_ref[gi]
    tk = TQ + 2 * hs
    gw = HEAD_GROUP * HEAD_DIM
    qi = lax.broadcasted_iota(jnp.int32, (TQ, tk), 0)
    kj = lax.broadcasted_iota(jnp.int32, (TQ, tk), 1)
    band = (kj >= qi) & (kj - qi <= 2 * hs)
    lane = lax.broadcasted_iota(jnp.int32, (TQ, LANES), 1)
    head_of_lane = lax.broadcasted_iota(jnp.int32, (TQ, gw), 1) // HEAD_DIM
    for gi, sub in [(gi, sub) for gi in range(gb) for sub in range(tl // TQ)]:
        a = sub * TQ
        kpos = kj + (i * tl + a - hs)
        bias = jnp.where(band & (kpos >= 0) & (kpos < sub_len), 0.0, NEG_INF).astype(F32)
        bias = jnp.concatenate([bias] * HEAD_GROUP, axis=0)
        st = jnp.zeros((TQ, LANES), F32)
        for g in range(N_HEADS // HEAD_GROUP):
            cols = slice(g * gw, (g + 1) * gw)
            q4 = q_ref[gi, a:a + TQ, cols]
            k4 = kbuf[gi, a:a + tk, cols]
            v4 = vbuf[gi, a:a + tk, cols]
            lhs = jnp.concatenate([jnp.where(head_of_lane == h, q4, jnp.zeros_like(q4)) for h in range(HEAD_GROUP)], axis=0)
            s = lax.dot_general(lhs, k4, (((1,), (1,)), ((), ())), preferred_element_type=F32) + bias
            m = jnp.max(s, axis=-1, keepdims=True)
            p = jnp.exp(s - m)
            l = jnp.sum(p, axis=-1, keepdims=True)
            o_all = jnp.dot(p.astype(BF16), v4, preferred_element_type=F32) / l
            lse = m + jnp.log(l)
            o = o_all[0:TQ]
            for h in range(HEAD_GROUP):
                rows = slice(h * TQ, (h + 1) * TQ)
                if h:
                    o = jnp.where(head_of_lane == h, o_all[rows], o)
                st = jnp.where(lane == g * HEAD_GROUP + h, lse[rows], st)
            o_ref[gi, a:a + TQ, cols] = o.astype(BF16)
        st_ref[gi, a:a + TQ, :] = st


def _banded_attention(q, k, v):
    g, sub_len, _ = q.shape
    tl = min(TL_ATTN, sub_len)
    gb = min(TL_ATTN // tl, g)
    hs = HALF_SPAN
    per = tl // hs
    last = sub_len // hs - 1
    cur = pl.BlockSpec((gb, tl, WIDTH), lambda b, i: (b, i, 0))
    prev = pl.BlockSpec((gb, hs, WIDTH), lambda b, i: (b, jnp.maximum(i * per - 1, 0), 0))
    nxt = pl.BlockSpec((gb, hs, WIDTH), lambda b, i: (b, jnp.minimum((i + 1) * per, last), 0))
    return pl.pallas_call(
        functools.partial(_attn_kernel, gb=gb, tl=tl, sub_len=sub_len),
        grid=(g // gb, sub_len // tl),
        in_specs=[cur, prev, cur, nxt, prev, cur, nxt],
        out_specs=[cur, pl.BlockSpec((gb, tl, LANES), lambda b, i: (b, i, 0))],
        out_shape=[jax.ShapeDtypeStruct((g, sub_len, WIDTH), BF16),
                   jax.ShapeDtypeStruct((g, sub_len, LANES), F32)],
        scratch_shapes=[pltpu.VMEM((gb, tl + 2 * hs, WIDTH), BF16), pltpu.VMEM((gb, tl + 2 * hs, WIDTH), BF16)],
        compiler_params=_cparams("arbitrary", "arbitrary"),
        name="banded_attention",
    )(q, k, k, k, v, v, v)


def _to_residues(z, b, s, d):
    w = z.shape[-1]
    if d == 1:
        return z.reshape(b, s, w)
    return z.reshape(b, s // d, d, w).transpose(0, 2, 1, 3).reshape(b * d, s // d, w)


def _from_residues(z, b, s, d):
    w = z.shape[-1]
    if d == 1:
        return z.reshape(b * s, w)
    return z.reshape(b, d, s // d, w).transpose(0, 2, 1, 3).reshape(b * s, w)


def _ret_kernel(q_ref, k_ref, v_ref, gate_ref, dmat_ref, wq_ref, wk_ref, dec_ref, gn_ref, o_ref, sf_scr, sb_scr, *, c, n_chunks):
    pair = 2 * HEAD_DIM
    lane = lax.broadcasted_iota(jnp.int32, (c, pair), 1)
    head0 = lane < HEAD_DIM
    blk_r = lax.broadcasted_iota(jnp.int32, (pair, pair), 0) // HEAD_DIM
    blk_c = lax.broadcasted_iota(jnp.int32, (pair, pair), 1) // HEAD_DIM
    same_head = blk_r == blk_c
    dec_f = dec_ref[0:1, :]
    dec_b = dec_ref[1:2, :]
    tn = (((0,), (0,)), ((), ()))
    nt = (((1,), (1,)), ((), ()))

    def rows_of(n):
        return pl.ds(pl.multiple_of(n * c, c), c)

    def kv_body(n, carry):
        rows = rows_of(n)
        kf32 = k_ref[0, rows, :].astype(F32)
        kw = jnp.concatenate([(kf32 * wk_ref[:, :pair]).astype(BF16), (kf32 * wk_ref[:, pair:]).astype(BF16)], axis=1)
        kv = lax.dot_general(kw, v_ref[0, rows, :], tn, preferred_element_type=F32)
        sf_scr[n] = jnp.where(same_head, kv[:pair], 0.0)
        sb_scr[n] = jnp.where(same_head, kv[pair:], 0.0)
        return carry

    lax.fori_loop(0, n_chunks, kv_body, 0, unroll=2)

    def scan_body(t, carry):
        sf, sb = carry
        nb = n_chunks - 1 - t
        kv_f = sf_scr[t]
        kv_b = sb_scr[nb]
        sf_scr[t] = sf
        sb_scr[nb] = sb
        return sf * dec_f + kv_f, sb * dec_b + kv_b

    zero = jnp.zeros((pair, pair), F32)
    lax.fori_loop(0, n_chunks, scan_body, (zero, zero))

    def out_body(n, carry):
        rows = rows_of(n)
        q = q_ref[0, rows, :]
        k = k_ref[0, rows, :]
        v = v_ref[0, rows, :]
        qf32 = q.astype(F32)
        intra = []
        for hh in range(2):
            mask = head0 if hh == 0 else jnp.logical_not(head0)
            kh = jnp.where(mask, k, jnp.zeros_like(k))
            s = lax.dot_general(q, kh, nt, preferred_element_type=F32)
            a = (s * dmat_ref[hh]).astype(BF16)
            intra.append(jnp.dot(a, v, preferred_element_type=F32))
        qw = jnp.concatenate([(qf32 * wq_ref[:, :pair]).astype(BF16), (qf32 * wq_ref[:, pair:]).astype(BF16)], axis=1)
        states = jnp.concatenate([sf_scr[n].astype(BF16), sb_scr[n].astype(BF16)], axis=0)
        tot = jnp.where(head0, intra[0], intra[1]) + jnp.dot(qw, states, preferred_element_type=F32)
        inv = 1.0 / HEAD_DIM
        s0 = jnp.sum(jnp.where(head0, tot, 0.0), axis=-1, keepdims=True)
        s1 = jnp.sum(jnp.where(head0, 0.0, tot), axis=-1, keepdims=True)
        xc = tot - jnp.where(head0, s0, s1) * inv
        sq = xc * xc
        v0 = jnp.sum(jnp.where(head0, sq, 0.0), axis=-1, keepdims=True)
        v1 = jnp.sum(jnp.where(head0, 0.0, sq), axis=-1, keepdims=True)
        y = xc * lax.rsqrt(jnp.where(head0, v0, v1) * inv + NORM_EPS)
        gt = gate_ref[0, rows, :].astype(F32)
        y = y * gn_ref[...] * (gt / (1.0 + jnp.exp(-gt)))
        o_ref[0, rows, :] = y.astype(BF16)
        return carry

    lax.fori_loop(0, n_chunks, out_body, 0, unroll=2)


def _retention_tables(c):
    hidx = jnp.arange(N_HEADS, dtype=F32)
    lg_f = jnp.log1p(-jnp.exp2(-(RET_DECAY_BASE_FWD + hidx)))
    lg_b = jnp.log1p(-jnp.exp2(-(RET_DECAY_BASE_BWD + hidx)))
    pos = jnp.arange(c, dtype=F32)
    diff = pos[:, None] - pos[None, :]
    dm_f = jnp.exp(jnp.maximum(diff, 0.0)[None] * lg_f[:, None, None])
    dm_b = jnp.exp(jnp.maximum(-diff, 0.0)[None] * lg_b[:, None, None])
    dmat = jnp.where((diff >= 0)[None], dm_f, dm_b)

    def per_lane(tab):
        t = jnp.repeat(tab[:, :, None], HEAD_DIM, axis=2)
        return t.reshape(N_HEADS // 2, 2, c, HEAD_DIM).transpose(0, 2, 1, 3).reshape(N_HEADS // 2, c, 2 * HEAD_DIM)

    wq_f = per_lane(jnp.exp((pos + 1.0)[None, :] * lg_f[:, None]))
    wq_b = per_lane(jnp.exp((c - pos)[None, :] * lg_b[:, None]))
    wk_f = per_lane(jnp.exp((c - 1.0 - pos)[None, :] * lg_f[:, None]))
    wk_b = per_lane(jnp.exp(pos[None, :] * lg_b[:, None]))
    wq = jnp.concatenate([wq_f, wq_b], axis=2)
    wk = jnp.concatenate([wk_f, wk_b], axis=2)
    dec = jnp.stack([jnp.repeat(jnp.exp(c * lg_f), HEAD_DIM), jnp.repeat(jnp.exp(c * lg_b), HEAD_DIM)], axis=0)
    dec = dec.reshape(2, N_HEADS // 2, 2 * HEAD_DIM).transpose(1, 0, 2)
    return dmat, wq, wk, dec


def _retention(q, k, v, gate, ret_norm_g, b, s):
    c = min(RET_CHUNK, s)
    n_chunks = s // c
    pair = 2 * HEAD_DIM
    dmat, wq, wk, dec = _retention_tables(c)
    seq_blk = pl.BlockSpec((1, s, pair), lambda bi, hp: (bi, 0, hp))
    r3 = lambda z: z.reshape(b, s, WIDTH)
    return pl.pallas_call(
        functools.partial(_ret_kernel, c=c, n_chunks=n_chunks),
        grid=(b, N_HEADS // 2),
        in_specs=[seq_blk, seq_blk, seq_blk, seq_blk,
                  pl.BlockSpec((2, c, c), lambda bi, hp: (hp, 0, 0)),
                  pl.BlockSpec((None, c, 2 * pair), lambda bi, hp: (hp, 0, 0)),
                  pl.BlockSpec((None, c, 2 * pair), lambda bi, hp: (hp, 0, 0)),
                  pl.BlockSpec((None, 2, pair), lambda bi, hp: (hp, 0, 0)),
                  pl.BlockSpec((1, pair), lambda bi, hp: (0, hp))],
        out_specs=seq_blk,
        out_shape=jax.ShapeDtypeStruct((b, s, WIDTH), BF16),
        scratch_shapes=[pltpu.VMEM((n_chunks, pair, pair), F32), pltpu.VMEM((n_chunks, pair, pair), F32)],
        compiler_params=_cparams("arbitrary", "arbitrary"),
        name="retention",
    )(r3(q), r3(k), r3(v), r3(gate), dmat, wq, wk, dec, ret_norm_g.reshape(1, WIDTH)).reshape(b * s, WIDTH)


ROW_TILE = 4
HALF_D = D_MODEL // 2
assert ROW_TILE * LANES == HALF_D


def _pack_bf16_pair(a, b):
    ua = pltpu.bitcast(a.astype(BF16).astype(F32), jnp.uint32)
    ub = pltpu.bitcast(b.astype(BF16).astype(F32), jnp.uint32)
    return ua | (ub >> 16)


def _unpack_bf16_pair(u):
    return pltpu.bitcast(u & jnp.uint32(0xFFFF0000), F32), pltpu.bitcast(u << 16, F32)


def _store_row_tiles(ref, val):
    n = val.shape[0]
    for s in range(ROW_TILE):
        lo = slice(s * LANES, (s + 1) * LANES)
        hi = slice(HALF_D + s * LANES, HALF_D + (s + 1) * LANES)
        ref[pl.ds(s, n, stride=ROW_TILE), :] = _pack_bf16_pair(val[:, lo], val[:, hi])


def _load_row_tiles(ref, n):
    first, second = [], []
    for s in range(ROW_TILE):
        a, b = _unpack_bf16_pair(ref[pl.ds(s, n, stride=ROW_TILE), :])
        first.append(a)
        second.append(b)
    return first, second


def _split_bf16(x):
    hi = x.astype(BF16)
    return hi, (x - hi.astype(F32)).astype(BF16)


def _outproj_kernel(x_ref, o1_ref, o2_ref, o3_ref, s1_ref, s2_ref, s3_ref, ret_ref, wout_ref, expand_ref,
                    gffn_ref, wr_ref, br_ref, tri_ref,
                    x1_ref, h2t_ref, ri_ref, gates_ref, cnt_ref, base_scr, nat_o, nat_s, grp_o):
    i = pl.program_id(0)
    tm = x_ref.shape[0]

    @pl.when(i == 0)
    def _():
        base_scr[...] = jnp.zeros_like(base_scr)

    o_nat = [[o1_ref[:, c * LANES:(c + 1) * LANES].astype(F32) for c in range(WIDTH // LANES)]]
    sts = [s1_ref[...]]
    n_col = WIDTH // LANES
    for bb in range(REGROUP):
        rows4 = pl.ds(bb, tm // REGROUP, stride=REGROUP)
        for c in range(n_col):
            nat_o[0, c, rows4, :] = o2_ref[0, bb, :, c * LANES:(c + 1) * LANES].astype(F32)
        nat_s[0, rows4, :] = s2_ref[0, bb]
        for aa in range(REGROUP):
            rows16 = pl.ds(aa, tm // REGROUP ** 2, stride=REGROUP)
            for c in range(n_col):
                grp_o[c, bb, rows16, :] = o3_ref[0, bb * REGROUP + aa, :, c * LANES:(c + 1) * LANES].astype(F32)
            grp_o[n_col, bb, rows16, :] = s3_ref[0, bb * REGROUP + aa]
        for c in range(n_col):
            nat_o[1, c, rows4, :] = grp_o[c, bb]
        nat_s[1, rows4, :] = grp_o[n_col, bb]
    for di in range(len(RESIDUE_DILATIONS)):
        o_nat.append([nat_o[di, c] for c in range(n_col)])
        sts.append(nat_s[di])

    mx = jnp.maximum(jnp.maximum(sts[0], sts[1]), sts[2])
    es = [jnp.exp(st - mx) for st in sts]
    den = es[0] + es[1] + es[2]
    attn = None
    for e, o_cols in zip(es, o_nat):
        hi, lo = _split_bf16(e / den)
        w_full = jnp.dot(jnp.concatenate([hi, lo], axis=1), expand_ref[...], preferred_element_type=F32)
        term = w_full * jnp.concatenate(o_cols, axis=1)
        attn = term if attn is None else attn + term
    mixed = jnp.concatenate([attn.astype(BF16), ret_ref[...]], axis=1)
    x1 = x_ref[...] + jnp.dot(mixed, wout_ref[...], preferred_element_type=F32)
    x1_ref[...] = x1

    h2 = x1 * lax.rsqrt(jnp.mean(x1 * x1, axis=-1, keepdims=True) + NORM_EPS) * gffn_ref[...]
    _store_row_tiles(h2t_ref, h2)

    hi, lo = _split_bf16(h2)
    logits = jnp.dot(jnp.concatenate([hi, lo, hi], axis=1), wr_ref[...], preferred_element_type=F32) + br_ref[...]
    tm = logits.shape[0]
    lane = lax.broadcasted_iota(jnp.int32, (tm, LANES), 1).astype(F32)
    work = logits
    vals, idxs = [], []
    onehot = jnp.zeros((tm, LANES), F32)
    for _k in range(TOP_K):
        mk = jnp.max(work, axis=-1, keepdims=True)
        ik = jnp.min(jnp.where(work == mk, lane, float(LANES)), axis=-1, keepdims=True)
        sel = lane == ik
        onehot = jnp.where(sel, 1.0, onehot)
        work = jnp.where(sel, -jnp.inf, work)
        vals.append(mk)
        idxs.append(ik)
    ex = [jnp.exp(vk - vals[0]) for vk in vals]
    tot = ex[0] + ex[1] + ex[2] + ex[3]
    before = jnp.dot(tri_ref[...], onehot.astype(BF16), preferred_element_type=F32) + base_scr[...]
    ri = jnp.zeros((tm, LANES), F32)
    gt = jnp.zeros((tm, LANES), F32)
    for kk in range(TOP_K):
        rank = jnp.sum(jnp.where(lane == idxs[kk], before, 0.0), axis=-1, keepdims=True)
        ri = jnp.where(lane == float(kk), idxs[kk], ri)
        ri = jnp.where(lane == float(TOP_K + kk), rank, ri)
        gt = jnp.where(lane == float(kk), ex[kk] / tot, gt)
    ri_ref[...] = ri.astype(jnp.int32)
    gates_ref[...] = gt
    new_base = base_scr[...] + jnp.sum(onehot, axis=0, keepdims=True)
    base_scr[...] = new_base
    cnt_ref[...] = new_base


def _outproj_router(x2d, outs, stats, ret, wout_bf16, norm_ffn_g, w_router, b_router, seq):
    t = x2d.shape[0]
    tm = TM_OUT
    per_seq = seq // tm
    row = lambda i: (i, 0)

    def res_spec(d, width):
        return pl.BlockSpec((1, d, tm // d, width), lambda i: (i // per_seq, 0, i % per_seq, 0))

    const = lambda i: (0, 0)
    head_of_col = jnp.arange(WIDTH) // HEAD_DIM
    expand = (jnp.arange(LANES)[:, None] == head_of_col[None, :]).astype(BF16)
    expand2 = jnp.concatenate([expand, expand], axis=0)
    wr = jnp.zeros((D_MODEL, LANES), F32).at[:, :N_EXPERTS].set(w_router)
    wr_hi, wr_lo = _split_bf16(wr)
    wr3 = jnp.concatenate([wr_hi, wr_hi, wr_lo], axis=0)
    br = jnp.full((1, LANES), NEG_INF, F32).at[0, :N_EXPERTS].set(b_router)
    tri = (jnp.arange(tm)[:, None] > jnp.arange(tm)[None, :]).astype(BF16)
    o_spec = pl.BlockSpec((tm, WIDTH), row)
    s_spec = pl.BlockSpec((tm, LANES), row)
    return pl.pallas_call(
        _outproj_kernel,
        grid=(t // tm,),
        in_specs=[pl.BlockSpec((tm, D_MODEL), row),
                  o_spec, *[res_spec(d, WIDTH) for d in RESIDUE_DILATIONS],
                  s_spec, *[res_spec(d, LANES) for d in RESIDUE_DILATIONS], o_spec,
                  pl.BlockSpec((D_MODEL, D_MODEL), const), pl.BlockSpec((2 * LANES, WIDTH), const),
                  pl.BlockSpec((1, D_MODEL), const), pl.BlockSpec((3 * D_MODEL, LANES), const),
                  pl.BlockSpec((1, LANES), const), pl.BlockSpec((tm, tm), const)],
        out_specs=[pl.BlockSpec((tm, D_MODEL), row), pl.BlockSpec((tm * ROW_TILE, LANES), row),
                   s_spec, s_spec, pl.BlockSpec((1, LANES), const)],
        out_shape=[jax.ShapeDtypeStruct((t, D_MODEL), F32), jax.ShapeDtypeStruct((t * ROW_TILE, LANES), jnp.uint32),
                   jax.ShapeDtypeStruct((t, LANES), jnp.int32), jax.ShapeDtypeStruct((t, LANES), F32),
                   jax.ShapeDtypeStruct((1, LANES), F32)],
        scratch_shapes=[pltpu.VMEM((1, LANES), F32),
                        pltpu.VMEM((len(RESIDUE_DILATIONS), WIDTH // LANES, tm, LANES), F32),
                        pltpu.VMEM((len(RESIDUE_DILATIONS), tm, LANES), F32),
                        pltpu.VMEM((WIDTH // LANES + 1, REGROUP, tm // REGROUP, LANES), F32)],
        compiler_params=_cparams("arbitrary"),
        name="outproj_router",
    )(x2d, *outs, *stats, ret, wout_bf16, expand2, norm_ffn_g.reshape(1, D_MODEL), wr3, br, tri)


def _tile_rows(r):
    return pl.ds(pl.multiple_of(r * ROW_TILE, ROW_TILE), ROW_TILE)


def _dispatch_kernel(pstart_ref, cnt_ref, pend_ref, dest_ref, h2t_ref, xs_hbm, zrow, sem, zsem, *, tg):
    i = pl.program_id(0)

    def zero_copy(dst):
        return pltpu.make_async_copy(zrow, xs_hbm.at[_tile_rows(dst)], zsem)

    @pl.when(i == 0)
    def _():
        zrow[...] = jnp.zeros_like(zrow)
        for e in range(N_EXPERTS):
            lo = pstart_ref[e] + cnt_ref[e]
            hi = pend_ref[e]

            def start(r, carry):
                zero_copy(r).start()
                return carry

            def wait(r, carry):
                zero_copy(r).wait()
                return carry

            lax.fori_loop(lo, hi, start, 0)
            lax.fori_loop(lo, hi, wait, 0)

    def row_copy(src, dst):
        return pltpu.make_async_copy(h2t_ref.at[_tile_rows(src)], xs_hbm.at[_tile_rows(dst)], sem)

    def start(t, carry):
        for kk in range(TOP_K):
            row_copy(t, dest_ref[0, 0, t * TOP_K + kk]).start(priority=kk % 2)
        return carry

    def wait(t, carry):
        for kk in range(TOP_K):
            row_copy(0, 0).wait()
        return carry

    lax.fori_loop(0, tg, start, 0, unroll=DMA_UNROLL)
    lax.fori_loop(0, tg, wait, 0, unroll=DMA_UNROLL)


def _dispatch(h2t, dest3, pstart, counts, pend, cap):
    t = h2t.shape[0] // ROW_TILE
    tg = TG
    return pl.pallas_call(
        functools.partial(_dispatch_kernel, tg=tg),
        grid_spec=pltpu.PrefetchScalarGridSpec(
            num_scalar_prefetch=3,
            grid=(t // tg,),
            in_specs=[pl.BlockSpec((1, 1, tg * TOP_K), lambda i, *_: (i, 0, 0), memory_space=pltpu.SMEM),
                      pl.BlockSpec((tg * ROW_TILE, LANES), lambda i, *_: (i, 0))],
            out_specs=pl.BlockSpec(memory_space=pl.ANY),
            scratch_shapes=[pltpu.VMEM((ROW_TILE, LANES), jnp.uint32), pltpu.SemaphoreType.DMA(()),
                            pltpu.SemaphoreType.DMA(())],
        ),
        out_shape=jax.ShapeDtypeStruct((cap * ROW_TILE, LANES), jnp.uint32),
        compiler_params=_cparams("arbitrary"),
        name="moe_dispatch",
    )(pstart, counts, pend, dest3, h2t)


def _expert_kernel(blk_e_ref, nvalid_ref, xs_ref, wgu_ref, bgu_ref, wd_ref, bd_ref, ys_ref, wgu_bf, wd_bf):
    i = pl.program_id(0)

    @pl.when((i == 0) | (blk_e_ref[i] != blk_e_ref[jnp.maximum(i - 1, 0)]))
    def _():
        wgu_bf[...] = wgu_ref[0].astype(BF16)
        wd_bf[...] = wd_ref[0].astype(BF16)

    @pl.when(i < nvalid_ref[0])
    def _():
        first, second = _load_row_tiles(xs_ref, BM)
        x = jnp.concatenate([p.astype(BF16) for p in first + second], axis=1)
        gu = jnp.dot(x, wgu_bf[...], preferred_element_type=F32) + bgu_ref[0]
        gate = jnp.minimum(gu[:, :EXPERT_FF], SWIGLU_LIMIT)
        up = jnp.clip(gu[:, EXPERT_FF:], -SWIGLU_LIMIT, SWIGLU_LIMIT)
        act = gate * (1.0 / (1.0 + jnp.exp(-SWIGLU_ALPHA * gate))) * (up + 1.0)
        y = jnp.dot(act.astype(BF16), wd_bf[...], preferred_element_type=F32) + bd_ref[0]
        _store_row_tiles(ys_ref, y)


def _experts(xs, blk_e, nvalid, wgu_bf16, bgu, wd_bf16, bd):
    cap = xs.shape[0] // ROW_TILE
    nblk = cap // BM

    def blk(i, be, nv):
        return (jnp.minimum(i, nv[0] - 1), 0)

    def by_expert(i, be, nv):
        return (be[i], 0, 0)

    return pl.pallas_call(
        _expert_kernel,
        grid_spec=pltpu.PrefetchScalarGridSpec(
            num_scalar_prefetch=2,
            grid=(nblk,),
            in_specs=[pl.BlockSpec((BM * ROW_TILE, LANES), blk),
                      pl.BlockSpec((1, D_MODEL, 2 * EXPERT_FF), by_expert),
                      pl.BlockSpec((1, 1, 2 * EXPERT_FF), by_expert),
                      pl.BlockSpec((1, EXPERT_FF, D_MODEL), by_expert),
                      pl.BlockSpec((1, 1, D_MODEL), by_expert)],
            out_specs=pl.BlockSpec((BM * ROW_TILE, LANES), blk),
            scratch_shapes=[pltpu.VMEM((D_MODEL, 2 * EXPERT_FF), BF16), pltpu.VMEM((EXPERT_FF, D_MODEL), BF16)],
        ),
        out_shape=jax.ShapeDtypeStruct((cap * ROW_TILE, LANES), jnp.uint32),
        compiler_params=_cparams("arbitrary"),
        name="moe_experts",
    )(blk_e, nvalid, xs, wgu_bf16, bgu.reshape(N_EXPERTS, 1, 2 * EXPERT_FF), wd_bf16, bd.reshape(N_EXPERTS, 1, D_MODEL))


def _combine_kernel(dest_ref, dest_next_ref, x1_ref, gates_ref, gfin_ref, ys_hbm, o_ref, buf, sem, *, tc):
    i = pl.program_id(0)
    n = pl.num_programs(0)

    def row_copy(src, slot, kk, t):
        return pltpu.make_async_copy(ys_hbm.at[_tile_rows(src)], buf.at[slot, kk, _tile_rows(t)], sem.at[slot])

    def issue(idx_ref, slot):
        def body(t, carry):
            for kk in range(TOP_K):
                row_copy(idx_ref[0, 0, t * TOP_K + kk], slot, kk, t).start(priority=kk % 2)
            return carry

        lax.fori_loop(0, tc, body, 0, unroll=DMA_UNROLL)

    @pl.when(i == 0)
    def _():
        issue(dest_ref, 0)

    @pl.when(i + 1 < n)
    def _():
        issue(dest_next_ref, (i + 1) % 2)

    slot = i % 2

    def wait(t, carry):
        for kk in range(TOP_K):
            row_copy(0, slot, kk, 0).wait()
        return carry

    lax.fori_loop(0, tc, wait, 0, unroll=DMA_UNROLL)
    g = gates_ref[...]
    gk = [jnp.broadcast_to(g[:, kk:kk + 1], (tc, LANES)) for kk in range(TOP_K)]
    zs = {}
    ssq = jnp.zeros((tc, 1), F32)
    for s in range(ROW_TILE):
        c_lo, c_hi = s * LANES, HALF_D + s * LANES
        z_lo = x1_ref[:, c_lo:c_lo + LANES]
        z_hi = x1_ref[:, c_hi:c_hi + LANES]
        for kk in range(TOP_K):
            a, b = _unpack_bf16_pair(buf[slot, kk, pl.ds(s, tc, stride=ROW_TILE), :])
            z_lo = z_lo + gk[kk] * a
            z_hi = z_hi + gk[kk] * b
        zs[c_lo], zs[c_hi] = z_lo, z_hi
        ssq = ssq + jnp.sum(z_lo * z_lo + z_hi * z_hi, axis=-1, keepdims=True)
    inv = lax.rsqrt(ssq * (1.0 / D_MODEL) + NORM_EPS)
    for c0, z in zs.items():
        o_ref[:, c0:c0 + LANES] = z * inv * gfin_ref[:, c0:c0 + LANES]


def _combine(x1, gates, dest3, ys, norm_final_g):
    t = x1.shape[0]
    tc = TC
    n = t // tc
    row = lambda i: (i, 0)
    return pl.pallas_call(
        functools.partial(_combine_kernel, tc=tc),
        grid=(n,),
        in_specs=[pl.BlockSpec((1, 1, tc * TOP_K), lambda i: (i, 0, 0), memory_space=pltpu.SMEM),
                  pl.BlockSpec((1, 1, tc * TOP_K), lambda i: (jnp.minimum(i + 1, n - 1), 0, 0),
                               memory_space=pltpu.SMEM),
                  pl.BlockSpec((tc, D_MODEL), row), pl.BlockSpec((tc, LANES), row),
                  pl.BlockSpec((1, D_MODEL), lambda i: (0, 0)),
                  pl.BlockSpec(memory_space=pl.ANY)],
        out_specs=pl.BlockSpec((tc, D_MODEL), row),
        scratch_shapes=[pltpu.VMEM((2, TOP_K, tc * ROW_TILE, LANES), jnp.uint32), pltpu.SemaphoreType.DMA((2,))],
        out_shape=jax.ShapeDtypeStruct((t, D_MODEL), F32),
        compiler_params=_cparams("arbitrary"),
        name="moe_combine",
    )(dest3, dest3, x1, gates, norm_final_g.reshape(1, D_MODEL), ys)


def _moe(x1, h2t, ri, gates, counts_f, wgu_bf16, bgu, wd_bf16, bd, norm_final_g):
    t = x1.shape[0]
    a = t * TOP_K
    cap = a + N_EXPERTS * BM
    nblk = cap // BM
    counts = counts_f[0, :N_EXPERTS].astype(jnp.int32)
    padded = ((counts + BM - 1) // BM) * BM
    pend = jnp.cumsum(padded)
    pstart = pend - padded
    nvalid = (pend[-1] // BM).reshape(1)
    first_row = jnp.minimum(jnp.arange(nblk, dtype=jnp.int32) * BM, pend[-1] - 1)
    blk_e = jnp.sum(pend[None, :] <= first_row[:, None], axis=1).astype(jnp.int32)
    idx, rank = ri[:, :TOP_K], ri[:, TOP_K:2 * TOP_K]
    onehot = idx[:, :, None] == jnp.arange(N_EXPERTS, dtype=jnp.int32)[None, None, :]
    dest = rank + jnp.sum(jnp.where(onehot, pstart[None, None, :], 0), axis=-1)
    xs = _dispatch(h2t, dest.reshape(t // TG, 1, TG * TOP_K), pstart, counts, pend, cap)
    ys = _experts(xs, blk_e, nvalid, wgu_bf16, bgu, wd_bf16, bd)
    return _combine(x1, gates, dest.reshape(t // TC, 1, TC * TOP_K), ys, norm_final_g)


def _encoder(x, p):
    b, s, _ = x.shape
    x2d = x.reshape(b * s, D_MODEL)
    cos_t, sin_t = _rope_tables(s)
    (qa, ka, va, qr, kr, vr, gr), by_residue = _inproj(x2d, p["norm_mix_g"], p["w_in"], cos_t, sin_t, s)
    outs, stats = [], []
    for d in DILATIONS:
        if d == 1:
            o, st = _banded_attention(*[z.reshape(b, s, WIDTH) for z in (qa, ka, va)])
            outs.append(o.reshape(b * s, WIDTH))
            stats.append(st.reshape(b * s, LANES))
        else:
            o, st = _banded_attention(*[z.reshape(b * d, s // d, WIDTH) for z in by_residue[d]])
            outs.append(o.reshape(b, d, s // d, WIDTH))
            stats.append(st.reshape(b, d, s // d, LANES))
    ret = _retention(qr, kr, vr, gr, p["ret_norm_g"], b, s)
    x1, h2t, ri, gates, counts = _outproj_router(x2d, outs, stats, ret, p["w_out"], p["norm_ffn_g"],
                                                 p["w_router"], p["b_router"], s)
    y = _moe(x1, h2t, ri, gates, counts, p["w_gate_up"], p["b_gate_up"], p["w_down"], p["b_down"], p["norm_final_g"])
    return y.reshape(b, s, D_MODEL)


def kernel(x_prompt, x_sample, norm_mix_g, w_in, ret_norm_g, w_out, norm_ffn_g, w_router, b_router, w_gate_up, b_gate_up, w_down, b_down, norm_final_g):
    assert norm_mix_g.shape[0] == 1, "single layer"
    p = dict(norm_mix_g=norm_mix_g[0], w_in=w_in[0].astype(BF16), ret_norm_g=ret_norm_g[0],
             w_out=w_out[0].astype(BF16), norm_ffn_g=norm_ffn_g[0], w_router=w_router[0], b_router=b_router[0],
             w_gate_up=w_gate_up[0], b_gate_up=b_gate_up[0], w_down=w_down[0],
             b_down=b_down[0], norm_final_g=norm_final_g)
    return (_encoder(x_prompt, p), _encoder(x_sample, p))
```

```python
import functools

import jax
import jax.numpy as jnp
from jax import lax
from jax.experimental import pallas as pl
from jax.experimental.pallas import tpu as pltpu
from jax.experimental.pallas import tpu_sc as plsc

D_MODEL = 1024
HEAD_DIM = 64
N_HEADS = 8
WIDTH = N_HEADS * HEAD_DIM
N_SLABS = 7
DILATIONS = (1, 4, 16)
HALF_SPAN = 64
ROPE_THETA = 10000.0
RET_DECAY_BASE_FWD = 5.0
RET_DECAY_BASE_BWD = 5.5
N_EXPERTS = 32
TOP_K = 4
EXPERT_FF = D_MODEL
SWIGLU_LIMIT = 7.0
SWIGLU_ALPHA = 1.702
NORM_EPS = 1e-6
NEG_INF = -1e30

LANES = 128
VMEM_LIMIT_BYTES = 48 * 1024 * 1024

TM_INPROJ = 512
TQ = 128
TL_ATTN = 512
HEAD_GROUP = 4
RET_CHUNK = 256
TM_OUT = 512
TG = 512
BM = 512
TC = 512
DMA_UNROLL = 4

F32 = jnp.float32
BF16 = jnp.bfloat16


def _cparams(*sem):
    return pltpu.CompilerParams(dimension_semantics=sem, vmem_limit_bytes=VMEM_LIMIT_BYTES)


_ROTATE = (True, True, False, True, True, False, False)
_SCALE = (HEAD_DIM ** -0.5, 1.0, 1.0, 1.0, HEAD_DIM ** -0.5, 1.0, 1.0)


N_ATTN_SLABS = 3
RESIDUE_DILATIONS = tuple(d for d in DILATIONS if d > 1)
REGROUP = 4
assert RESIDUE_DILATIONS == (REGROUP, REGROUP ** 2)


def _inproj_kernel(x_ref, g_ref, w_ref, cos_ref, sin_ref, *refs):
    out_refs = refs[:N_SLABS]
    res_refs = refs[N_SLABS:N_SLABS + N_ATTN_SLABS * len(RESIDUE_DILATIONS)]
    stage, stage4 = refs[-2:]
    x = x_ref[...]
    tm = x.shape[0]
    h = (x * lax.rsqrt(jnp.mean(x * x, axis=-1, keepdims=True) + NORM_EPS) * g_ref[...]).astype(BF16)
    cos = cos_ref[...]
    sin = sin_ref[...]
    lane = lax.broadcasted_iota(jnp.int32, cos.shape, 1)
    first_half = (lane & (HEAD_DIM - 1)) < HEAD_DIM // 2
    for j, o_ref in enumerate(out_refs):
        p = jnp.dot(h, w_ref[:, j * WIDTH:(j + 1) * WIDTH], preferred_element_type=F32)
        for c in range(WIDTH // LANES):
            cols = slice(c * LANES, (c + 1) * LANES)
            r = p[:, cols]
            if _ROTATE[j]:
                partner = jnp.where(first_half, pltpu.roll(r, LANES - HEAD_DIM // 2, 1), pltpu.roll(r, HEAD_DIM // 2, 1))
                r = r * cos + partner * sin
                if _SCALE[j] != 1.0:
                    r = r * _SCALE[j]
            o_ref[:, cols] = r.astype(BF16)
            if j < N_ATTN_SLABS:
                stage[j, c] = r
                dst4 = res_refs[j]
                dst16 = res_refs[N_ATTN_SLABS + j]
                for bb in range(REGROUP):
                    grp = stage[j, c, pl.ds(bb, tm // REGROUP, stride=REGROUP), :]
                    dst4[0, bb, :, cols] = grp.astype(BF16)
                    stage4[j, c, bb] = grp
                    for aa in range(REGROUP):
                        sub = stage4[j, c, bb, pl.ds(aa, tm // REGROUP ** 2, stride=REGROUP), :]
                        dst16[0, bb * REGROUP + aa, :, cols] = sub.astype(BF16)


def _inproj(x2d, g, w_bf16, cos_t, sin_t, seq):
    t = x2d.shape[0]
    tm = min(TM_INPROJ, seq)
    pos_blocks = seq // tm
    b = t // seq
    out = jax.ShapeDtypeStruct((t, WIDTH), BF16)
    row = lambda i: (i, 0)
    out_specs = [pl.BlockSpec((tm, WIDTH), row)] * N_SLABS
    out_shape = [out] * N_SLABS
    for d in RESIDUE_DILATIONS:
        out_specs += [pl.BlockSpec((1, d, tm // d, WIDTH), lambda i: (i // pos_blocks, 0, i % pos_blocks, 0))] * N_ATTN_SLABS
        out_shape += [jax.ShapeDtypeStruct((b, d, seq // d, WIDTH), BF16)] * N_ATTN_SLABS
    res = pl.pallas_call(
        _inproj_kernel,
        grid=(t // tm,),
        in_specs=[
            pl.BlockSpec((tm, D_MODEL), row),
            pl.BlockSpec((1, D_MODEL), lambda i: (0, 0)),
            pl.BlockSpec((D_MODEL, N_SLABS * WIDTH), lambda i: (0, 0)),
            pl.BlockSpec((tm, LANES), lambda i: (i % pos_blocks, 0)),
            pl.BlockSpec((tm, LANES), lambda i: (i % pos_blocks, 0)),
        ],
        out_specs=out_specs,
        out_shape=out_shape,
        scratch_shapes=[pltpu.VMEM((N_ATTN_SLABS, WIDTH // LANES, tm, LANES), F32),
                        pltpu.VMEM((N_ATTN_SLABS, WIDTH // LANES, REGROUP, tm // REGROUP, LANES), F32)],
        compiler_params=_cparams("arbitrary"),
        name="inproj",
    )(x2d, g.reshape(1, D_MODEL), w_bf16, cos_t, sin_t)
    natural = res[:N_SLABS]
    by_residue = {d: res[N_SLABS + di * N_ATTN_SLABS:N_SLABS + (di + 1) * N_ATTN_SLABS]
                  for di, d in enumerate(RESIDUE_DILATIONS)}
    return natural, by_residue


def _rope_tables(seq):
    half = HEAD_DIM // 2
    inv_freq = ROPE_THETA ** (-jnp.arange(0, HEAD_DIM, 2, dtype=F32) / HEAD_DIM)
    ang = jnp.arange(seq, dtype=F32)[:, None] * inv_freq[None, :]
    cos, sin = jnp.cos(ang), jnp.sin(ang)
    reps = LANES // HEAD_DIM
    cos_t = jnp.tile(jnp.concatenate([cos, cos], axis=1), (1, reps))
    sin_t = jnp.tile(jnp.concatenate([-sin, sin], axis=1), (1, reps))
    assert cos_t.shape == (seq, LANES) and half * 2 == HEAD_DIM
    return cos_t, sin_t


def _attn_kernel(q_ref, kp_ref, kc_ref, kn_ref, vp_ref, vc_ref, vn_ref, o_ref, st_ref, kbuf, vbuf, *, gb, tl, sub_len):
    i = pl.program_id(1)
    hs = HALF_SPAN
    for gi in range(gb):
        kbuf[gi, 0:hs] = kp_ref[gi]
        kbuf[gi, hs:hs + tl] = kc_ref[gi]
        kbuf[gi, hs + tl:hs + tl + hs] = kn_ref[gi]
        vbuf[gi, 0:hs] = vp_ref[gi]
        vbuf[gi, hs:hs + tl] = vc_ref[gi]
        vbuf[gi, hs + tl:hs + tl + hs] = vn_ref[gi]
    tk = TQ + 2 * hs
    gw = HEAD_GROUP * HEAD_DIM
    qi = lax.broadcasted_iota(jnp.int32, (TQ, tk), 0)
    kj = lax.broadcasted_iota(jnp.int32, (TQ, tk), 1)
    band = (kj >= qi) & (kj - qi <= 2 * hs)
    lane = lax.broadcasted_iota(jnp.int32, (TQ, LANES), 1)
    head_of_lane = lax.broadcasted_iota(jnp.int32, (TQ, gw), 1) // HEAD_DIM
    for gi, sub in [(gi, sub) for gi in range(gb) for sub in range(tl // TQ)]:
        a = sub * TQ
        kpos = kj + (i * tl + a - hs)
        bias = jnp.where(band & (kpos >= 0) & (kpos < sub_len), 0.0, NEG_INF).astype(F32)
        bias = jnp.concatenate([bias] * HEAD_GROUP, axis=0)
        st = jnp.zeros((TQ, LANES), F32)
        for g in range(N_HEADS // HEAD_GROUP):
            cols = slice(g * gw, (g + 1) * gw)
            q4 = q_ref[gi, a:a + TQ, cols]
            k4 = kbuf[gi, a:a + tk, cols]
            v4 = vbuf[gi, a:a + tk, cols]
            lhs = jnp.concatenate([jnp.where(head_of_lane == h, q4, jnp.zeros_like(q4)) for h in range(HEAD_GROUP)], axis=0)
            s = lax.dot_general(lhs, k4, (((1,), (1,)), ((), ())), preferred_element_type=F32) + bias
            m = jnp.max(s, axis=-1, keepdims=True)
            p = jnp.exp(s - m)
            l = jnp.sum(p, axis=-1, keepdims=True)
            o_all = jnp.dot(p.astype(BF16), v4, preferred_element_type=F32) / l
            lse = m + jnp.log(l)
            o = o_all[0:TQ]
            for h in range(HEAD_GROUP):
                rows = slice(h * TQ, (h + 1) * TQ)
                if h:
                    o = jnp.where(head_of_lane == h, o_all[rows], o)
                st = jnp.where(lane == g * HEAD_GROUP + h, lse[rows], st)
            o_ref[gi, a:a + TQ, cols] = o.astype(BF16)
        st_ref[gi, a:a + TQ, :] = st


def _banded_attention(q, k, v):
    g, sub_len, _ = q.shape
    tl = min(TL_ATTN, sub_len)
    gb = min(TL_ATTN // tl, g)
    hs = HALF_SPAN
    per = tl // hs
    last = sub_len // hs - 1
    cur = pl.BlockSpec((gb, tl, WIDTH), lambda b, i: (b, i, 0))
    prev = pl.BlockSpec((gb, hs, WIDTH), lambda b, i: (b, jnp.maximum(i * per - 1, 0), 0))
    nxt = pl.BlockSpec((gb, hs, WIDTH), lambda b, i: (b, jnp.minimum((i + 1) * per, last), 0))
    return pl.pallas_call(
        functools.partial(_attn_kernel, gb=gb, tl=tl, sub_len=sub_len),
        grid=(g // gb, sub_len // tl),
        in_specs=[cur, prev, cur, nxt, prev, cur, nxt],
        out_specs=[cur, pl.BlockSpec((gb, tl, LANES), lambda b, i: (b, i, 0))],
        out_shape=[jax.ShapeDtypeStruct((g, sub_len, WIDTH), BF16),
                   jax.ShapeDtypeStruct((g, sub_len, LANES), F32)],
        scratch_shapes=[pltpu.VMEM((gb, tl + 2 * hs, WIDTH), BF16), pltpu.VMEM((gb, tl + 2 * hs, WIDTH), BF16)],
        compiler_params=_cparams("arbitrary", "arbitrary"),
        name="banded_attention",
    )(q, k, k, k, v, v, v)


def _to_residues(z, b, s, d):
    w = z.shape[-1]
    if d == 1:
        return z.reshape(b, s, w)
    return z.reshape(b, s // d, d, w).transpose(0, 2, 1, 3).reshape(b * d, s // d, w)


def _from_residues(z, b, s, d):
    w = z.shape[-1]
    if d == 1:
        return z.reshape(b * s, w)
    return z.reshape(b, d, s // d, w).transpose(0, 2, 1, 3).reshape(b * s, w)


def _ret_kernel(q_ref, k_ref, v_ref, gate_ref, dmat_ref, wq_ref, wk_ref, dec_ref, gn_ref, o_ref, sf_scr, sb_scr, *, c, n_chunks):
    pair = 2 * HEAD_DIM
    lane = lax.broadcasted_iota(jnp.int32, (c, pair), 1)
    head0 = lane < HEAD_DIM
    blk_r = lax.broadcasted_iota(jnp.int32, (pair, pair), 0) // HEAD_DIM
    blk_c = lax.broadcasted_iota(jnp.int32, (pair, pair), 1) // HEAD_DIM
    same_head = blk_r == blk_c
    dec_f = dec_ref[0:1, :]
    dec_b = dec_ref[1:2, :]
    tn = (((0,), (0,)), ((), ()))
    nt = (((1,), (1,)), ((), ()))

    def rows_of(n):
        return pl.ds(pl.multiple_of(n * c, c), c)

    def kv_body(n, carry):
        rows = rows_of(n)
        kf32 = k_ref[0, rows, :].astype(F32)
        kw = jnp.concatenate([(kf32 * wk_ref[:, :pair]).astype(BF16), (kf32 * wk_ref[:, pair:]).astype(BF16)], axis=1)
        kv = lax.dot_general(kw, v_ref[0, rows, :], tn, preferred_element_type=F32)
        sf_scr[n] = jnp.where(same_head, kv[:pair], 0.0)
        sb_scr[n] = jnp.where(same_head, kv[pair:], 0.0)
        return carry

    lax.fori_loop(0, n_chunks, kv_body, 0, unroll=2)

    def scan_body(t, carry):
        sf, sb = carry
        nb = n_chunks - 1 - t
        kv_f = sf_scr[t]
        kv_b = sb_scr[nb]
        sf_scr[t] = sf
        sb_scr[nb] = sb
        return sf * dec_f + kv_f, sb * dec_b + kv_b

    zero = jnp.zeros((pair, pair), F32)
    lax.fori_loop(0, n_chunks, scan_body, (zero, zero))

    def out_body(n, carry):
        rows = rows_of(n)
        q = q_ref[0, rows, :]
        k = k_ref[0, rows, :]
        v = v_ref[0, rows, :]
        qf32 = q.astype(F32)
        intra = []
        for hh in range(2):
            mask = head0 if hh == 0 else jnp.logical_not(head0)
            kh = jnp.where(mask, k, jnp.zeros_like(k))
            s = lax.dot_general(q, kh, nt, preferred_element_type=F32)
            a = (s * dmat_ref[hh]).astype(BF16)
            intra.append(jnp.dot(a, v, preferred_element_type=F32))
        qw = jnp.concatenate([(qf32 * wq_ref[:, :pair]).astype(BF16), (qf32 * wq_ref[:, pair:]).astype(BF16)], axis=1)
        states = jnp.concatenate([sf_scr[n].astype(BF16), sb_scr[n].astype(BF16)], axis=0)
        tot = jnp.where(head0, intra[0], intra[1]) + jnp.dot(qw, states, preferred_element_type=F32)
        inv = 1.0 / HEAD_DIM
        s0 = jnp.sum(jnp.where(head0, tot, 0.0), axis=-1, keepdims=True)
        s1 = jnp.sum(jnp.where(head0, 0.0, tot), axis=-1, keepdims=True)
        xc = tot - jnp.where(head0, s0, s1) * inv
        sq = xc * xc
        v0 = jnp.sum(jnp.where(head0, sq, 0.0), axis=-1, keepdims=True)
        v1 = jnp.sum(jnp.where(head0, 0.0, sq), axis=-1, keepdims=True)
        y = xc * lax.rsqrt(jnp.where(head0, v0, v1) * inv + NORM_EPS)
        gt = gate_ref[0, rows, :].astype(F32)
        y = y * gn_ref[...] * (gt / (1.0 + jnp.exp(-gt)))
        o_ref[0, rows, :] = y.astype(BF16)
        return carry

    lax.fori_loop(0, n_chunks, out_body, 0, unroll=2)


def _retention_tables(c):
    hidx = jnp.arange(N_HEADS, dtype=F32)
    lg_f = jnp.log1p(-jnp.exp2(-(RET_DECAY_BASE_FWD + hidx)))
    lg_b = jnp.log1p(-jnp.exp2(-(RET_DECAY_BASE_BWD + hidx)))
    pos = jnp.arange(c, dtype=F32)
    diff = pos[:, None] - pos[None, :]
    dm_f = jnp.exp(jnp.maximum(diff, 0.0)[None] * lg_f[:, None, None])
    dm_b = jnp.exp(jnp.maximum(-diff, 0.0)[None] * lg_b[:, None, None])
    dmat = jnp.where((diff >= 0)[None], dm_f, dm_b)

    def per_lane(tab):
        t = jnp.repeat(tab[:, :, None], HEAD_DIM, axis=2)
        return t.reshape(N_HEADS // 2, 2, c, HEAD_DIM).transpose(0, 2, 1, 3).reshape(N_HEADS // 2, c, 2 * HEAD_DIM)

    wq_f = per_lane(jnp.exp((pos + 1.0)[None, :] * lg_f[:, None]))
    wq_b = per_lane(jnp.exp((c - pos)[None, :] * lg_b[:, None]))
    wk_f = per_lane(jnp.exp((c - 1.0 - pos)[None, :] * lg_f[:, None]))
    wk_b = per_lane(jnp.exp(pos[None, :] * lg_b[:, None]))
    wq = jnp.concatenate([wq_f, wq_b], axis=2)
    wk = jnp.concatenate([wk_f, wk_b], axis=2)
    dec = jnp.stack([jnp.repeat(jnp.exp(c * lg_f), HEAD_DIM), jnp.repeat(jnp.exp(c * lg_b), HEAD_DIM)], axis=0)
    dec = dec.reshape(2, N_HEADS // 2, 2 * HEAD_DIM).transpose(1, 0, 2)
    return dmat, wq, wk, dec


def _retention(q, k, v, gate, ret_norm_g, b, s):
    c = min(RET_CHUNK, s)
    n_chunks = s // c
    pair = 2 * HEAD_DIM
    dmat, wq, wk, dec = _retention_tables(c)
    seq_blk = pl.BlockSpec((1, s, pair), lambda bi, hp: (bi, 0, hp))
    r3 = lambda z: z.reshape(b, s, WIDTH)
    return pl.pallas_call(
        functools.partial(_ret_kernel, c=c, n_chunks=n_chunks),
        grid=(b, N_HEADS // 2),
        in_specs=[seq_blk, seq_blk, seq_blk, seq_blk,
                  pl.BlockSpec((2, c, c), lambda bi, hp: (hp, 0, 0)),
                  pl.BlockSpec((None, c, 2 * pair), lambda bi, hp: (hp, 0, 0)),
                  pl.BlockSpec((None, c, 2 * pair), lambda bi, hp: (hp, 0, 0)),
                  pl.BlockSpec((None, 2, pair), lambda bi, hp: (hp, 0, 0)),
                  pl.BlockSpec((1, pair), lambda bi, hp: (0, hp))],
        out_specs=seq_blk,
        out_shape=jax.ShapeDtypeStruct((b, s, WIDTH), BF16),
        scratch_shapes=[pltpu.VMEM((n_chunks, pair, pair), F32), pltpu.VMEM((n_chunks, pair, pair), F32)],
        compiler_params=_cparams("arbitrary", "arbitrary"),
        name="retention",
    )(r3(q), r3(k), r3(v), r3(gate), dmat, wq, wk, dec, ret_norm_g.reshape(1, WIDTH)).reshape(b * s, WIDTH)


ROW_TILE = 4
HALF_D = D_MODEL // 2
assert ROW_TILE * LANES == HALF_D


def _pack_bf16_pair(a, b):
    ua = pltpu.bitcast(a.astype(BF16).astype(F32), jnp.uint32)
    ub = pltpu.bitcast(b.astype(BF16).astype(F32), jnp.uint32)
    return ua | (ub >> 16)


def _unpack_bf16_pair(u):
    return pltpu.bitcast(u & jnp.uint32(0xFFFF0000), F32), pltpu.bitcast(u << 16, F32)


def _store_row_tiles(ref, val):
    n = val.shape[0]
    for s in range(ROW_TILE):
        lo = slice(s * LANES, (s + 1) * LANES)
        hi = slice(HALF_D + s * LANES, HALF_D + (s + 1) * LANES)
        ref[pl.ds(s, n, stride=ROW_TILE), :] = _pack_bf16_pair(val[:, lo], val[:, hi])


def _load_row_tiles(ref, n):
    first, second = [], []
    for s in range(ROW_TILE):
        a, b = _unpack_bf16_pair(ref[pl.ds(s, n, stride=ROW_TILE), :])
        first.append(a)
        second.append(b)
    return first, second


def _split_bf16(x):
    hi = x.astype(BF16)
    return hi, (x - hi.astype(F32)).astype(BF16)


def _outproj_kernel(x_ref, o1_ref, o2_ref, o3_ref, s1_ref, s2_ref, s3_ref, ret_ref, wout_ref, expand_ref,
                    gffn_ref, wr_ref, br_ref, tri_ref,
                    x1_ref, h2t_ref, ri_ref, gates_ref, cnt_ref, base_scr, nat_o, nat_s, grp_o):
    i = pl.program_id(0)
    tm = x_ref.shape[0]

    @pl.when(i == 0)
    def _():
        base_scr[...] = jnp.zeros_like(base_scr)

    o_nat = [[o1_ref[:, c * LANES:(c + 1) * LANES].astype(F32) for c in range(WIDTH // LANES)]]
    sts = [s1_ref[...]]
    n_col = WIDTH // LANES
    for bb in range(REGROUP):
        rows4 = pl.ds(bb, tm // REGROUP, stride=REGROUP)
        for c in range(n_col):
            nat_o[0, c, rows4, :] = o2_ref[0, bb, :, c * LANES:(c + 1) * LANES].astype(F32)
        nat_s[0, rows4, :] = s2_ref[0, bb]
        for aa in range(REGROUP):
            rows16 = pl.ds(aa, tm // REGROUP ** 2, stride=REGROUP)
            for c in range(n_col):
                grp_o[c, bb, rows16, :] = o3_ref[0, bb * REGROUP + aa, :, c * LANES:(c + 1) * LANES].astype(F32)
            grp_o[n_col, bb, rows16, :] = s3_ref[0, bb * REGROUP + aa]
        for c in range(n_col):
            nat_o[1, c, rows4, :] = grp_o[c, bb]
        nat_s[1, rows4, :] = grp_o[n_col, bb]
    for di in range(len(RESIDUE_DILATIONS)):
        o_nat.append([nat_o[di, c] for c in range(n_col)])
        sts.append(nat_s[di])

    mx = jnp.maximum(jnp.maximum(sts[0], sts[1]), sts[2])
    es = [jnp.exp(st - mx) for st in sts]
    den = es[0] + es[1] + es[2]
    attn = None
    for e, o_cols in zip(es, o_nat):
        hi, lo = _split_bf16(e / den)
        w_full = jnp.dot(jnp.concatenate([hi, lo], axis=1), expand_ref[...], preferred_element_type=F32)
        term = w_full * jnp.concatenate(o_cols, axis=1)
        attn = term if attn is None else attn + term
    mixed = jnp.concatenate([attn.astype(BF16), ret_ref[...]], axis=1)
    x1 = x_ref[...] + jnp.dot(mixed, wout_ref[...], preferred_element_type=F32)
    x1_ref[...] = x1

    h2 = x1 * lax.rsqrt(jnp.mean(x1 * x1, axis=-1, keepdims=True) + NORM_EPS) * gffn_ref[...]
    _store_row_tiles(h2t_ref, h2)

    hi, lo = _split_bf16(h2)
    logits = jnp.dot(jnp.concatenate([hi, lo, hi], axis=1), wr_ref[...], preferred_element_type=F32) + br_ref[...]
    tm = logits.shape[0]
    lane = lax.broadcasted_iota(jnp.int32, (tm, LANES), 1).astype(F32)
    work = logits
    vals, idxs = [], []
    onehot = jnp.zeros((tm, LANES), F32)
    for _k in range(TOP_K):
        mk = jnp.max(work, axis=-1, keepdims=True)
        ik = jnp.min(jnp.where(work == mk, lane, float(LANES)), axis=-1, keepdims=True)
        sel = lane == ik
        onehot = jnp.where(sel, 1.0, onehot)
        work = jnp.where(sel, -jnp.inf, work)
        vals.append(mk)
        idxs.append(ik)
    ex = [jnp.exp(vk - vals[0]) for vk in vals]
    tot = ex[0] + ex[1] + ex[2] + ex[3]
    before = jnp.dot(tri_ref[...], onehot.astype(BF16), preferred_element_type=F32) + base_scr[...]
    ri = jnp.zeros((tm, LANES), F32)
    gt = jnp.zeros((tm, LANES), F32)
    for kk in range(TOP_K):
        rank = jnp.sum(jnp.where(lane == idxs[kk], before, 0.0), axis=-1, keepdims=True)
        ri = jnp.where(lane == float(kk), idxs[kk], ri)
        ri = jnp.where(lane == float(TOP_K + kk), rank, ri)
        gt = jnp.where(lane == float(kk), ex[kk] / tot, gt)
    ri_ref[...] = ri.astype(jnp.int32)
    gates_ref[...] = gt
    new_base = base_scr[...] + jnp.sum(onehot, axis=0, keepdims=True)
    base_scr[...] = new_base
    cnt_ref[...] = new_base


def _outproj_router(x2d, outs, stats, ret, wout_bf16, norm_ffn_g, w_router, b_router, seq):
    t = x2d.shape[0]
    tm = TM_OUT
    per_seq = seq // tm
    row = lambda i: (i, 0)

    def res_spec(d, width):
        return pl.BlockSpec((1, d, tm // d, width), lambda i: (i // per_seq, 0, i % per_seq, 0))

    const = lambda i: (0, 0)
    head_of_col = jnp.arange(WIDTH) // HEAD_DIM
    expand = (jnp.arange(LANES)[:, None] == head_of_col[None, :]).astype(BF16)
    expand2 = jnp.concatenate([expand, expand], axis=0)
    wr = jnp.zeros((D_MODEL, LANES), F32).at[:, :N_EXPERTS].set(w_router)
    wr_hi, wr_lo = _split_bf16(wr)
    wr3 = jnp.concatenate([wr_hi, wr_hi, wr_lo], axis=0)
    br = jnp.full((1, LANES), NEG_INF, F32).at[0, :N_EXPERTS].set(b_router)
    tri = (jnp.arange(tm)[:, None] > jnp.arange(tm)[None, :]).astype(BF16)
    o_spec = pl.BlockSpec((tm, WIDTH), row)
    s_spec = pl.BlockSpec((tm, LANES), row)
    return pl.pallas_call(
        _outproj_kernel,
        grid=(t // tm,),
        in_specs=[pl.BlockSpec((tm, D_MODEL), row),
                  o_spec, *[res_spec(d, WIDTH) for d in RESIDUE_DILATIONS],
                  s_spec, *[res_spec(d, LANES) for d in RESIDUE_DILATIONS], o_spec,
                  pl.BlockSpec((D_MODEL, D_MODEL), const), pl.BlockSpec((2 * LANES, WIDTH), const),
                  pl.BlockSpec((1, D_MODEL), const), pl.BlockSpec((3 * D_MODEL, LANES), const),
                  pl.BlockSpec((1, LANES), const), pl.BlockSpec((tm, tm), const)],
        out_specs=[pl.BlockSpec((tm, D_MODEL), row), pl.BlockSpec((tm * ROW_TILE, LANES), row),
                   s_spec, s_spec, pl.BlockSpec((1, LANES), const)],
        out_shape=[jax.ShapeDtypeStruct((t, D_MODEL), F32), jax.ShapeDtypeStruct((t * ROW_TILE, LANES), jnp.uint32),
                   jax.ShapeDtypeStruct((t, LANES), jnp.int32), jax.ShapeDtypeStruct((t, LANES), F32),
                   jax.ShapeDtypeStruct((1, LANES), F32)],
        scratch_shapes=[pltpu.VMEM((1, LANES), F32),
                        pltpu.VMEM((len(RESIDUE_DILATIONS), WIDTH // LANES, tm, LANES), F32),
                        pltpu.VMEM((len(RESIDUE_DILATIONS), tm, LANES), F32),
                        pltpu.VMEM((WIDTH // LANES + 1, REGROUP, tm // REGROUP, LANES), F32)],
        compiler_params=_cparams("arbitrary"),
        name="outproj_router",
    )(x2d, *outs, *stats, ret, wout_bf16, expand2, norm_ffn_g.reshape(1, D_MODEL), wr3, br, tri)


def _tile_rows(r):
    return pl.ds(pl.multiple_of(r * ROW_TILE, ROW_TILE), ROW_TILE)


def _dispatch_kernel(pstart_ref, cnt_ref, pend_ref, dest_ref, h2t_ref, xs_hbm, zrow, sem, zsem, *, tg):
    i = pl.program_id(0)

    def zero_copy(dst):
        return pltpu.make_async_copy(zrow, xs_hbm.at[_tile_rows(dst)], zsem)

    @pl.when(i == 0)
    def _():
        zrow[...] = jnp.zeros_like(zrow)
        for e in range(N_EXPERTS):
            lo = pstart_ref[e] + cnt_ref[e]
            hi = pend_ref[e]

            def start(r, carry):
                zero_copy(r).start()
                return carry

            def wait(r, carry):
                zero_copy(r).wait()
                return carry

            lax.fori_loop(lo, hi, start, 0)
            lax.fori_loop(lo, hi, wait, 0)

    def row_copy(src, dst):
        return pltpu.make_async_copy(h2t_ref.at[_tile_rows(src)], xs_hbm.at[_tile_rows(dst)], sem)

    def start(t, carry):
        for kk in range(TOP_K):
            row_copy(t, dest_ref[0, 0, t * TOP_K + kk]).start(priority=kk % 2)
        return carry

    def wait(t, carry):
        for kk in range(TOP_K):
            row_copy(0, 0).wait()
        return carry

    lax.fori_loop(0, tg, start, 0, unroll=DMA_UNROLL)
    lax.fori_loop(0, tg, wait, 0, unroll=DMA_UNROLL)


def _dispatch(h2t, dest3, pstart, counts, pend, cap):
    t = h2t.shape[0] // ROW_TILE
    tg = TG
    return pl.pallas_call(
        functools.partial(_dispatch_kernel, tg=tg),
        grid_spec=pltpu.PrefetchScalarGridSpec(
            num_scalar_prefetch=3,
            grid=(t // tg,),
            in_specs=[pl.BlockSpec((1, 1, tg * TOP_K), lambda i, *_: (i, 0, 0), memory_space=pltpu.SMEM),
                      pl.BlockSpec((tg * ROW_TILE, LANES), lambda i, *_: (i, 0))],
            out_specs=pl.BlockSpec(memory_space=pl.ANY),
            scratch_shapes=[pltpu.VMEM((ROW_TILE, LANES), jnp.uint32), pltpu.SemaphoreType.DMA(()),
                            pltpu.SemaphoreType.DMA(())],
        ),
        out_shape=jax.ShapeDtypeStruct((cap * ROW_TILE, LANES), jnp.uint32),
        compiler_params=_cparams("arbitrary"),
        name="moe_dispatch",
    )(pstart, counts, pend, dest3, h2t)


def _expert_kernel(blk_e_ref, nvalid_ref, xs_ref, wgu_ref, bgu_ref, wd_ref, bd_ref, ys_ref, wgu_bf, wd_bf):
    i = pl.program_id(0)

    @pl.when((i == 0) | (blk_e_ref[i] != blk_e_ref[jnp.maximum(i - 1, 0)]))
    def _():
        wgu_bf[...] = wgu_ref[0].astype(BF16)
        wd_bf[...] = wd_ref[0].astype(BF16)

    @pl.when(i < nvalid_ref[0])
    def _():
        first, second = _load_row_tiles(xs_ref, BM)
        x = jnp.concatenate([p.astype(BF16) for p in first + second], axis=1)
        gu = jnp.dot(x, wgu_bf[...], preferred_element_type=F32) + bgu_ref[0]
        gate = jnp.minimum(gu[:, :EXPERT_FF], SWIGLU_LIMIT)
        up = jnp.clip(gu[:, EXPERT_FF:], -SWIGLU_LIMIT, SWIGLU_LIMIT)
        act = gate * (1.0 / (1.0 + jnp.exp(-SWIGLU_ALPHA * gate))) * (up + 1.0)
        y = jnp.dot(act.astype(BF16), wd_bf[...], preferred_element_type=F32) + bd_ref[0]
        _store_row_tiles(ys_ref, y)


def _experts(xs, blk_e, nvalid, wgu_bf16, bgu, wd_bf16, bd):
    cap = xs.shape[0] // ROW_TILE
    nblk = cap // BM

    def blk(i, be, nv):
        return (jnp.minimum(i, nv[0] - 1), 0)

    def by_expert(i, be, nv):
        return (be[i], 0, 0)

    return pl.pallas_call(
        _expert_kernel,
        grid_spec=pltpu.PrefetchScalarGridSpec(
            num_scalar_prefetch=2,
            grid=(nblk,),
            in_specs=[pl.BlockSpec((BM * ROW_TILE, LANES), blk),
                      pl.BlockSpec((1, D_MODEL, 2 * EXPERT_FF), by_expert),
                      pl.BlockSpec((1, 1, 2 * EXPERT_FF), by_expert),
                      pl.BlockSpec((1, EXPERT_FF, D_MODEL), by_expert),
                      pl.BlockSpec((1, 1, D_MODEL), by_expert)],
            out_specs=pl.BlockSpec((BM * ROW_TILE, LANES), blk),
            scratch_shapes=[pltpu.VMEM((D_MODEL, 2 * EXPERT_FF), BF16), pltpu.VMEM((EXPERT_FF, D_MODEL), BF16)],
        ),
        out_shape=jax.ShapeDtypeStruct((cap * ROW_TILE, LANES), jnp.uint32),
        compiler_params=_cparams("arbitrary"),
        name="moe_experts",
    )(blk_e, nvalid, xs, wgu_bf16, bgu.reshape(N_EXPERTS, 1, 2 * EXPERT_FF), wd_bf16, bd.reshape(N_EXPERTS, 1, D_MODEL))


SC_CORES = 2
SC_SUBCORES = 16
SC_WINDOW = 128


def _sc_gather_rows(table, idx):
    n_rows = idx.shape[0]
    workers = SC_CORES * SC_SUBCORES
    per_worker = n_rows // workers
    n_win = per_worker // SC_WINDOW
    assert per_worker * workers == n_rows and n_win * SC_WINDOW == per_worker
    mesh = plsc.VectorSubcoreMesh(core_axis_name="c", subcore_axis_name="s")

    @functools.partial(
        pl.kernel, mesh=mesh,
        out_type=jax.ShapeDtypeStruct((n_rows, LANES), table.dtype),
        scratch_types=[pltpu.VMEM((SC_WINDOW,), jnp.int32), pltpu.VMEM((SC_WINDOW, LANES), table.dtype),
                       pltpu.SemaphoreType.DMA],
        name="sc_gather_rows",
    )
    def gather(table_hbm, idx_hbm, out_hbm, idx_v, rows_v, sem):
        wid = lax.axis_index("s") * SC_CORES + lax.axis_index("c")
        base = wid * per_worker

        @pl.loop(0, n_win)
        def _(j):
            off = pl.multiple_of(base + j * SC_WINDOW, SC_WINDOW)
            pltpu.sync_copy(idx_hbm.at[pl.ds(off, SC_WINDOW)], idx_v)
            pltpu.async_copy(table_hbm.at[idx_v], rows_v, sem).wait()
            pltpu.sync_copy(rows_v, out_hbm.at[pl.ds(off, SC_WINDOW)])

    return gather(table, idx)


def _combine_kernel(x1_ref, gates_ref, gfin_ref, *refs, tc):
    row_refs, o_ref = refs[:TOP_K], refs[TOP_K]
    g = gates_ref[...]
    gk = [jnp.broadcast_to(g[:, kk:kk + 1], (tc, LANES)) for kk in range(TOP_K)]
    zs = {}
    ssq = jnp.zeros((tc, 1), F32)
    for s in range(ROW_TILE):
        c_lo, c_hi = s * LANES, HALF_D + s * LANES
        z_lo = x1_ref[:, c_lo:c_lo + LANES]
        z_hi = x1_ref[:, c_hi:c_hi + LANES]
        for kk in range(TOP_K):
            a, b = _unpack_bf16_pair(row_refs[kk][pl.ds(s, tc, stride=ROW_TILE), :])
            z_lo = z_lo + gk[kk] * a
            z_hi = z_hi + gk[kk] * b
        zs[c_lo], zs[c_hi] = z_lo, z_hi
        ssq = ssq + jnp.sum(z_lo * z_lo + z_hi * z_hi, axis=-1, keepdims=True)
    inv = lax.rsqrt(ssq * (1.0 / D_MODEL) + NORM_EPS)
    for c0, z in zs.items():
        o_ref[:, c0:c0 + LANES] = z * inv * gfin_ref[:, c0:c0 + LANES]


def _combine(x1, gates, gathered, norm_final_g):
    t = x1.shape[0]
    tc = TC
    row = lambda i: (i, 0)
    slot_specs = [pl.BlockSpec((None, tc * ROW_TILE, LANES), functools.partial(lambda i, kk: (kk, i, 0), kk=kk))
                  for kk in range(TOP_K)]
    return pl.pallas_call(
        functools.partial(_combine_kernel, tc=tc),
        grid=(t // tc,),
        in_specs=[pl.BlockSpec((tc, D_MODEL), row), pl.BlockSpec((tc, LANES), row),
                  pl.BlockSpec((1, D_MODEL), lambda i: (0, 0)), *slot_specs],
        out_specs=pl.BlockSpec((tc, D_MODEL), row),
        out_shape=jax.ShapeDtypeStruct((t, D_MODEL), F32),
        compiler_params=_cparams("arbitrary"),
        name="moe_combine",
    )(x1, gates, norm_final_g.reshape(1, D_MODEL), *([gathered] * TOP_K))


def _moe(x1, h2t, ri, gates, counts_f, wgu_bf16, bgu, wd_bf16, bd, norm_final_g):
    t = x1.shape[0]
    a = t * TOP_K
    cap = a + N_EXPERTS * BM
    nblk = cap // BM
    counts = counts_f[0, :N_EXPERTS].astype(jnp.int32)
    padded = ((counts + BM - 1) // BM) * BM
    pend = jnp.cumsum(padded)
    pstart = pend - padded
    nvalid = (pend[-1] // BM).reshape(1)
    first_row = jnp.minimum(jnp.arange(nblk, dtype=jnp.int32) * BM, pend[-1] - 1)
    blk_e = jnp.sum(pend[None, :] <= first_row[:, None], axis=1).astype(jnp.int32)
    idx, rank = ri[:, :TOP_K], ri[:, TOP_K:2 * TOP_K]
    onehot = idx[:, :, None] == jnp.arange(N_EXPERTS, dtype=jnp.int32)[None, None, :]
    dest = rank + jnp.sum(jnp.where(onehot, pstart[None, None, :], 0), axis=-1)
    xs = _dispatch(h2t, dest.reshape(t // TG, 1, TG * TOP_K), pstart, counts, pend, cap)
    ys = _experts(xs, blk_e, nvalid, wgu_bf16, bgu, wd_bf16, bd)
    piece = dest.T[:, :, None] * ROW_TILE + jnp.arange(ROW_TILE, dtype=jnp.int32)[None, None, :]
    gathered = _sc_gather_rows(ys, piece.reshape(TOP_K * t * ROW_TILE))
    return _combine(x1, gates, gathered.reshape(TOP_K, t * ROW_TILE, LANES), norm_final_g)


def _encoder(x, p):
    b, s, _ = x.shape
    x2d = x.reshape(b * s, D_MODEL)
    cos_t, sin_t = _rope_tables(s)
    (qa, ka, va, qr, kr, vr, gr), by_residue = _inproj(x2d, p["norm_mix_g"], p["w_in"], cos_t, sin_t, s)
    outs, stats = [], []
    for d in DILATIONS:
        if d == 1:
            o, st = _banded_attention(*[z.reshape(b, s, WIDTH) for z in (qa, ka, va)])
            outs.append(o.reshape(b * s, WIDTH))
            stats.append(st.reshape(b * s, LANES))
        else:
            o, st = _banded_attention(*[z.reshape(b * d, s // d, WIDTH) for z in by_residue[d]])
            outs.append(o.reshape(b, d, s // d, WIDTH))
            stats.append(st.reshape(b, d, s // d, LANES))
    ret = _retention(qr, kr, vr, gr, p["ret_norm_g"], b, s)
    x1, h2t, ri, gates, counts = _outproj_router(x2d, outs, stats, ret, p["w_out"], p["norm_ffn_g"],
                                                 p["w_router"], p["b_router"], s)
    y = _moe(x1, h2t, ri, gates, counts, p["w_gate_up"], p["b_gate_up"], p["w_down"], p["b_down"], p["norm_final_g"])
    return y.reshape(b, s, D_MODEL)


def kernel(x_prompt, x_sample, norm_mix_g, w_in, ret_norm_g, w_out, norm_ffn_g, w_router, b_router, w_gate_up, b_gate_up, w_down, b_down, norm_final_g):
    assert norm_mix_g.shape[0] == 1, "single layer"
    p = dict(norm_mix_g=norm_mix_g[0], w_in=w_in[0].astype(BF16), ret_norm_g=ret_norm_g[0],
             w_out=w_out[0].astype(BF16), norm_ffn_g=norm_ffn_g[0], w_router=w_router[0], b_router=b_router[0],
             w_gate_up=w_gate_up[0], b_gate_up=b_gate_up[0], w_down=w_down[0],
             b_down=b_down[0], norm_final_g=norm_final_g)
    return (_encoder(x_prompt, p), _encoder(x_sample, p))
```

```python
import functools

import jax
import jax.numpy as jnp
from jax import lax
from jax.experimental import pallas as pl
from jax.experimental.pallas import tpu as pltpu
from jax.experimental.pallas import tpu_sc as plsc

D_MODEL = 1024
HEAD_DIM = 64
N_HEADS = 8
WIDTH = N_HEADS * HEAD_DIM
N_SLABS = 7
DILATIONS = (1, 4, 16)
HALF_SPAN = 64
ROPE_THETA = 10000.0
RET_DECAY_BASE_FWD = 5.0
RET_DECAY_BASE_BWD = 5.5
N_EXPERTS = 32
TOP_K = 4
EXPERT_FF = D_MODEL
SWIGLU_LIMIT = 7.0
SWIGLU_ALPHA = 1.702
NORM_EPS = 1e-6
NEG_INF = -1e30

LANES = 128
VMEM_LIMIT_BYTES = 48 * 1024 * 1024

TM_INPROJ = 512
TQ = 128
TL_ATTN = 512
HEAD_GROUP = 4
RET_CHUNK = 256
TM_OUT = 512
TG = 512
BM = 512
TC = 512
DMA_UNROLL = 4

F32 = jnp.float32
BF16 = jnp.bfloat16


def _cparams(*sem):
    return pltpu.CompilerParams(dimension_semantics=sem, vmem_limit_bytes=VMEM_LIMIT_BYTES)


_ROTATE = (True, True, False, True, True, False, False)
_SCALE = (HEAD_DIM ** -0.5, 1.0, 1.0, 1.0, HEAD_DIM ** -0.5, 1.0, 1.0)


N_ATTN_SLABS = 3
RESIDUE_DILATIONS = tuple(d for d in DILATIONS if d > 1)
REGROUP = 4
assert RESIDUE_DILATIONS == (REGROUP, REGROUP ** 2)


def _inproj_kernel(x_ref, g_ref, w_ref, cos_ref, sin_ref, *refs):
    out_refs = refs[:N_SLABS]
    res_refs = refs[N_SLABS:N_SLABS + N_ATTN_SLABS * len(RESIDUE_DILATIONS)]
    stage, stage4 = refs[-2:]
    x = x_ref[...]
    tm = x.shape[0]
    h = (x * lax.rsqrt(jnp.mean(x * x, axis=-1, keepdims=True) + NORM_EPS) * g_ref[...]).astype(BF16)
    cos = cos_ref[...]
    sin = sin_ref[...]
    lane = lax.broadcasted_iota(jnp.int32, cos.shape, 1)
    first_half = (lane & (HEAD_DIM - 1)) < HEAD_DIM // 2
    for j, o_ref in enumerate(out_refs):
        p = jnp.dot(h, w_ref[:, j * WIDTH:(j + 1) * WIDTH], preferred_element_type=F32)
        for c in range(WIDTH // LANES):
            cols = slice(c * LANES, (c + 1) * LANES)
            r = p[:, cols]
            if _ROTATE[j]:
                partner = jnp.where(first_half, pltpu.roll(r, LANES - HEAD_DIM // 2, 1), pltpu.roll(r, HEAD_DIM // 2, 1))
                r = r * cos + partner * sin
                if _SCALE[j] != 1.0:
                    r = r * _SCALE[j]
            o_ref[:, cols] = r.astype(BF16)
            if j < N_ATTN_SLABS:
                stage[j, c] = r
                dst4 = res_refs[j]
                dst16 = res_refs[N_ATTN_SLABS + j]
                for bb in range(REGROUP):
                    grp = stage[j, c, pl.ds(bb, tm // REGROUP, stride=REGROUP), :]
                    dst4[0, bb, :, cols] = grp.astype(BF16)
                    stage4[j, c, bb] = grp
                    for aa in range(REGROUP):
                        sub = stage4[j, c, bb, pl.ds(aa, tm // REGROUP ** 2, stride=REGROUP), :]
                        dst16[0, bb * REGROUP + aa, :, cols] = sub.astype(BF16)


def _inproj(x2d, g, w_bf16, cos_t, sin_t, seq):
    t = x2d.shape[0]
    tm = min(TM_INPROJ, seq)
    pos_blocks = seq // tm
    b = t // seq
    out = jax.ShapeDtypeStruct((t, WIDTH), BF16)
    row = lambda i: (i, 0)
    out_specs = [pl.BlockSpec((tm, WIDTH), row)] * N_SLABS
    out_shape = [out] * N_SLABS
    for d in RESIDUE_DILATIONS:
        out_specs += [pl.BlockSpec((1, d, tm // d, WIDTH), lambda i: (i // pos_blocks, 0, i % pos_blocks, 0))] * N_ATTN_SLABS
        out_shape += [jax.ShapeDtypeStruct((b, d, seq // d, WIDTH), BF16)] * N_ATTN_SLABS
    res = pl.pallas_call(
        _inproj_kernel,
        grid=(t // tm,),
        in_specs=[
            pl.BlockSpec((tm, D_MODEL), row),
            pl.BlockSpec((1, D_MODEL), lambda i: (0, 0)),
            pl.BlockSpec((D_MODEL, N_SLABS * WIDTH), lambda i: (0, 0)),
            pl.BlockSpec((tm, LANES), lambda i: (i % pos_blocks, 0)),
            pl.BlockSpec((tm, LANES), lambda i: (i % pos_blocks, 0)),
        ],
        out_specs=out_specs,
        out_shape=out_shape,
        scratch_shapes=[pltpu.VMEM((N_ATTN_SLABS, WIDTH // LANES, tm, LANES), F32),
                        pltpu.VMEM((N_ATTN_SLABS, WIDTH // LANES, REGROUP, tm // REGROUP, LANES), F32)],
        compiler_params=_cparams("arbitrary"),
        name="inproj",
    )(x2d, g.reshape(1, D_MODEL), w_bf16, cos_t, sin_t)
    natural = res[:N_SLABS]
    by_residue = {d: res[N_SLABS + di * N_ATTN_SLABS:N_SLABS + (di + 1) * N_ATTN_SLABS]
                  for di, d in enumerate(RESIDUE_DILATIONS)}
    return natural, by_residue


def _rope_tables(seq):
    half = HEAD_DIM // 2
    inv_freq = ROPE_THETA ** (-jnp.arange(0, HEAD_DIM, 2, dtype=F32) / HEAD_DIM)
    ang = jnp.arange(seq, dtype=F32)[:, None] * inv_freq[None, :]
    cos, sin = jnp.cos(ang), jnp.sin(ang)
    reps = LANES // HEAD_DIM
    cos_t = jnp.tile(jnp.concatenate([cos, cos], axis=1), (1, reps))
    sin_t = jnp.tile(jnp.concatenate([-sin, sin], axis=1), (1, reps))
    assert cos_t.shape == (seq, LANES) and half * 2 == HEAD_DIM
    return cos_t, sin_t


def _attn_kernel(q_ref, kp_ref, kc_ref, kn_ref, vp_ref, vc_ref, vn_ref, o_ref, st_ref, kbuf, vbuf, *, gb, tl, sub_len):
    i = pl.program_id(1)
    hs = HALF_SPAN
    for gi in range(gb):
        kbuf[gi, 0:hs] = kp_ref[gi]
        kbuf[gi, hs:hs + tl] = kc_ref[gi]
        kbuf[gi, hs + tl:hs + tl + hs] = kn_ref[gi]
        vbuf[gi, 0:hs] = vp_ref[gi]
        vbuf[gi, hs:hs + tl] = vc_ref[gi]
        vbuf[gi, hs + tl:hs + tl + hs] = vn_ref[gi]
    tk = TQ + 2 * hs
    gw = HEAD_GROUP * HEAD_DIM
    qi = lax.broadcasted_iota(jnp.int32, (TQ, tk), 0)
    kj = lax.broadcasted_iota(jnp.int32, (TQ, tk), 1)
    band = (kj >= qi) & (kj - qi <= 2 * hs)
    lane = lax.broadcasted_iota(jnp.int32, (TQ, LANES), 1)
    head_of_lane = lax.broadcasted_iota(jnp.int32, (TQ, gw), 1) // HEAD_DIM
    for gi, sub in [(gi, sub) for gi in range(gb) for sub in range(tl // TQ)]:
        a = sub * TQ
        kpos = kj + (i * tl + a - hs)
        bias = jnp.where(band & (kpos >= 0) & (kpos < sub_len), 0.0, NEG_INF).astype(F32)
        bias = jnp.concatenate([bias] * HEAD_GROUP, axis=0)
        st = jnp.zeros((TQ, LANES), F32)
        for g in range(N_HEADS // HEAD_GROUP):
            cols = slice(g * gw, (g + 1) * gw)
            q4 = q_ref[gi, a:a + TQ, cols]
            k4 = kbuf[gi, a:a + tk, cols]
            v4 = vbuf[gi, a:a + tk, cols]
            lhs = jnp.concatenate([jnp.where(head_of_lane == h, q4, jnp.zeros_like(q4)) for h in range(HEAD_GROUP)], axis=0)
            s = lax.dot_general(lhs, k4, (((1,), (1,)), ((), ())), preferred_element_type=F32) + bias
            m = jnp.max(s, axis=-1, keepdims=True)
            p = jnp.exp(s - m)
            l = jnp.sum(p, axis=-1, keepdims=True)
            o_all = jnp.dot(p.astype(BF16), v4, preferred_element_type=F32) / l
            lse = m + jnp.log(l)
            o = o_all[0:TQ]
            for h in range(HEAD_GROUP):
                rows = slice(h * TQ, (h + 1) * TQ)
                if h:
                    o = jnp.where(head_of_lane == h, o_all[rows], o)
                st = jnp.where(lane == g * HEAD_GROUP + h, lse[rows], st)
            o_ref[gi, a:a + TQ, cols] = o.astype(BF16)
        st_ref[gi, a:a + TQ, :] = st


def _banded_attention(q, k, v):
    g, sub_len, _ = q.shape
    tl = min(TL_ATTN, sub_len)
    gb = min(TL_ATTN // tl, g)
    hs = HALF_SPAN
    per = tl // hs
    last = sub_len // hs - 1
    cur = pl.BlockSpec((gb, tl, WIDTH), lambda b, i: (b, i, 0))
    prev = pl.BlockSpec((gb, hs, WIDTH), lambda b, i: (b, jnp.maximum(i * per - 1, 0), 0))
    nxt = pl.BlockSpec((gb, hs, WIDTH), lambda b, i: (b, jnp.minimum((i + 1) * per, last), 0))
    return pl.pallas_call(
        functools.partial(_attn_kernel, gb=gb, tl=tl, sub_len=sub_len),
        grid=(g // gb, sub_len // tl),
        in_specs=[cur, prev, cur, nxt, prev, cur, nxt],
        out_specs=[cur, pl.BlockSpec((gb, tl, LANES), lambda b, i: (b, i, 0))],
        out_shape=[jax.ShapeDtypeStruct((g, sub_len, WIDTH), BF16),
                   jax.ShapeDtypeStruct((g, sub_len, LANES), F32)],
        scratch_shapes=[pltpu.VMEM((gb, tl + 2 * hs, WIDTH), BF16), pltpu.VMEM((gb, tl + 2 * hs, WIDTH), BF16)],
        compiler_params=_cparams("arbitrary", "arbitrary"),
        name="banded_attention",
    )(q, k, k, k, v, v, v)


def _to_residues(z, b, s, d):
    w = z.shape[-1]
    if d == 1:
        return z.reshape(b, s, w)
    return z.reshape(b, s // d, d, w).transpose(0, 2, 1, 3).reshape(b * d, s // d, w)


def _from_residues(z, b, s, d):
    w = z.shape[-1]
    if d == 1:
        return z.reshape(b * s, w)
    return z.reshape(b, d, s // d, w).transpose(0, 2, 1, 3).reshape(b * s, w)


def _ret_kernel(q_ref, k_ref, v_ref, gate_ref, dmat_ref, wq_ref, wk_ref, dec_ref, gn_ref, o_ref, sf_scr, sb_scr, *, c, n_chunks):
    pair = 2 * HEAD_DIM
    lane = lax.broadcasted_iota(jnp.int32, (c, pair), 1)
    head0 = lane < HEAD_DIM
    blk_r = lax.broadcasted_iota(jnp.int32, (pair, pair), 0) // HEAD_DIM
    blk_c = lax.broadcasted_iota(jnp.int32, (pair, pair), 1) // HEAD_DIM
    same_head = blk_r == blk_c
    dec_f = dec_ref[0:1, :]
    dec_b = dec_ref[1:2, :]
    tn = (((0,), (0,)), ((), ()))
    nt = (((1,), (1,)), ((), ()))

    def rows_of(n):
        return pl.ds(pl.multiple_of(n * c, c), c)

    def kv_body(n, carry):
        rows = rows_of(n)
        kf32 = k_ref[0, rows, :].astype(F32)
        kw = jnp.concatenate([(kf32 * wk_ref[:, :pair]).astype(BF16), (kf32 * wk_ref[:, pair:]).astype(BF16)], axis=1)
        kv = lax.dot_general(kw, v_ref[0, rows, :], tn, preferred_element_type=F32)
        sf_scr[n] = jnp.where(same_head, kv[:pair], 0.0)
        sb_scr[n] = jnp.where(same_head, kv[pair:], 0.0)
        return carry

    lax.fori_loop(0, n_chunks, kv_body, 0, unroll=2)

    def scan_body(t, carry):
        sf, sb = carry
        nb = n_chunks - 1 - t
        kv_f = sf_scr[t]
        kv_b = sb_scr[nb]
        sf_scr[t] = sf
        sb_scr[nb] = sb
        return sf * dec_f + kv_f, sb * dec_b + kv_b

    zero = jnp.zeros((pair, pair), F32)
    lax.fori_loop(0, n_chunks, scan_body, (zero, zero))

    def out_body(n, carry):
        rows = rows_of(n)
        q = q_ref[0, rows, :]
        k = k_ref[0, rows, :]
        v = v_ref[0, rows, :]
        qf32 = q.astype(F32)
        intra = []
        for hh in range(2):
            mask = head0 if hh == 0 else jnp.logical_not(head0)
            kh = jnp.where(mask, k, jnp.zeros_like(k))
            s = lax.dot_general(q, kh, nt, preferred_element_type=F32)
            a = (s * dmat_ref[hh]).astype(BF16)
            intra.append(jnp.dot(a, v, preferred_element_type=F32))
        qw = jnp.concatenate([(qf32 * wq_ref[:, :pair]).astype(BF16), (qf32 * wq_ref[:, pair:]).astype(BF16)], axis=1)
        states = jnp.concatenate([sf_scr[n].astype(BF16), sb_scr[n].astype(BF16)], axis=0)
        tot = jnp.where(head0, intra[0], intra[1]) + jnp.dot(qw, states, preferred_element_type=F32)
        inv = 1.0 / HEAD_DIM
        s0 = jnp.sum(jnp.where(head0, tot, 0.0), axis=-1, keepdims=True)
        s1 = jnp.sum(jnp.where(head0, 0.0, tot), axis=-1, keepdims=True)
        xc = tot - jnp.where(head0, s0, s1) * inv
        sq = xc * xc
        v0 = jnp.sum(jnp.where(head0, sq, 0.0), axis=-1, keepdims=True)
        v1 = jnp.sum(jnp.where(head0, 0.0, sq), axis=-1, keepdims=True)
        y = xc * lax.rsqrt(jnp.where(head0, v0, v1) * inv + NORM_EPS)
        gt = gate_ref[0, rows, :].astype(F32)
        y = y * gn_ref[...] * (gt / (1.0 + jnp.exp(-gt)))
        o_ref[0, rows, :] = y.astype(BF16)
        return carry

    lax.fori_loop(0, n_chunks, out_body, 0, unroll=2)


def _retention_tables(c):
    hidx = jnp.arange(N_HEADS, dtype=F32)
    lg_f = jnp.log1p(-jnp.exp2(-(RET_DECAY_BASE_FWD + hidx)))
    lg_b = jnp.log1p(-jnp.exp2(-(RET_DECAY_BASE_BWD + hidx)))
    pos = jnp.arange(c, dtype=F32)
    diff = pos[:, None] - pos[None, :]
    dm_f = jnp.exp(jnp.maximum(diff, 0.0)[None] * lg_f[:, None, None])
    dm_b = jnp.exp(jnp.maximum(-diff, 0.0)[None] * lg_b[:, None, None])
    dmat = jnp.where((diff >= 0)[None], dm_f, dm_b)

    def per_lane(tab):
        t = jnp.repeat(tab[:, :, None], HEAD_DIM, axis=2)
        return t.reshape(N_HEADS // 2, 2, c, HEAD_DIM).transpose(0, 2, 1, 3).reshape(N_HEADS // 2, c, 2 * HEAD_DIM)

    wq_f = per_lane(jnp.exp((pos + 1.0)[None, :] * lg_f[:, None]))
    wq_b = per_lane(jnp.exp((c - pos)[None, :] * lg_b[:, None]))
    wk_f = per_lane(jnp.exp((c - 1.0 - pos)[None, :] * lg_f[:, None]))
    wk_b = per_lane(jnp.exp(pos[None, :] * lg_b[:, None]))
    wq = jnp.concatenate([wq_f, wq_b], axis=2)
    wk = jnp.concatenate([wk_f, wk_b], axis=2)
    dec = jnp.stack([jnp.repeat(jnp.exp(c * lg_f), HEAD_DIM), jnp.repeat(jnp.exp(c * lg_b), HEAD_DIM)], axis=0)
    dec = dec.reshape(2, N_HEADS // 2, 2 * HEAD_DIM).transpose(1, 0, 2)
    return dmat, wq, wk, dec


def _retention(q, k, v, gate, ret_norm_g, b, s):
    c = min(RET_CHUNK, s)
    n_chunks = s // c
    pair = 2 * HEAD_DIM
    dmat, wq, wk, dec = _retention_tables(c)
    seq_blk = pl.BlockSpec((1, s, pair), lambda bi, hp: (bi, 0, hp))
    r3 = lambda z: z.reshape(b, s, WIDTH)
    return pl.pallas_call(
        functools.partial(_ret_kernel, c=c, n_chunks=n_chunks),
        grid=(b, N_HEADS // 2),
        in_specs=[seq_blk, seq_blk, seq_blk, seq_blk,
                  pl.BlockSpec((2, c, c), lambda bi, hp: (hp, 0, 0)),
                  pl.BlockSpec((None, c, 2 * pair), lambda bi, hp: (hp, 0, 0)),
                  pl.BlockSpec((None, c, 2 * pair), lambda bi, hp: (hp, 0, 0)),
                  pl.BlockSpec((None, 2, pair), lambda bi, hp: (hp, 0, 0)),
                  pl.BlockSpec((1, pair), lambda bi, hp: (0, hp))],
        out_specs=seq_blk,
        out_shape=jax.ShapeDtypeStruct((b, s, WIDTH), BF16),
        scratch_shapes=[pltpu.VMEM((n_chunks, pair, pair), F32), pltpu.VMEM((n_chunks, pair, pair), F32)],
        compiler_params=_cparams("arbitrary", "arbitrary"),
        name="retention",
    )(r3(q), r3(k), r3(v), r3(gate), dmat, wq, wk, dec, ret_norm_g.reshape(1, WIDTH)).reshape(b * s, WIDTH)


ROW_TILE = 4
HALF_D = D_MODEL // 2
assert ROW_TILE * LANES == HALF_D


def _pack_bf16_pair(a, b):
    ua = pltpu.bitcast(a.astype(BF16).astype(F32), jnp.uint32)
    ub = pltpu.bitcast(b.astype(BF16).astype(F32), jnp.uint32)
    return ua | (ub >> 16)


def _unpack_bf16_pair(u):
    return pltpu.bitcast(u & jnp.uint32(0xFFFF0000), F32), pltpu.bitcast(u << 16, F32)


def _store_row_tiles(ref, val):
    n = val.shape[0]
    for s in range(ROW_TILE):
        lo = slice(s * LANES, (s + 1) * LANES)
        hi = slice(HALF_D + s * LANES, HALF_D + (s + 1) * LANES)
        ref[pl.ds(s, n, stride=ROW_TILE), :] = _pack_bf16_pair(val[:, lo], val[:, hi])


def _load_row_tiles(ref, n):
    first, second = [], []
    for s in range(ROW_TILE):
        a, b = _unpack_bf16_pair(ref[pl.ds(s, n, stride=ROW_TILE), :])
        first.append(a)
        second.append(b)
    return first, second


def _split_bf16(x):
    hi = x.astype(BF16)
    return hi, (x - hi.astype(F32)).astype(BF16)


def _outproj_kernel(x_ref, o1_ref, o2_ref, o3_ref, s1_ref, s2_ref, s3_ref, ret_ref, wout_ref, expand_ref,
                    gffn_ref, wr_ref, br_ref, tri_ref,
                    x1_ref, h2t_ref, ri_ref, gates_ref, cnt_ref, base_scr, nat_o, nat_s, grp_o):
    i = pl.program_id(0)
    tm = x_ref.shape[0]

    @pl.when(i == 0)
    def _():
        base_scr[...] = jnp.zeros_like(base_scr)

    o_nat = [[o1_ref[:, c * LANES:(c + 1) * LANES].astype(F32) for c in range(WIDTH // LANES)]]
    sts = [s1_ref[...]]
    n_col = WIDTH // LANES
    for bb in range(REGROUP):
        rows4 = pl.ds(bb, tm // REGROUP, stride=REGROUP)
        for c in range(n_col):
            nat_o[0, c, rows4, :] = o2_ref[0, bb, :, c * LANES:(c + 1) * LANES].astype(F32)
        nat_s[0, rows4, :] = s2_ref[0, bb]
        for aa in range(REGROUP):
            rows16 = pl.ds(aa, tm // REGROUP ** 2, stride=REGROUP)
            for c in range(n_col):
                grp_o[c, bb, rows16, :] = o3_ref[0, bb * REGROUP + aa, :, c * LANES:(c + 1) * LANES].astype(F32)
            grp_o[n_col, bb, rows16, :] = s3_ref[0, bb * REGROUP + aa]
        for c in range(n_col):
            nat_o[1, c, rows4, :] = grp_o[c, bb]
        nat_s[1, rows4, :] = grp_o[n_col, bb]
    for di in range(len(RESIDUE_DILATIONS)):
        o_nat.append([nat_o[di, c] for c in range(n_col)])
        sts.append(nat_s[di])

    mx = jnp.maximum(jnp.maximum(sts[0], sts[1]), sts[2])
    es = [jnp.exp(st - mx) for st in sts]
    den = es[0] + es[1] + es[2]
    attn = None
    for e, o_cols in zip(es, o_nat):
        hi, lo = _split_bf16(e / den)
        w_full = jnp.dot(jnp.concatenate([hi, lo], axis=1), expand_ref[...], preferred_element_type=F32)
        term = w_full * jnp.concatenate(o_cols, axis=1)
        attn = term if attn is None else attn + term
    mixed = jnp.concatenate([attn.astype(BF16), ret_ref[...]], axis=1)
    x1 = x_ref[...] + jnp.dot(mixed, wout_ref[...], preferred_element_type=F32)
    x1_ref[...] = x1

    h2 = x1 * lax.rsqrt(jnp.mean(x1 * x1, axis=-1, keepdims=True) + NORM_EPS) * gffn_ref[...]
    _store_row_tiles(h2t_ref, h2)

    hi, lo = _split_bf16(h2)
    logits = jnp.dot(jnp.concatenate([hi, lo, hi], axis=1), wr_ref[...], preferred_element_type=F32) + br_ref[...]
    tm = logits.shape[0]
    lane = lax.broadcasted_iota(jnp.int32, (tm, LANES), 1).astype(F32)
    work = logits
    vals, idxs = [], []
    onehot = jnp.zeros((tm, LANES), F32)
    for _k in range(TOP_K):
        mk = jnp.max(work, axis=-1, keepdims=True)
        ik = jnp.min(jnp.where(work == mk, lane, float(LANES)), axis=-1, keepdims=True)
        sel = lane == ik
        onehot = jnp.where(sel, 1.0, onehot)
        work = jnp.where(sel, -jnp.inf, work)
        vals.append(mk)
        idxs.append(ik)
    ex = [jnp.exp(vk - vals[0]) for vk in vals]
    tot = ex[0] + ex[1] + ex[2] + ex[3]
    before = jnp.dot(tri_ref[...], onehot.astype(BF16), preferred_element_type=F32) + base_scr[...]
    ri = jnp.zeros((tm, LANES), F32)
    gt = jnp.zeros((tm, LANES), F32)
    for kk in range(TOP_K):
        rank = jnp.sum(jnp.where(lane == idxs[kk], before, 0.0), axis=-1, keepdims=True)
        ri = jnp.where(lane == float(kk), idxs[kk], ri)
        ri = jnp.where(lane == float(TOP_K + kk), rank, ri)
        gt = jnp.where(lane == float(kk), ex[kk] / tot, gt)
    ri_ref[...] = ri.astype(jnp.int32)
    gates_ref[...] = gt
    new_base = base_scr[...] + jnp.sum(onehot, axis=0, keepdims=True)
    base_scr[...] = new_base
    cnt_ref[...] = new_base


def _outproj_router(x2d, outs, stats, ret, wout_bf16, norm_ffn_g, w_router, b_router, seq):
    t = x2d.shape[0]
    tm = TM_OUT
    per_seq = seq // tm
    row = lambda i: (i, 0)

    def res_spec(d, width):
        return pl.BlockSpec((1, d, tm // d, width), lambda i: (i // per_seq, 0, i % per_seq, 0))

    const = lambda i: (0, 0)
    head_of_col = jnp.arange(WIDTH) // HEAD_DIM
    expand = (jnp.arange(LANES)[:, None] == head_of_col[None, :]).astype(BF16)
    expand2 = jnp.concatenate([expand, expand], axis=0)
    wr = jnp.zeros((D_MODEL, LANES), F32).at[:, :N_EXPERTS].set(w_router)
    wr_hi, wr_lo = _split_bf16(wr)
    wr3 = jnp.concatenate([wr_hi, wr_hi, wr_lo], axis=0)
    br = jnp.full((1, LANES), NEG_INF, F32).at[0, :N_EXPERTS].set(b_router)
    tri = (jnp.arange(tm)[:, None] > jnp.arange(tm)[None, :]).astype(BF16)
    o_spec = pl.BlockSpec((tm, WIDTH), row)
    s_spec = pl.BlockSpec((tm, LANES), row)
    return pl.pallas_call(
        _outproj_kernel,
        grid=(t // tm,),
        in_specs=[pl.BlockSpec((tm, D_MODEL), row),
                  o_spec, *[res_spec(d, WIDTH) for d in RESIDUE_DILATIONS],
                  s_spec, *[res_spec(d, LANES) for d in RESIDUE_DILATIONS], o_spec,
                  pl.BlockSpec((D_MODEL, D_MODEL), const), pl.BlockSpec((2 * LANES, WIDTH), const),
                  pl.BlockSpec((1, D_MODEL), const), pl.BlockSpec((3 * D_MODEL, LANES), const),
                  pl.BlockSpec((1, LANES), const), pl.BlockSpec((tm, tm), const)],
        out_specs=[pl.BlockSpec((tm, D_MODEL), row), pl.BlockSpec((tm * ROW_TILE, LANES), row),
                   s_spec, s_spec, pl.BlockSpec((1, LANES), const)],
        out_shape=[jax.ShapeDtypeStruct((t, D_MODEL), F32), jax.ShapeDtypeStruct((t * ROW_TILE, LANES), jnp.uint32),
                   jax.ShapeDtypeStruct((t, LANES), jnp.int32), jax.ShapeDtypeStruct((t, LANES), F32),
                   jax.ShapeDtypeStruct((1, LANES), F32)],
        scratch_shapes=[pltpu.VMEM((1, LANES), F32),
                        pltpu.VMEM((len(RESIDUE_DILATIONS), WIDTH // LANES, tm, LANES), F32),
                        pltpu.VMEM((len(RESIDUE_DILATIONS), tm, LANES), F32),
                        pltpu.VMEM((WIDTH // LANES + 1, REGROUP, tm // REGROUP, LANES), F32)],
        compiler_params=_cparams("arbitrary"),
        name="outproj_router",
    )(x2d, *outs, *stats, ret, wout_bf16, expand2, norm_ffn_g.reshape(1, D_MODEL), wr3, br, tri)


SC_CORES = 2
SC_SUBCORES = 16
SC_WINDOW = 128


def _sc_mesh():
    return plsc.VectorSubcoreMesh(core_axis_name="c", subcore_axis_name="s")


def _sc_scatter_rows(src, idx, n_out_rows):
    n_rows = src.shape[0]
    workers = SC_CORES * SC_SUBCORES
    per_worker = n_rows // workers
    n_win = per_worker // SC_WINDOW
    assert per_worker * workers == n_rows and n_win * SC_WINDOW == per_worker and idx.shape == (TOP_K * n_rows,)

    @functools.partial(
        pl.kernel, mesh=_sc_mesh(),
        out_type=jax.ShapeDtypeStruct((n_out_rows, LANES), src.dtype),
        scratch_types=[pltpu.VMEM((SC_WINDOW,), jnp.int32), pltpu.VMEM((SC_WINDOW, LANES), src.dtype)],
        name="sc_scatter_rows",
    )
    def scatter(src_hbm, idx_hbm, out_hbm, idx_v, rows_v):
        wid = lax.axis_index("s") * SC_CORES + lax.axis_index("c")
        base = wid * per_worker

        @pl.loop(0, n_win)
        def _(j):
            off = pl.multiple_of(base + j * SC_WINDOW, SC_WINDOW)
            pltpu.sync_copy(src_hbm.at[pl.ds(off, SC_WINDOW)], rows_v)
            for kk in range(TOP_K):
                pltpu.sync_copy(idx_hbm.at[pl.ds(pl.multiple_of(kk * n_rows + off, SC_WINDOW), SC_WINDOW)], idx_v)
                pltpu.sync_copy(rows_v, out_hbm.at[idx_v])

    return scatter(src, idx)


def _expert_kernel(blk_e_ref, nvalid_ref, rows_ref, xs_ref, wgu_ref, bgu_ref, wd_ref, bd_ref, ys_ref, wgu_bf, wd_bf):
    i = pl.program_id(0)

    @pl.when((i == 0) | (blk_e_ref[i] != blk_e_ref[jnp.maximum(i - 1, 0)]))
    def _():
        wgu_bf[...] = wgu_ref[0].astype(BF16)
        wd_bf[...] = wd_ref[0].astype(BF16)

    @pl.when(i < nvalid_ref[0])
    def _():
        first, second = _load_row_tiles(xs_ref, BM)
        live = lax.broadcasted_iota(jnp.int32, (BM, LANES), 0) < rows_ref[i]
        x = jnp.concatenate([jnp.where(live, p, 0.0).astype(BF16) for p in first + second], axis=1)
        gu = jnp.dot(x, wgu_bf[...], preferred_element_type=F32) + bgu_ref[0]
        gate = jnp.minimum(gu[:, :EXPERT_FF], SWIGLU_LIMIT)
        up = jnp.clip(gu[:, EXPERT_FF:], -SWIGLU_LIMIT, SWIGLU_LIMIT)
        act = gate * (1.0 / (1.0 + jnp.exp(-SWIGLU_ALPHA * gate))) * (up + 1.0)
        y = jnp.dot(act.astype(BF16), wd_bf[...], preferred_element_type=F32) + bd_ref[0]
        _store_row_tiles(ys_ref, y)


def _experts(xs, blk_e, nvalid, live_rows, wgu_bf16, bgu, wd_bf16, bd):
    cap = xs.shape[0] // ROW_TILE
    nblk = cap // BM

    def blk(i, be, nv, lr):
        return (jnp.minimum(i, nv[0] - 1), 0)

    def by_expert(i, be, nv, lr):
        return (be[i], 0, 0)

    return pl.pallas_call(
        _expert_kernel,
        grid_spec=pltpu.PrefetchScalarGridSpec(
            num_scalar_prefetch=3,
            grid=(nblk,),
            in_specs=[pl.BlockSpec((BM * ROW_TILE, LANES), blk),
                      pl.BlockSpec((1, D_MODEL, 2 * EXPERT_FF), by_expert),
                      pl.BlockSpec((1, 1, 2 * EXPERT_FF), by_expert),
                      pl.BlockSpec((1, EXPERT_FF, D_MODEL), by_expert),
                      pl.BlockSpec((1, 1, D_MODEL), by_expert)],
            out_specs=pl.BlockSpec((BM * ROW_TILE, LANES), blk),
            scratch_shapes=[pltpu.VMEM((D_MODEL, 2 * EXPERT_FF), BF16), pltpu.VMEM((EXPERT_FF, D_MODEL), BF16)],
        ),
        out_shape=jax.ShapeDtypeStruct((cap * ROW_TILE, LANES), jnp.uint32),
        compiler_params=_cparams("arbitrary"),
        name="moe_experts",
    )(blk_e, nvalid, live_rows, xs, wgu_bf16, bgu.reshape(N_EXPERTS, 1, 2 * EXPERT_FF), wd_bf16,
      bd.reshape(N_EXPERTS, 1, D_MODEL))


def _sc_gather_rows(table, idx):
    n_rows = idx.shape[0]
    workers = SC_CORES * SC_SUBCORES
    per_worker = n_rows // workers
    n_win = per_worker // SC_WINDOW
    assert per_worker * workers == n_rows and n_win * SC_WINDOW == per_worker

    @functools.partial(
        pl.kernel, mesh=_sc_mesh(),
        out_type=jax.ShapeDtypeStruct((n_rows, LANES), table.dtype),
        scratch_types=[pltpu.VMEM((SC_WINDOW,), jnp.int32), pltpu.VMEM((SC_WINDOW, LANES), table.dtype),
                       pltpu.SemaphoreType.DMA],
        name="sc_gather_rows",
    )
    def gather(table_hbm, idx_hbm, out_hbm, idx_v, rows_v, sem):
        wid = lax.axis_index("s") * SC_CORES + lax.axis_index("c")
        base = wid * per_worker

        @pl.loop(0, n_win)
        def _(j):
            off = pl.multiple_of(base + j * SC_WINDOW, SC_WINDOW)
            pltpu.sync_copy(idx_hbm.at[pl.ds(off, SC_WINDOW)], idx_v)
            pltpu.async_copy(table_hbm.at[idx_v], rows_v, sem).wait()
            pltpu.sync_copy(rows_v, out_hbm.at[pl.ds(off, SC_WINDOW)])

    return gather(table, idx)


def _combine_kernel(x1_ref, gates_ref, gfin_ref, *refs, tc):
    row_refs, o_ref = refs[:TOP_K], refs[TOP_K]
    g = gates_ref[...]
    gk = [jnp.broadcast_to(g[:, kk:kk + 1], (tc, LANES)) for kk in range(TOP_K)]
    zs = {}
    ssq = jnp.zeros((tc, 1), F32)
    for s in range(ROW_TILE):
        c_lo, c_hi = s * LANES, HALF_D + s * LANES
        z_lo = x1_ref[:, c_lo:c_lo + LANES]
        z_hi = x1_ref[:, c_hi:c_hi + LANES]
        for kk in range(TOP_K):
            a, b = _unpack_bf16_pair(row_refs[kk][pl.ds(s, tc, stride=ROW_TILE), :])
            z_lo = z_lo + gk[kk] * a
            z_hi = z_hi + gk[kk] * b
        zs[c_lo], zs[c_hi] = z_lo, z_hi
        ssq = ssq + jnp.sum(z_lo * z_lo + z_hi * z_hi, axis=-1, keepdims=True)
    inv = lax.rsqrt(ssq * (1.0 / D_MODEL) + NORM_EPS)
    for c0, z in zs.items():
        o_ref[:, c0:c0 + LANES] = z * inv * gfin_ref[:, c0:c0 + LANES]


def _combine(x1, gates, gathered, norm_final_g):
    t = x1.shape[0]
    tc = TC
    row = lambda i: (i, 0)
    slot_specs = [pl.BlockSpec((None, tc * ROW_TILE, LANES), functools.partial(lambda i, kk: (kk, i, 0), kk=kk))
                  for kk in range(TOP_K)]
    return pl.pallas_call(
        functools.partial(_combine_kernel, tc=tc),
        grid=(t // tc,),
        in_specs=[pl.BlockSpec((tc, D_MODEL), row), pl.BlockSpec((tc, LANES), row),
                  pl.BlockSpec((1, D_MODEL), lambda i: (0, 0)), *slot_specs],
        out_specs=pl.BlockSpec((tc, D_MODEL), row),
        out_shape=jax.ShapeDtypeStruct((t, D_MODEL), F32),
        compiler_params=_cparams("arbitrary"),
        name="moe_combine",
    )(x1, gates, norm_final_g.reshape(1, D_MODEL), *([gathered] * TOP_K))


def _moe(x1, h2t, ri, gates, counts_f, wgu_bf16, bgu, wd_bf16, bd, norm_final_g):
    t = x1.shape[0]
    a = t * TOP_K
    cap = a + N_EXPERTS * BM
    nblk = cap // BM
    counts = counts_f[0, :N_EXPERTS].astype(jnp.int32)
    padded = ((counts + BM - 1) // BM) * BM
    pend = jnp.cumsum(padded)
    pstart = pend - padded
    nvalid = (pend[-1] // BM).reshape(1)
    first_row = jnp.minimum(jnp.arange(nblk, dtype=jnp.int32) * BM, pend[-1] - 1)
    blk_e = jnp.sum(pend[None, :] <= first_row[:, None], axis=1).astype(jnp.int32)
    idx, rank = ri[:, :TOP_K], ri[:, TOP_K:2 * TOP_K]
    onehot = idx[:, :, None] == jnp.arange(N_EXPERTS, dtype=jnp.int32)[None, None, :]
    dest = rank + jnp.sum(jnp.where(onehot, pstart[None, None, :], 0), axis=-1)
    seg_end = (pstart + counts)[blk_e]
    live_rows = jnp.clip(seg_end - jnp.arange(nblk, dtype=jnp.int32) * BM, 0, BM).astype(jnp.int32)
    piece = dest.T[:, :, None] * ROW_TILE + jnp.arange(ROW_TILE, dtype=jnp.int32)[None, None, :]
    piece = piece.reshape(TOP_K * t * ROW_TILE)
    xs = _sc_scatter_rows(h2t, piece, cap * ROW_TILE)
    ys = _experts(xs, blk_e, nvalid, live_rows, wgu_bf16, bgu, wd_bf16, bd)
    gathered = _sc_gather_rows(ys, piece)
    return _combine(x1, gates, gathered.reshape(TOP_K, t * ROW_TILE, LANES), norm_final_g)


def _encoder(x, p):
    b, s, _ = x.shape
    x2d = x.reshape(b * s, D_MODEL)
    cos_t, sin_t = _rope_tables(s)
    (qa, ka, va, qr, kr, vr, gr), by_residue = _inproj(x2d, p["norm_mix_g"], p["w_in"], cos_t, sin_t, s)
    outs, stats = [], []
    for d in DILATIONS:
        if d == 1:
            o, st = _banded_attention(*[z.reshape(b, s, WIDTH) for z in (qa, ka, va)])
            outs.append(o.reshape(b * s, WIDTH))
            stats.append(st.reshape(b * s, LANES))
        else:
            o, st = _banded_attention(*[z.reshape(b * d, s // d, WIDTH) for z in by_residue[d]])
            outs.append(o.reshape(b, d, s // d, WIDTH))
            stats.append(st.reshape(b, d, s // d, LANES))
    ret = _retention(qr, kr, vr, gr, p["ret_norm_g"], b, s)
    x1, h2t, ri, gates, counts = _outproj_router(x2d, outs, stats, ret, p["w_out"], p["norm_ffn_g"],
                                                 p["w_router"], p["b_router"], s)
    y = _moe(x1, h2t, ri, gates, counts, p["w_gate_up"], p["b_gate_up"], p["w_down"], p["b_down"], p["norm_final_g"])
    return y.reshape(b, s, D_MODEL)


def kernel(x_prompt, x_sample, norm_mix_g, w_in, ret_norm_g, w_out, norm_ffn_g, w_router, b_router, w_gate_up, b_gate_up, w_down, b_down, norm_final_g):
    assert norm_mix_g.shape[0] == 1, "single layer"
    p = dict(norm_mix_g=norm_mix_g[0], w_in=w_in[0].astype(BF16), ret_norm_g=ret_norm_g[0],
             w_out=w_out[0].astype(BF16), norm_ffn_g=norm_ffn_g[0], w_router=w_router[0], b_router=b_router[0],
             w_gate_up=w_gate_up[0], b_gate_up=b_gate_up[0], w_down=w_down[0],
             b_down=b_down[0], norm_final_g=norm_final_g)
    return (_encoder(x_prompt, p), _encoder(x_sample, p))
```

```python
import functools

import jax
import jax.numpy as jnp
from jax import lax
from jax.experimental import pallas as pl
from jax.experimental.pallas import tpu as pltpu
from jax.experimental.pallas import tpu_sc as plsc

D_MODEL = 1024
HEAD_DIM = 64
N_HEADS = 8
WIDTH = N_HEADS * HEAD_DIM
N_SLABS = 7
DILATIONS = (1, 4, 16)
HALF_SPAN = 64
ROPE_THETA = 10000.0
RET_DECAY_BASE_FWD = 5.0
RET_DECAY_BASE_BWD = 5.5
N_EXPERTS = 32
TOP_K = 4
EXPERT_FF = D_MODEL
SWIGLU_LIMIT = 7.0
SWIGLU_ALPHA = 1.702
NORM_EPS = 1e-6
NEG_INF = -1e30

LANES = 128
VMEM_LIMIT_BYTES = 48 * 1024 * 1024

TM_INPROJ = 512
TQ = 128
TL_ATTN = 512
HEAD_GROUP = 4
RET_CHUNK = 256
TM_OUT = 512
OUT_SPLIT = 2
RET_UNROLL = 4
BM = 512
TC = 512

F32 = jnp.float32
BF16 = jnp.bfloat16


def _cparams(*sem):
    return pltpu.CompilerParams(dimension_semantics=sem, vmem_limit_bytes=VMEM_LIMIT_BYTES)


_ROTATE = (True, True, False, True, True, False, False)
_SCALE = (HEAD_DIM ** -0.5, 1.0, 1.0, 1.0, HEAD_DIM ** -0.5, 1.0, 1.0)


N_ATTN_SLABS = 3
RESIDUE_DILATIONS = tuple(d for d in DILATIONS if d > 1)
REGROUP = 4
assert RESIDUE_DILATIONS == (REGROUP, REGROUP ** 2)


def _inproj_kernel(x_ref, g_ref, w_ref, cos_ref, sin_ref, *refs):
    out_refs = refs[:N_SLABS]
    res_refs = refs[N_SLABS:N_SLABS + N_ATTN_SLABS * len(RESIDUE_DILATIONS)]
    stage, stage4 = refs[-2:]
    x = x_ref[...]
    tm = x.shape[0]
    h = (x * lax.rsqrt(jnp.mean(x * x, axis=-1, keepdims=True) + NORM_EPS) * g_ref[...]).astype(BF16)
    cos = cos_ref[...]
    sin = sin_ref[...]
    lane = lax.broadcasted_iota(jnp.int32, cos.shape, 1)
    first_half = (lane & (HEAD_DIM - 1)) < HEAD_DIM // 2
    for j, o_ref in enumerate(out_refs):
        p = jnp.dot(h, w_ref[:, j * WIDTH:(j + 1) * WIDTH], preferred_element_type=F32)
        for c in range(WIDTH // LANES):
            cols = slice(c * LANES, (c + 1) * LANES)
            r = p[:, cols]
            if _ROTATE[j]:
                partner = jnp.where(first_half, pltpu.roll(r, LANES - HEAD_DIM // 2, 1), pltpu.roll(r, HEAD_DIM // 2, 1))
                r = r * cos + partner * sin
                if _SCALE[j] != 1.0:
                    r = r * _SCALE[j]
            o_ref[:, cols] = r.astype(BF16)
            if j < N_ATTN_SLABS:
                stage[j, c] = r
                dst4 = res_refs[j]
                dst16 = res_refs[N_ATTN_SLABS + j]
                for bb in range(REGROUP):
                    grp = stage[j, c, pl.ds(bb, tm // REGROUP, stride=REGROUP), :]
                    dst4[0, bb, :, cols] = grp.astype(BF16)
                    stage4[j, c, bb] = grp
                    for aa in range(REGROUP):
                        sub = stage4[j, c, bb, pl.ds(aa, tm // REGROUP ** 2, stride=REGROUP), :]
                        dst16[0, bb * REGROUP + aa, :, cols] = sub.astype(BF16)


def _inproj(x2d, g, w_bf16, cos_t, sin_t, seq):
    t = x2d.shape[0]
    tm = min(TM_INPROJ, seq)
    pos_blocks = seq // tm
    b = t // seq
    out = jax.ShapeDtypeStruct((t, WIDTH), BF16)
    row = lambda i: (i, 0)
    out_specs = [pl.BlockSpec((tm, WIDTH), row)] * N_SLABS
    out_shape = [out] * N_SLABS
    for d in RESIDUE_DILATIONS:
        out_specs += [pl.BlockSpec((1, d, tm // d, WIDTH), lambda i: (i // pos_blocks, 0, i % pos_blocks, 0))] * N_ATTN_SLABS
        out_shape += [jax.ShapeDtypeStruct((b, d, seq // d, WIDTH), BF16)] * N_ATTN_SLABS
    res = pl.pallas_call(
        _inproj_kernel,
        grid=(t // tm,),
        in_specs=[
            pl.BlockSpec((tm, D_MODEL), row),
            pl.BlockSpec((1, D_MODEL), lambda i: (0, 0)),
            pl.BlockSpec((D_MODEL, N_SLABS * WIDTH), lambda i: (0, 0)),
            pl.BlockSpec((tm, LANES), lambda i: (i % pos_blocks, 0)),
            pl.BlockSpec((tm, LANES), lambda i: (i % pos_blocks, 0)),
        ],
        out_specs=out_specs,
        out_shape=out_shape,
        scratch_shapes=[pltpu.VMEM((N_ATTN_SLABS, WIDTH // LANES, tm, LANES), F32),
                        pltpu.VMEM((N_ATTN_SLABS, WIDTH // LANES, REGROUP, tm // REGROUP, LANES), F32)],
        compiler_params=_cparams("arbitrary"),
        name="inproj",
    )(x2d, g.reshape(1, D_MODEL), w_bf16, cos_t, sin_t)
    natural = res[:N_SLABS]
    by_residue = {d: res[N_SLABS + di * N_ATTN_SLABS:N_SLABS + (di + 1) * N_ATTN_SLABS]
                  for di, d in enumerate(RESIDUE_DILATIONS)}
    return natural, by_residue


def _rope_tables(seq):
    half = HEAD_DIM // 2
    inv_freq = ROPE_THETA ** (-jnp.arange(0, HEAD_DIM, 2, dtype=F32) / HEAD_DIM)
    ang = jnp.arange(seq, dtype=F32)[:, None] * inv_freq[None, :]
    cos, sin = jnp.cos(ang), jnp.sin(ang)
    reps = LANES // HEAD_DIM
    cos_t = jnp.tile(jnp.concatenate([cos, cos], axis=1), (1, reps))
    sin_t = jnp.tile(jnp.concatenate([-sin, sin], axis=1), (1, reps))
    assert cos_t.shape == (seq, LANES) and half * 2 == HEAD_DIM
    return cos_t, sin_t


def _attn_kernel(q_ref, kp_ref, kc_ref, kn_ref, vp_ref, vc_ref, vn_ref, o_ref, st_ref, kbuf, vbuf, *, gb, tl, sub_len):
    i = pl.program_id(1)
    hs = HALF_SPAN
    for gi in range(gb):
        kbuf[gi, 0:hs] = kp_ref[gi]
        kbuf[gi, hs:hs + tl] = kc_ref[gi]
        kbuf[gi, hs + tl:hs + tl + hs] = kn_ref[gi]
        vbuf[gi, 0:hs] = vp_ref[gi]
        vbuf[gi, hs:hs + tl] = vc_ref[gi]
        vbuf[gi, hs + tl:hs + tl + hs] = vn_ref[gi]
    tk = TQ + 2 * hs
    gw = HEAD_GROUP * HEAD_DIM
    qi = lax.broadcasted_iota(jnp.int32, (TQ, tk), 0)
    kj = lax.broadcasted_iota(jnp.int32, (TQ, tk), 1)
    band = (kj >= qi) & (kj - qi <= 2 * hs)
    lane = lax.broadcasted_iota(jnp.int32, (TQ, LANES), 1)
    head_of_lane = lax.broadcasted_iota(jnp.int32, (TQ, gw), 1) // HEAD_DIM
    for gi, sub in [(gi, sub) for gi in range(gb) for sub in range(tl // TQ)]:
        a = sub * TQ
        kpos = kj + (i * tl + a - hs)
        bias = jnp.where(band & (kpos >= 0) & (kpos < sub_len), 0.0, NEG_INF).astype(F32)
        bias = jnp.concatenate([bias] * HEAD_GROUP, axis=0)
        st = jnp.zeros((TQ, LANES), F32)
        for g in range(N_HEADS // HEAD_GROUP):
            cols = slice(g * gw, (g + 1) * gw)
            q4 = q_ref[gi, a:a + TQ, cols]
            k4 = kbuf[gi, a:a + tk, cols]
            v4 = vbuf[gi, a:a + tk, cols]
            lhs = jnp.concatenate([jnp.where(head_of_lane == h, q4, jnp.zeros_like(q4)) for h in range(HEAD_GROUP)], axis=0)
            s = lax.dot_general(lhs, k4, (((1,), (1,)), ((), ())), preferred_element_type=F32) + bias
            m = jnp.max(s, axis=-1, keepdims=True)
            p = jnp.exp(s - m)
            l = jnp.sum(p, axis=-1, keepdims=True)
            o_all = jnp.dot(p.astype(BF16), v4, preferred_element_type=F32) / l
            lse = m + jnp.log(l)
            o = o_all[0:TQ]
            for h in range(HEAD_GROUP):
                rows = slice(h * TQ, (h + 1) * TQ)
                if h:
                    o = jnp.where(head_of_lane == h, o_all[rows], o)
                st = jnp.where(lane == g * HEAD_GROUP + h, lse[rows], st)
            o_ref[gi, a:a + TQ, cols] = o.astype(BF16)
        st_ref[gi, a:a + TQ, :] = st


def _banded_attention(q, k, v):
    g, sub_len, _ = q.shape
    tl = min(TL_ATTN, sub_len)
    gb = min(TL_ATTN // tl, g)
    hs = HALF_SPAN
    per = tl // hs
    last = sub_len // hs - 1
    cur = pl.BlockSpec((gb, tl, WIDTH), lambda b, i: (b, i, 0))
    prev = pl.BlockSpec((gb, hs, WIDTH), lambda b, i: (b, jnp.maximum(i * per - 1, 0), 0))
    nxt = pl.BlockSpec((gb, hs, WIDTH), lambda b, i: (b, jnp.minimum((i + 1) * per, last), 0))
    return pl.pallas_call(
        functools.partial(_attn_kernel, gb=gb, tl=tl, sub_len=sub_len),
        grid=(g // gb, sub_len // tl),
        in_specs=[cur, prev, cur, nxt, prev, cur, nxt],
        out_specs=[cur, pl.BlockSpec((gb, tl, LANES), lambda b, i: (b, i, 0))],
        out_shape=[jax.ShapeDtypeStruct((g, sub_len, WIDTH), BF16),
                   jax.ShapeDtypeStruct((g, sub_len, LANES), F32)],
        scratch_shapes=[pltpu.VMEM((gb, tl + 2 * hs, WIDTH), BF16), pltpu.VMEM((gb, tl + 2 * hs, WIDTH), BF16)],
        compiler_params=_cparams("arbitrary", "arbitrary"),
        name="banded_attention",
    )(q, k, k, k, v, v, v)


def _ret_kernel(q_ref, k_ref, v_ref, gate_ref, dmat_ref, wq_ref, wk_ref, dec_ref, gn_ref, o_ref, sf_scr, sb_scr, *, c, n_chunks):
    pair = 2 * HEAD_DIM
    lane = lax.broadcasted_iota(jnp.int32, (c, pair), 1)
    head0 = lane < HEAD_DIM
    blk_r = lax.broadcasted_iota(jnp.int32, (pair, pair), 0) // HEAD_DIM
    blk_c = lax.broadcasted_iota(jnp.int32, (pair, pair), 1) // HEAD_DIM
    same_head = blk_r == blk_c
    dec_f = dec_ref[0:1, :]
    dec_b = dec_ref[1:2, :]
    tn = (((0,), (0,)), ((), ()))
    nt = (((1,), (1,)), ((), ()))

    def rows_of(n):
        return pl.ds(pl.multiple_of(n * c, c), c)

    def kv_body(n, carry):
        rows = rows_of(n)
        kf32 = k_ref[0, rows, :].astype(F32)
        kw = jnp.concatenate([(kf32 * wk_ref[:, :pair]).astype(BF16), (kf32 * wk_ref[:, pair:]).astype(BF16)], axis=1)
        kv = lax.dot_general(kw, v_ref[0, rows, :], tn, preferred_element_type=F32)
        sf_scr[n] = jnp.where(same_head, kv[:pair], 0.0)
        sb_scr[n] = jnp.where(same_head, kv[pair:], 0.0)
        return carry

    lax.fori_loop(0, n_chunks, kv_body, 0, unroll=RET_UNROLL)

    def scan_body(t, carry):
        sf, sb = carry
        nb = n_chunks - 1 - t
        kv_f = sf_scr[t]
        kv_b = sb_scr[nb]
        sf_scr[t] = sf
        sb_scr[nb] = sb
        return sf * dec_f + kv_f, sb * dec_b + kv_b

    zero = jnp.zeros((pair, pair), F32)
    lax.fori_loop(0, n_chunks, scan_body, (zero, zero))

    def out_body(n, carry):
        rows = rows_of(n)
        q = q_ref[0, rows, :]
        k = k_ref[0, rows, :]
        v = v_ref[0, rows, :]
        qf32 = q.astype(F32)
        intra = []
        for hh in range(2):
            mask = head0 if hh == 0 else jnp.logical_not(head0)
            kh = jnp.where(mask, k, jnp.zeros_like(k))
            s = lax.dot_general(q, kh, nt, preferred_element_type=F32)
            a = (s * dmat_ref[hh]).astype(BF16)
            intra.append(jnp.dot(a, v, preferred_element_type=F32))
        qw = jnp.concatenate([(qf32 * wq_ref[:, :pair]).astype(BF16), (qf32 * wq_ref[:, pair:]).astype(BF16)], axis=1)
        states = jnp.concatenate([sf_scr[n].astype(BF16), sb_scr[n].astype(BF16)], axis=0)
        tot = jnp.where(head0, intra[0], intra[1]) + jnp.dot(qw, states, preferred_element_type=F32)
        inv = 1.0 / HEAD_DIM
        s0 = jnp.sum(jnp.where(head0, tot, 0.0), axis=-1, keepdims=True)
        s1 = jnp.sum(jnp.where(head0, 0.0, tot), axis=-1, keepdims=True)
        xc = tot - jnp.where(head0, s0, s1) * inv
        sq = xc * xc
        v0 = jnp.sum(jnp.where(head0, sq, 0.0), axis=-1, keepdims=True)
        v1 = jnp.sum(jnp.where(head0, 0.0, sq), axis=-1, keepdims=True)
        y = xc * lax.rsqrt(jnp.where(head0, v0, v1) * inv + NORM_EPS)
        gt = gate_ref[0, rows, :].astype(F32)
        y = y * gn_ref[...] * (gt / (1.0 + jnp.exp(-gt)))
        o_ref[0, rows, :] = y.astype(BF16)
        return carry

    lax.fori_loop(0, n_chunks, out_body, 0, unroll=RET_UNROLL)


def _retention_tables(c):
    hidx = jnp.arange(N_HEADS, dtype=F32)
    lg_f = jnp.log1p(-jnp.exp2(-(RET_DECAY_BASE_FWD + hidx)))
    lg_b = jnp.log1p(-jnp.exp2(-(RET_DECAY_BASE_BWD + hidx)))
    pos = jnp.arange(c, dtype=F32)
    diff = pos[:, None] - pos[None, :]
    dm_f = jnp.exp(jnp.maximum(diff, 0.0)[None] * lg_f[:, None, None])
    dm_b = jnp.exp(jnp.maximum(-diff, 0.0)[None] * lg_b[:, None, None])
    dmat = jnp.where((diff >= 0)[None], dm_f, dm_b)

    def per_lane(tab):
        t = jnp.repeat(tab[:, :, None], HEAD_DIM, axis=2)
        return t.reshape(N_HEADS // 2, 2, c, HEAD_DIM).transpose(0, 2, 1, 3).reshape(N_HEADS // 2, c, 2 * HEAD_DIM)

    wq_f = per_lane(jnp.exp((pos + 1.0)[None, :] * lg_f[:, None]))
    wq_b = per_lane(jnp.exp((c - pos)[None, :] * lg_b[:, None]))
    wk_f = per_lane(jnp.exp((c - 1.0 - pos)[None, :] * lg_f[:, None]))
    wk_b = per_lane(jnp.exp(pos[None, :] * lg_b[:, None]))
    wq = jnp.concatenate([wq_f, wq_b], axis=2)
    wk = jnp.concatenate([wk_f, wk_b], axis=2)
    dec = jnp.stack([jnp.repeat(jnp.exp(c * lg_f), HEAD_DIM), jnp.repeat(jnp.exp(c * lg_b), HEAD_DIM)], axis=0)
    dec = dec.reshape(2, N_HEADS // 2, 2 * HEAD_DIM).transpose(1, 0, 2)
    return dmat, wq, wk, dec


def _retention(q, k, v, gate, ret_norm_g, b, s):
    c = min(RET_CHUNK, s)
    n_chunks = s // c
    pair = 2 * HEAD_DIM
    dmat, wq, wk, dec = _retention_tables(c)
    seq_blk = pl.BlockSpec((1, s, pair), lambda bi, hp: (bi, 0, hp))
    r3 = lambda z: z.reshape(b, s, WIDTH)
    return pl.pallas_call(
        functools.partial(_ret_kernel, c=c, n_chunks=n_chunks),
        grid=(b, N_HEADS // 2),
        in_specs=[seq_blk, seq_blk, seq_blk, seq_blk,
                  pl.BlockSpec((2, c, c), lambda bi, hp: (hp, 0, 0)),
                  pl.BlockSpec((None, c, 2 * pair), lambda bi, hp: (hp, 0, 0)),
                  pl.BlockSpec((None, c, 2 * pair), lambda bi, hp: (hp, 0, 0)),
                  pl.BlockSpec((None, 2, pair), lambda bi, hp: (hp, 0, 0)),
                  pl.BlockSpec((1, pair), lambda bi, hp: (0, hp))],
        out_specs=seq_blk,
        out_shape=jax.ShapeDtypeStruct((b, s, WIDTH), BF16),
        scratch_shapes=[pltpu.VMEM((n_chunks, pair, pair), F32), pltpu.VMEM((n_chunks, pair, pair), F32)],
        compiler_params=_cparams("arbitrary", "arbitrary"),
        name="retention",
    )(r3(q), r3(k), r3(v), r3(gate), dmat, wq, wk, dec, ret_norm_g.reshape(1, WIDTH)).reshape(b * s, WIDTH)


N_PIECES = 4
HALF_D = D_MODEL // 2
assert N_PIECES * LANES == HALF_D


def _pack_bf16_pair(a, b):
    ua = pltpu.bitcast(a.astype(BF16).astype(F32), jnp.uint32)
    ub = pltpu.bitcast(b.astype(BF16).astype(F32), jnp.uint32)
    return ua | (ub >> 16)


def _unpack_bf16_pair(u):
    return pltpu.bitcast(u & jnp.uint32(0xFFFF0000), F32), pltpu.bitcast(u << 16, F32)


def _store_pieces(ref, val):
    for s in range(N_PIECES):
        lo = slice(s * LANES, (s + 1) * LANES)
        hi = slice(HALF_D + s * LANES, HALF_D + (s + 1) * LANES)
        ref[s] = _pack_bf16_pair(val[:, lo], val[:, hi])


def _load_pieces(ref):
    first, second = [], []
    for s in range(N_PIECES):
        a, b = _unpack_bf16_pair(ref[s])
        first.append(a)
        second.append(b)
    return first, second


def _split_bf16(x):
    hi = x.astype(BF16)
    return hi, (x - hi.astype(F32)).astype(BF16)


def _outproj_kernel(x_ref, o1_ref, o2_ref, o3_ref, s1_ref, s2_ref, s3_ref, ret_ref, wout_ref, expand_ref,
                    gffn_ref, wr_ref, br_ref, tri_ref,
                    x1_ref, h2t_ref, ri_ref, gates_ref, cnt_ref, base_scr, nat_o, nat_s, grp_o):
    i = pl.program_id(0)
    tm = x_ref.shape[0]

    @pl.when(i == 0)
    def _():
        base_scr[...] = jnp.zeros_like(base_scr)

    n_col = WIDTH // LANES
    for bb in range(REGROUP):
        rows4 = pl.ds(bb, tm // REGROUP, stride=REGROUP)
        for c in range(n_col):
            nat_o[0, c, rows4, :] = o2_ref[0, bb, :, c * LANES:(c + 1) * LANES].astype(F32)
        nat_s[0, rows4, :] = s2_ref[0, bb]
        for aa in range(REGROUP):
            rows16 = pl.ds(aa, tm // REGROUP ** 2, stride=REGROUP)
            for c in range(n_col):
                grp_o[c, bb, rows16, :] = o3_ref[0, bb * REGROUP + aa, :, c * LANES:(c + 1) * LANES].astype(F32)
            grp_o[n_col, bb, rows16, :] = s3_ref[0, bb * REGROUP + aa]
        for c in range(n_col):
            nat_o[1, c, rows4, :] = grp_o[c, bb]
        nat_s[1, rows4, :] = grp_o[n_col, bb]
    hm = tm // OUT_SPLIT
    lane = lax.broadcasted_iota(jnp.int32, (hm, LANES), 1).astype(F32)
    base = base_scr[...]
    for r0 in range(0, tm, hm):
        rows = slice(r0, r0 + hm)
        o_nat = [[o1_ref[rows, c * LANES:(c + 1) * LANES].astype(F32) for c in range(n_col)]]
        sts = [s1_ref[rows, :]]
        for di in range(len(RESIDUE_DILATIONS)):
            o_nat.append([nat_o[di, c, rows, :] for c in range(n_col)])
            sts.append(nat_s[di, rows, :])

        mx = jnp.maximum(jnp.maximum(sts[0], sts[1]), sts[2])
        es = [jnp.exp(st - mx) for st in sts]
        den = es[0] + es[1] + es[2]
        attn = None
        for e, o_cols in zip(es, o_nat):
            hi, lo = _split_bf16(e / den)
            w_full = jnp.dot(jnp.concatenate([hi, lo], axis=1), expand_ref[...], preferred_element_type=F32)
            term = w_full * jnp.concatenate(o_cols, axis=1)
            attn = term if attn is None else attn + term
        mixed = jnp.concatenate([attn.astype(BF16), ret_ref[rows, :]], axis=1)
        x1 = x_ref[rows, :] + jnp.dot(mixed, wout_ref[...], preferred_element_type=F32)
        x1_ref[rows, :] = x1

        h2 = x1 * lax.rsqrt(jnp.mean(x1 * x1, axis=-1, keepdims=True) + NORM_EPS) * gffn_ref[...]
        for s in range(N_PIECES):
            h2t_ref[s, rows, :] = _pack_bf16_pair(h2[:, s * LANES:(s + 1) * LANES],
                                                  h2[:, HALF_D + s * LANES:HALF_D + (s + 1) * LANES])

        hi, lo = _split_bf16(h2)
        logits = jnp.dot(jnp.concatenate([hi, lo, hi], axis=1), wr_ref[...], preferred_element_type=F32) + br_ref[...]
        work = logits
        vals, idxs = [], []
        onehot = jnp.zeros((hm, LANES), F32)
        for _k in range(TOP_K):
            mk = jnp.max(work, axis=-1, keepdims=True)
            ik = jnp.min(jnp.where(work == mk, lane, float(LANES)), axis=-1, keepdims=True)
            sel = lane == ik
            onehot = jnp.where(sel, 1.0, onehot)
            work = jnp.where(sel, -jnp.inf, work)
            vals.append(mk)
            idxs.append(ik)
        ex = [jnp.exp(vk - vals[0]) for vk in vals]
        tot = ex[0] + ex[1] + ex[2] + ex[3]
        before = jnp.dot(tri_ref[...], onehot.astype(BF16), preferred_element_type=F32) + base
        ri = jnp.zeros((hm, LANES), F32)
        gt = jnp.zeros((hm, LANES), F32)
        for kk in range(TOP_K):
            rank = jnp.sum(jnp.where(lane == idxs[kk], before, 0.0), axis=-1, keepdims=True)
            ri = jnp.where(lane == float(kk), idxs[kk], ri)
            ri = jnp.where(lane == float(TOP_K + kk), rank, ri)
            gt = jnp.where(lane == float(kk), ex[kk] / tot, gt)
        ri_ref[:, rows] = jnp.transpose(ri)[:2 * TOP_K].astype(jnp.int32)
        gates_ref[rows, :] = gt
        base = base + jnp.sum(onehot, axis=0, keepdims=True)
    base_scr[...] = base
    cnt_ref[...] = base


def _outproj_router(x2d, outs, stats, ret, wout_bf16, norm_ffn_g, w_router, b_router, seq):
    t = x2d.shape[0]
    tm = TM_OUT
    per_seq = seq // tm
    row = lambda i: (i, 0)

    def res_spec(d, width):
        return pl.BlockSpec((1, d, tm // d, width), lambda i: (i // per_seq, 0, i % per_seq, 0))

    const = lambda i: (0, 0)
    head_of_col = jnp.arange(WIDTH) // HEAD_DIM
    expand = (jnp.arange(LANES)[:, None] == head_of_col[None, :]).astype(BF16)
    expand2 = jnp.concatenate([expand, expand], axis=0)
    wr = jnp.zeros((D_MODEL, LANES), F32).at[:, :N_EXPERTS].set(w_router)
    wr_hi, wr_lo = _split_bf16(wr)
    wr3 = jnp.concatenate([wr_hi, wr_hi, wr_lo], axis=0)
    br = jnp.full((1, LANES), NEG_INF, F32).at[0, :N_EXPERTS].set(b_router)
    hm = tm // OUT_SPLIT
    tri = (jnp.arange(hm)[:, None] > jnp.arange(hm)[None, :]).astype(BF16)
    o_spec = pl.BlockSpec((tm, WIDTH), row)
    s_spec = pl.BlockSpec((tm, LANES), row)
    return pl.pallas_call(
        _outproj_kernel,
        grid=(t // tm,),
        in_specs=[pl.BlockSpec((tm, D_MODEL), row),
                  o_spec, *[res_spec(d, WIDTH) for d in RESIDUE_DILATIONS],
                  s_spec, *[res_spec(d, LANES) for d in RESIDUE_DILATIONS], o_spec,
                  pl.BlockSpec((D_MODEL, D_MODEL), const), pl.BlockSpec((2 * LANES, WIDTH), const),
                  pl.BlockSpec((1, D_MODEL), const), pl.BlockSpec((3 * D_MODEL, LANES), const),
                  pl.BlockSpec((1, LANES), const), pl.BlockSpec((hm, hm), const)],
        out_specs=[pl.BlockSpec((tm, D_MODEL), row), pl.BlockSpec((N_PIECES, tm, LANES), lambda i: (0, i, 0)),
                   pl.BlockSpec((2 * TOP_K, tm), lambda i: (0, i)), s_spec, pl.BlockSpec((1, LANES), const)],
        out_shape=[jax.ShapeDtypeStruct((t, D_MODEL), F32), jax.ShapeDtypeStruct((N_PIECES, t, LANES), jnp.uint32),
                   jax.ShapeDtypeStruct((2 * TOP_K, t), jnp.int32), jax.ShapeDtypeStruct((t, LANES), F32),
                   jax.ShapeDtypeStruct((1, LANES), F32)],
        scratch_shapes=[pltpu.VMEM((1, LANES), F32),
                        pltpu.VMEM((len(RESIDUE_DILATIONS), WIDTH // LANES, tm, LANES), F32),
                        pltpu.VMEM((len(RESIDUE_DILATIONS), tm, LANES), F32),
                        pltpu.VMEM((WIDTH // LANES + 1, REGROUP, tm // REGROUP, LANES), F32)],
        compiler_params=_cparams("arbitrary"),
        name="outproj_router",
    )(x2d, *outs, *stats, ret, wout_bf16, expand2, norm_ffn_g.reshape(1, D_MODEL), wr3, br, tri)


SC_CORES = 2
SC_SUBCORES = 16
SC_WINDOW = 128


def _sc_mesh():
    return plsc.VectorSubcoreMesh(core_axis_name="c", subcore_axis_name="s")


def _sc_scatter_rows(src, idx, n_out_rows):
    n_rows = src.shape[0]
    workers = SC_CORES * SC_SUBCORES
    per_worker = n_rows // workers
    n_win = per_worker // SC_WINDOW
    assert per_worker * workers == n_rows and n_win * SC_WINDOW == per_worker and idx.shape == (TOP_K * n_rows,)

    @functools.partial(
        pl.kernel, mesh=_sc_mesh(),
        out_type=jax.ShapeDtypeStruct((n_out_rows, LANES), src.dtype),
        scratch_types=[pltpu.VMEM((SC_WINDOW,), jnp.int32), pltpu.VMEM((SC_WINDOW, LANES), src.dtype)],
        name="sc_scatter_rows",
    )
    def scatter(src_hbm, idx_hbm, out_hbm, idx_v, rows_v):
        wid = lax.axis_index("s") * SC_CORES + lax.axis_index("c")
        base = wid * per_worker

        @pl.loop(0, n_win)
        def _(j):
            off = pl.multiple_of(base + j * SC_WINDOW, SC_WINDOW)
            pltpu.sync_copy(src_hbm.at[pl.ds(off, SC_WINDOW)], rows_v)
            for kk in range(TOP_K):
                pltpu.sync_copy(idx_hbm.at[pl.ds(pl.multiple_of(kk * n_rows + off, SC_WINDOW), SC_WINDOW)], idx_v)
                pltpu.sync_copy(rows_v, out_hbm.at[idx_v])

    return scatter(src, idx)


def _expert_kernel(blk_e_ref, nvalid_ref, rows_ref, xs_ref, wgu_ref, bgu_ref, wd_ref, bd_ref, ys_ref, wgu_bf, wd_bf):
    i = pl.program_id(0)

    @pl.when((i == 0) | (blk_e_ref[i] != blk_e_ref[jnp.maximum(i - 1, 0)]))
    def _():
        wgu_bf[...] = wgu_ref[0].astype(BF16)
        wd_bf[...] = wd_ref[0].astype(BF16)

    @pl.when(i < nvalid_ref[0])
    def _():
        first, second = _load_pieces(xs_ref)
        live = lax.broadcasted_iota(jnp.int32, (BM, LANES), 0) < rows_ref[i]
        x = jnp.concatenate([jnp.where(live, p, 0.0).astype(BF16) for p in first + second], axis=1)
        gu = jnp.dot(x, wgu_bf[...], preferred_element_type=F32) + bgu_ref[0]
        gate = jnp.minimum(gu[:, :EXPERT_FF], SWIGLU_LIMIT)
        up = jnp.clip(gu[:, EXPERT_FF:], -SWIGLU_LIMIT, SWIGLU_LIMIT)
        act = gate * (1.0 / (1.0 + jnp.exp(-SWIGLU_ALPHA * gate))) * (up + 1.0)
        y = jnp.dot(act.astype(BF16), wd_bf[...], preferred_element_type=F32) + bd_ref[0]
        _store_pieces(ys_ref, y)


def _experts(xs, blk_e, nvalid, live_rows, wgu_bf16, bgu, wd_bf16, bd):
    cap = xs.shape[1]
    nblk = cap // BM

    def blk(i, be, nv, lr):
        return (0, jnp.minimum(i, nv[0] - 1), 0)

    def by_expert(i, be, nv, lr):
        return (be[i], 0, 0)

    return pl.pallas_call(
        _expert_kernel,
        grid_spec=pltpu.PrefetchScalarGridSpec(
            num_scalar_prefetch=3,
            grid=(nblk,),
            in_specs=[pl.BlockSpec((N_PIECES, BM, LANES), blk),
                      pl.BlockSpec((1, D_MODEL, 2 * EXPERT_FF), by_expert),
                      pl.BlockSpec((1, 1, 2 * EXPERT_FF), by_expert),
                      pl.BlockSpec((1, EXPERT_FF, D_MODEL), by_expert),
                      pl.BlockSpec((1, 1, D_MODEL), by_expert)],
            out_specs=pl.BlockSpec((N_PIECES, BM, LANES), blk),
            scratch_shapes=[pltpu.VMEM((D_MODEL, 2 * EXPERT_FF), BF16), pltpu.VMEM((EXPERT_FF, D_MODEL), BF16)],
        ),
        out_shape=jax.ShapeDtypeStruct((N_PIECES, cap, LANES), jnp.uint32),
        compiler_params=_cparams("arbitrary"),
        name="moe_experts",
    )(blk_e, nvalid, live_rows, xs, wgu_bf16, bgu.reshape(N_EXPERTS, 1, 2 * EXPERT_FF), wd_bf16,
      bd.reshape(N_EXPERTS, 1, D_MODEL))


def _sc_gather_rows(table, idx):
    n_rows = idx.shape[0]
    workers = SC_CORES * SC_SUBCORES
    per_worker = n_rows // workers
    n_win = per_worker // SC_WINDOW
    assert per_worker * workers == n_rows and n_win * SC_WINDOW == per_worker

    @functools.partial(
        pl.kernel, mesh=_sc_mesh(),
        out_type=jax.ShapeDtypeStruct((n_rows, LANES), table.dtype),
        scratch_types=[pltpu.VMEM((SC_WINDOW,), jnp.int32), pltpu.VMEM((SC_WINDOW, LANES), table.dtype),
                       pltpu.SemaphoreType.DMA],
        name="sc_gather_rows",
    )
    def gather(table_hbm, idx_hbm, out_hbm, idx_v, rows_v, sem):
        wid = lax.axis_index("s") * SC_CORES + lax.axis_index("c")
        base = wid * per_worker

        @pl.loop(0, n_win)
        def _(j):
            off = pl.multiple_of(base + j * SC_WINDOW, SC_WINDOW)
            pltpu.sync_copy(idx_hbm.at[pl.ds(off, SC_WINDOW)], idx_v)
            pltpu.async_copy(table_hbm.at[idx_v], rows_v, sem).wait()
            pltpu.sync_copy(rows_v, out_hbm.at[pl.ds(off, SC_WINDOW)])

    return gather(table, idx)


def _combine_kernel(x1_ref, gates_ref, gfin_ref, *refs, tc):
    piece_refs, o_ref = refs[:TOP_K * N_PIECES], refs[TOP_K * N_PIECES]
    g = gates_ref[...]
    gk = [jnp.broadcast_to(g[:, kk:kk + 1], (tc, LANES)) for kk in range(TOP_K)]
    zs = {}
    ssq = jnp.zeros((tc, 1), F32)
    for s in range(N_PIECES):
        c_lo, c_hi = s * LANES, HALF_D + s * LANES
        z_lo = x1_ref[:, c_lo:c_lo + LANES]
        z_hi = x1_ref[:, c_hi:c_hi + LANES]
        for kk in range(TOP_K):
            a, b = _unpack_bf16_pair(piece_refs[kk * N_PIECES + s][...])
            z_lo = z_lo + gk[kk] * a
            z_hi = z_hi + gk[kk] * b
        zs[c_lo], zs[c_hi] = z_lo, z_hi
        ssq = ssq + jnp.sum(z_lo * z_lo + z_hi * z_hi, axis=-1, keepdims=True)
    inv = lax.rsqrt(ssq * (1.0 / D_MODEL) + NORM_EPS)
    for c0, z in zs.items():
        o_ref[:, c0:c0 + LANES] = z * inv * gfin_ref[:, c0:c0 + LANES]


def _combine(x1, gates, gathered, norm_final_g):
    t = x1.shape[0]
    tc = TC
    row = lambda i: (i, 0)
    slot_specs = [pl.BlockSpec((None, tc, LANES), functools.partial(lambda i, j: (j, i, 0), j=j))
                  for j in range(TOP_K * N_PIECES)]
    return pl.pallas_call(
        functools.partial(_combine_kernel, tc=tc),
        grid=(t // tc,),
        in_specs=[pl.BlockSpec((tc, D_MODEL), row), pl.BlockSpec((tc, LANES), row),
                  pl.BlockSpec((1, D_MODEL), lambda i: (0, 0)), *slot_specs],
        out_specs=pl.BlockSpec((tc, D_MODEL), row),
        out_shape=jax.ShapeDtypeStruct((t, D_MODEL), F32),
        compiler_params=_cparams("arbitrary"),
        name="moe_combine",
    )(x1, gates, norm_final_g.reshape(1, D_MODEL), *([gathered] * (TOP_K * N_PIECES)))


def _moe(x1, h2t, ri, gates, counts_f, wgu_bf16, bgu, wd_bf16, bd, norm_final_g):
    t = x1.shape[0]
    a = t * TOP_K
    cap = a + N_EXPERTS * BM
    nblk = cap // BM
    counts = counts_f[0, :N_EXPERTS].astype(jnp.int32)
    padded = ((counts + BM - 1) // BM) * BM
    pend = jnp.cumsum(padded)
    pstart = pend - padded
    nvalid = (pend[-1] // BM).reshape(1)
    first_row = jnp.minimum(jnp.arange(nblk, dtype=jnp.int32) * BM, pend[-1] - 1)
    blk_e = jnp.sum(pend[None, :] <= first_row[:, None], axis=1).astype(jnp.int32)
    idx, rank = ri[:TOP_K], ri[TOP_K:2 * TOP_K]
    onehot = idx[None, :, :] == jnp.arange(N_EXPERTS, dtype=jnp.int32)[:, None, None]
    dest = rank + jnp.sum(jnp.where(onehot, pstart[:, None, None], 0), axis=0)
    seg_end = (pstart + counts)[blk_e]
    live_rows = jnp.clip(seg_end - jnp.arange(nblk, dtype=jnp.int32) * BM, 0, BM).astype(jnp.int32)
    piece = dest[:, None, :] + (jnp.arange(N_PIECES, dtype=jnp.int32) * cap)[None, :, None]
    piece = piece.reshape(TOP_K * N_PIECES * t)
    xs = _sc_scatter_rows(h2t.reshape(N_PIECES * t, LANES), piece, N_PIECES * cap)
    ys = _experts(xs.reshape(N_PIECES, cap, LANES), blk_e, nvalid, live_rows, wgu_bf16, bgu, wd_bf16, bd)
    gathered = _sc_gather_rows(ys.reshape(N_PIECES * cap, LANES), piece)
    return _combine(x1, gates, gathered.reshape(TOP_K * N_PIECES, t, LANES), norm_final_g)


def _encoder(x, p):
    b, s, _ = x.shape
    x2d = x.reshape(b * s, D_MODEL)
    cos_t, sin_t = _rope_tables(s)
    (qa, ka, va, qr, kr, vr, gr), by_residue = _inproj(x2d, p["norm_mix_g"], p["w_in"], cos_t, sin_t, s)
    outs, stats = [], []
    for d in DILATIONS:
        if d == 1:
            o, st = _banded_attention(*[z.reshape(b, s, WIDTH) for z in (qa, ka, va)])
            outs.append(o.reshape(b * s, WIDTH))
            stats.append(st.reshape(b * s, LANES))
        else:
            o, st = _banded_attention(*[z.reshape(b * d, s // d, WIDTH) for z in by_residue[d]])
            outs.append(o.reshape(b, d, s // d, WIDTH))
            stats.append(st.reshape(b, d, s // d, LANES))
    ret = _retention(qr, kr, vr, gr, p["ret_norm_g"], b, s)
    x1, h2t, ri, gates, counts = _outproj_router(x2d, outs, stats, ret, p["w_out"], p["norm_ffn_g"],
                                                 p["w_router"], p["b_router"], s)
    y = _moe(x1, h2t, ri, gates, counts, p["w_gate_up"], p["b_gate_up"], p["w_down"], p["b_down"], p["norm_final_g"])
    return y.reshape(b, s, D_MODEL)


def kernel(x_prompt, x_sample, norm_mix_g, w_in, ret_norm_g, w_out, norm_ffn_g, w_router, b_router, w_gate_up, b_gate_up, w_down, b_down, norm_final_g):
    assert norm_mix_g.shape[0] == 1, "single layer"
    p = dict(norm_mix_g=norm_mix_g[0], w_in=w_in[0].astype(BF16), ret_norm_g=ret_norm_g[0],
             w_out=w_out[0].astype(BF16), norm_ffn_g=norm_ffn_g[0], w_router=w_router[0], b_router=b_router[0],
             w_gate_up=w_gate_up[0], b_gate_up=b_gate_up[0], w_down=w_down[0],
             b_down=b_down[0], norm_final_g=norm_final_g)
    return (_encoder(x_prompt, p), _encoder(x_sample, p))
```

```python
import functools

import jax
import jax.numpy as jnp
from jax import lax
from jax.experimental import pallas as pl
from jax.experimental.pallas import tpu as pltpu
from jax.experimental.pallas import tpu_sc as plsc

D_MODEL = 1024
HEAD_DIM = 64
N_HEADS = 8
WIDTH = N_HEADS * HEAD_DIM
N_SLABS = 7
DILATIONS = (1, 4, 16)
HALF_SPAN = 64
ROPE_THETA = 10000.0
RET_DECAY_BASE_FWD = 5.0
RET_DECAY_BASE_BWD = 5.5
N_EXPERTS = 32
TOP_K = 4
EXPERT_FF = D_MODEL
SWIGLU_LIMIT = 7.0
SWIGLU_ALPHA = 1.702
NORM_EPS = 1e-6
NEG_INF = -1e30

LANES = 128
VMEM_LIMIT_BYTES = 48 * 1024 * 1024

TM_INPROJ = 512
TQ = 128
TL_ATTN = 1024
HEAD_GROUP = 4
RET_CHUNK = 256
TM_OUT = 512
OUT_SPLIT = 2
RET_UNROLL = 4
BM = 512
EXPERT_SUB = 128
TC = 1024

F32 = jnp.float32
BF16 = jnp.bfloat16


def _cparams(*sem):
    return pltpu.CompilerParams(dimension_semantics=sem, vmem_limit_bytes=VMEM_LIMIT_BYTES)


_ROTATE = (True, True, False, True, True, False, False)
_SCALE = (HEAD_DIM ** -0.5, 1.0, 1.0, 1.0, HEAD_DIM ** -0.5, 1.0, 1.0)


N_ATTN_SLABS = 3
RESIDUE_DILATIONS = tuple(d for d in DILATIONS if d > 1)
REGROUP = 4
assert RESIDUE_DILATIONS == (REGROUP, REGROUP ** 2)


def _inproj_kernel(x_ref, g_ref, w_ref, cos_ref, sin_ref, *refs):
    out_refs = refs[:N_SLABS]
    res_refs = refs[N_SLABS:N_SLABS + N_ATTN_SLABS * len(RESIDUE_DILATIONS)]
    stage, stage4 = refs[-2:]
    x = x_ref[...]
    tm = x.shape[0]
    h = (x * lax.rsqrt(jnp.mean(x * x, axis=-1, keepdims=True) + NORM_EPS) * g_ref[...]).astype(BF16)
    cos = cos_ref[...]
    sin = sin_ref[...]
    lane = lax.broadcasted_iota(jnp.int32, cos.shape, 1)
    first_half = (lane & (HEAD_DIM - 1)) < HEAD_DIM // 2
    for j, o_ref in enumerate(out_refs):
        p = jnp.dot(h, w_ref[:, j * WIDTH:(j + 1) * WIDTH], preferred_element_type=F32)
        for c in range(WIDTH // LANES):
            cols = slice(c * LANES, (c + 1) * LANES)
            r = p[:, cols]
            if _ROTATE[j]:
                partner = jnp.where(first_half, pltpu.roll(r, LANES - HEAD_DIM // 2, 1), pltpu.roll(r, HEAD_DIM // 2, 1))
                r = r * cos + partner * sin
                if _SCALE[j] != 1.0:
                    r = r * _SCALE[j]
            o_ref[:, cols] = r.astype(BF16)
            if j < N_ATTN_SLABS:
                stage[j, c] = r
                dst4 = res_refs[j]
                dst16 = res_refs[N_ATTN_SLABS + j]
                for bb in range(REGROUP):
                    grp = stage[j, c, pl.ds(bb, tm // REGROUP, stride=REGROUP), :]
                    dst4[0, bb, :, cols] = grp.astype(BF16)
                    stage4[j, c, bb] = grp
                    for aa in range(REGROUP):
                        sub = stage4[j, c, bb, pl.ds(aa, tm // REGROUP ** 2, stride=REGROUP), :]
                        dst16[0, bb * REGROUP + aa, :, cols] = sub.astype(BF16)


def _inproj(x2d, g, w_bf16, cos_t, sin_t, seq):
    t = x2d.shape[0]
    tm = min(TM_INPROJ, seq)
    pos_blocks = seq // tm
    b = t // seq
    out = jax.ShapeDtypeStruct((t, WIDTH), BF16)
    row = lambda i: (i, 0)
    out_specs = [pl.BlockSpec((tm, WIDTH), row)] * N_SLABS
    out_shape = [out] * N_SLABS
    for d in RESIDUE_DILATIONS:
        out_specs += [pl.BlockSpec((1, d, tm // d, WIDTH), lambda i: (i // pos_blocks, 0, i % pos_blocks, 0))] * N_ATTN_SLABS
        out_shape += [jax.ShapeDtypeStruct((b, d, seq // d, WIDTH), BF16)] * N_ATTN_SLABS
    res = pl.pallas_call(
        _inproj_kernel,
        grid=(t // tm,),
        in_specs=[
            pl.BlockSpec((tm, D_MODEL), row),
            pl.BlockSpec((1, D_MODEL), lambda i: (0, 0)),
            pl.BlockSpec((D_MODEL, N_SLABS * WIDTH), lambda i: (0, 0)),
            pl.BlockSpec((tm, LANES), lambda i: (i % pos_blocks, 0)),
            pl.BlockSpec((tm, LANES), lambda i: (i % pos_blocks, 0)),
        ],
        out_specs=out_specs,
        out_shape=out_shape,
        scratch_shapes=[pltpu.VMEM((N_ATTN_SLABS, WIDTH // LANES, tm, LANES), F32),
                        pltpu.VMEM((N_ATTN_SLABS, WIDTH // LANES, REGROUP, tm // REGROUP, LANES), F32)],
        compiler_params=_cparams("arbitrary"),
        name="inproj",
    )(x2d, g.reshape(1, D_MODEL), w_bf16, cos_t, sin_t)
    natural = res[:N_SLABS]
    by_residue = {d: res[N_SLABS + di * N_ATTN_SLABS:N_SLABS + (di + 1) * N_ATTN_SLABS]
                  for di, d in enumerate(RESIDUE_DILATIONS)}
    return natural, by_residue


def _rope_tables(seq):
    half = HEAD_DIM // 2
    inv_freq = ROPE_THETA ** (-jnp.arange(0, HEAD_DIM, 2, dtype=F32) / HEAD_DIM)
    ang = jnp.arange(seq, dtype=F32)[:, None] * inv_freq[None, :]
    cos, sin = jnp.cos(ang), jnp.sin(ang)
    reps = LANES // HEAD_DIM
    cos_t = jnp.tile(jnp.concatenate([cos, cos], axis=1), (1, reps))
    sin_t = jnp.tile(jnp.concatenate([-sin, sin], axis=1), (1, reps))
    assert cos_t.shape == (seq, LANES) and half * 2 == HEAD_DIM
    return cos_t, sin_t


def _attn_kernel(q_ref, kp_ref, kc_ref, kn_ref, vp_ref, vc_ref, vn_ref, o_ref, st_ref, kbuf, vbuf, *, gb, tl, sub_len):
    i = pl.program_id(1)
    hs = HALF_SPAN
    for gi in range(gb):
        kbuf[gi, 0:hs] = kp_ref[gi]
        kbuf[gi, hs:hs + tl] = kc_ref[gi]
        kbuf[gi, hs + tl:hs + tl + hs] = kn_ref[gi]
        vbuf[gi, 0:hs] = vp_ref[gi]
        vbuf[gi, hs:hs + tl] = vc_ref[gi]
        vbuf[gi, hs + tl:hs + tl + hs] = vn_ref[gi]
    tk = TQ + 2 * hs
    gw = HEAD_GROUP * HEAD_DIM
    qi = lax.broadcasted_iota(jnp.int32, (TQ, tk), 0)
    kj = lax.broadcasted_iota(jnp.int32, (TQ, tk), 1)
    band = (kj >= qi) & (kj - qi <= 2 * hs)
    lane = lax.broadcasted_iota(jnp.int32, (TQ, LANES), 1)
    head_of_lane = lax.broadcasted_iota(jnp.int32, (TQ, gw), 1) // HEAD_DIM
    for gi, sub in [(gi, sub) for gi in range(gb) for sub in range(tl // TQ)]:
        a = sub * TQ
        kpos = kj + (i * tl + a - hs)
        bias = jnp.where(band & (kpos >= 0) & (kpos < sub_len), 0.0, NEG_INF).astype(F32)
        bias = jnp.concatenate([bias] * HEAD_GROUP, axis=0)
        st = jnp.zeros((TQ, LANES), F32)
        for g in range(N_HEADS // HEAD_GROUP):
            cols = slice(g * gw, (g + 1) * gw)
            q4 = q_ref[gi, a:a + TQ, cols]
            k4 = kbuf[gi, a:a + tk, cols]
            v4 = vbuf[gi, a:a + tk, cols]
            lhs = jnp.concatenate([jnp.where(head_of_lane == h, q4, jnp.zeros_like(q4)) for h in range(HEAD_GROUP)], axis=0)
            s = lax.dot_general(lhs, k4, (((1,), (1,)), ((), ())), preferred_element_type=F32) + bias
            m = jnp.max(s, axis=-1, keepdims=True)
            p = jnp.exp(s - m)
            l = jnp.sum(p, axis=-1, keepdims=True)
            o_all = jnp.dot(p.astype(BF16), v4, preferred_element_type=F32) / l
            lse = m + jnp.log(l)
            o = o_all[0:TQ]
            for h in range(HEAD_GROUP):
                rows = slice(h * TQ, (h + 1) * TQ)
                if h:
                    o = jnp.where(head_of_lane == h, o_all[rows], o)
                st = jnp.where(lane == g * HEAD_GROUP + h, lse[rows], st)
            o_ref[gi, a:a + TQ, cols] = o.astype(BF16)
        st_ref[gi, a:a + TQ, :] = st


def _banded_attention(q, k, v):
    g, sub_len, _ = q.shape
    tl = min(TL_ATTN, sub_len)
    gb = min(TL_ATTN // tl, g)
    hs = HALF_SPAN
    per = tl // hs
    last = sub_len // hs - 1
    cur = pl.BlockSpec((gb, tl, WIDTH), lambda b, i: (b, i, 0))
    prev = pl.BlockSpec((gb, hs, WIDTH), lambda b, i: (b, jnp.maximum(i * per - 1, 0), 0))
    nxt = pl.BlockSpec((gb, hs, WIDTH), lambda b, i: (b, jnp.minimum((i + 1) * per, last), 0))
    return pl.pallas_call(
        functools.partial(_attn_kernel, gb=gb, tl=tl, sub_len=sub_len),
        grid=(g // gb, sub_len // tl),
        in_specs=[cur, prev, cur, nxt, prev, cur, nxt],
        out_specs=[cur, pl.BlockSpec((gb, tl, LANES), lambda b, i: (b, i, 0))],
        out_shape=[jax.ShapeDtypeStruct((g, sub_len, WIDTH), BF16),
                   jax.ShapeDtypeStruct((g, sub_len, LANES), F32)],
        scratch_shapes=[pltpu.VMEM((gb, tl + 2 * hs, WIDTH), BF16), pltpu.VMEM((gb, tl + 2 * hs, WIDTH), BF16)],
        compiler_params=_cparams("arbitrary", "arbitrary"),
        name="banded_attention",
    )(q, k, k, k, v, v, v)


def _ret_kernel(q_ref, k_ref, v_ref, gate_ref, dmat_ref, wq_ref, wk_ref, dec_ref, gn_ref, o_ref, sf_scr, sb_scr, *, c, n_chunks):
    pair = 2 * HEAD_DIM
    lane = lax.broadcasted_iota(jnp.int32, (c, pair), 1)
    head0 = lane < HEAD_DIM
    blk_r = lax.broadcasted_iota(jnp.int32, (pair, pair), 0) // HEAD_DIM
    blk_c = lax.broadcasted_iota(jnp.int32, (pair, pair), 1) // HEAD_DIM
    same_head = blk_r == blk_c
    dec_f = dec_ref[0:1, :]
    dec_b = dec_ref[1:2, :]
    tn = (((0,), (0,)), ((), ()))
    nt = (((1,), (1,)), ((), ()))

    def rows_of(n):
        return pl.ds(pl.multiple_of(n * c, c), c)

    def kv_body(n, carry):
        rows = rows_of(n)
        kf32 = k_ref[0, rows, :].astype(F32)
        kw = jnp.concatenate([(kf32 * wk_ref[:, :pair]).astype(BF16), (kf32 * wk_ref[:, pair:]).astype(BF16)], axis=1)
        kv = lax.dot_general(kw, v_ref[0, rows, :], tn, preferred_element_type=F32)
        sf_scr[n] = jnp.where(same_head, kv[:pair], 0.0)
        sb_scr[n] = jnp.where(same_head, kv[pair:], 0.0)
        return carry

    lax.fori_loop(0, n_chunks, kv_body, 0, unroll=RET_UNROLL)

    def scan_body(t, carry):
        sf, sb = carry
        nb = n_chunks - 1 - t
        kv_f = sf_scr[t]
        kv_b = sb_scr[nb]
        sf_scr[t] = sf
        sb_scr[nb] = sb
        return sf * dec_f + kv_f, sb * dec_b + kv_b

    zero = jnp.zeros((pair, pair), F32)
    lax.fori_loop(0, n_chunks, scan_body, (zero, zero))

    def out_body(n, carry):
        rows = rows_of(n)
        q = q_ref[0, rows, :]
        k = k_ref[0, rows, :]
        v = v_ref[0, rows, :]
        qf32 = q.astype(F32)
        intra = []
        for hh in range(2):
            mask = head0 if hh == 0 else jnp.logical_not(head0)
            kh = jnp.where(mask, k, jnp.zeros_like(k))
            s = lax.dot_general(q, kh, nt, preferred_element_type=F32)
            a = (s * dmat_ref[hh]).astype(BF16)
            intra.append(jnp.dot(a, v, preferred_element_type=F32))
        qw = jnp.concatenate([(qf32 * wq_ref[:, :pair]).astype(BF16), (qf32 * wq_ref[:, pair:]).astype(BF16)], axis=1)
        states = jnp.concatenate([sf_scr[n].astype(BF16), sb_scr[n].astype(BF16)], axis=0)
        tot = jnp.where(head0, intra[0], intra[1]) + jnp.dot(qw, states, preferred_element_type=F32)
        inv = 1.0 / HEAD_DIM
        s0 = jnp.sum(jnp.where(head0, tot, 0.0), axis=-1, keepdims=True)
        s1 = jnp.sum(jnp.where(head0, 0.0, tot), axis=-1, keepdims=True)
        xc = tot - jnp.where(head0, s0, s1) * inv
        sq = xc * xc
        v0 = jnp.sum(jnp.where(head0, sq, 0.0), axis=-1, keepdims=True)
        v1 = jnp.sum(jnp.where(head0, 0.0, sq), axis=-1, keepdims=True)
        y = xc * lax.rsqrt(jnp.where(head0, v0, v1) * inv + NORM_EPS)
        gt = gate_ref[0, rows, :].astype(F32)
        y = y * gn_ref[...] * (gt / (1.0 + jnp.exp(-gt)))
        o_ref[0, rows, :] = y.astype(BF16)
        return carry

    lax.fori_loop(0, n_chunks, out_body, 0, unroll=RET_UNROLL)


def _retention_tables(c):
    hidx = jnp.arange(N_HEADS, dtype=F32)
    lg_f = jnp.log1p(-jnp.exp2(-(RET_DECAY_BASE_FWD + hidx)))
    lg_b = jnp.log1p(-jnp.exp2(-(RET_DECAY_BASE_BWD + hidx)))
    pos = jnp.arange(c, dtype=F32)
    diff = pos[:, None] - pos[None, :]
    dm_f = jnp.exp(jnp.maximum(diff, 0.0)[None] * lg_f[:, None, None])
    dm_b = jnp.exp(jnp.maximum(-diff, 0.0)[None] * lg_b[:, None, None])
    dmat = jnp.where((diff >= 0)[None], dm_f, dm_b)

    def per_lane(tab):
        t = jnp.repeat(tab[:, :, None], HEAD_DIM, axis=2)
        return t.reshape(N_HEADS // 2, 2, c, HEAD_DIM).transpose(0, 2, 1, 3).reshape(N_HEADS // 2, c, 2 * HEAD_DIM)

    wq_f = per_lane(jnp.exp((pos + 1.0)[None, :] * lg_f[:, None]))
    wq_b = per_lane(jnp.exp((c - pos)[None, :] * lg_b[:, None]))
    wk_f = per_lane(jnp.exp((c - 1.0 - pos)[None, :] * lg_f[:, None]))
    wk_b = per_lane(jnp.exp(pos[None, :] * lg_b[:, None]))
    wq = jnp.concatenate([wq_f, wq_b], axis=2)
    wk = jnp.concatenate([wk_f, wk_b], axis=2)
    dec = jnp.stack([jnp.repeat(jnp.exp(c * lg_f), HEAD_DIM), jnp.repeat(jnp.exp(c * lg_b), HEAD_DIM)], axis=0)
    dec = dec.reshape(2, N_HEADS // 2, 2 * HEAD_DIM).transpose(1, 0, 2)
    return dmat, wq, wk, dec


def _retention(q, k, v, gate, ret_norm_g, b, s):
    c = min(RET_CHUNK, s)
    n_chunks = s // c
    pair = 2 * HEAD_DIM
    dmat, wq, wk, dec = _retention_tables(c)
    seq_blk = pl.BlockSpec((1, s, pair), lambda bi, hp: (bi, 0, hp))
    r3 = lambda z: z.reshape(b, s, WIDTH)
    return pl.pallas_call(
        functools.partial(_ret_kernel, c=c, n_chunks=n_chunks),
        grid=(b, N_HEADS // 2),
        in_specs=[seq_blk, seq_blk, seq_blk, seq_blk,
                  pl.BlockSpec((2, c, c), lambda bi, hp: (hp, 0, 0)),
                  pl.BlockSpec((None, c, 2 * pair), lambda bi, hp: (hp, 0, 0)),
                  pl.BlockSpec((None, c, 2 * pair), lambda bi, hp: (hp, 0, 0)),
                  pl.BlockSpec((None, 2, pair), lambda bi, hp: (hp, 0, 0)),
                  pl.BlockSpec((1, pair), lambda bi, hp: (0, hp))],
        out_specs=seq_blk,
        out_shape=jax.ShapeDtypeStruct((b, s, WIDTH), BF16),
        scratch_shapes=[pltpu.VMEM((n_chunks, pair, pair), F32), pltpu.VMEM((n_chunks, pair, pair), F32)],
        compiler_params=_cparams("arbitrary", "arbitrary"),
        name="retention",
    )(r3(q), r3(k), r3(v), r3(gate), dmat, wq, wk, dec, ret_norm_g.reshape(1, WIDTH)).reshape(b * s, WIDTH)


N_PIECES = 4
HALF_D = D_MODEL // 2
assert N_PIECES * LANES == HALF_D


def _pack_bf16_pair(a, b):
    ua = pltpu.bitcast(a.astype(BF16).astype(F32), jnp.uint32)
    ub = pltpu.bitcast(b.astype(BF16).astype(F32), jnp.uint32)
    return ua | (ub >> 16)


def _unpack_bf16_pair(u):
    return pltpu.bitcast(u & jnp.uint32(0xFFFF0000), F32), pltpu.bitcast(u << 16, F32)


def _store_pieces(ref, val):
    for s in range(N_PIECES):
        lo = slice(s * LANES, (s + 1) * LANES)
        hi = slice(HALF_D + s * LANES, HALF_D + (s + 1) * LANES)
        ref[s] = _pack_bf16_pair(val[:, lo], val[:, hi])


def _load_pieces(ref):
    first, second = [], []
    for s in range(N_PIECES):
        a, b = _unpack_bf16_pair(ref[s])
        first.append(a)
        second.append(b)
    return first, second


def _split_bf16(x):
    hi = x.astype(BF16)
    return hi, (x - hi.astype(F32)).astype(BF16)


def _outproj_kernel(x_ref, o1_ref, o2_ref, o3_ref, s1_ref, s2_ref, s3_ref, ret_ref, wout_ref, expand_ref,
                    gffn_ref, wr_ref, br_ref, tri_ref,
                    x1_ref, h2t_ref, ri_ref, gates_ref, cnt_ref, base_scr, nat_o, nat_s, grp_o):
    i = pl.program_id(0)
    tm = x_ref.shape[0]

    @pl.when(i == 0)
    def _():
        base_scr[...] = jnp.zeros_like(base_scr)

    n_col = WIDTH // LANES
    for bb in range(REGROUP):
        rows4 = pl.ds(bb, tm // REGROUP, stride=REGROUP)
        for c in range(n_col):
            nat_o[0, c, rows4, :] = o2_ref[0, bb, :, c * LANES:(c + 1) * LANES].astype(F32)
        nat_s[0, rows4, :] = s2_ref[0, bb]
        for aa in range(REGROUP):
            rows16 = pl.ds(aa, tm // REGROUP ** 2, stride=REGROUP)
            for c in range(n_col):
                grp_o[c, bb, rows16, :] = o3_ref[0, bb * REGROUP + aa, :, c * LANES:(c + 1) * LANES].astype(F32)
            grp_o[n_col, bb, rows16, :] = s3_ref[0, bb * REGROUP + aa]
        for c in range(n_col):
            nat_o[1, c, rows4, :] = grp_o[c, bb]
        nat_s[1, rows4, :] = grp_o[n_col, bb]
    hm = tm // OUT_SPLIT
    lane = lax.broadcasted_iota(jnp.int32, (hm, LANES), 1).astype(F32)
    base = base_scr[...]
    for r0 in range(0, tm, hm):
        rows = slice(r0, r0 + hm)
        o_nat = [[o1_ref[rows, c * LANES:(c + 1) * LANES].astype(F32) for c in range(n_col)]]
        sts = [s1_ref[rows, :]]
        for di in range(len(RESIDUE_DILATIONS)):
            o_nat.append([nat_o[di, c, rows, :] for c in range(n_col)])
            sts.append(nat_s[di, rows, :])

        mx = jnp.maximum(jnp.maximum(sts[0], sts[1]), sts[2])
        es = [jnp.exp(st - mx) for st in sts]
        den = es[0] + es[1] + es[2]
        attn = None
        for e, o_cols in zip(es, o_nat):
            hi, lo = _split_bf16(e / den)
            w_full = jnp.dot(jnp.concatenate([hi, lo], axis=1), expand_ref[...], preferred_element_type=F32)
            term = w_full * jnp.concatenate(o_cols, axis=1)
            attn = term if attn is None else attn + term
        mixed = jnp.concatenate([attn.astype(BF16), ret_ref[rows, :]], axis=1)
        x1 = x_ref[rows, :] + jnp.dot(mixed, wout_ref[...], preferred_element_type=F32)
        x1_ref[rows, :] = x1

        h2 = x1 * lax.rsqrt(jnp.mean(x1 * x1, axis=-1, keepdims=True) + NORM_EPS) * gffn_ref[...]
        for s in range(N_PIECES):
            h2t_ref[s, rows, :] = _pack_bf16_pair(h2[:, s * LANES:(s + 1) * LANES],
                                                  h2[:, HALF_D + s * LANES:HALF_D + (s + 1) * LANES])

        hi, lo = _split_bf16(h2)
        logits = jnp.dot(jnp.concatenate([hi, lo, hi], axis=1), wr_ref[...], preferred_element_type=F32) + br_ref[...]
        work = logits
        vals, idxs = [], []
        onehot = jnp.zeros((hm, LANES), F32)
        for _k in range(TOP_K):
            mk = jnp.max(work, axis=-1, keepdims=True)
            ik = jnp.min(jnp.where(work == mk, lane, float(LANES)), axis=-1, keepdims=True)
            sel = lane == ik
            onehot = jnp.where(sel, 1.0, onehot)
            work = jnp.where(sel, -jnp.inf, work)
            vals.append(mk)
            idxs.append(ik)
        ex = [jnp.exp(vk - vals[0]) for vk in vals]
        tot = ex[0] + ex[1] + ex[2] + ex[3]
        before = jnp.dot(tri_ref[...], onehot.astype(BF16), preferred_element_type=F32) + base
        ri = jnp.zeros((hm, LANES), F32)
        gt = jnp.zeros((hm, LANES), F32)
        for kk in range(TOP_K):
            rank = jnp.sum(jnp.where(lane == idxs[kk], before, 0.0), axis=-1, keepdims=True)
            ri = jnp.where(lane == float(kk), idxs[kk], ri)
            ri = jnp.where(lane == float(TOP_K + kk), rank, ri)
            gt = jnp.where(lane == float(kk), ex[kk] / tot, gt)
        ri_ref[:, rows] = jnp.transpose(ri)[:2 * TOP_K].astype(jnp.int32)
        gates_ref[rows, :] = gt
        base = base + jnp.sum(onehot, axis=0, keepdims=True)
    base_scr[...] = base
    cnt_ref[...] = base


def _outproj_router(x2d, outs, stats, ret, wout_bf16, norm_ffn_g, w_router, b_router, seq):
    t = x2d.shape[0]
    tm = TM_OUT
    per_seq = seq // tm
    row = lambda i: (i, 0)

    def res_spec(d, width):
        return pl.BlockSpec((1, d, tm // d, width), lambda i: (i // per_seq, 0, i % per_seq, 0))

    const = lambda i: (0, 0)
    head_of_col = jnp.arange(WIDTH) // HEAD_DIM
    expand = (jnp.arange(LANES)[:, None] == head_of_col[None, :]).astype(BF16)
    expand2 = jnp.concatenate([expand, expand], axis=0)
    wr = jnp.zeros((D_MODEL, LANES), F32).at[:, :N_EXPERTS].set(w_router)
    wr_hi, wr_lo = _split_bf16(wr)
    wr3 = jnp.concatenate([wr_hi, wr_hi, wr_lo], axis=0)
    br = jnp.full((1, LANES), NEG_INF, F32).at[0, :N_EXPERTS].set(b_router)
    hm = tm // OUT_SPLIT
    tri = (jnp.arange(hm)[:, None] > jnp.arange(hm)[None, :]).astype(BF16)
    o_spec = pl.BlockSpec((tm, WIDTH), row)
    s_spec = pl.BlockSpec((tm, LANES), row)
    return pl.pallas_call(
        _outproj_kernel,
        grid=(t // tm,),
        in_specs=[pl.BlockSpec((tm, D_MODEL), row),
                  o_spec, *[res_spec(d, WIDTH) for d in RESIDUE_DILATIONS],
                  s_spec, *[res_spec(d, LANES) for d in RESIDUE_DILATIONS], o_spec,
                  pl.BlockSpec((D_MODEL, D_MODEL), const), pl.BlockSpec((2 * LANES, WIDTH), const),
                  pl.BlockSpec((1, D_MODEL), const), pl.BlockSpec((3 * D_MODEL, LANES), const),
                  pl.BlockSpec((1, LANES), const), pl.BlockSpec((hm, hm), const)],
        out_specs=[pl.BlockSpec((tm, D_MODEL), row), pl.BlockSpec((N_PIECES, tm, LANES), lambda i: (0, i, 0)),
                   pl.BlockSpec((2 * TOP_K, tm), lambda i: (0, i)), s_spec, pl.BlockSpec((1, LANES), const)],
        out_shape=[jax.ShapeDtypeStruct((t, D_MODEL), F32), jax.ShapeDtypeStruct((N_PIECES, t, LANES), jnp.uint32),
                   jax.ShapeDtypeStruct((2 * TOP_K, t), jnp.int32), jax.ShapeDtypeStruct((t, LANES), F32),
                   jax.ShapeDtypeStruct((1, LANES), F32)],
        scratch_shapes=[pltpu.VMEM((1, LANES), F32),
                        pltpu.VMEM((len(RESIDUE_DILATIONS), WIDTH // LANES, tm, LANES), F32),
                        pltpu.VMEM((len(RESIDUE_DILATIONS), tm, LANES), F32),
                        pltpu.VMEM((WIDTH // LANES + 1, REGROUP, tm // REGROUP, LANES), F32)],
        compiler_params=_cparams("arbitrary"),
        name="outproj_router",
    )(x2d, *outs, *stats, ret, wout_bf16, expand2, norm_ffn_g.reshape(1, D_MODEL), wr3, br, tri)


SC_CORES = 2
SC_SUBCORES = 16
SC_WINDOW = 128


def _sc_mesh():
    return plsc.VectorSubcoreMesh(core_axis_name="c", subcore_axis_name="s")


def _sc_scatter_rows(src, idx, n_out_rows):
    n_rows = src.shape[0]
    workers = SC_CORES * SC_SUBCORES
    per_worker = n_rows // workers
    n_win = per_worker // SC_WINDOW
    assert per_worker * workers == n_rows and n_win * SC_WINDOW == per_worker and idx.shape == (TOP_K * n_rows,)

    @functools.partial(
        pl.kernel, mesh=_sc_mesh(),
        out_type=jax.ShapeDtypeStruct((n_out_rows, LANES), src.dtype),
        scratch_types=[pltpu.VMEM((SC_WINDOW,), jnp.int32), pltpu.VMEM((SC_WINDOW, LANES), src.dtype)],
        name="sc_scatter_rows",
    )
    def scatter(src_hbm, idx_hbm, out_hbm, idx_v, rows_v):
        wid = lax.axis_index("s") * SC_CORES + lax.axis_index("c")
        base = wid * per_worker

        @pl.loop(0, n_win)
        def _(j):
            off = pl.multiple_of(base + j * SC_WINDOW, SC_WINDOW)
            pltpu.sync_copy(src_hbm.at[pl.ds(off, SC_WINDOW)], rows_v)
            for kk in range(TOP_K):
                pltpu.sync_copy(idx_hbm.at[pl.ds(pl.multiple_of(kk * n_rows + off, SC_WINDOW), SC_WINDOW)], idx_v)
                pltpu.sync_copy(rows_v, out_hbm.at[idx_v])

    return scatter(src, idx)


def _expert_kernel(blk_e_ref, nvalid_ref, rows_ref, xs_ref, wgu_ref, bgu_ref, wd_ref, bd_ref, ys_ref, wgu_bf, wd_bf):
    i = pl.program_id(0)

    @pl.when((i == 0) | (blk_e_ref[i] != blk_e_ref[jnp.maximum(i - 1, 0)]))
    def _():
        wgu_bf[...] = wgu_ref[0].astype(BF16)
        wd_bf[...] = wd_ref[0].astype(BF16)

    def mlp(m):
        parts = [_unpack_bf16_pair(xs_ref[s, :m, :]) for s in range(N_PIECES)]
        cols = [p[0] for p in parts] + [p[1] for p in parts]
        live = lax.broadcasted_iota(jnp.int32, (m, LANES), 0) < rows_ref[i]
        x = jnp.concatenate([jnp.where(live, p, 0.0).astype(BF16) for p in cols], axis=1)
        gu = jnp.dot(x, wgu_bf[...], preferred_element_type=F32) + bgu_ref[0]
        gate = jnp.minimum(gu[:, :EXPERT_FF], SWIGLU_LIMIT)
        up = jnp.clip(gu[:, EXPERT_FF:], -SWIGLU_LIMIT, SWIGLU_LIMIT)
        act = gate * (1.0 / (1.0 + jnp.exp(-SWIGLU_ALPHA * gate))) * (up + 1.0)
        y = jnp.dot(act.astype(BF16), wd_bf[...], preferred_element_type=F32) + bd_ref[0]
        for s in range(N_PIECES):
            ys_ref[s, :m, :] = _pack_bf16_pair(y[:, s * LANES:(s + 1) * LANES],
                                               y[:, HALF_D + s * LANES:HALF_D + (s + 1) * LANES])

    n_sub = lax.shift_right_logical(rows_ref[i] + (EXPERT_SUB - 1), EXPERT_SUB.bit_length() - 1)
    for v in range(1, BM // EXPERT_SUB + 1):
        pl.when((i < nvalid_ref[0]) & (n_sub == v))(functools.partial(mlp, v * EXPERT_SUB))


def _experts(xs, blk_e, nvalid, live_rows, wgu_bf16, bgu, wd_bf16, bd):
    cap = xs.shape[1]
    nblk = cap // BM

    def blk(i, be, nv, lr):
        return (0, jnp.minimum(i, nv[0] - 1), 0)

    def by_expert(i, be, nv, lr):
        return (be[i], 0, 0)

    return pl.pallas_call(
        _expert_kernel,
        grid_spec=pltpu.PrefetchScalarGridSpec(
            num_scalar_prefetch=3,
            grid=(nblk,),
            in_specs=[pl.BlockSpec((N_PIECES, BM, LANES), blk),
                      pl.BlockSpec((1, D_MODEL, 2 * EXPERT_FF), by_expert),
                      pl.BlockSpec((1, 1, 2 * EXPERT_FF), by_expert),
                      pl.BlockSpec((1, EXPERT_FF, D_MODEL), by_expert),
                      pl.BlockSpec((1, 1, D_MODEL), by_expert)],
            out_specs=pl.BlockSpec((N_PIECES, BM, LANES), blk),
            scratch_shapes=[pltpu.VMEM((D_MODEL, 2 * EXPERT_FF), BF16), pltpu.VMEM((EXPERT_FF, D_MODEL), BF16)],
        ),
        out_shape=jax.ShapeDtypeStruct((N_PIECES, cap, LANES), jnp.uint32),
        compiler_params=_cparams("arbitrary"),
        name="moe_experts",
    )(blk_e, nvalid, live_rows, xs, wgu_bf16, bgu.reshape(N_EXPERTS, 1, 2 * EXPERT_FF), wd_bf16,
      bd.reshape(N_EXPERTS, 1, D_MODEL))


def _sc_gather_rows(table, idx):
    n_rows = idx.shape[0]
    workers = SC_CORES * SC_SUBCORES
    per_worker = n_rows // workers
    n_win = per_worker // SC_WINDOW
    assert per_worker * workers == n_rows and n_win * SC_WINDOW == per_worker

    @functools.partial(
        pl.kernel, mesh=_sc_mesh(),
        out_type=jax.ShapeDtypeStruct((n_rows, LANES), table.dtype),
        scratch_types=[pltpu.VMEM((SC_WINDOW,), jnp.int32), pltpu.VMEM((SC_WINDOW, LANES), table.dtype),
                       pltpu.SemaphoreType.DMA],
        name="sc_gather_rows",
    )
    def gather(table_hbm, idx_hbm, out_hbm, idx_v, rows_v, sem):
        wid = lax.axis_index("s") * SC_CORES + lax.axis_index("c")
        base = wid * per_worker

        @pl.loop(0, n_win)
        def _(j):
            off = pl.multiple_of(base + j * SC_WINDOW, SC_WINDOW)
            pltpu.sync_copy(idx_hbm.at[pl.ds(off, SC_WINDOW)], idx_v)
            pltpu.async_copy(table_hbm.at[idx_v], rows_v, sem).wait()
            pltpu.sync_copy(rows_v, out_hbm.at[pl.ds(off, SC_WINDOW)])

    return gather(table, idx)


def _combine_kernel(x1_ref, gates_ref, gfin_ref, *refs, tc):
    piece_refs, o_ref = refs[:TOP_K * N_PIECES], refs[TOP_K * N_PIECES]
    g = gates_ref[...]
    gk = [jnp.broadcast_to(g[:, kk:kk + 1], (tc, LANES)) for kk in range(TOP_K)]
    zs = {}
    ssq = jnp.zeros((tc, 1), F32)
    for s in range(N_PIECES):
        c_lo, c_hi = s * LANES, HALF_D + s * LANES
        z_lo = x1_ref[:, c_lo:c_lo + LANES]
        z_hi = x1_ref[:, c_hi:c_hi + LANES]
        for kk in range(TOP_K):
            a, b = _unpack_bf16_pair(piece_refs[kk * N_PIECES + s][...])
            z_lo = z_lo + gk[kk] * a
            z_hi = z_hi + gk[kk] * b
        zs[c_lo], zs[c_hi] = z_lo, z_hi
        ssq = ssq + jnp.sum(z_lo * z_lo + z_hi * z_hi, axis=-1, keepdims=True)
    inv = lax.rsqrt(ssq * (1.0 / D_MODEL) + NORM_EPS)
    for c0, z in zs.items():
        o_ref[:, c0:c0 + LANES] = z * inv * gfin_ref[:, c0:c0 + LANES]


def _combine(x1, gates, gathered, norm_final_g):
    t = x1.shape[0]
    tc = TC
    row = lambda i: (i, 0)
    slot_specs = [pl.BlockSpec((None, tc, LANES), functools.partial(lambda i, j: (j, i, 0), j=j))
                  for j in range(TOP_K * N_PIECES)]
    return pl.pallas_call(
        functools.partial(_combine_kernel, tc=tc),
        grid=(t // tc,),
        in_specs=[pl.BlockSpec((tc, D_MODEL), row), pl.BlockSpec((tc, LANES), row),
                  pl.BlockSpec((1, D_MODEL), lambda i: (0, 0)), *slot_specs],
        out_specs=pl.BlockSpec((tc, D_MODEL), row),
        out_shape=jax.ShapeDtypeStruct((t, D_MODEL), F32),
        compiler_params=_cparams("arbitrary"),
        name="moe_combine",
    )(x1, gates, norm_final_g.reshape(1, D_MODEL), *([gathered] * (TOP_K * N_PIECES)))


def _moe(x1, h2t, ri, gates, counts_f, wgu_bf16, bgu, wd_bf16, bd, norm_final_g):
    t = x1.shape[0]
    a = t * TOP_K
    cap = a + N_EXPERTS * BM
    nblk = cap // BM
    counts = counts_f[0, :N_EXPERTS].astype(jnp.int32)
    padded = ((counts + BM - 1) // BM) * BM
    pend = jnp.cumsum(padded)
    pstart = pend - padded
    nvalid = (pend[-1] // BM).reshape(1)
    first_row = jnp.minimum(jnp.arange(nblk, dtype=jnp.int32) * BM, pend[-1] - 1)
    blk_e = jnp.sum(pend[None, :] <= first_row[:, None], axis=1).astype(jnp.int32)
    idx, rank = ri[:TOP_K], ri[TOP_K:2 * TOP_K]
    onehot = idx[None, :, :] == jnp.arange(N_EXPERTS, dtype=jnp.int32)[:, None, None]
    dest = rank + jnp.sum(jnp.where(onehot, pstart[:, None, None], 0), axis=0)
    seg_end = (pstart + counts)[blk_e]
    live_rows = jnp.clip(seg_end - jnp.arange(nblk, dtype=jnp.int32) * BM, 0, BM).astype(jnp.int32)
    piece = dest[:, None, :] + (jnp.arange(N_PIECES, dtype=jnp.int32) * cap)[None, :, None]
    piece = piece.reshape(TOP_K * N_PIECES * t)
    xs = _sc_scatter_rows(h2t.reshape(N_PIECES * t, LANES), piece, N_PIECES * cap)
    ys = _experts(xs.reshape(N_PIECES, cap, LANES), blk_e, nvalid, live_rows, wgu_bf16, bgu, wd_bf16, bd)
    gathered = _sc_gather_rows(ys.reshape(N_PIECES * cap, LANES), piece)
    return _combine(x1, gates, gathered.reshape(TOP_K * N_PIECES, t, LANES), norm_final_g)


def _encoder(x, p):
    b, s, _ = x.shape
    x2d = x.reshape(b * s, D_MODEL)
    cos_t, sin_t = _rope_tables(s)
    (qa, ka, va, qr, kr, vr, gr), by_residue = _inproj(x2d, p["norm_mix_g"], p["w_in"], cos_t, sin_t, s)
    outs, stats = [], []
    for d in DILATIONS:
        if d == 1:
            o, st = _banded_attention(*[z.reshape(b, s, WIDTH) for z in (qa, ka, va)])
            outs.append(o.reshape(b * s, WIDTH))
            stats.append(st.reshape(b * s, LANES))
        else:
            o, st = _banded_attention(*[z.reshape(b * d, s // d, WIDTH) for z in by_residue[d]])
            outs.append(o.reshape(b, d, s // d, WIDTH))
            stats.append(st.reshape(b, d, s // d, LANES))
    ret = _retention(qr, kr, vr, gr, p["ret_norm_g"], b, s)
    x1, h2t, ri, gates, counts = _outproj_router(x2d, outs, stats, ret, p["w_out"], p["norm_ffn_g"],
                                                 p["w_router"], p["b_router"], s)
    y = _moe(x1, h2t, ri, gates, counts, p["w_gate_up"], p["b_gate_up"], p["w_down"], p["b_down"], p["norm_final_g"])
    return y.reshape(b, s, D_MODEL)


def kernel(x_prompt, x_sample, norm_mix_g, w_in, ret_norm_g, w_out, norm_ffn_g, w_router, b_router, w_gate_up, b_gate_up, w_down, b_down, norm_final_g):
    assert norm_mix_g.shape[0] == 1, "single layer"
    p = dict(norm_mix_g=norm_mix_g[0], w_in=w_in[0].astype(BF16), ret_norm_g=ret_norm_g[0],
             w_out=w_out[0].astype(BF16), norm_ffn_g=norm_ffn_g[0], w_router=w_router[0], b_router=b_router[0],
             w_gate_up=w_gate_up[0], b_gate_up=b_gate_up[0], w_down=w_down[0],
             b_down=b_down[0], norm_final_g=norm_final_g)
    return (_encoder(x_prompt, p), _encoder(x_sample, p))
```

```python
import functools

import jax
import jax.numpy as jnp
from jax import lax
from jax.experimental import pallas as pl
from jax.experimental.pallas import tpu as pltpu
from jax.experimental.pallas import tpu_sc as plsc

D_MODEL = 1024
HEAD_DIM = 64
N_HEADS = 8
WIDTH = N_HEADS * HEAD_DIM
N_SLABS = 7
DILATIONS = (1, 4, 16)
HALF_SPAN = 64
ROPE_THETA = 10000.0
RET_DECAY_BASE_FWD = 5.0
RET_DECAY_BASE_BWD = 5.5
N_EXPERTS = 32
TOP_K = 4
EXPERT_FF = D_MODEL
SWIGLU_LIMIT = 7.0
SWIGLU_ALPHA = 1.702
NORM_EPS = 1e-6
NEG_INF = -1e30

LANES = 128
VMEM_LIMIT_BYTES = 48 * 1024 * 1024

TM_INPROJ = 512
TQ = 128
TL_ATTN = 2048
HEAD_GROUP = 4
RET_CHUNK = 256
TM_OUT = 512
OUT_SPLIT = 2
RET_UNROLL = 8
BM = 512
EXPERT_SUB = 128
TC = 1024

F32 = jnp.float32
BF16 = jnp.bfloat16


def _cparams(*sem):
    return pltpu.CompilerParams(dimension_semantics=sem, vmem_limit_bytes=VMEM_LIMIT_BYTES)


_ROTATE = (True, True, False, True, True, False, False)
_SCALE = (HEAD_DIM ** -0.5, 1.0, 1.0, 1.0, HEAD_DIM ** -0.5, 1.0, 1.0)


N_ATTN_SLABS = 3
RESIDUE_DILATIONS = tuple(d for d in DILATIONS if d > 1)
REGROUP = 4
assert RESIDUE_DILATIONS == (REGROUP, REGROUP ** 2)


def _inproj_kernel(x_ref, g_ref, w_ref, cos_ref, sin_ref, *refs):
    out_refs = refs[:N_SLABS]
    res_refs = refs[N_SLABS:N_SLABS + N_ATTN_SLABS * len(RESIDUE_DILATIONS)]
    stage, stage4 = refs[-2:]
    x = x_ref[...]
    tm = x.shape[0]
    h = (x * lax.rsqrt(jnp.mean(x * x, axis=-1, keepdims=True) + NORM_EPS) * g_ref[...]).astype(BF16)
    cos = cos_ref[...]
    sin = sin_ref[...]
    lane = lax.broadcasted_iota(jnp.int32, cos.shape, 1)
    first_half = (lane & (HEAD_DIM - 1)) < HEAD_DIM // 2
    for j, o_ref in enumerate(out_refs):
        p = jnp.dot(h, w_ref[:, j * WIDTH:(j + 1) * WIDTH], preferred_element_type=F32)
        for c in range(WIDTH // LANES):
            cols = slice(c * LANES, (c + 1) * LANES)
            r = p[:, cols]
            if _ROTATE[j]:
                partner = jnp.where(first_half, pltpu.roll(r, LANES - HEAD_DIM // 2, 1), pltpu.roll(r, HEAD_DIM // 2, 1))
                r = r * cos + partner * sin
                if _SCALE[j] != 1.0:
                    r = r * _SCALE[j]
            o_ref[:, cols] = r.astype(BF16)
            if j < N_ATTN_SLABS:
                stage[j, c] = r
                dst4 = res_refs[j]
                dst16 = res_refs[N_ATTN_SLABS + j]
                for bb in range(REGROUP):
                    grp = stage[j, c, pl.ds(bb, tm // REGROUP, stride=REGROUP), :]
                    dst4[0, bb, :, cols] = grp.astype(BF16)
                    stage4[j, c, bb] = grp
                    for aa in range(REGROUP):
                        sub = stage4[j, c, bb, pl.ds(aa, tm // REGROUP ** 2, stride=REGROUP), :]
                        dst16[0, bb * REGROUP + aa, :, cols] = sub.astype(BF16)


def _inproj(x2d, g, w_bf16, cos_t, sin_t, seq):
    t = x2d.shape[0]
    tm = min(TM_INPROJ, seq)
    pos_blocks = seq // tm
    b = t // seq
    out = jax.ShapeDtypeStruct((t, WIDTH), BF16)
    row = lambda i: (i, 0)
    out_specs = [pl.BlockSpec((tm, WIDTH), row)] * N_SLABS
    out_shape = [out] * N_SLABS
    for d in RESIDUE_DILATIONS:
        out_specs += [pl.BlockSpec((1, d, tm // d, WIDTH), lambda i: (i // pos_blocks, 0, i % pos_blocks, 0))] * N_ATTN_SLABS
        out_shape += [jax.ShapeDtypeStruct((b, d, seq // d, WIDTH), BF16)] * N_ATTN_SLABS
    res = pl.pallas_call(
        _inproj_kernel,
        grid=(t // tm,),
        in_specs=[
            pl.BlockSpec((tm, D_MODEL), row),
            pl.BlockSpec((1, D_MODEL), lambda i: (0, 0)),
            pl.BlockSpec((D_MODEL, N_SLABS * WIDTH), lambda i: (0, 0)),
            pl.BlockSpec((tm, LANES), lambda i: (i % pos_blocks, 0)),
            pl.BlockSpec((tm, LANES), lambda i: (i % pos_blocks, 0)),
        ],
        out_specs=out_specs,
        out_shape=out_shape,
        scratch_shapes=[pltpu.VMEM((N_ATTN_SLABS, WIDTH // LANES, tm, LANES), F32),
                        pltpu.VMEM((N_ATTN_SLABS, WIDTH // LANES, REGROUP, tm // REGROUP, LANES), F32)],
        compiler_params=_cparams("arbitrary"),
        name="inproj",
    )(x2d, g.reshape(1, D_MODEL), w_bf16, cos_t, sin_t)
    natural = res[:N_SLABS]
    by_residue = {d: res[N_SLABS + di * N_ATTN_SLABS:N_SLABS + (di + 1) * N_ATTN_SLABS]
                  for di, d in enumerate(RESIDUE_DILATIONS)}
    return natural, by_residue


def _rope_tables(seq):
    half = HEAD_DIM // 2
    inv_freq = ROPE_THETA ** (-jnp.arange(0, HEAD_DIM, 2, dtype=F32) / HEAD_DIM)
    ang = jnp.arange(seq, dtype=F32)[:, None] * inv_freq[None, :]
    cos, sin = jnp.cos(ang), jnp.sin(ang)
    reps = LANES // HEAD_DIM
    cos_t = jnp.tile(jnp.concatenate([cos, cos], axis=1), (1, reps))
    sin_t = jnp.tile(jnp.concatenate([-sin, sin], axis=1), (1, reps))
    assert cos_t.shape == (seq, LANES) and half * 2 == HEAD_DIM
    return cos_t, sin_t


def _attn_kernel(q_ref, kp_ref, kc_ref, kn_ref, vp_ref, vc_ref, vn_ref, o_ref, st_ref, kbuf, vbuf, *, gb, tl, sub_len):
    i = pl.program_id(1)
    hs = HALF_SPAN
    for gi in range(gb):
        kbuf[gi, 0:hs] = kp_ref[gi]
        kbuf[gi, hs:hs + tl] = kc_ref[gi]
        kbuf[gi, hs + tl:hs + tl + hs] = kn_ref[gi]
        vbuf[gi, 0:hs] = vp_ref[gi]
        vbuf[gi, hs:hs + tl] = vc_ref[gi]
        vbuf[gi, hs + tl:hs + tl + hs] = vn_ref[gi]
    tk = TQ + 2 * hs
    gw = HEAD_GROUP * HEAD_DIM
    qi = lax.broadcasted_iota(jnp.int32, (TQ, tk), 0)
    kj = lax.broadcasted_iota(jnp.int32, (TQ, tk), 1)
    band = (kj >= qi) & (kj - qi <= 2 * hs)
    lane = lax.broadcasted_iota(jnp.int32, (TQ, LANES), 1)
    head_of_lane = lax.broadcasted_iota(jnp.int32, (TQ, gw), 1) // HEAD_DIM
    for gi, sub in [(gi, sub) for gi in range(gb) for sub in range(tl // TQ)]:
        a = sub * TQ
        kpos = kj + (i * tl + a - hs)
        bias = jnp.where(band & (kpos >= 0) & (kpos < sub_len), 0.0, NEG_INF).astype(F32)
        bias = jnp.concatenate([bias] * HEAD_GROUP, axis=0)
        st = jnp.zeros((TQ, LANES), F32)
        for g in range(N_HEADS // HEAD_GROUP):
            cols = slice(g * gw, (g + 1) * gw)
            q4 = q_ref[gi, a:a + TQ, cols]
            k4 = kbuf[gi, a:a + tk, cols]
            v4 = vbuf[gi, a:a + tk, cols]
            lhs = jnp.concatenate([jnp.where(head_of_lane == h, q4, jnp.zeros_like(q4)) for h in range(HEAD_GROUP)], axis=0)
            s = lax.dot_general(lhs, k4, (((1,), (1,)), ((), ())), preferred_element_type=F32) + bias
            m = jnp.max(s, axis=-1, keepdims=True)
            p = jnp.exp(s - m)
            l = jnp.sum(p, axis=-1, keepdims=True)
            o_all = jnp.dot(p.astype(BF16), v4, preferred_element_type=F32) / l
            lse = m + jnp.log(l)
            o = o_all[0:TQ]
            for h in range(HEAD_GROUP):
                rows = slice(h * TQ, (h + 1) * TQ)
                if h:
                    o = jnp.where(head_of_lane == h, o_all[rows], o)
                st = jnp.where(lane == g * HEAD_GROUP + h, lse[rows], st)
            o_ref[gi, a:a + TQ, cols] = o.astype(BF16)
        st_ref[gi, a:a + TQ, :] = st


def _banded_attention(q, k, v):
    g, sub_len, _ = q.shape
    tl = min(TL_ATTN, sub_len)
    gb = min(TL_ATTN // tl, g)
    hs = HALF_SPAN
    per = tl // hs
    last = sub_len // hs - 1
    cur = pl.BlockSpec((gb, tl, WIDTH), lambda b, i: (b, i, 0))
    prev = pl.BlockSpec((gb, hs, WIDTH), lambda b, i: (b, jnp.maximum(i * per - 1, 0), 0))
    nxt = pl.BlockSpec((gb, hs, WIDTH), lambda b, i: (b, jnp.minimum((i + 1) * per, last), 0))
    return pl.pallas_call(
        functools.partial(_attn_kernel, gb=gb, tl=tl, sub_len=sub_len),
        grid=(g // gb, sub_len // tl),
        in_specs=[cur, prev, cur, nxt, prev, cur, nxt],
        out_specs=[cur, pl.BlockSpec((gb, tl, LANES), lambda b, i: (b, i, 0))],
        out_shape=[jax.ShapeDtypeStruct((g, sub_len, WIDTH), BF16),
                   jax.ShapeDtypeStruct((g, sub_len, LANES), F32)],
        scratch_shapes=[pltpu.VMEM((gb, tl + 2 * hs, WIDTH), BF16), pltpu.VMEM((gb, tl + 2 * hs, WIDTH), BF16)],
        compiler_params=_cparams("arbitrary", "arbitrary"),
        name="banded_attention",
    )(q, k, k, k, v, v, v)


def _ret_kernel(q_ref, k_ref, v_ref, gate_ref, dmat_ref, wq_ref, wk_ref, dec_ref, gn_ref, o_ref, sf_scr, sb_scr, *, c, n_chunks):
    pair = 2 * HEAD_DIM
    lane = lax.broadcasted_iota(jnp.int32, (c, pair), 1)
    head0 = lane < HEAD_DIM
    blk_r = lax.broadcasted_iota(jnp.int32, (pair, pair), 0) // HEAD_DIM
    blk_c = lax.broadcasted_iota(jnp.int32, (pair, pair), 1) // HEAD_DIM
    same_head = blk_r == blk_c
    dec_f = dec_ref[0:1, :]
    dec_b = dec_ref[1:2, :]
    tn = (((0,), (0,)), ((), ()))
    nt = (((1,), (1,)), ((), ()))

    def rows_of(n):
        return pl.ds(pl.multiple_of(n * c, c), c)

    def kv_body(n, carry):
        rows = rows_of(n)
        kf32 = k_ref[0, rows, :].astype(F32)
        kw = jnp.concatenate([(kf32 * wk_ref[:, :pair]).astype(BF16), (kf32 * wk_ref[:, pair:]).astype(BF16)], axis=1)
        kv = lax.dot_general(kw, v_ref[0, rows, :], tn, preferred_element_type=F32)
        sf_scr[n] = jnp.where(same_head, kv[:pair], 0.0)
        sb_scr[n] = jnp.where(same_head, kv[pair:], 0.0)
        return carry

    lax.fori_loop(0, n_chunks, kv_body, 0, unroll=RET_UNROLL)

    def scan_body(t, carry):
        sf, sb = carry
        nb = n_chunks - 1 - t
        kv_f = sf_scr[t]
        kv_b = sb_scr[nb]
        sf_scr[t] = sf
        sb_scr[nb] = sb
        return sf * dec_f + kv_f, sb * dec_b + kv_b

    zero = jnp.zeros((pair, pair), F32)
    lax.fori_loop(0, n_chunks, scan_body, (zero, zero))

    def out_body(n, carry):
        rows = rows_of(n)
        q = q_ref[0, rows, :]
        k = k_ref[0, rows, :]
        v = v_ref[0, rows, :]
        qf32 = q.astype(F32)
        intra = []
        for hh in range(2):
            mask = head0 if hh == 0 else jnp.logical_not(head0)
            kh = jnp.where(mask, k, jnp.zeros_like(k))
            s = lax.dot_general(q, kh, nt, preferred_element_type=F32)
            a = (s * dmat_ref[hh]).astype(BF16)
            intra.append(jnp.dot(a, v, preferred_element_type=F32))
        qw = jnp.concatenate([(qf32 * wq_ref[:, :pair]).astype(BF16), (qf32 * wq_ref[:, pair:]).astype(BF16)], axis=1)
        states = jnp.concatenate([sf_scr[n].astype(BF16), sb_scr[n].astype(BF16)], axis=0)
        tot = jnp.where(head0, intra[0], intra[1]) + jnp.dot(qw, states, preferred_element_type=F32)
        inv = 1.0 / HEAD_DIM
        s0 = jnp.sum(jnp.where(head0, tot, 0.0), axis=-1, keepdims=True)
        s1 = jnp.sum(jnp.where(head0, 0.0, tot), axis=-1, keepdims=True)
        xc = tot - jnp.where(head0, s0, s1) * inv
        sq = xc * xc
        v0 = jnp.sum(jnp.where(head0, sq, 0.0), axis=-1, keepdims=True)
        v1 = jnp.sum(jnp.where(head0, 0.0, sq), axis=-1, keepdims=True)
        y = xc * lax.rsqrt(jnp.where(head0, v0, v1) * inv + NORM_EPS)
        gt = gate_ref[0, rows, :].astype(F32)
        y = y * gn_ref[...] * (gt / (1.0 + jnp.exp(-gt)))
        o_ref[0, rows, :] = y.astype(BF16)
        return carry

    lax.fori_loop(0, n_chunks, out_body, 0, unroll=RET_UNROLL)


def _retention_tables(c):
    hidx = jnp.arange(N_HEADS, dtype=F32)
    lg_f = jnp.log1p(-jnp.exp2(-(RET_DECAY_BASE_FWD + hidx)))
    lg_b = jnp.log1p(-jnp.exp2(-(RET_DECAY_BASE_BWD + hidx)))
    pos = jnp.arange(c, dtype=F32)
    diff = pos[:, None] - pos[None, :]
    dm_f = jnp.exp(jnp.maximum(diff, 0.0)[None] * lg_f[:, None, None])
    dm_b = jnp.exp(jnp.maximum(-diff, 0.0)[None] * lg_b[:, None, None])
    dmat = jnp.where((diff >= 0)[None], dm_f, dm_b)

    def per_lane(tab):
        t = jnp.repeat(tab[:, :, None], HEAD_DIM, axis=2)
        return t.reshape(N_HEADS // 2, 2, c, HEAD_DIM).transpose(0, 2, 1, 3).reshape(N_HEADS // 2, c, 2 * HEAD_DIM)

    wq_f = per_lane(jnp.exp((pos + 1.0)[None, :] * lg_f[:, None]))
    wq_b = per_lane(jnp.exp((c - pos)[None, :] * lg_b[:, None]))
    wk_f = per_lane(jnp.exp((c - 1.0 - pos)[None, :] * lg_f[:, None]))
    wk_b = per_lane(jnp.exp(pos[None, :] * lg_b[:, None]))
    wq = jnp.concatenate([wq_f, wq_b], axis=2)
    wk = jnp.concatenate([wk_f, wk_b], axis=2)
    dec = jnp.stack([jnp.repeat(jnp.exp(c * lg_f), HEAD_DIM), jnp.repeat(jnp.exp(c * lg_b), HEAD_DIM)], axis=0)
    dec = dec.reshape(2, N_HEADS // 2, 2 * HEAD_DIM).transpose(1, 0, 2)
    return dmat, wq, wk, dec


def _retention(q, k, v, gate, ret_norm_g, b, s):
    c = min(RET_CHUNK, s)
    n_chunks = s // c
    pair = 2 * HEAD_DIM
    dmat, wq, wk, dec = _retention_tables(c)
    seq_blk = pl.BlockSpec((1, s, pair), lambda bi, hp: (bi, 0, hp))
    r3 = lambda z: z.reshape(b, s, WIDTH)
    return pl.pallas_call(
        functools.partial(_ret_kernel, c=c, n_chunks=n_chunks),
        grid=(b, N_HEADS // 2),
        in_specs=[seq_blk, seq_blk, seq_blk, seq_blk,
                  pl.BlockSpec((2, c, c), lambda bi, hp: (hp, 0, 0)),
                  pl.BlockSpec((None, c, 2 * pair), lambda bi, hp: (hp, 0, 0)),
                  pl.BlockSpec((None, c, 2 * pair), lambda bi, hp: (hp, 0, 0)),
                  pl.BlockSpec((None, 2, pair), lambda bi, hp: (hp, 0, 0)),
                  pl.BlockSpec((1, pair), lambda bi, hp: (0, hp))],
        out_specs=seq_blk,
        out_shape=jax.ShapeDtypeStruct((b, s, WIDTH), BF16),
        scratch_shapes=[pltpu.VMEM((n_chunks, pair, pair), F32), pltpu.VMEM((n_chunks, pair, pair), F32)],
        compiler_params=_cparams("arbitrary", "arbitrary"),
        name="retention",
    )(r3(q), r3(k), r3(v), r3(gate), dmat, wq, wk, dec, ret_norm_g.reshape(1, WIDTH)).reshape(b * s, WIDTH)


N_PIECES = 4
HALF_D = D_MODEL // 2
assert N_PIECES * LANES == HALF_D


def _pack_bf16_pair(a, b):
    ua = pltpu.bitcast(a.astype(BF16).astype(F32), jnp.uint32)
    ub = pltpu.bitcast(b.astype(BF16).astype(F32), jnp.uint32)
    return ua | (ub >> 16)


def _unpack_bf16_pair(u):
    return pltpu.bitcast(u & jnp.uint32(0xFFFF0000), F32), pltpu.bitcast(u << 16, F32)


def _store_pieces(ref, val):
    for s in range(N_PIECES):
        lo = slice(s * LANES, (s + 1) * LANES)
        hi = slice(HALF_D + s * LANES, HALF_D + (s + 1) * LANES)
        ref[s] = _pack_bf16_pair(val[:, lo], val[:, hi])


def _load_pieces(ref):
    first, second = [], []
    for s in range(N_PIECES):
        a, b = _unpack_bf16_pair(ref[s])
        first.append(a)
        second.append(b)
    return first, second


def _split_bf16(x):
    hi = x.astype(BF16)
    return hi, (x - hi.astype(F32)).astype(BF16)


def _outproj_kernel(x_ref, o1_ref, o2_ref, o3_ref, s1_ref, s2_ref, s3_ref, ret_ref, wout_ref, expand_ref,
                    gffn_ref, wr_ref, br_ref, tri_ref,
                    x1_ref, h2t_ref, ri_ref, gates_ref, cnt_ref, base_scr, nat_o, nat_s, grp_o):
    i = pl.program_id(0)
    tm = x_ref.shape[0]

    @pl.when(i == 0)
    def _():
        base_scr[...] = jnp.zeros_like(base_scr)

    n_col = WIDTH // LANES
    for bb in range(REGROUP):
        rows4 = pl.ds(bb, tm // REGROUP, stride=REGROUP)
        for c in range(n_col):
            nat_o[0, c, rows4, :] = o2_ref[0, bb, :, c * LANES:(c + 1) * LANES].astype(F32)
        nat_s[0, rows4, :] = s2_ref[0, bb]
        for aa in range(REGROUP):
            rows16 = pl.ds(aa, tm // REGROUP ** 2, stride=REGROUP)
            for c in range(n_col):
                grp_o[c, bb, rows16, :] = o3_ref[0, bb * REGROUP + aa, :, c * LANES:(c + 1) * LANES].astype(F32)
            grp_o[n_col, bb, rows16, :] = s3_ref[0, bb * REGROUP + aa]
        for c in range(n_col):
            nat_o[1, c, rows4, :] = grp_o[c, bb]
        nat_s[1, rows4, :] = grp_o[n_col, bb]
    hm = tm // OUT_SPLIT
    lane = lax.broadcasted_iota(jnp.int32, (hm, LANES), 1).astype(F32)
    base = base_scr[...]
    for r0 in range(0, tm, hm):
        rows = slice(r0, r0 + hm)
        o_nat = [[o1_ref[rows, c * LANES:(c + 1) * LANES].astype(F32) for c in range(n_col)]]
        sts = [s1_ref[rows, :]]
        for di in range(len(RESIDUE_DILATIONS)):
            o_nat.append([nat_o[di, c, rows, :] for c in range(n_col)])
            sts.append(nat_s[di, rows, :])

        mx = jnp.maximum(jnp.maximum(sts[0], sts[1]), sts[2])
        es = [jnp.exp(st - mx) for st in sts]
        den = es[0] + es[1] + es[2]
        attn = None
        for e, o_cols in zip(es, o_nat):
            hi, lo = _split_bf16(e / den)
            w_full = jnp.dot(jnp.concatenate([hi, lo], axis=1), expand_ref[...], preferred_element_type=F32)
            term = w_full * jnp.concatenate(o_cols, axis=1)
            attn = term if attn is None else attn + term
        mixed = jnp.concatenate([attn.astype(BF16), ret_ref[rows, :]], axis=1)
        x1 = x_ref[rows, :] + jnp.dot(mixed, wout_ref[...], preferred_element_type=F32)
        x1_ref[rows, :] = x1

        h2 = x1 * lax.rsqrt(jnp.mean(x1 * x1, axis=-1, keepdims=True) + NORM_EPS) * gffn_ref[...]
        for s in range(N_PIECES):
            h2t_ref[s, rows, :] = _pack_bf16_pair(h2[:, s * LANES:(s + 1) * LANES],
                                                  h2[:, HALF_D + s * LANES:HALF_D + (s + 1) * LANES])

        hi, lo = _split_bf16(h2)
        logits = jnp.dot(jnp.concatenate([hi, lo, hi], axis=1), wr_ref[...], preferred_element_type=F32) + br_ref[...]
        work = logits
        vals, idxs = [], []
        onehot = jnp.zeros((hm, LANES), F32)
        for _k in range(TOP_K):
            mk = jnp.max(work, axis=-1, keepdims=True)
            ik = jnp.min(jnp.where(work == mk, lane, float(LANES)), axis=-1, keepdims=True)
            sel = lane == ik
            onehot = jnp.where(sel, 1.0, onehot)
            work = jnp.where(sel, -jnp.inf, work)
            vals.append(mk)
            idxs.append(ik)
        ex = [jnp.exp(vk - vals[0]) for vk in vals]
        tot = ex[0] + ex[1] + ex[2] + ex[3]
        before = jnp.dot(tri_ref[...], onehot.astype(BF16), preferred_element_type=F32) + base
        ri = jnp.zeros((hm, LANES), F32)
        gt = jnp.zeros((hm, LANES), F32)
        for kk in range(TOP_K):
            rank = jnp.sum(jnp.where(lane == idxs[kk], before, 0.0), axis=-1, keepdims=True)
            ri = jnp.where(lane == float(kk), idxs[kk], ri)
            ri = jnp.where(lane == float(TOP_K + kk), rank, ri)
            gt = jnp.where(lane == float(kk), ex[kk] / tot, gt)
        ri_ref[:, rows] = jnp.transpose(ri)[:2 * TOP_K].astype(jnp.int32)
        gates_ref[rows, :] = gt
        base = base + jnp.sum(onehot, axis=0, keepdims=True)
    base_scr[...] = base
    cnt_ref[...] = base


def _outproj_router(x2d, outs, stats, ret, wout_bf16, norm_ffn_g, w_router, b_router, seq):
    t = x2d.shape[0]
    tm = TM_OUT
    per_seq = seq // tm
    row = lambda i: (i, 0)

    def res_spec(d, width):
        return pl.BlockSpec((1, d, tm // d, width), lambda i: (i // per_seq, 0, i % per_seq, 0))

    const = lambda i: (0, 0)
    head_of_col = jnp.arange(WIDTH) // HEAD_DIM
    expand = (jnp.arange(LANES)[:, None] == head_of_col[None, :]).astype(BF16)
    expand2 = jnp.concatenate([expand, expand], axis=0)
    wr = jnp.zeros((D_MODEL, LANES), F32).at[:, :N_EXPERTS].set(w_router)
    wr_hi, wr_lo = _split_bf16(wr)
    wr3 = jnp.concatenate([wr_hi, wr_hi, wr_lo], axis=0)
    br = jnp.full((1, LANES), NEG_INF, F32).at[0, :N_EXPERTS].set(b_router)
    hm = tm // OUT_SPLIT
    tri = (jnp.arange(hm)[:, None] > jnp.arange(hm)[None, :]).astype(BF16)
    o_spec = pl.BlockSpec((tm, WIDTH), row)
    s_spec = pl.BlockSpec((tm, LANES), row)
    return pl.pallas_call(
        _outproj_kernel,
        grid=(t // tm,),
        in_specs=[pl.BlockSpec((tm, D_MODEL), row),
                  o_spec, *[res_spec(d, WIDTH) for d in RESIDUE_DILATIONS],
                  s_spec, *[res_spec(d, LANES) for d in RESIDUE_DILATIONS], o_spec,
                  pl.BlockSpec((D_MODEL, D_MODEL), const), pl.BlockSpec((2 * LANES, WIDTH), const),
                  pl.BlockSpec((1, D_MODEL), const), pl.BlockSpec((3 * D_MODEL, LANES), const),
                  pl.BlockSpec((1, LANES), const), pl.BlockSpec((hm, hm), const)],
        out_specs=[pl.BlockSpec((tm, D_MODEL), row), pl.BlockSpec((N_PIECES, tm, LANES), lambda i: (0, i, 0)),
                   pl.BlockSpec((2 * TOP_K, tm), lambda i: (0, i)), s_spec, pl.BlockSpec((1, LANES), const)],
        out_shape=[jax.ShapeDtypeStruct((t, D_MODEL), F32), jax.ShapeDtypeStruct((N_PIECES, t, LANES), jnp.uint32),
                   jax.ShapeDtypeStruct((2 * TOP_K, t), jnp.int32), jax.ShapeDtypeStruct((t, LANES), F32),
                   jax.ShapeDtypeStruct((1, LANES), F32)],
        scratch_shapes=[pltpu.VMEM((1, LANES), F32),
                        pltpu.VMEM((len(RESIDUE_DILATIONS), WIDTH // LANES, tm, LANES), F32),
                        pltpu.VMEM((len(RESIDUE_DILATIONS), tm, LANES), F32),
                        pltpu.VMEM((WIDTH // LANES + 1, REGROUP, tm // REGROUP, LANES), F32)],
        compiler_params=_cparams("arbitrary"),
        name="outproj_router",
    )(x2d, *outs, *stats, ret, wout_bf16, expand2, norm_ffn_g.reshape(1, D_MODEL), wr3, br, tri)


SC_CORES = 2
SC_SUBCORES = 16
SC_WINDOW = 128


def _sc_mesh():
    return plsc.VectorSubcoreMesh(core_axis_name="c", subcore_axis_name="s")


def _sc_scatter_rows(src, idx, n_out_rows):
    n_rows = src.shape[0]
    workers = SC_CORES * SC_SUBCORES
    per_worker = n_rows // workers
    n_win = per_worker // SC_WINDOW
    assert per_worker * workers == n_rows and n_win * SC_WINDOW == per_worker and idx.shape == (TOP_K * n_rows,)

    @functools.partial(
        pl.kernel, mesh=_sc_mesh(),
        out_type=jax.ShapeDtypeStruct((n_out_rows, LANES), src.dtype),
        scratch_types=[pltpu.VMEM((SC_WINDOW,), jnp.int32), pltpu.VMEM((SC_WINDOW, LANES), src.dtype)],
        name="sc_scatter_rows",
    )
    def scatter(src_hbm, idx_hbm, out_hbm, idx_v, rows_v):
        wid = lax.axis_index("s") * SC_CORES + lax.axis_index("c")
        base = wid * per_worker

        @pl.loop(0, n_win)
        def _(j):
            off = pl.multiple_of(base + j * SC_WINDOW, SC_WINDOW)
            pltpu.sync_copy(src_hbm.at[pl.ds(off, SC_WINDOW)], rows_v)
            for kk in range(TOP_K):
                pltpu.sync_copy(idx_hbm.at[pl.ds(pl.multiple_of(kk * n_rows + off, SC_WINDOW), SC_WINDOW)], idx_v)
                pltpu.sync_copy(rows_v, out_hbm.at[idx_v])

    return scatter(src, idx)


def _expert_kernel(blk_e_ref, nvalid_ref, rows_ref, xs_ref, wgu_ref, bgu_ref, wd_ref, bd_ref, ys_ref, wgu_bf, wd_bf):
    i = pl.program_id(0)

    @pl.when((i == 0) | (blk_e_ref[i] != blk_e_ref[jnp.maximum(i - 1, 0)]))
    def _():
        wgu_bf[...] = wgu_ref[0].astype(BF16)
        wd_bf[...] = wd_ref[0].astype(BF16)

    def mlp(m):
        parts = [_unpack_bf16_pair(xs_ref[s, :m, :]) for s in range(N_PIECES)]
        cols = [p[0] for p in parts] + [p[1] for p in parts]
        live = lax.broadcasted_iota(jnp.int32, (m, LANES), 0) < rows_ref[i]
        x = jnp.concatenate([jnp.where(live, p, 0.0).astype(BF16) for p in cols], axis=1)
        gu = jnp.dot(x, wgu_bf[...], preferred_element_type=F32) + bgu_ref[0]
        gate = jnp.minimum(gu[:, :EXPERT_FF], SWIGLU_LIMIT)
        up = jnp.clip(gu[:, EXPERT_FF:], -SWIGLU_LIMIT, SWIGLU_LIMIT)
        act = gate * (1.0 / (1.0 + jnp.exp(-SWIGLU_ALPHA * gate))) * (up + 1.0)
        y = jnp.dot(act.astype(BF16), wd_bf[...], preferred_element_type=F32) + bd_ref[0]
        for s in range(N_PIECES):
            ys_ref[s, :m, :] = _pack_bf16_pair(y[:, s * LANES:(s + 1) * LANES],
                                               y[:, HALF_D + s * LANES:HALF_D + (s + 1) * LANES])

    n_sub = lax.shift_right_logical(rows_ref[i] + (EXPERT_SUB - 1), EXPERT_SUB.bit_length() - 1)
    for v in range(1, BM // EXPERT_SUB + 1):
        pl.when((i < nvalid_ref[0]) & (n_sub == v))(functools.partial(mlp, v * EXPERT_SUB))


def _experts(xs, blk_e, nvalid, live_rows, wgu_bf16, bgu, wd_bf16, bd):
    cap = xs.shape[1]
    nblk = cap // BM

    def blk(i, be, nv, lr):
        return (0, jnp.minimum(i, nv[0] - 1), 0)

    def by_expert(i, be, nv, lr):
        return (be[i], 0, 0)

    return pl.pallas_call(
        _expert_kernel,
        grid_spec=pltpu.PrefetchScalarGridSpec(
            num_scalar_prefetch=3,
            grid=(nblk,),
            in_specs=[pl.BlockSpec((N_PIECES, BM, LANES), blk),
                      pl.BlockSpec((1, D_MODEL, 2 * EXPERT_FF), by_expert),
                      pl.BlockSpec((1, 1, 2 * EXPERT_FF), by_expert),
                      pl.BlockSpec((1, EXPERT_FF, D_MODEL), by_expert),
                      pl.BlockSpec((1, 1, D_MODEL), by_expert)],
            out_specs=pl.BlockSpec((N_PIECES, BM, LANES), blk),
            scratch_shapes=[pltpu.VMEM((D_MODEL, 2 * EXPERT_FF), BF16), pltpu.VMEM((EXPERT_FF, D_MODEL), BF16)],
        ),
        out_shape=jax.ShapeDtypeStruct((N_PIECES, cap, LANES), jnp.uint32),
        compiler_params=_cparams("arbitrary"),
        name="moe_experts",
    )(blk_e, nvalid, live_rows, xs, wgu_bf16, bgu.reshape(N_EXPERTS, 1, 2 * EXPERT_FF), wd_bf16,
      bd.reshape(N_EXPERTS, 1, D_MODEL))


def _sc_gather_rows(table, idx):
    n_rows = idx.shape[0]
    workers = SC_CORES * SC_SUBCORES
    per_worker = n_rows // workers
    n_win = per_worker // SC_WINDOW
    assert per_worker * workers == n_rows and n_win * SC_WINDOW == per_worker

    @functools.partial(
        pl.kernel, mesh=_sc_mesh(),
        out_type=jax.ShapeDtypeStruct((n_rows, LANES), table.dtype),
        scratch_types=[pltpu.VMEM((SC_WINDOW,), jnp.int32), pltpu.VMEM((SC_WINDOW, LANES), table.dtype),
                       pltpu.SemaphoreType.DMA],
        name="sc_gather_rows",
    )
    def gather(table_hbm, idx_hbm, out_hbm, idx_v, rows_v, sem):
        wid = lax.axis_index("s") * SC_CORES + lax.axis_index("c")
        base = wid * per_worker

        @pl.loop(0, n_win)
        def _(j):
            off = pl.multiple_of(base + j * SC_WINDOW, SC_WINDOW)
            pltpu.sync_copy(idx_hbm.at[pl.ds(off, SC_WINDOW)], idx_v)
            pltpu.async_copy(table_hbm.at[idx_v], rows_v, sem).wait()
            pltpu.sync_copy(rows_v, out_hbm.at[pl.ds(off, SC_WINDOW)])

    return gather(table, idx)


def _combine_kernel(x1_ref, gates_ref, gfin_ref, *refs, tc):
    piece_refs, o_ref = refs[:TOP_K * N_PIECES], refs[TOP_K * N_PIECES]
    g = gates_ref[...]
    gk = [jnp.broadcast_to(g[:, kk:kk + 1], (tc, LANES)) for kk in range(TOP_K)]
    zs = {}
    ssq = jnp.zeros((tc, 1), F32)
    for s in range(N_PIECES):
        c_lo, c_hi = s * LANES, HALF_D + s * LANES
        z_lo = x1_ref[:, c_lo:c_lo + LANES]
        z_hi = x1_ref[:, c_hi:c_hi + LANES]
        for kk in range(TOP_K):
            a, b = _unpack_bf16_pair(piece_refs[kk * N_PIECES + s][...])
            z_lo = z_lo + gk[kk] * a
            z_hi = z_hi + gk[kk] * b
        zs[c_lo], zs[c_hi] = z_lo, z_hi
        ssq = ssq + jnp.sum(z_lo * z_lo + z_hi * z_hi, axis=-1, keepdims=True)
    inv = lax.rsqrt(ssq * (1.0 / D_MODEL) + NORM_EPS)
    for c0, z in zs.items():
        o_ref[:, c0:c0 + LANES] = z * inv * gfin_ref[:, c0:c0 + LANES]


def _combine(x1, gates, gathered, norm_final_g):
    t = x1.shape[0]
    tc = TC
    row = lambda i: (i, 0)
    slot_specs = [pl.BlockSpec((None, tc, LANES), functools.partial(lambda i, j: (j, i, 0), j=j))
                  for j in range(TOP_K * N_PIECES)]
    return pl.pallas_call(
        functools.partial(_combine_kernel, tc=tc),
        grid=(t // tc,),
        in_specs=[pl.BlockSpec((tc, D_MODEL), row), pl.BlockSpec((tc, LANES), row),
                  pl.BlockSpec((1, D_MODEL), lambda i: (0, 0)), *slot_specs],
        out_specs=pl.BlockSpec((tc, D_MODEL), row),
        out_shape=jax.ShapeDtypeStruct((t, D_MODEL), F32),
        compiler_params=_cparams("arbitrary"),
        name="moe_combine",
    )(x1, gates, norm_final_g.reshape(1, D_MODEL), *([gathered] * (TOP_K * N_PIECES)))


def _moe(x1, h2t, ri, gates, counts_f, wgu_bf16, bgu, wd_bf16, bd, norm_final_g):
    t = x1.shape[0]
    a = t * TOP_K
    cap = a + N_EXPERTS * BM
    nblk = cap // BM
    counts = counts_f[0, :N_EXPERTS].astype(jnp.int32)
    padded = ((counts + BM - 1) // BM) * BM
    pend = jnp.cumsum(padded)
    pstart = pend - padded
    nvalid = (pend[-1] // BM).reshape(1)
    first_row = jnp.minimum(jnp.arange(nblk, dtype=jnp.int32) * BM, pend[-1] - 1)
    blk_e = jnp.sum(pend[None, :] <= first_row[:, None], axis=1).astype(jnp.int32)
    idx, rank = ri[:TOP_K], ri[TOP_K:2 * TOP_K]
    onehot = idx[None, :, :] == jnp.arange(N_EXPERTS, dtype=jnp.int32)[:, None, None]
    dest = rank + jnp.sum(jnp.where(onehot, pstart[:, None, None], 0), axis=0)
    seg_end = (pstart + counts)[blk_e]
    live_rows = jnp.clip(seg_end - jnp.arange(nblk, dtype=jnp.int32) * BM, 0, BM).astype(jnp.int32)
    piece = dest[:, None, :] + (jnp.arange(N_PIECES, dtype=jnp.int32) * cap)[None, :, None]
    piece = piece.reshape(TOP_K * N_PIECES * t)
    xs = _sc_scatter_rows(h2t.reshape(N_PIECES * t, LANES), piece, N_PIECES * cap)
    ys = _experts(xs.reshape(N_PIECES, cap, LANES), blk_e, nvalid, live_rows, wgu_bf16, bgu, wd_bf16, bd)
    gathered = _sc_gather_rows(ys.reshape(N_PIECES * cap, LANES), piece)
    return _combine(x1, gates, gathered.reshape(TOP_K * N_PIECES, t, LANES), norm_final_g)


def _encoder(x, p):
    b, s, _ = x.shape
    x2d = x.reshape(b * s, D_MODEL)
    cos_t, sin_t = _rope_tables(s)
    (qa, ka, va, qr, kr, vr, gr), by_residue = _inproj(x2d, p["norm_mix_g"], p["w_in"], cos_t, sin_t, s)
    outs, stats = [], []
    for d in DILATIONS:
        if d == 1:
            o, st = _banded_attention(*[z.reshape(b, s, WIDTH) for z in (qa, ka, va)])
            outs.append(o.reshape(b * s, WIDTH))
            stats.append(st.reshape(b * s, LANES))
        else:
            o, st = _banded_attention(*[z.reshape(b * d, s // d, WIDTH) for z in by_residue[d]])
            outs.append(o.reshape(b, d, s // d, WIDTH))
            stats.append(st.reshape(b, d, s // d, LANES))
    ret = _retention(qr, kr, vr, gr, p["ret_norm_g"], b, s)
    x1, h2t, ri, gates, counts = _outproj_router(x2d, outs, stats, ret, p["w_out"], p["norm_ffn_g"],
                                                 p["w_router"], p["b_router"], s)
    y = _moe(x1, h2t, ri, gates, counts, p["w_gate_up"], p["b_gate_up"], p["w_down"], p["b_down"], p["norm_final_g"])
    return y.reshape(b, s, D_MODEL)


def kernel(x_prompt, x_sample, norm_mix_g, w_in, ret_norm_g, w_out, norm_ffn_g, w_router, b_router, w_gate_up, b_gate_up, w_down, b_down, norm_final_g):
    assert norm_mix_g.shape[0] == 1, "single layer"
    p = dict(norm_mix_g=norm_mix_g[0], w_in=w_in[0].astype(BF16), ret_norm_g=ret_norm_g[0],
             w_out=w_out[0].astype(BF16), norm_ffn_g=norm_ffn_g[0], w_router=w_router[0], b_router=b_router[0],
             w_gate_up=w_gate_up[0], b_gate_up=b_gate_up[0], w_down=w_down[0],
             b_down=b_down[0], norm_final_g=norm_final_g)
    return (_encoder(x_prompt, p), _encoder(x_sample, p))
```

```python
import functools

import jax
import jax.numpy as jnp
from jax import lax
from jax.experimental import pallas as pl
from jax.experimental.pallas import tpu as pltpu
from jax.experimental.pallas import tpu_sc as plsc

D_MODEL = 1024
HEAD_DIM = 64
N_HEADS = 8
WIDTH = N_HEADS * HEAD_DIM
N_SLABS = 7
DILATIONS = (1, 4, 16)
HALF_SPAN = 64
ROPE_THETA = 10000.0
RET_DECAY_BASE_FWD = 5.0
RET_DECAY_BASE_BWD = 5.5
N_EXPERTS = 32
TOP_K = 4
EXPERT_FF = D_MODEL
SWIGLU_LIMIT = 7.0
SWIGLU_ALPHA = 1.702
NORM_EPS = 1e-6
NEG_INF = -1e30

LANES = 128
VMEM_LIMIT_BYTES = 48 * 1024 * 1024

TM_INPROJ = 512
TQ = 128
TL_ATTN = 2048
HEAD_GROUP = 4
RET_CHUNK = 256
TM_OUT = 1024
OUT_SPLIT = 4
RET_UNROLL = 8
BM = 1024
EXPERT_SUB = 256
TC = 1024

F32 = jnp.float32
BF16 = jnp.bfloat16


def _cparams(*sem):
    return pltpu.CompilerParams(dimension_semantics=sem, vmem_limit_bytes=VMEM_LIMIT_BYTES)


_ROTATE = (True, True, False, True, True, False, False)
_SCALE = (HEAD_DIM ** -0.5, 1.0, 1.0, 1.0, HEAD_DIM ** -0.5, 1.0, 1.0)


N_ATTN_SLABS = 3
RESIDUE_DILATIONS = tuple(d for d in DILATIONS if d > 1)
REGROUP = 4
assert RESIDUE_DILATIONS == (REGROUP, REGROUP ** 2)


def _inproj_kernel(x_ref, g_ref, w_ref, cos_ref, sin_ref, *refs):
    out_refs = refs[:N_SLABS]
    res_refs = refs[N_SLABS:N_SLABS + N_ATTN_SLABS * len(RESIDUE_DILATIONS)]
    stage, stage4 = refs[-2:]
    x = x_ref[...]
    tm = x.shape[0]
    h = (x * lax.rsqrt(jnp.mean(x * x, axis=-1, keepdims=True) + NORM_EPS) * g_ref[...]).astype(BF16)
    cos = cos_ref[...]
    sin = sin_ref[...]
    lane = lax.broadcasted_iota(jnp.int32, cos.shape, 1)
    first_half = (lane & (HEAD_DIM - 1)) < HEAD_DIM // 2
    for j, o_ref in enumerate(out_refs):
        p = jnp.dot(h, w_ref[:, j * WIDTH:(j + 1) * WIDTH], preferred_element_type=F32)
        for c in range(WIDTH // LANES):
            cols = slice(c * LANES, (c + 1) * LANES)
            r = p[:, cols]
            if _ROTATE[j]:
                partner = jnp.where(first_half, pltpu.roll(r, LANES - HEAD_DIM // 2, 1), pltpu.roll(r, HEAD_DIM // 2, 1))
                r = r * cos + partner * sin
                if _SCALE[j] != 1.0:
                    r = r * _SCALE[j]
            o_ref[:, cols] = r.astype(BF16)
            if j < N_ATTN_SLABS:
                stage[j, c] = r
                dst4 = res_refs[j]
                dst16 = res_refs[N_ATTN_SLABS + j]
                for bb in range(REGROUP):
                    grp = stage[j, c, pl.ds(bb, tm // REGROUP, stride=REGROUP), :]
                    dst4[0, bb, :, cols] = grp.astype(BF16)
                    stage4[j, c, bb] = grp
                    for aa in range(REGROUP):
                        sub = stage4[j, c, bb, pl.ds(aa, tm // REGROUP ** 2, stride=REGROUP), :]
                        dst16[0, bb * REGROUP + aa, :, cols] = sub.astype(BF16)


def _inproj(x2d, g, w_bf16, cos_t, sin_t, seq):
    t = x2d.shape[0]
    tm = min(TM_INPROJ, seq)
    pos_blocks = seq // tm
    b = t // seq
    out = jax.ShapeDtypeStruct((t, WIDTH), BF16)
    row = lambda i: (i, 0)
    out_specs = [pl.BlockSpec((tm, WIDTH), row)] * N_SLABS
    out_shape = [out] * N_SLABS
    for d in RESIDUE_DILATIONS:
        out_specs += [pl.BlockSpec((1, d, tm // d, WIDTH), lambda i: (i // pos_blocks, 0, i % pos_blocks, 0))] * N_ATTN_SLABS
        out_shape += [jax.ShapeDtypeStruct((b, d, seq // d, WIDTH), BF16)] * N_ATTN_SLABS
    res = pl.pallas_call(
        _inproj_kernel,
        grid=(t // tm,),
        in_specs=[
            pl.BlockSpec((tm, D_MODEL), row),
            pl.BlockSpec((1, D_MODEL), lambda i: (0, 0)),
            pl.BlockSpec((D_MODEL, N_SLABS * WIDTH), lambda i: (0, 0)),
            pl.BlockSpec((tm, LANES), lambda i: (i % pos_blocks, 0)),
            pl.BlockSpec((tm, LANES), lambda i: (i % pos_blocks, 0)),
        ],
        out_specs=out_specs,
        out_shape=out_shape,
        scratch_shapes=[pltpu.VMEM((N_ATTN_SLABS, WIDTH // LANES, tm, LANES), F32),
                        pltpu.VMEM((N_ATTN_SLABS, WIDTH // LANES, REGROUP, tm // REGROUP, LANES), F32)],
        compiler_params=_cparams("arbitrary"),
        name="inproj",
    )(x2d, g.reshape(1, D_MODEL), w_bf16, cos_t, sin_t)
    natural = res[:N_SLABS]
    by_residue = {d: res[N_SLABS + di * N_ATTN_SLABS:N_SLABS + (di + 1) * N_ATTN_SLABS]
                  for di, d in enumerate(RESIDUE_DILATIONS)}
    return natural, by_residue


def _rope_tables(seq):
    half = HEAD_DIM // 2
    inv_freq = ROPE_THETA ** (-jnp.arange(0, HEAD_DIM, 2, dtype=F32) / HEAD_DIM)
    ang = jnp.arange(seq, dtype=F32)[:, None] * inv_freq[None, :]
    cos, sin = jnp.cos(ang), jnp.sin(ang)
    reps = LANES // HEAD_DIM
    cos_t = jnp.tile(jnp.concatenate([cos, cos], axis=1), (1, reps))
    sin_t = jnp.tile(jnp.concatenate([-sin, sin], axis=1), (1, reps))
    assert cos_t.shape == (seq, LANES) and half * 2 == HEAD_DIM
    return cos_t, sin_t


def _attn_kernel(q_ref, kp_ref, kc_ref, kn_ref, vp_ref, vc_ref, vn_ref, o_ref, st_ref, kbuf, vbuf, *, gb, tl, sub_len):
    i = pl.program_id(1)
    hs = HALF_SPAN
    for gi in range(gb):
        kbuf[gi, 0:hs] = kp_ref[gi]
        kbuf[gi, hs:hs + tl] = kc_ref[gi]
        kbuf[gi, hs + tl:hs + tl + hs] = kn_ref[gi]
        vbuf[gi, 0:hs] = vp_ref[gi]
        vbuf[gi, hs:hs + tl] = vc_ref[gi]
        vbuf[gi, hs + tl:hs + tl + hs] = vn_ref[gi]
    tk = TQ + 2 * hs
    gw = HEAD_GROUP * HEAD_DIM
    qi = lax.broadcasted_iota(jnp.int32, (TQ, tk), 0)
    kj = lax.broadcasted_iota(jnp.int32, (TQ, tk), 1)
    band_bias = jnp.where((kj >= qi) & (kj - qi <= 2 * hs), 0.0, NEG_INF).astype(F32)
    key_col = lax.broadcasted_iota(jnp.int32, (1, tk), 1)
    lane = lax.broadcasted_iota(jnp.int32, (TQ, LANES), 1)
    head_of_lane = lax.broadcasted_iota(jnp.int32, (TQ, gw), 1) // HEAD_DIM
    n_sub = tl // TQ
    for gi, sub in [(gi, sub) for gi in range(gb) for sub in range(n_sub)]:
        a = sub * TQ
        bias = band_bias
        if sub == 0 or sub == n_sub - 1:
            kpos = key_col + (i * tl + a - hs)
            bias = bias + jnp.where((kpos >= 0) & (kpos < sub_len), 0.0, NEG_INF).astype(F32)
        bias = jnp.concatenate([bias] * HEAD_GROUP, axis=0)
        st = jnp.zeros((TQ, LANES), F32)
        for g in range(N_HEADS // HEAD_GROUP):
            cols = slice(g * gw, (g + 1) * gw)
            q4 = q_ref[gi, a:a + TQ, cols]
            k4 = kbuf[gi, a:a + tk, cols]
            v4 = vbuf[gi, a:a + tk, cols]
            lhs = jnp.concatenate([jnp.where(head_of_lane == h, q4, jnp.zeros_like(q4)) for h in range(HEAD_GROUP)], axis=0)
            s = lax.dot_general(lhs, k4, (((1,), (1,)), ((), ())), preferred_element_type=F32) + bias
            m = jnp.max(s, axis=-1, keepdims=True)
            p = jnp.exp(s - m)
            l = jnp.sum(p, axis=-1, keepdims=True)
            o_all = jnp.dot(p.astype(BF16), v4, preferred_element_type=F32) / l
            lse = m + jnp.log(l)
            o = o_all[0:TQ]
            for h in range(HEAD_GROUP):
                rows = slice(h * TQ, (h + 1) * TQ)
                if h:
                    o = jnp.where(head_of_lane == h, o_all[rows], o)
                st = jnp.where(lane == g * HEAD_GROUP + h, lse[rows], st)
            o_ref[gi, a:a + TQ, cols] = o.astype(BF16)
        st_ref[gi, a:a + TQ, :] = st


def _banded_attention(q, k, v):
    g, sub_len, _ = q.shape
    tl = min(TL_ATTN, sub_len)
    gb = min(TL_ATTN // tl, g)
    hs = HALF_SPAN
    per = tl // hs
    last = sub_len // hs - 1
    cur = pl.BlockSpec((gb, tl, WIDTH), lambda b, i: (b, i, 0))
    prev = pl.BlockSpec((gb, hs, WIDTH), lambda b, i: (b, jnp.maximum(i * per - 1, 0), 0))
    nxt = pl.BlockSpec((gb, hs, WIDTH), lambda b, i: (b, jnp.minimum((i + 1) * per, last), 0))
    return pl.pallas_call(
        functools.partial(_attn_kernel, gb=gb, tl=tl, sub_len=sub_len),
        grid=(g // gb, sub_len // tl),
        in_specs=[cur, prev, cur, nxt, prev, cur, nxt],
        out_specs=[cur, pl.BlockSpec((gb, tl, LANES), lambda b, i: (b, i, 0))],
        out_shape=[jax.ShapeDtypeStruct((g, sub_len, WIDTH), BF16),
                   jax.ShapeDtypeStruct((g, sub_len, LANES), F32)],
        scratch_shapes=[pltpu.VMEM((gb, tl + 2 * hs, WIDTH), BF16), pltpu.VMEM((gb, tl + 2 * hs, WIDTH), BF16)],
        compiler_params=_cparams("arbitrary", "arbitrary"),
        name="banded_attention",
    )(q, k, k, k, v, v, v)


def _ret_kernel(q_ref, k_ref, v_ref, gate_ref, dmat_ref, wq_ref, wk_ref, dec_ref, gn_ref, o_ref, sf_scr, sb_scr, *, c, n_chunks):
    pair = 2 * HEAD_DIM
    lane = lax.broadcasted_iota(jnp.int32, (c, pair), 1)
    head0 = lane < HEAD_DIM
    blk_r = lax.broadcasted_iota(jnp.int32, (pair, pair), 0) // HEAD_DIM
    blk_c = lax.broadcasted_iota(jnp.int32, (pair, pair), 1) // HEAD_DIM
    same_head = blk_r == blk_c
    dec_f = dec_ref[0:1, :]
    dec_b = dec_ref[1:2, :]
    tn = (((0,), (0,)), ((), ()))
    nt = (((1,), (1,)), ((), ()))

    def rows_of(n):
        return pl.ds(pl.multiple_of(n * c, c), c)

    def kv_body(n, carry):
        rows = rows_of(n)
        kf32 = k_ref[0, rows, :].astype(F32)
        kw = jnp.concatenate([(kf32 * wk_ref[:, :pair]).astype(BF16), (kf32 * wk_ref[:, pair:]).astype(BF16)], axis=1)
        kv = lax.dot_general(kw, v_ref[0, rows, :], tn, preferred_element_type=F32)
        sf_scr[n] = jnp.where(same_head, kv[:pair], 0.0)
        sb_scr[n] = jnp.where(same_head, kv[pair:], 0.0)
        return carry

    lax.fori_loop(0, n_chunks, kv_body, 0, unroll=RET_UNROLL)

    def scan_body(t, carry):
        sf, sb = carry
        nb = n_chunks - 1 - t
        kv_f = sf_scr[t]
        kv_b = sb_scr[nb]
        sf_scr[t] = sf
        sb_scr[nb] = sb
        return sf * dec_f + kv_f, sb * dec_b + kv_b

    zero = jnp.zeros((pair, pair), F32)
    lax.fori_loop(0, n_chunks, scan_body, (zero, zero))

    def out_body(n, carry):
        rows = rows_of(n)
        q = q_ref[0, rows, :]
        k = k_ref[0, rows, :]
        v = v_ref[0, rows, :]
        qf32 = q.astype(F32)
        intra = []
        for hh in range(2):
            mask = head0 if hh == 0 else jnp.logical_not(head0)
            kh = jnp.where(mask, k, jnp.zeros_like(k))
            s = lax.dot_general(q, kh, nt, preferred_element_type=F32)
            a = (s * dmat_ref[hh]).astype(BF16)
            intra.append(jnp.dot(a, v, preferred_element_type=F32))
        qw = jnp.concatenate([(qf32 * wq_ref[:, :pair]).astype(BF16), (qf32 * wq_ref[:, pair:]).astype(BF16)], axis=1)
        states = jnp.concatenate([sf_scr[n].astype(BF16), sb_scr[n].astype(BF16)], axis=0)
        tot = jnp.where(head0, intra[0], intra[1]) + jnp.dot(qw, states, preferred_element_type=F32)
        inv = 1.0 / HEAD_DIM
        s0 = jnp.sum(jnp.where(head0, tot, 0.0), axis=-1, keepdims=True)
        s1 = jnp.sum(jnp.where(head0, 0.0, tot), axis=-1, keepdims=True)
        xc = tot - jnp.where(head0, s0, s1) * inv
        sq = xc * xc
        v0 = jnp.sum(jnp.where(head0, sq, 0.0), axis=-1, keepdims=True)
        v1 = jnp.sum(jnp.where(head0, 0.0, sq), axis=-1, keepdims=True)
        y = xc * lax.rsqrt(jnp.where(head0, v0, v1) * inv + NORM_EPS)
        gt = gate_ref[0, rows, :].astype(F32)
        y = y * gn_ref[...] * (gt / (1.0 + jnp.exp(-gt)))
        o_ref[0, rows, :] = y.astype(BF16)
        return carry

    lax.fori_loop(0, n_chunks, out_body, 0, unroll=RET_UNROLL)


def _retention_tables(c):
    hidx = jnp.arange(N_HEADS, dtype=F32)
    lg_f = jnp.log1p(-jnp.exp2(-(RET_DECAY_BASE_FWD + hidx)))
    lg_b = jnp.log1p(-jnp.exp2(-(RET_DECAY_BASE_BWD + hidx)))
    pos = jnp.arange(c, dtype=F32)
    diff = pos[:, None] - pos[None, :]
    dm_f = jnp.exp(jnp.maximum(diff, 0.0)[None] * lg_f[:, None, None])
    dm_b = jnp.exp(jnp.maximum(-diff, 0.0)[None] * lg_b[:, None, None])
    dmat = jnp.where((diff >= 0)[None], dm_f, dm_b)

    def per_lane(tab):
        t = jnp.repeat(tab[:, :, None], HEAD_DIM, axis=2)
        return t.reshape(N_HEADS // 2, 2, c, HEAD_DIM).transpose(0, 2, 1, 3).reshape(N_HEADS // 2, c, 2 * HEAD_DIM)

    wq_f = per_lane(jnp.exp((pos + 1.0)[None, :] * lg_f[:, None]))
    wq_b = per_lane(jnp.exp((c - pos)[None, :] * lg_b[:, None]))
    wk_f = per_lane(jnp.exp((c - 1.0 - pos)[None, :] * lg_f[:, None]))
    wk_b = per_lane(jnp.exp(pos[None, :] * lg_b[:, None]))
    wq = jnp.concatenate([wq_f, wq_b], axis=2)
    wk = jnp.concatenate([wk_f, wk_b], axis=2)
    dec = jnp.stack([jnp.repeat(jnp.exp(c * lg_f), HEAD_DIM), jnp.repeat(jnp.exp(c * lg_b), HEAD_DIM)], axis=0)
    dec = dec.reshape(2, N_HEADS // 2, 2 * HEAD_DIM).transpose(1, 0, 2)
    return dmat, wq, wk, dec


def _retention(q, k, v, gate, ret_norm_g, b, s):
    c = min(RET_CHUNK, s)
    n_chunks = s // c
    pair = 2 * HEAD_DIM
    dmat, wq, wk, dec = _retention_tables(c)
    seq_blk = pl.BlockSpec((1, s, pair), lambda bi, hp: (bi, 0, hp))
    r3 = lambda z: z.reshape(b, s, WIDTH)
    return pl.pallas_call(
        functools.partial(_ret_kernel, c=c, n_chunks=n_chunks),
        grid=(b, N_HEADS // 2),
        in_specs=[seq_blk, seq_blk, seq_blk, seq_blk,
                  pl.BlockSpec((2, c, c), lambda bi, hp: (hp, 0, 0)),
                  pl.BlockSpec((None, c, 2 * pair), lambda bi, hp: (hp, 0, 0)),
                  pl.BlockSpec((None, c, 2 * pair), lambda bi, hp: (hp, 0, 0)),
                  pl.BlockSpec((None, 2, pair), lambda bi, hp: (hp, 0, 0)),
                  pl.BlockSpec((1, pair), lambda bi, hp: (0, hp))],
        out_specs=seq_blk,
        out_shape=jax.ShapeDtypeStruct((b, s, WIDTH), BF16),
        scratch_shapes=[pltpu.VMEM((n_chunks, pair, pair), F32), pltpu.VMEM((n_chunks, pair, pair), F32)],
        compiler_params=_cparams("arbitrary", "arbitrary"),
        name="retention",
    )(r3(q), r3(k), r3(v), r3(gate), dmat, wq, wk, dec, ret_norm_g.reshape(1, WIDTH)).reshape(b * s, WIDTH)


N_PIECES = 4
HALF_D = D_MODEL // 2
assert N_PIECES * LANES == HALF_D


def _pack_bf16_pair(a, b):
    ua = pltpu.bitcast(a.astype(BF16).astype(F32), jnp.uint32)
    ub = pltpu.bitcast(b.astype(BF16).astype(F32), jnp.uint32)
    return ua | (ub >> 16)


def _unpack_bf16_pair(u):
    return pltpu.bitcast(u & jnp.uint32(0xFFFF0000), F32), pltpu.bitcast(u << 16, F32)


def _store_pieces(ref, val):
    for s in range(N_PIECES):
        lo = slice(s * LANES, (s + 1) * LANES)
        hi = slice(HALF_D + s * LANES, HALF_D + (s + 1) * LANES)
        ref[s] = _pack_bf16_pair(val[:, lo], val[:, hi])


def _load_pieces(ref):
    first, second = [], []
    for s in range(N_PIECES):
        a, b = _unpack_bf16_pair(ref[s])
        first.append(a)
        second.append(b)
    return first, second


def _split_bf16(x):
    hi = x.astype(BF16)
    return hi, (x - hi.astype(F32)).astype(BF16)


def _outproj_kernel(x_ref, o1_ref, o2_ref, o3_ref, s1_ref, s2_ref, s3_ref, ret_ref, wout_ref, expand_ref,
                    gffn_ref, wr_ref, br_ref, tri_ref,
                    x1_ref, h2t_ref, ri_ref, gates_ref, cnt_ref, base_scr, nat_o, nat_s, grp_o):
    i = pl.program_id(0)
    tm = x_ref.shape[0]

    @pl.when(i == 0)
    def _():
        base_scr[...] = jnp.zeros_like(base_scr)

    n_col = WIDTH // LANES
    for bb in range(REGROUP):
        rows4 = pl.ds(bb, tm // REGROUP, stride=REGROUP)
        for c in range(n_col):
            nat_o[0, c, rows4, :] = o2_ref[0, bb, :, c * LANES:(c + 1) * LANES].astype(F32)
        nat_s[0, rows4, :] = s2_ref[0, bb]
        for aa in range(REGROUP):
            rows16 = pl.ds(aa, tm // REGROUP ** 2, stride=REGROUP)
            for c in range(n_col):
                grp_o[c, bb, rows16, :] = o3_ref[0, bb * REGROUP + aa, :, c * LANES:(c + 1) * LANES].astype(F32)
            grp_o[n_col, bb, rows16, :] = s3_ref[0, bb * REGROUP + aa]
        for c in range(n_col):
            nat_o[1, c, rows4, :] = grp_o[c, bb]
        nat_s[1, rows4, :] = grp_o[n_col, bb]
    hm = tm // OUT_SPLIT
    lane = lax.broadcasted_iota(jnp.int32, (hm, LANES), 1).astype(F32)
    base = base_scr[...]
    for r0 in range(0, tm, hm):
        rows = slice(r0, r0 + hm)
        o_nat = [[o1_ref[rows, c * LANES:(c + 1) * LANES].astype(F32) for c in range(n_col)]]
        sts = [s1_ref[rows, :]]
        for di in range(len(RESIDUE_DILATIONS)):
            o_nat.append([nat_o[di, c, rows, :] for c in range(n_col)])
            sts.append(nat_s[di, rows, :])

        mx = jnp.maximum(jnp.maximum(sts[0], sts[1]), sts[2])
        es = [jnp.exp(st - mx) for st in sts]
        den = es[0] + es[1] + es[2]
        attn = None
        for e, o_cols in zip(es, o_nat):
            hi, lo = _split_bf16(e / den)
            w_full = jnp.dot(jnp.concatenate([hi, lo], axis=1), expand_ref[...], preferred_element_type=F32)
            term = w_full * jnp.concatenate(o_cols, axis=1)
            attn = term if attn is None else attn + term
        mixed = jnp.concatenate([attn.astype(BF16), ret_ref[rows, :]], axis=1)
        x1 = x_ref[rows, :] + jnp.dot(mixed, wout_ref[...], preferred_element_type=F32)
        x1_ref[rows, :] = x1

        h2 = x1 * lax.rsqrt(jnp.mean(x1 * x1, axis=-1, keepdims=True) + NORM_EPS) * gffn_ref[...]
        for s in range(N_PIECES):
            h2t_ref[s, rows, :] = _pack_bf16_pair(h2[:, s * LANES:(s + 1) * LANES],
                                                  h2[:, HALF_D + s * LANES:HALF_D + (s + 1) * LANES])

        hi, lo = _split_bf16(h2)
        logits = jnp.dot(jnp.concatenate([hi, lo, hi], axis=1), wr_ref[...], preferred_element_type=F32) + br_ref[...]
        work = logits
        vals, idxs = [], []
        onehot = jnp.zeros((hm, LANES), F32)
        for _k in range(TOP_K):
            mk = jnp.max(work, axis=-1, keepdims=True)
            ik = jnp.min(jnp.where(work == mk, lane, float(LANES)), axis=-1, keepdims=True)
            sel = lane == ik
            onehot = jnp.where(sel, 1.0, onehot)
            work = jnp.where(sel, -jnp.inf, work)
            vals.append(mk)
            idxs.append(ik)
        ex = [jnp.exp(vk - vals[0]) for vk in vals]
        tot = ex[0] + ex[1] + ex[2] + ex[3]
        before = jnp.dot(tri_ref[...], onehot.astype(BF16), preferred_element_type=F32) + base
        ri = jnp.zeros((hm, LANES), F32)
        gt = jnp.zeros((hm, LANES), F32)
        for kk in range(TOP_K):
            rank = jnp.sum(jnp.where(lane == idxs[kk], before, 0.0), axis=-1, keepdims=True)
            ri = jnp.where(lane == float(kk), idxs[kk], ri)
            ri = jnp.where(lane == float(TOP_K + kk), rank, ri)
            gt = jnp.where(lane == float(kk), ex[kk] / tot, gt)
        ri_ref[:, rows] = jnp.transpose(ri)[:2 * TOP_K].astype(jnp.int32)
        gates_ref[rows, :] = gt
        base = base + jnp.sum(onehot, axis=0, keepdims=True)
    base_scr[...] = base
    cnt_ref[...] = base


def _outproj_router(x2d, outs, stats, ret, wout_bf16, norm_ffn_g, w_router, b_router, seq):
    t = x2d.shape[0]
    tm = TM_OUT
    per_seq = seq // tm
    row = lambda i: (i, 0)

    def res_spec(d, width):
        return pl.BlockSpec((1, d, tm // d, width), lambda i: (i // per_seq, 0, i % per_seq, 0))

    const = lambda i: (0, 0)
    head_of_col = jnp.arange(WIDTH) // HEAD_DIM
    expand = (jnp.arange(LANES)[:, None] == head_of_col[None, :]).astype(BF16)
    expand2 = jnp.concatenate([expand, expand], axis=0)
    wr = jnp.zeros((D_MODEL, LANES), F32).at[:, :N_EXPERTS].set(w_router)
    wr_hi, wr_lo = _split_bf16(wr)
    wr3 = jnp.concatenate([wr_hi, wr_hi, wr_lo], axis=0)
    br = jnp.full((1, LANES), NEG_INF, F32).at[0, :N_EXPERTS].set(b_router)
    hm = tm // OUT_SPLIT
    tri = (jnp.arange(hm)[:, None] > jnp.arange(hm)[None, :]).astype(BF16)
    o_spec = pl.BlockSpec((tm, WIDTH), row)
    s_spec = pl.BlockSpec((tm, LANES), row)
    return pl.pallas_call(
        _outproj_kernel,
        grid=(t // tm,),
        in_specs=[pl.BlockSpec((tm, D_MODEL), row),
                  o_spec, *[res_spec(d, WIDTH) for d in RESIDUE_DILATIONS],
                  s_spec, *[res_spec(d, LANES) for d in RESIDUE_DILATIONS], o_spec,
                  pl.BlockSpec((D_MODEL, D_MODEL), const), pl.BlockSpec((2 * LANES, WIDTH), const),
                  pl.BlockSpec((1, D_MODEL), const), pl.BlockSpec((3 * D_MODEL, LANES), const),
                  pl.BlockSpec((1, LANES), const), pl.BlockSpec((hm, hm), const)],
        out_specs=[pl.BlockSpec((tm, D_MODEL), row), pl.BlockSpec((N_PIECES, tm, LANES), lambda i: (0, i, 0)),
                   pl.BlockSpec((2 * TOP_K, tm), lambda i: (0, i)), s_spec, pl.BlockSpec((1, LANES), const)],
        out_shape=[jax.ShapeDtypeStruct((t, D_MODEL), F32), jax.ShapeDtypeStruct((N_PIECES, t, LANES), jnp.uint32),
                   jax.ShapeDtypeStruct((2 * TOP_K, t), jnp.int32), jax.ShapeDtypeStruct((t, LANES), F32),
                   jax.ShapeDtypeStruct((1, LANES), F32)],
        scratch_shapes=[pltpu.VMEM((1, LANES), F32),
                        pltpu.VMEM((len(RESIDUE_DILATIONS), WIDTH // LANES, tm, LANES), F32),
                        pltpu.VMEM((len(RESIDUE_DILATIONS), tm, LANES), F32),
                        pltpu.VMEM((WIDTH // LANES + 1, REGROUP, tm // REGROUP, LANES), F32)],
        compiler_params=_cparams("arbitrary"),
        name="outproj_router",
    )(x2d, *outs, *stats, ret, wout_bf16, expand2, norm_ffn_g.reshape(1, D_MODEL), wr3, br, tri)


SC_CORES = 2
SC_SUBCORES = 16
SC_WINDOW = 128


def _sc_mesh():
    return plsc.VectorSubcoreMesh(core_axis_name="c", subcore_axis_name="s")


def _sc_scatter_rows(src, idx, n_out_rows):
    n_rows = src.shape[0]
    workers = SC_CORES * SC_SUBCORES
    per_worker = n_rows // workers
    n_win = per_worker // SC_WINDOW
    assert per_worker * workers == n_rows and n_win * SC_WINDOW == per_worker and idx.shape == (TOP_K * n_rows,)

    @functools.partial(
        pl.kernel, mesh=_sc_mesh(),
        out_type=jax.ShapeDtypeStruct((n_out_rows, LANES), src.dtype),
        scratch_types=[pltpu.VMEM((SC_WINDOW,), jnp.int32), pltpu.VMEM((SC_WINDOW, LANES), src.dtype)],
        name="sc_scatter_rows",
    )
    def scatter(src_hbm, idx_hbm, out_hbm, idx_v, rows_v):
        wid = lax.axis_index("s") * SC_CORES + lax.axis_index("c")
        base = wid * per_worker

        @pl.loop(0, n_win)
        def _(j):
            off = pl.multiple_of(base + j * SC_WINDOW, SC_WINDOW)
            pltpu.sync_copy(src_hbm.at[pl.ds(off, SC_WINDOW)], rows_v)
            for kk in range(TOP_K):
                pltpu.sync_copy(idx_hbm.at[pl.ds(pl.multiple_of(kk * n_rows + off, SC_WINDOW), SC_WINDOW)], idx_v)
                pltpu.sync_copy(rows_v, out_hbm.at[idx_v])

    return scatter(src, idx)


def _expert_kernel(blk_e_ref, nvalid_ref, rows_ref, xs_ref, wgu_ref, bgu_ref, wd_ref, bd_ref, ys_ref, wgu_bf, wd_bf):
    i = pl.program_id(0)

    @pl.when((i == 0) | (blk_e_ref[i] != blk_e_ref[jnp.maximum(i - 1, 0)]))
    def _():
        wgu_bf[...] = wgu_ref[0].astype(BF16)
        wd_bf[...] = wd_ref[0].astype(BF16)

    def mlp(m):
        parts = [_unpack_bf16_pair(xs_ref[s, :m, :]) for s in range(N_PIECES)]
        cols = [p[0] for p in parts] + [p[1] for p in parts]
        live = lax.broadcasted_iota(jnp.int32, (m, LANES), 0) < rows_ref[i]
        x = jnp.concatenate([jnp.where(live, p, 0.0).astype(BF16) for p in cols], axis=1)
        gu = jnp.dot(x, wgu_bf[...], preferred_element_type=F32) + bgu_ref[0]
        gate = jnp.minimum(gu[:, :EXPERT_FF], SWIGLU_LIMIT)
        up = jnp.clip(gu[:, EXPERT_FF:], -SWIGLU_LIMIT, SWIGLU_LIMIT)
        act = gate * (1.0 / (1.0 + jnp.exp(-SWIGLU_ALPHA * gate))) * (up + 1.0)
        y = jnp.dot(act.astype(BF16), wd_bf[...], preferred_element_type=F32) + bd_ref[0]
        for s in range(N_PIECES):
            ys_ref[s, :m, :] = _pack_bf16_pair(y[:, s * LANES:(s + 1) * LANES],
                                               y[:, HALF_D + s * LANES:HALF_D + (s + 1) * LANES])

    n_sub = lax.shift_right_logical(rows_ref[i] + (EXPERT_SUB - 1), EXPERT_SUB.bit_length() - 1)
    for v in range(1, BM // EXPERT_SUB + 1):
        pl.when((i < nvalid_ref[0]) & (n_sub == v))(functools.partial(mlp, v * EXPERT_SUB))


def _experts(xs, blk_e, nvalid, live_rows, wgu_bf16, bgu, wd_bf16, bd):
    cap = xs.shape[1]
    nblk = cap // BM

    def blk(i, be, nv, lr):
        return (0, jnp.minimum(i, nv[0] - 1), 0)

    def by_expert(i, be, nv, lr):
        return (be[i], 0, 0)

    return pl.pallas_call(
        _expert_kernel,
        grid_spec=pltpu.PrefetchScalarGridSpec(
            num_scalar_prefetch=3,
            grid=(nblk,),
            in_specs=[pl.BlockSpec((N_PIECES, BM, LANES), blk),
                      pl.BlockSpec((1, D_MODEL, 2 * EXPERT_FF), by_expert),
                      pl.BlockSpec((1, 1, 2 * EXPERT_FF), by_expert),
                      pl.BlockSpec((1, EXPERT_FF, D_MODEL), by_expert),
                      pl.BlockSpec((1, 1, D_MODEL), by_expert)],
            out_specs=pl.BlockSpec((N_PIECES, BM, LANES), blk),
            scratch_shapes=[pltpu.VMEM((D_MODEL, 2 * EXPERT_FF), BF16), pltpu.VMEM((EXPERT_FF, D_MODEL), BF16)],
        ),
        out_shape=jax.ShapeDtypeStruct((N_PIECES, cap, LANES), jnp.uint32),
        compiler_params=_cparams("arbitrary"),
        name="moe_experts",
    )(blk_e, nvalid, live_rows, xs, wgu_bf16, bgu.reshape(N_EXPERTS, 1, 2 * EXPERT_FF), wd_bf16,
      bd.reshape(N_EXPERTS, 1, D_MODEL))


def _sc_gather_rows(table, idx):
    n_rows = idx.shape[0]
    workers = SC_CORES * SC_SUBCORES
    per_worker = n_rows // workers
    n_win = per_worker // SC_WINDOW
    assert per_worker * workers == n_rows and n_win * SC_WINDOW == per_worker

    @functools.partial(
        pl.kernel, mesh=_sc_mesh(),
        out_type=jax.ShapeDtypeStruct((n_rows, LANES), table.dtype),
        scratch_types=[pltpu.VMEM((SC_WINDOW,), jnp.int32), pltpu.VMEM((SC_WINDOW, LANES), table.dtype),
                       pltpu.SemaphoreType.DMA],
        name="sc_gather_rows",
    )
    def gather(table_hbm, idx_hbm, out_hbm, idx_v, rows_v, sem):
        wid = lax.axis_index("s") * SC_CORES + lax.axis_index("c")
        base = wid * per_worker

        @pl.loop(0, n_win)
        def _(j):
            off = pl.multiple_of(base + j * SC_WINDOW, SC_WINDOW)
            pltpu.sync_copy(idx_hbm.at[pl.ds(off, SC_WINDOW)], idx_v)
            pltpu.async_copy(table_hbm.at[idx_v], rows_v, sem).wait()
            pltpu.sync_copy(rows_v, out_hbm.at[pl.ds(off, SC_WINDOW)])

    return gather(table, idx)


def _combine_kernel(x1_ref, gates_ref, gfin_ref, *refs, tc):
    piece_refs, o_ref = refs[:TOP_K * N_PIECES], refs[TOP_K * N_PIECES]
    g = gates_ref[...]
    gk = [jnp.broadcast_to(g[:, kk:kk + 1], (tc, LANES)) for kk in range(TOP_K)]
    zs = {}
    ssq = jnp.zeros((tc, 1), F32)
    for s in range(N_PIECES):
        c_lo, c_hi = s * LANES, HALF_D + s * LANES
        z_lo = x1_ref[:, c_lo:c_lo + LANES]
        z_hi = x1_ref[:, c_hi:c_hi + LANES]
        for kk in range(TOP_K):
            a, b = _unpack_bf16_pair(piece_refs[kk * N_PIECES + s][...])
            z_lo = z_lo + gk[kk] * a
            z_hi = z_hi + gk[kk] * b
        zs[c_lo], zs[c_hi] = z_lo, z_hi
        ssq = ssq + jnp.sum(z_lo * z_lo + z_hi * z_hi, axis=-1, keepdims=True)
    inv = lax.rsqrt(ssq * (1.0 / D_MODEL) + NORM_EPS)
    for c0, z in zs.items():
        o_ref[:, c0:c0 + LANES] = z * inv * gfin_ref[:, c0:c0 + LANES]


def _combine(x1, gates, gathered, norm_final_g):
    t = x1.shape[0]
    tc = TC
    row = lambda i: (i, 0)
    slot_specs = [pl.BlockSpec((None, tc, LANES), functools.partial(lambda i, j: (j, i, 0), j=j))
                  for j in range(TOP_K * N_PIECES)]
    return pl.pallas_call(
        functools.partial(_combine_kernel, tc=tc),
        grid=(t // tc,),
        in_specs=[pl.BlockSpec((tc, D_MODEL), row), pl.BlockSpec((tc, LANES), row),
                  pl.BlockSpec((1, D_MODEL), lambda i: (0, 0)), *slot_specs],
        out_specs=pl.BlockSpec((tc, D_MODEL), row),
        out_shape=jax.ShapeDtypeStruct((t, D_MODEL), F32),
        compiler_params=_cparams("arbitrary"),
        name="moe_combine",
    )(x1, gates, norm_final_g.reshape(1, D_MODEL), *([gathered] * (TOP_K * N_PIECES)))


def _moe(x1, h2t, ri, gates, counts_f, wgu_bf16, bgu, wd_bf16, bd, norm_final_g):
    t = x1.shape[0]
    a = t * TOP_K
    cap = a + N_EXPERTS * BM
    nblk = cap // BM
    counts = counts_f[0, :N_EXPERTS].astype(jnp.int32)
    padded = ((counts + BM - 1) // BM) * BM
    pend = jnp.cumsum(padded)
    pstart = pend - padded
    nvalid = (pend[-1] // BM).reshape(1)
    first_row = jnp.minimum(jnp.arange(nblk, dtype=jnp.int32) * BM, pend[-1] - 1)
    blk_e = jnp.sum(pend[None, :] <= first_row[:, None], axis=1).astype(jnp.int32)
    idx, rank = ri[:TOP_K], ri[TOP_K:2 * TOP_K]
    onehot = idx[None, :, :] == jnp.arange(N_EXPERTS, dtype=jnp.int32)[:, None, None]
    dest = rank + jnp.sum(jnp.where(onehot, pstart[:, None, None], 0), axis=0)
    seg_end = (pstart + counts)[blk_e]
    live_rows = jnp.clip(seg_end - jnp.arange(nblk, dtype=jnp.int32) * BM, 0, BM).astype(jnp.int32)
    piece = dest[:, None, :] + (jnp.arange(N_PIECES, dtype=jnp.int32) * cap)[None, :, None]
    piece = piece.reshape(TOP_K * N_PIECES * t)
    xs = _sc_scatter_rows(h2t.reshape(N_PIECES * t, LANES), piece, N_PIECES * cap)
    ys = _experts(xs.reshape(N_PIECES, cap, LANES), blk_e, nvalid, live_rows, wgu_bf16, bgu, wd_bf16, bd)
    gathered = _sc_gather_rows(ys.reshape(N_PIECES * cap, LANES), piece)
    return _combine(x1, gates, gathered.reshape(TOP_K * N_PIECES, t, LANES), norm_final_g)


def _encoder(x, p):
    b, s, _ = x.shape
    x2d = x.reshape(b * s, D_MODEL)
    cos_t, sin_t = _rope_tables(s)
    (qa, ka, va, qr, kr, vr, gr), by_residue = _inproj(x2d, p["norm_mix_g"], p["w_in"], cos_t, sin_t, s)
    outs, stats = [], []
    for d in DILATIONS:
        if d == 1:
            o, st = _banded_attention(*[z.reshape(b, s, WIDTH) for z in (qa, ka, va)])
            outs.append(o.reshape(b * s, WIDTH))
            stats.append(st.reshape(b * s, LANES))
        else:
            o, st = _banded_attention(*[z.reshape(b * d, s // d, WIDTH) for z in by_residue[d]])
            outs.append(o.reshape(b, d, s // d, WIDTH))
            stats.append(st.reshape(b, d, s // d, LANES))
    ret = _retention(qr, kr, vr, gr, p["ret_norm_g"], b, s)
    x1, h2t, ri, gates, counts = _outproj_router(x2d, outs, stats, ret, p["w_out"], p["norm_ffn_g"],
                                                 p["w_router"], p["b_router"], s)
    y = _moe(x1, h2t, ri, gates, counts, p["w_gate_up"], p["b_gate_up"], p["w_down"], p["b_down"], p["norm_final_g"])
    return y.reshape(b, s, D_MODEL)


def kernel(x_prompt, x_sample, norm_mix_g, w_in, ret_norm_g, w_out, norm_ffn_g, w_router, b_router, w_gate_up, b_gate_up, w_down, b_down, norm_final_g):
    assert norm_mix_g.shape[0] == 1, "single layer"
    p = dict(norm_mix_g=norm_mix_g[0], w_in=w_in[0].astype(BF16), ret_norm_g=ret_norm_g[0],
             w_out=w_out[0].astype(BF16), norm_ffn_g=norm_ffn_g[0], w_router=w_router[0], b_router=b_router[0],
             w_gate_up=w_gate_up[0], b_gate_up=b_gate_up[0], w_down=w_down[0],
             b_down=b_down[0], norm_final_g=norm_final_g)
    return (_encoder(x_prompt, p), _encoder(x_sample, p))
```

```python
import functools

import jax
import jax.numpy as jnp
from jax import lax
from jax.experimental import pallas as pl
from jax.experimental.pallas import tpu as pltpu
from jax.experimental.pallas import tpu_sc as plsc

D_MODEL = 1024
HEAD_DIM = 64
N_HEADS = 8
WIDTH = N_HEADS * HEAD_DIM
N_SLABS = 7
DILATIONS = (1, 4, 16)
HALF_SPAN = 64
ROPE_THETA = 10000.0
RET_DECAY_BASE_FWD = 5.0
RET_DECAY_BASE_BWD = 5.5
N_EXPERTS = 32
TOP_K = 4
EXPERT_FF = D_MODEL
SWIGLU_LIMIT = 7.0
SWIGLU_ALPHA = 1.702
NORM_EPS = 1e-6
NEG_INF = -1e30

LANES = 128
VMEM_LIMIT_BYTES = 48 * 1024 * 1024

TM_INPROJ = 512
TQ = 128
TL_ATTN = 2048
HEAD_GROUP = 4
RET_CHUNK = 256
TM_OUT = 1024
OUT_SPLIT = 4
RET_UNROLL = 8
BM = 1024
EXPERT_SUB = 256
TC = 1024

F32 = jnp.float32
BF16 = jnp.bfloat16


def _cparams(*sem):
    return pltpu.CompilerParams(dimension_semantics=sem, vmem_limit_bytes=VMEM_LIMIT_BYTES)


_ROTATE = (True, True, False, True, True, False, False)
ATTN_SCALE = HEAD_DIM ** -0.5
_SCALE = (ATTN_SCALE, 1.0, 1.0, 1.0, ATTN_SCALE, 1.0, 1.0)


N_ATTN_SLABS = 3
RESIDUE_DILATIONS = tuple(d for d in DILATIONS if d > 1)
REGROUP = 4
assert RESIDUE_DILATIONS == (REGROUP, REGROUP ** 2)


def _inproj_kernel(x_ref, g_ref, w_ref, cos_ref, sin_ref, *refs):
    out_refs = refs[:N_SLABS]
    res_refs = refs[N_SLABS:N_SLABS + N_ATTN_SLABS * len(RESIDUE_DILATIONS)]
    stage, stage4 = refs[-2:]
    x = x_ref[...]
    tm = x.shape[0]
    h = (x * lax.rsqrt(jnp.mean(x * x, axis=-1, keepdims=True) + NORM_EPS) * g_ref[...]).astype(BF16)
    cos = cos_ref[...]
    sin = sin_ref[...]
    cos_scaled = cos * ATTN_SCALE
    sin_scaled = sin * ATTN_SCALE
    lane = lax.broadcasted_iota(jnp.int32, cos.shape, 1)
    first_half = (lane & (HEAD_DIM - 1)) < HEAD_DIM // 2
    for j, o_ref in enumerate(out_refs):
        p = jnp.dot(h, w_ref[:, j * WIDTH:(j + 1) * WIDTH], preferred_element_type=F32)
        for c in range(WIDTH // LANES):
            cols = slice(c * LANES, (c + 1) * LANES)
            r = p[:, cols]
            if _ROTATE[j]:
                partner = jnp.where(first_half, pltpu.roll(r, LANES - HEAD_DIM // 2, 1), pltpu.roll(r, HEAD_DIM // 2, 1))
                cs, sn = (cos, sin) if _SCALE[j] == 1.0 else (cos_scaled, sin_scaled)
                r = r * cs + partner * sn
            o_ref[:, cols] = r.astype(BF16)
            if j < N_ATTN_SLABS:
                stage[j, c] = r
                dst4 = res_refs[j]
                dst16 = res_refs[N_ATTN_SLABS + j]
                for bb in range(REGROUP):
                    grp = stage[j, c, pl.ds(bb, tm // REGROUP, stride=REGROUP), :]
                    dst4[0, bb, :, cols] = grp.astype(BF16)
                    stage4[j, c, bb] = grp
                    for aa in range(REGROUP):
                        sub = stage4[j, c, bb, pl.ds(aa, tm // REGROUP ** 2, stride=REGROUP), :]
                        dst16[0, bb * REGROUP + aa, :, cols] = sub.astype(BF16)


def _inproj(x2d, g, w_bf16, cos_t, sin_t, seq):
    t = x2d.shape[0]
    tm = min(TM_INPROJ, seq)
    pos_blocks = seq // tm
    b = t // seq
    out = jax.ShapeDtypeStruct((t, WIDTH), BF16)
    row = lambda i: (i, 0)
    out_specs = [pl.BlockSpec((tm, WIDTH), row)] * N_SLABS
    out_shape = [out] * N_SLABS
    for d in RESIDUE_DILATIONS:
        out_specs += [pl.BlockSpec((1, d, tm // d, WIDTH), lambda i: (i // pos_blocks, 0, i % pos_blocks, 0))] * N_ATTN_SLABS
        out_shape += [jax.ShapeDtypeStruct((b, d, seq // d, WIDTH), BF16)] * N_ATTN_SLABS
    res = pl.pallas_call(
        _inproj_kernel,
        grid=(t // tm,),
        in_specs=[
            pl.BlockSpec((tm, D_MODEL), row),
            pl.BlockSpec((1, D_MODEL), lambda i: (0, 0)),
            pl.BlockSpec((D_MODEL, N_SLABS * WIDTH), lambda i: (0, 0)),
            pl.BlockSpec((tm, LANES), lambda i: (i % pos_blocks, 0)),
            pl.BlockSpec((tm, LANES), lambda i: (i % pos_blocks, 0)),
        ],
        out_specs=out_specs,
        out_shape=out_shape,
        scratch_shapes=[pltpu.VMEM((N_ATTN_SLABS, WIDTH // LANES, tm, LANES), F32),
                        pltpu.VMEM((N_ATTN_SLABS, WIDTH // LANES, REGROUP, tm // REGROUP, LANES), F32)],
        compiler_params=_cparams("arbitrary"),
        name="inproj",
    )(x2d, g.reshape(1, D_MODEL), w_bf16, cos_t, sin_t)
    natural = res[:N_SLABS]
    by_residue = {d: res[N_SLABS + di * N_ATTN_SLABS:N_SLABS + (di + 1) * N_ATTN_SLABS]
                  for di, d in enumerate(RESIDUE_DILATIONS)}
    return natural, by_residue


def _rope_tables(seq):
    half = HEAD_DIM // 2
    inv_freq = ROPE_THETA ** (-jnp.arange(0, HEAD_DIM, 2, dtype=F32) / HEAD_DIM)
    ang = jnp.arange(seq, dtype=F32)[:, None] * inv_freq[None, :]
    cos, sin = jnp.cos(ang), jnp.sin(ang)
    reps = LANES // HEAD_DIM
    cos_t = jnp.tile(jnp.concatenate([cos, cos], axis=1), (1, reps))
    sin_t = jnp.tile(jnp.concatenate([-sin, sin], axis=1), (1, reps))
    assert cos_t.shape == (seq, LANES) and half * 2 == HEAD_DIM
    return cos_t, sin_t


def _attn_kernel(q_ref, kp_ref, kc_ref, kn_ref, vp_ref, vc_ref, vn_ref, o_ref, st_ref, kbuf, vbuf, *, gb, tl, sub_len):
    i = pl.program_id(1)
    hs = HALF_SPAN
    for gi in range(gb):
        kbuf[gi, 0:hs] = kp_ref[gi]
        kbuf[gi, hs:hs + tl] = kc_ref[gi]
        kbuf[gi, hs + tl:hs + tl + hs] = kn_ref[gi]
        vbuf[gi, 0:hs] = vp_ref[gi]
        vbuf[gi, hs:hs + tl] = vc_ref[gi]
        vbuf[gi, hs + tl:hs + tl + hs] = vn_ref[gi]
    tk = TQ + 2 * hs
    gw = HEAD_GROUP * HEAD_DIM
    qi = lax.broadcasted_iota(jnp.int32, (TQ, tk), 0)
    kj = lax.broadcasted_iota(jnp.int32, (TQ, tk), 1)
    band_bias = jnp.where((kj >= qi) & (kj - qi <= 2 * hs), 0.0, NEG_INF).astype(F32)
    key_col = lax.broadcasted_iota(jnp.int32, (1, tk), 1)
    lane = lax.broadcasted_iota(jnp.int32, (TQ, LANES), 1)
    head_of_lane = lax.broadcasted_iota(jnp.int32, (TQ, gw), 1) // HEAD_DIM
    n_sub = tl // TQ
    for gi, sub in [(gi, sub) for gi in range(gb) for sub in range(n_sub)]:
        a = sub * TQ
        bias = band_bias
        if sub == 0 or sub == n_sub - 1:
            kpos = key_col + (i * tl + a - hs)
            bias = bias + jnp.where((kpos >= 0) & (kpos < sub_len), 0.0, NEG_INF).astype(F32)
        bias = jnp.concatenate([bias] * HEAD_GROUP, axis=0)
        st = jnp.zeros((TQ, LANES), F32)
        for g in range(N_HEADS // HEAD_GROUP):
            cols = slice(g * gw, (g + 1) * gw)
            q4 = q_ref[gi, a:a + TQ, cols]
            k4 = kbuf[gi, a:a + tk, cols]
            v4 = vbuf[gi, a:a + tk, cols]
            lhs = jnp.concatenate([jnp.where(head_of_lane == h, q4, jnp.zeros_like(q4)) for h in range(HEAD_GROUP)], axis=0)
            s = lax.dot_general(lhs, k4, (((1,), (1,)), ((), ())), preferred_element_type=F32) + bias
            m = jnp.max(s, axis=-1, keepdims=True)
            p = jnp.exp(s - m)
            l = jnp.sum(p, axis=-1, keepdims=True)
            o_all = jnp.dot(p.astype(BF16), v4, preferred_element_type=F32) / l
            lse = m + jnp.log(l)
            o = o_all[0:TQ]
            for h in range(HEAD_GROUP):
                rows = slice(h * TQ, (h + 1) * TQ)
                if h:
                    o = jnp.where(head_of_lane == h, o_all[rows], o)
                st = jnp.where(lane == g * HEAD_GROUP + h, lse[rows], st)
            o_ref[gi, a:a + TQ, cols] = o.astype(BF16)
        st_ref[gi, a:a + TQ, :] = st


def _banded_attention(q, k, v):
    g, sub_len, _ = q.shape
    tl = min(TL_ATTN, sub_len)
    gb = min(TL_ATTN // tl, g)
    hs = HALF_SPAN
    per = tl // hs
    last = sub_len // hs - 1
    cur = pl.BlockSpec((gb, tl, WIDTH), lambda b, i: (b, i, 0))
    prev = pl.BlockSpec((gb, hs, WIDTH), lambda b, i: (b, jnp.maximum(i * per - 1, 0), 0))
    nxt = pl.BlockSpec((gb, hs, WIDTH), lambda b, i: (b, jnp.minimum((i + 1) * per, last), 0))
    return pl.pallas_call(
        functools.partial(_attn_kernel, gb=gb, tl=tl, sub_len=sub_len),
        grid=(g // gb, sub_len // tl),
        in_specs=[cur, prev, cur, nxt, prev, cur, nxt],
        out_specs=[cur, pl.BlockSpec((gb, tl, LANES), lambda b, i: (b, i, 0))],
        out_shape=[jax.ShapeDtypeStruct((g, sub_len, WIDTH), BF16),
                   jax.ShapeDtypeStruct((g, sub_len, LANES), F32)],
        scratch_shapes=[pltpu.VMEM((gb, tl + 2 * hs, WIDTH), BF16), pltpu.VMEM((gb, tl + 2 * hs, WIDTH), BF16)],
        compiler_params=_cparams("arbitrary", "arbitrary"),
        name="banded_attention",
    )(q, k, k, k, v, v, v)


def _ret_kernel(q_ref, k_ref, v_ref, gate_ref, dmat_ref, wq_ref, wk_ref, dec_ref, gn_ref, o_ref, sf_scr, sb_scr, *, c, n_chunks):
    pair = 2 * HEAD_DIM
    lane = lax.broadcasted_iota(jnp.int32, (c, pair), 1)
    head0 = lane < HEAD_DIM
    blk_r = lax.broadcasted_iota(jnp.int32, (pair, pair), 0) // HEAD_DIM
    blk_c = lax.broadcasted_iota(jnp.int32, (pair, pair), 1) // HEAD_DIM
    same_head = blk_r == blk_c
    dec_f = dec_ref[0:1, :]
    dec_b = dec_ref[1:2, :]
    tn = (((0,), (0,)), ((), ()))
    nt = (((1,), (1,)), ((), ()))

    def rows_of(n):
        return pl.ds(pl.multiple_of(n * c, c), c)

    def kv_body(n, carry):
        rows = rows_of(n)
        kf32 = k_ref[0, rows, :].astype(F32)
        kw = jnp.concatenate([(kf32 * wk_ref[:, :pair]).astype(BF16), (kf32 * wk_ref[:, pair:]).astype(BF16)], axis=1)
        kv = lax.dot_general(kw, v_ref[0, rows, :], tn, preferred_element_type=F32)
        sf_scr[n] = jnp.where(same_head, kv[:pair], 0.0)
        sb_scr[n] = jnp.where(same_head, kv[pair:], 0.0)
        return carry

    lax.fori_loop(0, n_chunks, kv_body, 0, unroll=RET_UNROLL)

    def scan_body(t, carry):
        sf, sb = carry
        nb = n_chunks - 1 - t
        kv_f = sf_scr[t]
        kv_b = sb_scr[nb]
        sf_scr[t] = sf
        sb_scr[nb] = sb
        return sf * dec_f + kv_f, sb * dec_b + kv_b

    zero = jnp.zeros((pair, pair), F32)
    lax.fori_loop(0, n_chunks, scan_body, (zero, zero))

    def out_body(n, carry):
        rows = rows_of(n)
        q = q_ref[0, rows, :]
        k = k_ref[0, rows, :]
        v = v_ref[0, rows, :]
        qf32 = q.astype(F32)
        intra = []
        for hh in range(2):
            mask = head0 if hh == 0 else jnp.logical_not(head0)
            kh = jnp.where(mask, k, jnp.zeros_like(k))
            s = lax.dot_general(q, kh, nt, preferred_element_type=F32)
            a = (s * dmat_ref[hh]).astype(BF16)
            intra.append(jnp.dot(a, v, preferred_element_type=F32))
        qw = jnp.concatenate([(qf32 * wq_ref[:, :pair]).astype(BF16), (qf32 * wq_ref[:, pair:]).astype(BF16)], axis=1)
        states = jnp.concatenate([sf_scr[n].astype(BF16), sb_scr[n].astype(BF16)], axis=0)
        tot = jnp.where(head0, intra[0], intra[1]) + jnp.dot(qw, states, preferred_element_type=F32)
        inv = 1.0 / HEAD_DIM
        s0 = jnp.sum(jnp.where(head0, tot, 0.0), axis=-1, keepdims=True)
        s1 = jnp.sum(jnp.where(head0, 0.0, tot), axis=-1, keepdims=True)
        xc = tot - jnp.where(head0, s0, s1) * inv
        sq = xc * xc
        v0 = jnp.sum(jnp.where(head0, sq, 0.0), axis=-1, keepdims=True)
        v1 = jnp.sum(jnp.where(head0, 0.0, sq), axis=-1, keepdims=True)
        y = xc * lax.rsqrt(jnp.where(head0, v0, v1) * inv + NORM_EPS)
        gt = gate_ref[0, rows, :].astype(F32)
        y = y * gn_ref[...] * (gt / (1.0 + jnp.exp(-gt)))
        o_ref[0, rows, :] = y.astype(BF16)
        return carry

    lax.fori_loop(0, n_chunks, out_body, 0, unroll=RET_UNROLL)


def _retention_tables(c):
    hidx = jnp.arange(N_HEADS, dtype=F32)
    lg_f = jnp.log1p(-jnp.exp2(-(RET_DECAY_BASE_FWD + hidx)))
    lg_b = jnp.log1p(-jnp.exp2(-(RET_DECAY_BASE_BWD + hidx)))
    pos = jnp.arange(c, dtype=F32)
    diff = pos[:, None] - pos[None, :]
    dm_f = jnp.exp(jnp.maximum(diff, 0.0)[None] * lg_f[:, None, None])
    dm_b = jnp.exp(jnp.maximum(-diff, 0.0)[None] * lg_b[:, None, None])
    dmat = jnp.where((diff >= 0)[None], dm_f, dm_b)

    def per_lane(tab):
        t = jnp.repeat(tab[:, :, None], HEAD_DIM, axis=2)
        return t.reshape(N_HEADS // 2, 2, c, HEAD_DIM).transpose(0, 2, 1, 3).reshape(N_HEADS // 2, c, 2 * HEAD_DIM)

    wq_f = per_lane(jnp.exp((pos + 1.0)[None, :] * lg_f[:, None]))
    wq_b = per_lane(jnp.exp((c - pos)[None, :] * lg_b[:, None]))
    wk_f = per_lane(jnp.exp((c - 1.0 - pos)[None, :] * lg_f[:, None]))
    wk_b = per_lane(jnp.exp(pos[None, :] * lg_b[:, None]))
    wq = jnp.concatenate([wq_f, wq_b], axis=2)
    wk = jnp.concatenate([wk_f, wk_b], axis=2)
    dec = jnp.stack([jnp.repeat(jnp.exp(c * lg_f), HEAD_DIM), jnp.repeat(jnp.exp(c * lg_b), HEAD_DIM)], axis=0)
    dec = dec.reshape(2, N_HEADS // 2, 2 * HEAD_DIM).transpose(1, 0, 2)
    return dmat, wq, wk, dec


def _retention(q, k, v, gate, ret_norm_g, b, s):
    c = min(RET_CHUNK, s)
    n_chunks = s // c
    pair = 2 * HEAD_DIM
    dmat, wq, wk, dec = _retention_tables(c)
    seq_blk = pl.BlockSpec((1, s, pair), lambda bi, hp: (bi, 0, hp))
    r3 = lambda z: z.reshape(b, s, WIDTH)
    return pl.pallas_call(
        functools.partial(_ret_kernel, c=c, n_chunks=n_chunks),
        grid=(b, N_HEADS // 2),
        in_specs=[seq_blk, seq_blk, seq_blk, seq_blk,
                  pl.BlockSpec((2, c, c), lambda bi, hp: (hp, 0, 0)),
                  pl.BlockSpec((None, c, 2 * pair), lambda bi, hp: (hp, 0, 0)),
                  pl.BlockSpec((None, c, 2 * pair), lambda bi, hp: (hp, 0, 0)),
                  pl.BlockSpec((None, 2, pair), lambda bi, hp: (hp, 0, 0)),
                  pl.BlockSpec((1, pair), lambda bi, hp: (0, hp))],
        out_specs=seq_blk,
        out_shape=jax.ShapeDtypeStruct((b, s, WIDTH), BF16),
        scratch_shapes=[pltpu.VMEM((n_chunks, pair, pair), F32), pltpu.VMEM((n_chunks, pair, pair), F32)],
        compiler_params=_cparams("arbitrary", "arbitrary"),
        name="retention",
    )(r3(q), r3(k), r3(v), r3(gate), dmat, wq, wk, dec, ret_norm_g.reshape(1, WIDTH)).reshape(b * s, WIDTH)


N_PIECES = 4
HALF_D = D_MODEL // 2
assert N_PIECES * LANES == HALF_D


def _pack_bf16_pair(a, b):
    ua = pltpu.bitcast(a.astype(BF16).astype(F32), jnp.uint32)
    ub = pltpu.bitcast(b.astype(BF16).astype(F32), jnp.uint32)
    return ua | (ub >> 16)


def _unpack_bf16_pair(u):
    return pltpu.bitcast(u & jnp.uint32(0xFFFF0000), F32), pltpu.bitcast(u << 16, F32)


def _store_pieces(ref, rows, val):
    for s in range(N_PIECES):
        lo = slice(s * LANES, (s + 1) * LANES)
        hi = slice(HALF_D + s * LANES, HALF_D + (s + 1) * LANES)
        ref[s, rows, :] = _pack_bf16_pair(val[:, lo], val[:, hi])


def _load_pieces(ref, rows):
    parts = [_unpack_bf16_pair(ref[s, rows, :]) for s in range(N_PIECES)]
    return [p[0] for p in parts] + [p[1] for p in parts]


def _split_bf16(x):
    hi = x.astype(BF16)
    return hi, (x - hi.astype(F32)).astype(BF16)


def _outproj_kernel(x_ref, o1_ref, o2_ref, o3_ref, s1_ref, s2_ref, s3_ref, ret_ref, wout_ref, expand_ref,
                    gffn_ref, wr_ref, br_ref, tri_ref,
                    x1_ref, h2t_ref, ri_ref, gates_ref, cnt_ref, base_scr, nat_o, nat_s, grp_o):
    i = pl.program_id(0)
    tm = x_ref.shape[0]

    @pl.when(i == 0)
    def _():
        base_scr[...] = jnp.zeros_like(base_scr)

    n_col = WIDTH // LANES
    for bb in range(REGROUP):
        rows4 = pl.ds(bb, tm // REGROUP, stride=REGROUP)
        for c in range(n_col):
            nat_o[0, c, rows4, :] = o2_ref[0, bb, :, c * LANES:(c + 1) * LANES].astype(F32)
        nat_s[0, rows4, :] = s2_ref[0, bb]
        for aa in range(REGROUP):
            rows16 = pl.ds(aa, tm // REGROUP ** 2, stride=REGROUP)
            for c in range(n_col):
                grp_o[c, bb, rows16, :] = o3_ref[0, bb * REGROUP + aa, :, c * LANES:(c + 1) * LANES].astype(F32)
            grp_o[n_col, bb, rows16, :] = s3_ref[0, bb * REGROUP + aa]
        for c in range(n_col):
            nat_o[1, c, rows4, :] = grp_o[c, bb]
        nat_s[1, rows4, :] = grp_o[n_col, bb]
    hm = tm // OUT_SPLIT
    lane = lax.broadcasted_iota(jnp.int32, (hm, LANES), 1).astype(F32)
    base = base_scr[...]
    for r0 in range(0, tm, hm):
        rows = slice(r0, r0 + hm)
        o_nat = [[o1_ref[rows, c * LANES:(c + 1) * LANES].astype(F32) for c in range(n_col)]]
        sts = [s1_ref[rows, :]]
        for di in range(len(RESIDUE_DILATIONS)):
            o_nat.append([nat_o[di, c, rows, :] for c in range(n_col)])
            sts.append(nat_s[di, rows, :])

        mx = jnp.maximum(jnp.maximum(sts[0], sts[1]), sts[2])
        es = [jnp.exp(st - mx) for st in sts]
        den = es[0] + es[1] + es[2]
        attn = None
        for e, o_cols in zip(es, o_nat):
            hi, lo = _split_bf16(e / den)
            w_full = jnp.dot(jnp.concatenate([hi, lo], axis=1), expand_ref[...], preferred_element_type=F32)
            term = w_full * jnp.concatenate(o_cols, axis=1)
            attn = term if attn is None else attn + term
        mixed = jnp.concatenate([attn.astype(BF16), ret_ref[rows, :]], axis=1)
        x1 = x_ref[rows, :] + jnp.dot(mixed, wout_ref[...], preferred_element_type=F32)
        x1_ref[rows, :] = x1

        h2 = x1 * lax.rsqrt(jnp.mean(x1 * x1, axis=-1, keepdims=True) + NORM_EPS) * gffn_ref[...]
        _store_pieces(h2t_ref, rows, h2)

        hi, lo = _split_bf16(h2)
        prod = jnp.dot(jnp.concatenate([hi, lo], axis=1), wr_ref[...], preferred_element_type=F32)
        logits = prod + pltpu.roll(prod, LANES - N_EXPERTS, 1) + br_ref[...]
        work = logits
        vals, idxs = [], []
        onehot = jnp.zeros((hm, LANES), F32)
        for _k in range(TOP_K):
            mk = jnp.max(work, axis=-1, keepdims=True)
            ik = jnp.min(jnp.where(work == mk, lane, float(LANES)), axis=-1, keepdims=True)
            sel = lane == ik
            onehot = jnp.where(sel, 1.0, onehot)
            work = jnp.where(sel, -jnp.inf, work)
            vals.append(mk)
            idxs.append(ik)
        ex = [jnp.exp(vk - vals[0]) for vk in vals]
        tot = ex[0] + ex[1] + ex[2] + ex[3]
        before = jnp.dot(tri_ref[...], onehot.astype(BF16), preferred_element_type=F32) + base
        ri = jnp.zeros((hm, LANES), F32)
        gt = jnp.zeros((hm, LANES), F32)
        for kk in range(TOP_K):
            rank = jnp.sum(jnp.where(lane == idxs[kk], before, 0.0), axis=-1, keepdims=True)
            ri = jnp.where(lane == float(kk), idxs[kk], ri)
            ri = jnp.where(lane == float(TOP_K + kk), rank, ri)
            gt = jnp.where(lane == float(kk), ex[kk] / tot, gt)
        ri_ref[:, rows] = jnp.transpose(ri)[:2 * TOP_K].astype(jnp.int32)
        gates_ref[rows, :] = gt
        base = base + jnp.sum(onehot, axis=0, keepdims=True)
    base_scr[...] = base
    cnt_ref[...] = base


def _outproj_router(x2d, outs, stats, ret, wout_bf16, norm_ffn_g, w_router, b_router, seq):
    t = x2d.shape[0]
    tm = TM_OUT
    per_seq = seq // tm
    row = lambda i: (i, 0)

    def res_spec(d, width):
        return pl.BlockSpec((1, d, tm // d, width), lambda i: (i // per_seq, 0, i % per_seq, 0))

    const = lambda i: (0, 0)
    head_of_col = jnp.arange(WIDTH) // HEAD_DIM
    expand = (jnp.arange(LANES)[:, None] == head_of_col[None, :]).astype(BF16)
    expand2 = jnp.concatenate([expand, expand], axis=0)
    wr_hi, wr_lo = _split_bf16(w_router)
    zeros = jnp.zeros((D_MODEL, LANES), BF16)
    wr2 = jnp.concatenate([zeros.at[:, :N_EXPERTS].set(wr_hi).at[:, N_EXPERTS:2 * N_EXPERTS].set(wr_lo),
                           zeros.at[:, :N_EXPERTS].set(wr_hi)], axis=0)
    br = jnp.full((1, LANES), NEG_INF, F32).at[0, :N_EXPERTS].set(b_router)
    hm = tm // OUT_SPLIT
    tri = (jnp.arange(hm)[:, None] > jnp.arange(hm)[None, :]).astype(BF16)
    o_spec = pl.BlockSpec((tm, WIDTH), row)
    s_spec = pl.BlockSpec((tm, LANES), row)
    return pl.pallas_call(
        _outproj_kernel,
        grid=(t // tm,),
        in_specs=[pl.BlockSpec((tm, D_MODEL), row),
                  o_spec, *[res_spec(d, WIDTH) for d in RESIDUE_DILATIONS],
                  s_spec, *[res_spec(d, LANES) for d in RESIDUE_DILATIONS], o_spec,
                  pl.BlockSpec((D_MODEL, D_MODEL), const), pl.BlockSpec((2 * LANES, WIDTH), const),
                  pl.BlockSpec((1, D_MODEL), const), pl.BlockSpec((2 * D_MODEL, LANES), const),
                  pl.BlockSpec((1, LANES), const), pl.BlockSpec((hm, hm), const)],
        out_specs=[pl.BlockSpec((tm, D_MODEL), row), pl.BlockSpec((N_PIECES, tm, LANES), lambda i: (0, i, 0)),
                   pl.BlockSpec((2 * TOP_K, tm), lambda i: (0, i)), s_spec, pl.BlockSpec((1, LANES), const)],
        out_shape=[jax.ShapeDtypeStruct((t, D_MODEL), F32), jax.ShapeDtypeStruct((N_PIECES, t, LANES), jnp.uint32),
                   jax.ShapeDtypeStruct((2 * TOP_K, t), jnp.int32), jax.ShapeDtypeStruct((t, LANES), F32),
                   jax.ShapeDtypeStruct((1, LANES), F32)],
        scratch_shapes=[pltpu.VMEM((1, LANES), F32),
                        pltpu.VMEM((len(RESIDUE_DILATIONS), WIDTH // LANES, tm, LANES), F32),
                        pltpu.VMEM((len(RESIDUE_DILATIONS), tm, LANES), F32),
                        pltpu.VMEM((WIDTH // LANES + 1, REGROUP, tm // REGROUP, LANES), F32)],
        compiler_params=_cparams("arbitrary"),
        name="outproj_router",
    )(x2d, *outs, *stats, ret, wout_bf16, expand2, norm_ffn_g.reshape(1, D_MODEL), wr2, br, tri)


SC_CORES = 2
SC_SUBCORES = 16
SC_WINDOW = 128


def _sc_mesh():
    return plsc.VectorSubcoreMesh(core_axis_name="c", subcore_axis_name="s")


def _sc_scatter_rows(src, idx, n_out_rows):
    n_rows = src.shape[0]
    workers = SC_CORES * SC_SUBCORES
    per_worker = n_rows // workers
    n_win = per_worker // SC_WINDOW
    assert per_worker * workers == n_rows and n_win * SC_WINDOW == per_worker and idx.shape == (TOP_K * n_rows,)

    @functools.partial(
        pl.kernel, mesh=_sc_mesh(),
        out_type=jax.ShapeDtypeStruct((n_out_rows, LANES), src.dtype),
        scratch_types=[pltpu.VMEM((SC_WINDOW,), jnp.int32), pltpu.VMEM((SC_WINDOW, LANES), src.dtype)],
        name="sc_scatter_rows",
    )
    def scatter(src_hbm, idx_hbm, out_hbm, idx_v, rows_v):
        wid = lax.axis_index("s") * SC_CORES + lax.axis_index("c")
        base = wid * per_worker

        @pl.loop(0, n_win)
        def _(j):
            off = pl.multiple_of(base + j * SC_WINDOW, SC_WINDOW)
            pltpu.sync_copy(src_hbm.at[pl.ds(off, SC_WINDOW)], rows_v)
            for kk in range(TOP_K):
                pltpu.sync_copy(idx_hbm.at[pl.ds(pl.multiple_of(kk * n_rows + off, SC_WINDOW), SC_WINDOW)], idx_v)
                pltpu.sync_copy(rows_v, out_hbm.at[idx_v])

    return scatter(src, idx)


def _expert_kernel(blk_e_ref, nvalid_ref, rows_ref, xs_ref, wgu_ref, bgu_ref, wd_ref, bd_ref, ys_ref, wgu_bf, wd_bf):
    i = pl.program_id(0)

    @pl.when((i == 0) | (blk_e_ref[i] != blk_e_ref[jnp.maximum(i - 1, 0)]))
    def _():
        wgu_bf[...] = wgu_ref[0].astype(BF16)
        wd_bf[...] = wd_ref[0].astype(BF16)

    def mlp(m):
        live = lax.broadcasted_iota(jnp.int32, (m, LANES), 0) < rows_ref[i]
        x = jnp.concatenate([jnp.where(live, p, 0.0).astype(BF16) for p in _load_pieces(xs_ref, slice(0, m))], axis=1)
        gu = jnp.dot(x, wgu_bf[...], preferred_element_type=F32) + bgu_ref[0]
        gate = jnp.minimum(gu[:, :EXPERT_FF], SWIGLU_LIMIT)
        up = jnp.clip(gu[:, EXPERT_FF:], -SWIGLU_LIMIT, SWIGLU_LIMIT)
        act = gate * (1.0 / (1.0 + jnp.exp(-SWIGLU_ALPHA * gate))) * (up + 1.0)
        y = jnp.dot(act.astype(BF16), wd_bf[...], preferred_element_type=F32) + bd_ref[0]
        _store_pieces(ys_ref, slice(0, m), y)

    n_sub = lax.shift_right_logical(rows_ref[i] + (EXPERT_SUB - 1), EXPERT_SUB.bit_length() - 1)
    for v in range(1, BM // EXPERT_SUB + 1):
        pl.when((i < nvalid_ref[0]) & (n_sub == v))(functools.partial(mlp, v * EXPERT_SUB))


def _experts(xs, blk_e, nvalid, live_rows, wgu_bf16, bgu, wd_bf16, bd):
    cap = xs.shape[1]
    nblk = cap // BM

    def blk(i, be, nv, lr):
        return (0, jnp.minimum(i, nv[0] - 1), 0)

    def by_expert(i, be, nv, lr):
        return (be[i], 0, 0)

    return pl.pallas_call(
        _expert_kernel,
        grid_spec=pltpu.PrefetchScalarGridSpec(
            num_scalar_prefetch=3,
            grid=(nblk,),
            in_specs=[pl.BlockSpec((N_PIECES, BM, LANES), blk),
                      pl.BlockSpec((1, D_MODEL, 2 * EXPERT_FF), by_expert),
                      pl.BlockSpec((1, 1, 2 * EXPERT_FF), by_expert),
                      pl.BlockSpec((1, EXPERT_FF, D_MODEL), by_expert),
                      pl.BlockSpec((1, 1, D_MODEL), by_expert)],
            out_specs=pl.BlockSpec((N_PIECES, BM, LANES), blk),
            scratch_shapes=[pltpu.VMEM((D_MODEL, 2 * EXPERT_FF), BF16), pltpu.VMEM((EXPERT_FF, D_MODEL), BF16)],
        ),
        out_shape=jax.ShapeDtypeStruct((N_PIECES, cap, LANES), jnp.uint32),
        compiler_params=_cparams("arbitrary"),
        name="moe_experts",
    )(blk_e, nvalid, live_rows, xs, wgu_bf16, bgu.reshape(N_EXPERTS, 1, 2 * EXPERT_FF), wd_bf16,
      bd.reshape(N_EXPERTS, 1, D_MODEL))


def _sc_gather_rows(table, idx):
    n_rows = idx.shape[0]
    workers = SC_CORES * SC_SUBCORES
    per_worker = n_rows // workers
    n_win = per_worker // SC_WINDOW
    assert per_worker * workers == n_rows and n_win * SC_WINDOW == per_worker

    @functools.partial(
        pl.kernel, mesh=_sc_mesh(),
        out_type=jax.ShapeDtypeStruct((n_rows, LANES), table.dtype),
        scratch_types=[pltpu.VMEM((SC_WINDOW,), jnp.int32), pltpu.VMEM((SC_WINDOW, LANES), table.dtype),
                       pltpu.SemaphoreType.DMA],
        name="sc_gather_rows",
    )
    def gather(table_hbm, idx_hbm, out_hbm, idx_v, rows_v, sem):
        wid = lax.axis_index("s") * SC_CORES + lax.axis_index("c")
        base = wid * per_worker

        @pl.loop(0, n_win)
        def _(j):
            off = pl.multiple_of(base + j * SC_WINDOW, SC_WINDOW)
            pltpu.sync_copy(idx_hbm.at[pl.ds(off, SC_WINDOW)], idx_v)
            pltpu.async_copy(table_hbm.at[idx_v], rows_v, sem).wait()
            pltpu.sync_copy(rows_v, out_hbm.at[pl.ds(off, SC_WINDOW)])

    return gather(table, idx)


def _combine_kernel(x1_ref, gates_ref, gfin_ref, *refs, tc):
    piece_refs, o_ref = refs[:TOP_K * N_PIECES], refs[TOP_K * N_PIECES]
    g = gates_ref[...]
    gk = [jnp.broadcast_to(g[:, kk:kk + 1], (tc, LANES)) for kk in range(TOP_K)]
    zs = {}
    ssq = jnp.zeros((tc, 1), F32)
    for s in range(N_PIECES):
        c_lo, c_hi = s * LANES, HALF_D + s * LANES
        z_lo = x1_ref[:, c_lo:c_lo + LANES]
        z_hi = x1_ref[:, c_hi:c_hi + LANES]
        for kk in range(TOP_K):
            a, b = _unpack_bf16_pair(piece_refs[kk * N_PIECES + s][...])
            z_lo = z_lo + gk[kk] * a
            z_hi = z_hi + gk[kk] * b
        zs[c_lo], zs[c_hi] = z_lo, z_hi
        ssq = ssq + jnp.sum(z_lo * z_lo + z_hi * z_hi, axis=-1, keepdims=True)
    inv = lax.rsqrt(ssq * (1.0 / D_MODEL) + NORM_EPS)
    for c0, z in zs.items():
        o_ref[:, c0:c0 + LANES] = z * inv * gfin_ref[:, c0:c0 + LANES]


def _combine(x1, gates, gathered, norm_final_g):
    t = x1.shape[0]
    tc = TC
    row = lambda i: (i, 0)
    slot_specs = [pl.BlockSpec((None, tc, LANES), functools.partial(lambda i, j: (j, i, 0), j=j))
                  for j in range(TOP_K * N_PIECES)]
    return pl.pallas_call(
        functools.partial(_combine_kernel, tc=tc),
        grid=(t // tc,),
        in_specs=[pl.BlockSpec((tc, D_MODEL), row), pl.BlockSpec((tc, LANES), row),
                  pl.BlockSpec((1, D_MODEL), lambda i: (0, 0)), *slot_specs],
        out_specs=pl.BlockSpec((tc, D_MODEL), row),
        out_shape=jax.ShapeDtypeStruct((t, D_MODEL), F32),
        compiler_params=_cparams("arbitrary"),
        name="moe_combine",
    )(x1, gates, norm_final_g.reshape(1, D_MODEL), *([gathered] * (TOP_K * N_PIECES)))


def _moe(x1, h2t, ri, gates, counts_f, wgu_bf16, bgu, wd_bf16, bd, norm_final_g):
    t = x1.shape[0]
    a = t * TOP_K
    cap = a + N_EXPERTS * BM
    nblk = cap // BM
    counts = counts_f[0, :N_EXPERTS].astype(jnp.int32)
    padded = ((counts + BM - 1) // BM) * BM
    pend = jnp.cumsum(padded)
    pstart = pend - padded
    nvalid = (pend[-1] // BM).reshape(1)
    first_row = jnp.minimum(jnp.arange(nblk, dtype=jnp.int32) * BM, pend[-1] - 1)
    blk_e = jnp.sum(pend[None, :] <= first_row[:, None], axis=1).astype(jnp.int32)
    idx, rank = ri[:TOP_K], ri[TOP_K:2 * TOP_K]
    onehot = idx[None, :, :] == jnp.arange(N_EXPERTS, dtype=jnp.int32)[:, None, None]
    dest = rank + jnp.sum(jnp.where(onehot, pstart[:, None, None], 0), axis=0)
    seg_end = (pstart + counts)[blk_e]
    live_rows = jnp.clip(seg_end - jnp.arange(nblk, dtype=jnp.int32) * BM, 0, BM).astype(jnp.int32)
    piece = dest[:, None, :] + (jnp.arange(N_PIECES, dtype=jnp.int32) * cap)[None, :, None]
    piece = piece.reshape(TOP_K * N_PIECES * t)
    xs = _sc_scatter_rows(h2t.reshape(N_PIECES * t, LANES), piece, N_PIECES * cap)
    ys = _experts(xs.reshape(N_PIECES, cap, LANES), blk_e, nvalid, live_rows, wgu_bf16, bgu, wd_bf16, bd)
    gathered = _sc_gather_rows(ys.reshape(N_PIECES * cap, LANES), piece)
    return _combine(x1, gates, gathered.reshape(TOP_K * N_PIECES, t, LANES), norm_final_g)


def _encoder(x, p):
    b, s, d_model = x.shape
    assert d_model == D_MODEL and s % (max(DILATIONS) * TQ) == 0 and (b * s) % max(TM_OUT, TC) == 0, x.shape
    x2d = x.reshape(b * s, D_MODEL)
    cos_t, sin_t = _rope_tables(s)
    (qa, ka, va, qr, kr, vr, gr), by_residue = _inproj(x2d, p["norm_mix_g"], p["w_in"], cos_t, sin_t, s)
    outs, stats = [], []
    for d in DILATIONS:
        if d == 1:
            o, st = _banded_attention(*[z.reshape(b, s, WIDTH) for z in (qa, ka, va)])
            outs.append(o.reshape(b * s, WIDTH))
            stats.append(st.reshape(b * s, LANES))
        else:
            o, st = _banded_attention(*[z.reshape(b * d, s // d, WIDTH) for z in by_residue[d]])
            outs.append(o.reshape(b, d, s // d, WIDTH))
            stats.append(st.reshape(b, d, s // d, LANES))
    ret = _retention(qr, kr, vr, gr, p["ret_norm_g"], b, s)
    x1, h2t, ri, gates, counts = _outproj_router(x2d, outs, stats, ret, p["w_out"], p["norm_ffn_g"],
                                                 p["w_router"], p["b_router"], s)
    y = _moe(x1, h2t, ri, gates, counts, p["w_gate_up"], p["b_gate_up"], p["w_down"], p["b_down"], p["norm_final_g"])
    return y.reshape(b, s, D_MODEL)


def kernel(x_prompt, x_sample, norm_mix_g, w_in, ret_norm_g, w_out, norm_ffn_g, w_router, b_router, w_gate_up, b_gate_up, w_down, b_down, norm_final_g):
    assert norm_mix_g.shape[0] == 1, "single layer"
    p = dict(norm_mix_g=norm_mix_g[0], w_in=w_in[0].astype(BF16), ret_norm_g=ret_norm_g[0],
             w_out=w_out[0].astype(BF16), norm_ffn_g=norm_ffn_g[0], w_router=w_router[0], b_router=b_router[0],
             w_gate_up=w_gate_up[0], b_gate_up=b_gate_up[0], w_down=w_down[0],
             b_down=b_down[0], norm_final_g=norm_final_g)
    return (_encoder(x_prompt, p), _encoder(x_sample, p))
```

```python
import functools

import jax
import jax.numpy as jnp
from jax import lax
from jax.experimental import pallas as pl
from jax.experimental.pallas import tpu as pltpu
from jax.experimental.pallas import tpu_sc as plsc

D_MODEL = 1024
HEAD_DIM = 64
N_HEADS = 8
WIDTH = N_HEADS * HEAD_DIM
N_SLABS = 7
DILATIONS = (1, 4, 16)
HALF_SPAN = 64
ROPE_THETA = 10000.0
RET_DECAY_BASE_FWD = 5.0
RET_DECAY_BASE_BWD = 5.5
N_EXPERTS = 32
TOP_K = 4
EXPERT_FF = D_MODEL
SWIGLU_LIMIT = 7.0
SWIGLU_ALPHA = 1.702
NORM_EPS = 1e-6
NEG_INF = -1e30

LANES = 128
VMEM_LIMIT_BYTES = 56 * 1024 * 1024

TM_INPROJ = 1024
TQ = 128
TL_ATTN = 2048
HEAD_GROUP = 4
RET_CHUNK = 256
TM_OUT = 1024
OUT_SPLIT = 4
RET_UNROLL = 8
BM = 1024
EXPERT_SUB = 256
TC = 1024

F32 = jnp.float32
BF16 = jnp.bfloat16


def _cparams(*sem):
    return pltpu.CompilerParams(dimension_semantics=sem, vmem_limit_bytes=VMEM_LIMIT_BYTES)


_ROTATE = (True, True, False, True, True, False, False)
ATTN_SCALE = HEAD_DIM ** -0.5
_SCALE = (ATTN_SCALE, 1.0, 1.0, 1.0, ATTN_SCALE, 1.0, 1.0)


N_ATTN_SLABS = 3
RESIDUE_DILATIONS = tuple(d for d in DILATIONS if d > 1)
REGROUP = 4
STAGE_SLOTS = 2
assert RESIDUE_DILATIONS == (REGROUP, REGROUP ** 2)


def _inproj_kernel(x_ref, g_ref, w_ref, cos_ref, sin_ref, *refs):
    out_refs = refs[:N_SLABS]
    res_refs = refs[N_SLABS:N_SLABS + N_ATTN_SLABS * len(RESIDUE_DILATIONS)]
    stage, stage4 = refs[-2:]
    x = x_ref[...]
    tm = x.shape[0]
    h = (x * lax.rsqrt(jnp.mean(x * x, axis=-1, keepdims=True) + NORM_EPS) * g_ref[...]).astype(BF16)
    cos = cos_ref[...]
    sin = sin_ref[...]
    cos_scaled = cos * ATTN_SCALE
    sin_scaled = sin * ATTN_SCALE
    lane = lax.broadcasted_iota(jnp.int32, cos.shape, 1)
    first_half = (lane & (HEAD_DIM - 1)) < HEAD_DIM // 2
    for j, o_ref in enumerate(out_refs):
        p = jnp.dot(h, w_ref[:, j * WIDTH:(j + 1) * WIDTH], preferred_element_type=F32)
        for c in range(WIDTH // LANES):
            cols = slice(c * LANES, (c + 1) * LANES)
            r = p[:, cols]
            if _ROTATE[j]:
                partner = jnp.where(first_half, pltpu.roll(r, LANES - HEAD_DIM // 2, 1), pltpu.roll(r, HEAD_DIM // 2, 1))
                cs, sn = (cos, sin) if _SCALE[j] == 1.0 else (cos_scaled, sin_scaled)
                r = r * cs + partner * sn
            o_ref[:, cols] = r.astype(BF16)
            if j < N_ATTN_SLABS:
                js = j % STAGE_SLOTS
                stage[js, c] = r
                dst4 = res_refs[j]
                dst16 = res_refs[N_ATTN_SLABS + j]
                for bb in range(REGROUP):
                    grp = stage[js, c, pl.ds(bb, tm // REGROUP, stride=REGROUP), :]
                    dst4[0, bb, :, cols] = grp.astype(BF16)
                    stage4[js, c, bb] = grp
                    for aa in range(REGROUP):
                        sub = stage4[js, c, bb, pl.ds(aa, tm // REGROUP ** 2, stride=REGROUP), :]
                        dst16[0, bb * REGROUP + aa, :, cols] = sub.astype(BF16)


def _inproj(x2d, g, w_bf16, cos_t, sin_t, seq):
    t = x2d.shape[0]
    tm = min(TM_INPROJ, seq)
    pos_blocks = seq // tm
    b = t // seq
    out = jax.ShapeDtypeStruct((t, WIDTH), BF16)
    row = lambda i: (i, 0)
    out_specs = [pl.BlockSpec((tm, WIDTH), row)] * N_SLABS
    out_shape = [out] * N_SLABS
    for d in RESIDUE_DILATIONS:
        out_specs += [pl.BlockSpec((1, d, tm // d, WIDTH), lambda i: (i // pos_blocks, 0, i % pos_blocks, 0))] * N_ATTN_SLABS
        out_shape += [jax.ShapeDtypeStruct((b, d, seq // d, WIDTH), BF16)] * N_ATTN_SLABS
    res = pl.pallas_call(
        _inproj_kernel,
        grid=(t // tm,),
        in_specs=[
            pl.BlockSpec((tm, D_MODEL), row),
            pl.BlockSpec((1, D_MODEL), lambda i: (0, 0)),
            pl.BlockSpec((D_MODEL, N_SLABS * WIDTH), lambda i: (0, 0)),
            pl.BlockSpec((tm, LANES), lambda i: (i % pos_blocks, 0)),
            pl.BlockSpec((tm, LANES), lambda i: (i % pos_blocks, 0)),
        ],
        out_specs=out_specs,
        out_shape=out_shape,
        scratch_shapes=[pltpu.VMEM((STAGE_SLOTS, WIDTH // LANES, tm, LANES), F32),
                        pltpu.VMEM((STAGE_SLOTS, WIDTH // LANES, REGROUP, tm // REGROUP, LANES), F32)],
        compiler_params=_cparams("arbitrary"),
        name="inproj",
    )(x2d, g.reshape(1, D_MODEL), w_bf16, cos_t, sin_t)
    natural = res[:N_SLABS]
    by_residue = {d: res[N_SLABS + di * N_ATTN_SLABS:N_SLABS + (di + 1) * N_ATTN_SLABS]
                  for di, d in enumerate(RESIDUE_DILATIONS)}
    return natural, by_residue


def _rope_tables(seq):
    half = HEAD_DIM // 2
    inv_freq = ROPE_THETA ** (-jnp.arange(0, HEAD_DIM, 2, dtype=F32) / HEAD_DIM)
    ang = jnp.arange(seq, dtype=F32)[:, None] * inv_freq[None, :]
    cos, sin = jnp.cos(ang), jnp.sin(ang)
    reps = LANES // HEAD_DIM
    cos_t = jnp.tile(jnp.concatenate([cos, cos], axis=1), (1, reps))
    sin_t = jnp.tile(jnp.concatenate([-sin, sin], axis=1), (1, reps))
    assert cos_t.shape == (seq, LANES) and half * 2 == HEAD_DIM
    return cos_t, sin_t


def _attn_kernel(q_ref, kp_ref, kc_ref, kn_ref, vp_ref, vc_ref, vn_ref, o_ref, st_ref, kbuf, vbuf, *, gb, tl, sub_len):
    i = pl.program_id(1)
    hs = HALF_SPAN
    for gi in range(gb):
        kbuf[gi, 0:hs] = kp_ref[gi]
        kbuf[gi, hs:hs + tl] = kc_ref[gi]
        kbuf[gi, hs + tl:hs + tl + hs] = kn_ref[gi]
        vbuf[gi, 0:hs] = vp_ref[gi]
        vbuf[gi, hs:hs + tl] = vc_ref[gi]
        vbuf[gi, hs + tl:hs + tl + hs] = vn_ref[gi]
    tk = TQ + 2 * hs
    gw = HEAD_GROUP * HEAD_DIM
    qi = lax.broadcasted_iota(jnp.int32, (TQ, tk), 0)
    kj = lax.broadcasted_iota(jnp.int32, (TQ, tk), 1)
    band_bias = jnp.where((kj >= qi) & (kj - qi <= 2 * hs), 0.0, NEG_INF).astype(F32)
    key_col = lax.broadcasted_iota(jnp.int32, (1, tk), 1)
    lane = lax.broadcasted_iota(jnp.int32, (TQ, LANES), 1)
    head_of_lane = lax.broadcasted_iota(jnp.int32, (TQ, gw), 1) // HEAD_DIM
    n_sub = tl // TQ
    for gi, sub in [(gi, sub) for gi in range(gb) for sub in range(n_sub)]:
        a = sub * TQ
        bias = band_bias
        if sub == 0 or sub == n_sub - 1:
            kpos = key_col + (i * tl + a - hs)
            bias = bias + jnp.where((kpos >= 0) & (kpos < sub_len), 0.0, NEG_INF).astype(F32)
        bias = jnp.concatenate([bias] * HEAD_GROUP, axis=0)
        st = jnp.zeros((TQ, LANES), F32)
        for g in range(N_HEADS // HEAD_GROUP):
            cols = slice(g * gw, (g + 1) * gw)
            q4 = q_ref[gi, a:a + TQ, cols]
            k4 = kbuf[gi, a:a + tk, cols]
            v4 = vbuf[gi, a:a + tk, cols]
            lhs = jnp.concatenate([jnp.where(head_of_lane == h, q4, jnp.zeros_like(q4)) for h in range(HEAD_GROUP)], axis=0)
            s = lax.dot_general(lhs, k4, (((1,), (1,)), ((), ())), preferred_element_type=F32) + bias
            m = jnp.max(s, axis=-1, keepdims=True)
            p = jnp.exp(s - m)
            l = jnp.sum(p, axis=-1, keepdims=True)
            o_all = jnp.dot(p.astype(BF16), v4, preferred_element_type=F32) / l
            lse = m + jnp.log(l)
            o = o_all[0:TQ]
            for h in range(HEAD_GROUP):
                rows = slice(h * TQ, (h + 1) * TQ)
                if h:
                    o = jnp.where(head_of_lane == h, o_all[rows], o)
                st = jnp.where(lane == g * HEAD_GROUP + h, lse[rows], st)
            o_ref[gi, a:a + TQ, cols] = o.astype(BF16)
        st_ref[gi, a:a + TQ, :] = st


def _banded_attention(q, k, v):
    g, sub_len, _ = q.shape
    tl = min(TL_ATTN, sub_len)
    gb = min(TL_ATTN // tl, g)
    hs = HALF_SPAN
    per = tl // hs
    last = sub_len // hs - 1
    cur = pl.BlockSpec((gb, tl, WIDTH), lambda b, i: (b, i, 0))
    prev = pl.BlockSpec((gb, hs, WIDTH), lambda b, i: (b, jnp.maximum(i * per - 1, 0), 0))
    nxt = pl.BlockSpec((gb, hs, WIDTH), lambda b, i: (b, jnp.minimum((i + 1) * per, last), 0))
    return pl.pallas_call(
        functools.partial(_attn_kernel, gb=gb, tl=tl, sub_len=sub_len),
        grid=(g // gb, sub_len // tl),
        in_specs=[cur, prev, cur, nxt, prev, cur, nxt],
        out_specs=[cur, pl.BlockSpec((gb, tl, LANES), lambda b, i: (b, i, 0))],
        out_shape=[jax.ShapeDtypeStruct((g, sub_len, WIDTH), BF16),
                   jax.ShapeDtypeStruct((g, sub_len, LANES), F32)],
        scratch_shapes=[pltpu.VMEM((gb, tl + 2 * hs, WIDTH), BF16), pltpu.VMEM((gb, tl + 2 * hs, WIDTH), BF16)],
        compiler_params=_cparams("arbitrary", "arbitrary"),
        name="banded_attention",
    )(q, k, k, k, v, v, v)


def _ret_kernel(q_ref, k_ref, v_ref, gate_ref, dmat_ref, wq_ref, wk_ref, dec_ref, gn_ref, o_ref, sf_scr, sb_scr, *, c, n_chunks):
    pair = 2 * HEAD_DIM
    lane = lax.broadcasted_iota(jnp.int32, (c, pair), 1)
    head0 = lane < HEAD_DIM
    blk_r = lax.broadcasted_iota(jnp.int32, (pair, pair), 0) // HEAD_DIM
    blk_c = lax.broadcasted_iota(jnp.int32, (pair, pair), 1) // HEAD_DIM
    same_head = blk_r == blk_c
    dec_f = dec_ref[0:1, :]
    dec_b = dec_ref[1:2, :]
    tn = (((0,), (0,)), ((), ()))
    nt = (((1,), (1,)), ((), ()))

    def rows_of(n):
        return pl.ds(pl.multiple_of(n * c, c), c)

    def kv_body(n, carry):
        rows = rows_of(n)
        kf32 = k_ref[0, rows, :].astype(F32)
        kw = jnp.concatenate([(kf32 * wk_ref[:, :pair]).astype(BF16), (kf32 * wk_ref[:, pair:]).astype(BF16)], axis=1)
        kv = lax.dot_general(kw, v_ref[0, rows, :], tn, preferred_element_type=F32)
        sf_scr[n] = jnp.where(same_head, kv[:pair], 0.0)
        sb_scr[n] = jnp.where(same_head, kv[pair:], 0.0)
        return carry

    lax.fori_loop(0, n_chunks, kv_body, 0, unroll=RET_UNROLL)

    def scan_body(t, carry):
        sf, sb = carry
        nb = n_chunks - 1 - t
        kv_f = sf_scr[t]
        kv_b = sb_scr[nb]
        sf_scr[t] = sf
        sb_scr[nb] = sb
        return sf * dec_f + kv_f, sb * dec_b + kv_b

    zero = jnp.zeros((pair, pair), F32)
    lax.fori_loop(0, n_chunks, scan_body, (zero, zero))

    def out_body(n, carry):
        rows = rows_of(n)
        q = q_ref[0, rows, :]
        k = k_ref[0, rows, :]
        v = v_ref[0, rows, :]
        qf32 = q.astype(F32)
        intra = []
        for hh in range(2):
            mask = head0 if hh == 0 else jnp.logical_not(head0)
            kh = jnp.where(mask, k, jnp.zeros_like(k))
            s = lax.dot_general(q, kh, nt, preferred_element_type=F32)
            a = (s * dmat_ref[hh]).astype(BF16)
            intra.append(jnp.dot(a, v, preferred_element_type=F32))
        qw = jnp.concatenate([(qf32 * wq_ref[:, :pair]).astype(BF16), (qf32 * wq_ref[:, pair:]).astype(BF16)], axis=1)
        states = jnp.concatenate([sf_scr[n].astype(BF16), sb_scr[n].astype(BF16)], axis=0)
        tot = jnp.where(head0, intra[0], intra[1]) + jnp.dot(qw, states, preferred_element_type=F32)
        inv = 1.0 / HEAD_DIM
        s0 = jnp.sum(jnp.where(head0, tot, 0.0), axis=-1, keepdims=True)
        s1 = jnp.sum(jnp.where(head0, 0.0, tot), axis=-1, keepdims=True)
        xc = tot - jnp.where(head0, s0, s1) * inv
        sq = xc * xc
        v0 = jnp.sum(jnp.where(head0, sq, 0.0), axis=-1, keepdims=True)
        v1 = jnp.sum(jnp.where(head0, 0.0, sq), axis=-1, keepdims=True)
        y = xc * lax.rsqrt(jnp.where(head0, v0, v1) * inv + NORM_EPS)
        gt = gate_ref[0, rows, :].astype(F32)
        y = y * gn_ref[...] * (gt / (1.0 + jnp.exp(-gt)))
        o_ref[0, rows, :] = y.astype(BF16)
        return carry

    lax.fori_loop(0, n_chunks, out_body, 0, unroll=RET_UNROLL)


def _retention_tables(c):
    hidx = jnp.arange(N_HEADS, dtype=F32)
    lg_f = jnp.log1p(-jnp.exp2(-(RET_DECAY_BASE_FWD + hidx)))
    lg_b = jnp.log1p(-jnp.exp2(-(RET_DECAY_BASE_BWD + hidx)))
    pos = jnp.arange(c, dtype=F32)
    diff = pos[:, None] - pos[None, :]
    dm_f = jnp.exp(jnp.maximum(diff, 0.0)[None] * lg_f[:, None, None])
    dm_b = jnp.exp(jnp.maximum(-diff, 0.0)[None] * lg_b[:, None, None])
    dmat = jnp.where((diff >= 0)[None], dm_f, dm_b)

    def per_lane(tab):
        t = jnp.repeat(tab[:, :, None], HEAD_DIM, axis=2)
        return t.reshape(N_HEADS // 2, 2, c, HEAD_DIM).transpose(0, 2, 1, 3).reshape(N_HEADS // 2, c, 2 * HEAD_DIM)

    wq_f = per_lane(jnp.exp((pos + 1.0)[None, :] * lg_f[:, None]))
    wq_b = per_lane(jnp.exp((c - pos)[None, :] * lg_b[:, None]))
    wk_f = per_lane(jnp.exp((c - 1.0 - pos)[None, :] * lg_f[:, None]))
    wk_b = per_lane(jnp.exp(pos[None, :] * lg_b[:, None]))
    wq = jnp.concatenate([wq_f, wq_b], axis=2)
    wk = jnp.concatenate([wk_f, wk_b], axis=2)
    dec = jnp.stack([jnp.repeat(jnp.exp(c * lg_f), HEAD_DIM), jnp.repeat(jnp.exp(c * lg_b), HEAD_DIM)], axis=0)
    dec = dec.reshape(2, N_HEADS // 2, 2 * HEAD_DIM).transpose(1, 0, 2)
    return dmat, wq, wk, dec


def _retention(q, k, v, gate, ret_norm_g, b, s):
    c = min(RET_CHUNK, s)
    n_chunks = s // c
    pair = 2 * HEAD_DIM
    dmat, wq, wk, dec = _retention_tables(c)
    seq_blk = pl.BlockSpec((1, s, pair), lambda bi, hp: (bi, 0, hp))
    r3 = lambda z: z.reshape(b, s, WIDTH)
    return pl.pallas_call(
        functools.partial(_ret_kernel, c=c, n_chunks=n_chunks),
        grid=(b, N_HEADS // 2),
        in_specs=[seq_blk, seq_blk, seq_blk, seq_blk,
                  pl.BlockSpec((2, c, c), lambda bi, hp: (hp, 0, 0)),
                  pl.BlockSpec((None, c, 2 * pair), lambda bi, hp: (hp, 0, 0)),
                  pl.BlockSpec((None, c, 2 * pair), lambda bi, hp: (hp, 0, 0)),
                  pl.BlockSpec((None, 2, pair), lambda bi, hp: (hp, 0, 0)),
                  pl.BlockSpec((1, pair), lambda bi, hp: (0, hp))],
        out_specs=seq_blk,
        out_shape=jax.ShapeDtypeStruct((b, s, WIDTH), BF16),
        scratch_shapes=[pltpu.VMEM((n_chunks, pair, pair), F32), pltpu.VMEM((n_chunks, pair, pair), F32)],
        compiler_params=_cparams("arbitrary", "arbitrary"),
        name="retention",
    )(r3(q), r3(k), r3(v), r3(gate), dmat, wq, wk, dec, ret_norm_g.reshape(1, WIDTH)).reshape(b * s, WIDTH)


N_PIECES = 4
HALF_D = D_MODEL // 2
assert N_PIECES * LANES == HALF_D


def _pack_bf16_pair(a, b):
    ua = pltpu.bitcast(a.astype(BF16).astype(F32), jnp.uint32)
    ub = pltpu.bitcast(b.astype(BF16).astype(F32), jnp.uint32)
    return ua | (ub >> 16)


def _unpack_bf16_pair(u):
    return pltpu.bitcast(u & jnp.uint32(0xFFFF0000), F32), pltpu.bitcast(u << 16, F32)


def _store_pieces(ref, rows, val):
    for s in range(N_PIECES):
        lo = slice(s * LANES, (s + 1) * LANES)
        hi = slice(HALF_D + s * LANES, HALF_D + (s + 1) * LANES)
        ref[s, rows, :] = _pack_bf16_pair(val[:, lo], val[:, hi])


def _load_pieces(ref, rows):
    parts = [_unpack_bf16_pair(ref[s, rows, :]) for s in range(N_PIECES)]
    return [p[0] for p in parts] + [p[1] for p in parts]


def _split_bf16(x):
    hi = x.astype(BF16)
    return hi, (x - hi.astype(F32)).astype(BF16)


def _outproj_kernel(x_ref, o1_ref, o2_ref, o3_ref, s1_ref, s2_ref, s3_ref, ret_ref, wout_ref, expand_ref,
                    gffn_ref, wr_ref, br_ref, tri_ref,
                    x1_ref, h2t_ref, ri_ref, gates_ref, cnt_ref, base_scr, nat_o, nat_s, grp_o):
    i = pl.program_id(0)
    tm = x_ref.shape[0]

    @pl.when(i == 0)
    def _():
        base_scr[...] = jnp.zeros_like(base_scr)

    n_col = WIDTH // LANES
    for bb in range(REGROUP):
        rows4 = pl.ds(bb, tm // REGROUP, stride=REGROUP)
        for c in range(n_col):
            nat_o[0, c, rows4, :] = o2_ref[0, bb, :, c * LANES:(c + 1) * LANES].astype(F32)
        nat_s[0, rows4, :] = s2_ref[0, bb]
        for aa in range(REGROUP):
            rows16 = pl.ds(aa, tm // REGROUP ** 2, stride=REGROUP)
            for c in range(n_col):
                grp_o[c, bb, rows16, :] = o3_ref[0, bb * REGROUP + aa, :, c * LANES:(c + 1) * LANES].astype(F32)
            grp_o[n_col, bb, rows16, :] = s3_ref[0, bb * REGROUP + aa]
        for c in range(n_col):
            nat_o[1, c, rows4, :] = grp_o[c, bb]
        nat_s[1, rows4, :] = grp_o[n_col, bb]
    hm = tm // OUT_SPLIT
    lane = lax.broadcasted_iota(jnp.int32, (hm, LANES), 1).astype(F32)
    base = base_scr[...]
    for r0 in range(0, tm, hm):
        rows = slice(r0, r0 + hm)
        o_nat = [[o1_ref[rows, c * LANES:(c + 1) * LANES].astype(F32) for c in range(n_col)]]
        sts = [s1_ref[rows, :]]
        for di in range(len(RESIDUE_DILATIONS)):
            o_nat.append([nat_o[di, c, rows, :] for c in range(n_col)])
            sts.append(nat_s[di, rows, :])

        mx = jnp.maximum(jnp.maximum(sts[0], sts[1]), sts[2])
        es = [jnp.exp(st - mx) for st in sts]
        den = es[0] + es[1] + es[2]
        attn = jnp.concatenate(o_nat[-1], axis=1)
        for e, o_cols in zip(es[:-1], o_nat[:-1]):
            hi, lo = _split_bf16(e / den)
            w_full = jnp.dot(jnp.concatenate([hi, lo], axis=1), expand_ref[...], preferred_element_type=F32)
            attn = attn + w_full * (jnp.concatenate(o_cols, axis=1) - jnp.concatenate(o_nat[-1], axis=1))
        mixed = jnp.concatenate([attn.astype(BF16), ret_ref[rows, :]], axis=1)
        x1 = x_ref[rows, :] + jnp.dot(mixed, wout_ref[...], preferred_element_type=F32)
        x1_ref[rows, :] = x1

        h2 = x1 * lax.rsqrt(jnp.mean(x1 * x1, axis=-1, keepdims=True) + NORM_EPS) * gffn_ref[...]
        _store_pieces(h2t_ref, rows, h2)

        hi, lo = _split_bf16(h2)
        prod = jnp.dot(jnp.concatenate([hi, lo], axis=1), wr_ref[...], preferred_element_type=F32)
        logits = prod + pltpu.roll(prod, LANES - N_EXPERTS, 1) + br_ref[...]
        work = logits
        vals, idxs = [], []
        onehot = jnp.zeros((hm, LANES), F32)
        for _k in range(TOP_K):
            mk = jnp.max(work, axis=-1, keepdims=True)
            ik = jnp.min(jnp.where(work == mk, lane, float(LANES)), axis=-1, keepdims=True)
            sel = lane == ik
            onehot = jnp.where(sel, 1.0, onehot)
            work = jnp.where(sel, -jnp.inf, work)
            vals.append(mk)
            idxs.append(ik)
        ex = [jnp.exp(vk - vals[0]) for vk in vals]
        tot = ex[0] + ex[1] + ex[2] + ex[3]
        before = jnp.dot(tri_ref[...], onehot.astype(BF16), preferred_element_type=F32) + base
        ri = jnp.zeros((hm, LANES), F32)
        gt = jnp.zeros((hm, LANES), F32)
        for kk in range(TOP_K):
            rank = jnp.sum(jnp.where(lane == idxs[kk], before, 0.0), axis=-1, keepdims=True)
            ri = jnp.where(lane == float(kk), idxs[kk], ri)
            ri = jnp.where(lane == float(TOP_K + kk), rank, ri)
            gt = jnp.where(lane == float(kk), ex[kk] / tot, gt)
        ri_ref[:, rows] = jnp.transpose(ri)[:2 * TOP_K].astype(jnp.int32)
        gates_ref[rows, :] = gt
        base = base + jnp.sum(onehot, axis=0, keepdims=True)
    base_scr[...] = base
    cnt_ref[...] = base


def _outproj_router(x2d, outs, stats, ret, wout_bf16, norm_ffn_g, w_router, b_router, seq):
    t = x2d.shape[0]
    tm = TM_OUT
    per_seq = seq // tm
    row = lambda i: (i, 0)

    def res_spec(d, width):
        return pl.BlockSpec((1, d, tm // d, width), lambda i: (i // per_seq, 0, i % per_seq, 0))

    const = lambda i: (0, 0)
    head_of_col = jnp.arange(WIDTH) // HEAD_DIM
    expand = (jnp.arange(LANES)[:, None] == head_of_col[None, :]).astype(BF16)
    expand2 = jnp.concatenate([expand, expand], axis=0)
    wr_hi, wr_lo = _split_bf16(w_router)
    zeros = jnp.zeros((D_MODEL, LANES), BF16)
    wr2 = jnp.concatenate([zeros.at[:, :N_EXPERTS].set(wr_hi).at[:, N_EXPERTS:2 * N_EXPERTS].set(wr_lo),
                           zeros.at[:, :N_EXPERTS].set(wr_hi)], axis=0)
    br = jnp.full((1, LANES), NEG_INF, F32).at[0, :N_EXPERTS].set(b_router)
    hm = tm // OUT_SPLIT
    tri = (jnp.arange(hm)[:, None] > jnp.arange(hm)[None, :]).astype(BF16)
    o_spec = pl.BlockSpec((tm, WIDTH), row)
    s_spec = pl.BlockSpec((tm, LANES), row)
    return pl.pallas_call(
        _outproj_kernel,
        grid=(t // tm,),
        in_specs=[pl.BlockSpec((tm, D_MODEL), row),
                  o_spec, *[res_spec(d, WIDTH) for d in RESIDUE_DILATIONS],
                  s_spec, *[res_spec(d, LANES) for d in RESIDUE_DILATIONS], o_spec,
                  pl.BlockSpec((D_MODEL, D_MODEL), const), pl.BlockSpec((2 * LANES, WIDTH), const),
                  pl.BlockSpec((1, D_MODEL), const), pl.BlockSpec((2 * D_MODEL, LANES), const),
                  pl.BlockSpec((1, LANES), const), pl.BlockSpec((hm, hm), const)],
        out_specs=[pl.BlockSpec((tm, D_MODEL), row), pl.BlockSpec((N_PIECES, tm, LANES), lambda i: (0, i, 0)),
                   pl.BlockSpec((2 * TOP_K, tm), lambda i: (0, i)), s_spec, pl.BlockSpec((1, LANES), const)],
        out_shape=[jax.ShapeDtypeStruct((t, D_MODEL), F32), jax.ShapeDtypeStruct((N_PIECES, t, LANES), jnp.uint32),
                   jax.ShapeDtypeStruct((2 * TOP_K, t), jnp.int32), jax.ShapeDtypeStruct((t, LANES), F32),
                   jax.ShapeDtypeStruct((1, LANES), F32)],
        scratch_shapes=[pltpu.VMEM((1, LANES), F32),
                        pltpu.VMEM((len(RESIDUE_DILATIONS), WIDTH // LANES, tm, LANES), F32),
                        pltpu.VMEM((len(RESIDUE_DILATIONS), tm, LANES), F32),
                        pltpu.VMEM((WIDTH // LANES + 1, REGROUP, tm // REGROUP, LANES), F32)],
        compiler_params=_cparams("arbitrary"),
        name="outproj_router",
    )(x2d, *outs, *stats, ret, wout_bf16, expand2, norm_ffn_g.reshape(1, D_MODEL), wr2, br, tri)


SC_CORES = 2
SC_SUBCORES = 16
SC_WINDOW = 128


def _sc_mesh():
    return plsc.VectorSubcoreMesh(core_axis_name="c", subcore_axis_name="s")


def _sc_scatter_rows(src, idx, n_out_rows):
    n_rows = src.shape[0]
    workers = SC_CORES * SC_SUBCORES
    per_worker = n_rows // workers
    n_win = per_worker // SC_WINDOW
    assert per_worker * workers == n_rows and n_win * SC_WINDOW == per_worker and idx.shape == (TOP_K * n_rows,)

    @functools.partial(
        pl.kernel, mesh=_sc_mesh(),
        out_type=jax.ShapeDtypeStruct((n_out_rows, LANES), src.dtype),
        scratch_types=[pltpu.VMEM((SC_WINDOW,), jnp.int32), pltpu.VMEM((SC_WINDOW, LANES), src.dtype)],
        name="sc_scatter_rows",
    )
    def scatter(src_hbm, idx_hbm, out_hbm, idx_v, rows_v):
        wid = lax.axis_index("s") * SC_CORES + lax.axis_index("c")
        base = wid * per_worker

        @pl.loop(0, n_win)
        def _(j):
            off = pl.multiple_of(base + j * SC_WINDOW, SC_WINDOW)
            pltpu.sync_copy(src_hbm.at[pl.ds(off, SC_WINDOW)], rows_v)
            for kk in range(TOP_K):
                pltpu.sync_copy(idx_hbm.at[pl.ds(pl.multiple_of(kk * n_rows + off, SC_WINDOW), SC_WINDOW)], idx_v)
                pltpu.sync_copy(rows_v, out_hbm.at[idx_v])

    return scatter(src, idx)


def _expert_kernel(blk_e_ref, nvalid_ref, rows_ref, xs_ref, wgu_ref, bgu_ref, wd_ref, bd_ref, ys_ref, wgu_bf, wd_bf):
    i = pl.program_id(0)

    @pl.when((i == 0) | (blk_e_ref[i] != blk_e_ref[jnp.maximum(i - 1, 0)]))
    def _():
        wgu_bf[...] = wgu_ref[0].astype(BF16)
        wd_bf[...] = wd_ref[0].astype(BF16)

    def mlp(m):
        live = lax.broadcasted_iota(jnp.int32, (m, LANES), 0) < rows_ref[i]
        x = jnp.concatenate([jnp.where(live, p, 0.0).astype(BF16) for p in _load_pieces(xs_ref, slice(0, m))], axis=1)
        gu = jnp.dot(x, wgu_bf[...], preferred_element_type=F32) + bgu_ref[0]
        gate = jnp.minimum(gu[:, :EXPERT_FF], SWIGLU_LIMIT)
        up = jnp.clip(gu[:, EXPERT_FF:], -SWIGLU_LIMIT, SWIGLU_LIMIT)
        act = gate * (1.0 / (1.0 + jnp.exp(-SWIGLU_ALPHA * gate))) * (up + 1.0)
        y = jnp.dot(act.astype(BF16), wd_bf[...], preferred_element_type=F32) + bd_ref[0]
        _store_pieces(ys_ref, slice(0, m), y)

    n_sub = lax.shift_right_logical(rows_ref[i] + (EXPERT_SUB - 1), EXPERT_SUB.bit_length() - 1)
    for v in range(1, BM // EXPERT_SUB + 1):
        pl.when((i < nvalid_ref[0]) & (n_sub == v))(functools.partial(mlp, v * EXPERT_SUB))


def _experts(xs, blk_e, nvalid, live_rows, wgu_bf16, bgu, wd_bf16, bd):
    cap = xs.shape[1]
    nblk = cap // BM

    def blk(i, be, nv, lr):
        return (0, jnp.minimum(i, nv[0] - 1), 0)

    def by_expert(i, be, nv, lr):
        return (be[i], 0, 0)

    return pl.pallas_call(
        _expert_kernel,
        grid_spec=pltpu.PrefetchScalarGridSpec(
            num_scalar_prefetch=3,
            grid=(nblk,),
            in_specs=[pl.BlockSpec((N_PIECES, BM, LANES), blk),
                      pl.BlockSpec((1, D_MODEL, 2 * EXPERT_FF), by_expert),
                      pl.BlockSpec((1, 1, 2 * EXPERT_FF), by_expert),
                      pl.BlockSpec((1, EXPERT_FF, D_MODEL), by_expert),
                      pl.BlockSpec((1, 1, D_MODEL), by_expert)],
            out_specs=pl.BlockSpec((N_PIECES, BM, LANES), blk),
            scratch_shapes=[pltpu.VMEM((D_MODEL, 2 * EXPERT_FF), BF16), pltpu.VMEM((EXPERT_FF, D_MODEL), BF16)],
        ),
        out_shape=jax.ShapeDtypeStruct((N_PIECES, cap, LANES), jnp.uint32),
        compiler_params=_cparams("arbitrary"),
        name="moe_experts",
    )(blk_e, nvalid, live_rows, xs, wgu_bf16, bgu.reshape(N_EXPERTS, 1, 2 * EXPERT_FF), wd_bf16,
      bd.reshape(N_EXPERTS, 1, D_MODEL))


def _sc_gather_rows(table, idx):
    n_rows = idx.shape[0]
    workers = SC_CORES * SC_SUBCORES
    per_worker = n_rows // workers
    n_win = per_worker // SC_WINDOW
    assert per_worker * workers == n_rows and n_win * SC_WINDOW == per_worker

    @functools.partial(
        pl.kernel, mesh=_sc_mesh(),
        out_type=jax.ShapeDtypeStruct((n_rows, LANES), table.dtype),
        scratch_types=[pltpu.VMEM((SC_WINDOW,), jnp.int32), pltpu.VMEM((SC_WINDOW, LANES), table.dtype),
                       pltpu.SemaphoreType.DMA],
        name="sc_gather_rows",
    )
    def gather(table_hbm, idx_hbm, out_hbm, idx_v, rows_v, sem):
        wid = lax.axis_index("s") * SC_CORES + lax.axis_index("c")
        base = wid * per_worker

        @pl.loop(0, n_win)
        def _(j):
            off = pl.multiple_of(base + j * SC_WINDOW, SC_WINDOW)
            pltpu.sync_copy(idx_hbm.at[pl.ds(off, SC_WINDOW)], idx_v)
            pltpu.async_copy(table_hbm.at[idx_v], rows_v, sem).wait()
            pltpu.sync_copy(rows_v, out_hbm.at[pl.ds(off, SC_WINDOW)])

    return gather(table, idx)


def _combine_kernel(x1_ref, gates_ref, gfin_ref, *refs, tc):
    piece_refs, o_ref = refs[:TOP_K * N_PIECES], refs[TOP_K * N_PIECES]
    g = gates_ref[...]
    gk = [jnp.broadcast_to(g[:, kk:kk + 1], (tc, LANES)) for kk in range(TOP_K)]
    zs = {}
    ssq = jnp.zeros((tc, 1), F32)
    for s in range(N_PIECES):
        c_lo, c_hi = s * LANES, HALF_D + s * LANES
        z_lo = x1_ref[:, c_lo:c_lo + LANES]
        z_hi = x1_ref[:, c_hi:c_hi + LANES]
        for kk in range(TOP_K):
            a, b = _unpack_bf16_pair(piece_refs[kk * N_PIECES + s][...])
            z_lo = z_lo + gk[kk] * a
            z_hi = z_hi + gk[kk] * b
        zs[c_lo], zs[c_hi] = z_lo, z_hi
        ssq = ssq + jnp.sum(z_lo * z_lo + z_hi * z_hi, axis=-1, keepdims=True)
    inv = lax.rsqrt(ssq * (1.0 / D_MODEL) + NORM_EPS)
    for c0, z in zs.items():
        o_ref[:, c0:c0 + LANES] = z * inv * gfin_ref[:, c0:c0 + LANES]


def _combine(x1, gates, gathered, norm_final_g):
    t = x1.shape[0]
    tc = TC
    row = lambda i: (i, 0)
    slot_specs = [pl.BlockSpec((None, tc, LANES), functools.partial(lambda i, j: (j, i, 0), j=j))
                  for j in range(TOP_K * N_PIECES)]
    return pl.pallas_call(
        functools.partial(_combine_kernel, tc=tc),
        grid=(t // tc,),
        in_specs=[pl.BlockSpec((tc, D_MODEL), row), pl.BlockSpec((tc, LANES), row),
                  pl.BlockSpec((1, D_MODEL), lambda i: (0, 0)), *slot_specs],
        out_specs=pl.BlockSpec((tc, D_MODEL), row),
        out_shape=jax.ShapeDtypeStruct((t, D_MODEL), F32),
        compiler_params=_cparams("arbitrary"),
        name="moe_combine",
    )(x1, gates, norm_final_g.reshape(1, D_MODEL), *([gathered] * (TOP_K * N_PIECES)))


def _moe(x1, h2t, ri, gates, counts_f, wgu_bf16, bgu, wd_bf16, bd, norm_final_g):
    t = x1.shape[0]
    a = t * TOP_K
    cap = a + N_EXPERTS * BM
    nblk = cap // BM
    counts = counts_f[0, :N_EXPERTS].astype(jnp.int32)
    padded = ((counts + BM - 1) // BM) * BM
    pend = jnp.cumsum(padded)
    pstart = pend - padded
    nvalid = (pend[-1] // BM).reshape(1)
    first_row = jnp.minimum(jnp.arange(nblk, dtype=jnp.int32) * BM, pend[-1] - 1)
    blk_e = jnp.sum(pend[None, :] <= first_row[:, None], axis=1).astype(jnp.int32)
    idx, rank = ri[:TOP_K], ri[TOP_K:2 * TOP_K]
    onehot = idx[None, :, :] == jnp.arange(N_EXPERTS, dtype=jnp.int32)[:, None, None]
    dest = rank + jnp.sum(jnp.where(onehot, pstart[:, None, None], 0), axis=0)
    seg_end = (pstart + counts)[blk_e]
    live_rows = jnp.clip(seg_end - jnp.arange(nblk, dtype=jnp.int32) * BM, 0, BM).astype(jnp.int32)
    piece = dest[:, None, :] + (jnp.arange(N_PIECES, dtype=jnp.int32) * cap)[None, :, None]
    piece = piece.reshape(TOP_K * N_PIECES * t)
    xs = _sc_scatter_rows(h2t.reshape(N_PIECES * t, LANES), piece, N_PIECES * cap)
    ys = _experts(xs.reshape(N_PIECES, cap, LANES), blk_e, nvalid, live_rows, wgu_bf16, bgu, wd_bf16, bd)
    gathered = _sc_gather_rows(ys.reshape(N_PIECES * cap, LANES), piece)
    return _combine(x1, gates, gathered.reshape(TOP_K * N_PIECES, t, LANES), norm_final_g)


def _encoder(x, p):
    b, s, d_model = x.shape
    assert d_model == D_MODEL and s % (max(DILATIONS) * TQ) == 0 and (b * s) % max(TM_OUT, TC) == 0, x.shape
    x2d = x.reshape(b * s, D_MODEL)
    cos_t, sin_t = _rope_tables(s)
    (qa, ka, va, qr, kr, vr, gr), by_residue = _inproj(x2d, p["norm_mix_g"], p["w_in"], cos_t, sin_t, s)
    outs, stats = [], []
    for d in DILATIONS:
        if d == 1:
            o, st = _banded_attention(*[z.reshape(b, s, WIDTH) for z in (qa, ka, va)])
            outs.append(o.reshape(b * s, WIDTH))
            stats.append(st.reshape(b * s, LANES))
        else:
            o, st = _banded_attention(*[z.reshape(b * d, s // d, WIDTH) for z in by_residue[d]])
            outs.append(o.reshape(b, d, s // d, WIDTH))
            stats.append(st.reshape(b, d, s // d, LANES))
    ret = _retention(qr, kr, vr, gr, p["ret_norm_g"], b, s)
    x1, h2t, ri, gates, counts = _outproj_router(x2d, outs, stats, ret, p["w_out"], p["norm_ffn_g"],
                                                 p["w_router"], p["b_router"], s)
    y = _moe(x1, h2t, ri, gates, counts, p["w_gate_up"], p["b_gate_up"], p["w_down"], p["b_down"], p["norm_final_g"])
    return y.reshape(b, s, D_MODEL)


def kernel(x_prompt, x_sample, norm_mix_g, w_in, ret_norm_g, w_out, norm_ffn_g, w_router, b_router, w_gate_up, b_gate_up, w_down, b_down, norm_final_g):
    assert norm_mix_g.shape[0] == 1, "single layer"
    p = dict(norm_mix_g=norm_mix_g[0], w_in=w_in[0].astype(BF16), ret_norm_g=ret_norm_g[0],
             w_out=w_out[0].astype(BF16), norm_ffn_g=norm_ffn_g[0], w_router=w_router[0], b_router=b_router[0],
             w_gate_up=w_gate_up[0], b_gate_up=b_gate_up[0], w_down=w_down[0],
             b_down=b_down[0], norm_final_g=norm_final_g)
    y_sample = _encoder(x_sample, p)
    y_prompt = _encoder(x_prompt, p)
    return (y_prompt, y_sample)
```

```python
import functools

import jax
import jax.numpy as jnp
from jax import lax
from jax.experimental import pallas as pl
from jax.experimental.pallas import tpu as pltpu
from jax.experimental.pallas import tpu_sc as plsc

D_MODEL = 1024
HEAD_DIM = 64
N_HEADS = 8
WIDTH = N_HEADS * HEAD_DIM
N_SLABS = 7
DILATIONS = (1, 4, 16)
HALF_SPAN = 64
ROPE_THETA = 10000.0
RET_DECAY_BASE_FWD = 5.0
RET_DECAY_BASE_BWD = 5.5
N_EXPERTS = 32
TOP_K = 4
EXPERT_FF = D_MODEL
SWIGLU_LIMIT = 7.0
SWIGLU_ALPHA = 1.702
NORM_EPS = 1e-6
NEG_INF = -1e30

LANES = 128
VMEM_LIMIT_BYTES = 56 * 1024 * 1024

TM_INPROJ = 1024
TQ = 128
TL_ATTN = 2048
HEAD_GROUP = 4
RET_CHUNK = 256
TM_OUT = 1024
OUT_SPLIT = 4
RET_UNROLL = 8
BM = 1024
EXPERT_SUB = 256
TC = 1024

F32 = jnp.float32
BF16 = jnp.bfloat16


def _cparams(*sem):
    return pltpu.CompilerParams(dimension_semantics=sem, vmem_limit_bytes=VMEM_LIMIT_BYTES)


_ROTATE = (True, True, False, True, True, False, False)
ATTN_SCALE = HEAD_DIM ** -0.5
_SCALE = (ATTN_SCALE, 1.0, 1.0, 1.0, ATTN_SCALE, 1.0, 1.0)


N_ATTN_SLABS = 3
RESIDUE_DILATIONS = tuple(d for d in DILATIONS if d > 1)
REGROUP = 4
STAGE_SLOTS = 2
assert RESIDUE_DILATIONS == (REGROUP, REGROUP ** 2)


def _inproj_kernel(x_ref, g_ref, w_ref, cos_ref, sin_ref, *refs):
    out_refs = refs[:N_SLABS]
    res_refs = refs[N_SLABS:N_SLABS + N_ATTN_SLABS * len(RESIDUE_DILATIONS)]
    stage, stage4 = refs[-2:]
    x = x_ref[...]
    tm = x.shape[0]
    h = (x * lax.rsqrt(jnp.mean(x * x, axis=-1, keepdims=True) + NORM_EPS) * g_ref[...]).astype(BF16)
    cos = cos_ref[...]
    sin = sin_ref[...]
    cos_scaled = cos * ATTN_SCALE
    sin_scaled = sin * ATTN_SCALE
    lane = lax.broadcasted_iota(jnp.int32, cos.shape, 1)
    first_half = (lane & (HEAD_DIM - 1)) < HEAD_DIM // 2
    for j, o_ref in enumerate(out_refs):
        p = jnp.dot(h, w_ref[:, j * WIDTH:(j + 1) * WIDTH], preferred_element_type=F32)
        for c in range(WIDTH // LANES):
            cols = slice(c * LANES, (c + 1) * LANES)
            r = p[:, cols]
            if _ROTATE[j]:
                partner = jnp.where(first_half, pltpu.roll(r, LANES - HEAD_DIM // 2, 1), pltpu.roll(r, HEAD_DIM // 2, 1))
                cs, sn = (cos, sin) if _SCALE[j] == 1.0 else (cos_scaled, sin_scaled)
                r = r * cs + partner * sn
            o_ref[:, cols] = r.astype(BF16)
            if j < N_ATTN_SLABS:
                js = j % STAGE_SLOTS
                stage[js, c] = r
                dst4 = res_refs[j]
                dst16 = res_refs[N_ATTN_SLABS + j]
                for bb in range(REGROUP):
                    grp = stage[js, c, pl.ds(bb, tm // REGROUP, stride=REGROUP), :]
                    dst4[0, bb, :, cols] = grp.astype(BF16)
                    stage4[js, c, bb] = grp
                    for aa in range(REGROUP):
                        sub = stage4[js, c, bb, pl.ds(aa, tm // REGROUP ** 2, stride=REGROUP), :]
                        dst16[0, bb * REGROUP + aa, :, cols] = sub.astype(BF16)


def _inproj(x2d, g, w_bf16, cos_t, sin_t, seq):
    t = x2d.shape[0]
    tm = min(TM_INPROJ, seq)
    pos_blocks = seq // tm
    b = t // seq
    out = jax.ShapeDtypeStruct((t, WIDTH), BF16)
    row = lambda i: (i, 0)
    out_specs = [pl.BlockSpec((tm, WIDTH), row)] * N_SLABS
    out_shape = [out] * N_SLABS
    for d in RESIDUE_DILATIONS:
        out_specs += [pl.BlockSpec((1, d, tm // d, WIDTH), lambda i: (i // pos_blocks, 0, i % pos_blocks, 0))] * N_ATTN_SLABS
        out_shape += [jax.ShapeDtypeStruct((b, d, seq // d, WIDTH), BF16)] * N_ATTN_SLABS
    res = pl.pallas_call(
        _inproj_kernel,
        grid=(t // tm,),
        in_specs=[
            pl.BlockSpec((tm, D_MODEL), row),
            pl.BlockSpec((1, D_MODEL), lambda i: (0, 0)),
            pl.BlockSpec((D_MODEL, N_SLABS * WIDTH), lambda i: (0, 0)),
            pl.BlockSpec((tm, LANES), lambda i: (i % pos_blocks, 0)),
            pl.BlockSpec((tm, LANES), lambda i: (i % pos_blocks, 0)),
        ],
        out_specs=out_specs,
        out_shape=out_shape,
        scratch_shapes=[pltpu.VMEM((STAGE_SLOTS, WIDTH // LANES, tm, LANES), F32),
                        pltpu.VMEM((STAGE_SLOTS, WIDTH // LANES, REGROUP, tm // REGROUP, LANES), F32)],
        compiler_params=_cparams("arbitrary"),
        name="inproj",
    )(x2d, g.reshape(1, D_MODEL), w_bf16, cos_t, sin_t)
    natural = res[:N_SLABS]
    by_residue = {d: res[N_SLABS + di * N_ATTN_SLABS:N_SLABS + (di + 1) * N_ATTN_SLABS]
                  for di, d in enumerate(RESIDUE_DILATIONS)}
    return natural, by_residue


def _rope_tables(seq):
    half = HEAD_DIM // 2
    inv_freq = ROPE_THETA ** (-jnp.arange(0, HEAD_DIM, 2, dtype=F32) / HEAD_DIM)
    ang = jnp.arange(seq, dtype=F32)[:, None] * inv_freq[None, :]
    cos, sin = jnp.cos(ang), jnp.sin(ang)
    reps = LANES // HEAD_DIM
    cos_t = jnp.tile(jnp.concatenate([cos, cos], axis=1), (1, reps))
    sin_t = jnp.tile(jnp.concatenate([-sin, sin], axis=1), (1, reps))
    assert cos_t.shape == (seq, LANES) and half * 2 == HEAD_DIM
    return cos_t, sin_t


def _attn_kernel(q_ref, kp_ref, kc_ref, kn_ref, vp_ref, vc_ref, vn_ref, o_ref, st_ref, kbuf, vbuf, *, gb, tl, sub_len):
    i = pl.program_id(1)
    hs = HALF_SPAN
    for gi in range(gb):
        kbuf[gi, 0:hs] = kp_ref[gi]
        kbuf[gi, hs:hs + tl] = kc_ref[gi]
        kbuf[gi, hs + tl:hs + tl + hs] = kn_ref[gi]
        vbuf[gi, 0:hs] = vp_ref[gi]
        vbuf[gi, hs:hs + tl] = vc_ref[gi]
        vbuf[gi, hs + tl:hs + tl + hs] = vn_ref[gi]
    tk = TQ + 2 * hs
    gw = HEAD_GROUP * HEAD_DIM
    qi = lax.broadcasted_iota(jnp.int32, (TQ, tk), 0)
    kj = lax.broadcasted_iota(jnp.int32, (TQ, tk), 1)
    band_bias = jnp.where((kj >= qi) & (kj - qi <= 2 * hs), 0.0, NEG_INF).astype(F32)
    key_col = lax.broadcasted_iota(jnp.int32, (1, tk), 1)
    lane = lax.broadcasted_iota(jnp.int32, (TQ, LANES), 1)
    head_of_lane = lax.broadcasted_iota(jnp.int32, (TQ, gw), 1) // HEAD_DIM
    n_sub = tl // TQ
    for gi, sub in [(gi, sub) for gi in range(gb) for sub in range(n_sub)]:
        a = sub * TQ
        bias = band_bias
        if sub == 0 or sub == n_sub - 1:
            kpos = key_col + (i * tl + a - hs)
            bias = bias + jnp.where((kpos >= 0) & (kpos < sub_len), 0.0, NEG_INF).astype(F32)
        bias = jnp.concatenate([bias] * HEAD_GROUP, axis=0)
        st = jnp.zeros((TQ, LANES), F32)
        for g in range(N_HEADS // HEAD_GROUP):
            cols = slice(g * gw, (g + 1) * gw)
            q4 = q_ref[gi, a:a + TQ, cols]
            k4 = kbuf[gi, a:a + tk, cols]
            v4 = vbuf[gi, a:a + tk, cols]
            lhs = jnp.concatenate([jnp.where(head_of_lane == h, q4, jnp.zeros_like(q4)) for h in range(HEAD_GROUP)], axis=0)
            s = lax.dot_general(lhs, k4, (((1,), (1,)), ((), ())), preferred_element_type=F32) + bias
            m = jnp.max(s, axis=-1, keepdims=True)
            p = jnp.exp(s - m)
            l = jnp.sum(p, axis=-1, keepdims=True)
            o_all = jnp.dot(p.astype(BF16), v4, preferred_element_type=F32)
            inv = 1.0 / l
            lse = m + jnp.log(l)
            halves = []
            for hp in range(HEAD_GROUP // 2):
                blk = slice(hp * LANES, (hp + 1) * LANES)
                ra, rb = slice(2 * hp * TQ, (2 * hp + 1) * TQ), slice((2 * hp + 1) * TQ, (2 * hp + 2) * TQ)
                halves.append(jnp.where(lane < HEAD_DIM, o_all[ra, blk] * inv[ra], o_all[rb, blk] * inv[rb]))
            for h in range(HEAD_GROUP):
                st = jnp.where(lane == g * HEAD_GROUP + h, lse[h * TQ:(h + 1) * TQ], st)
            o_ref[gi, a:a + TQ, cols] = jnp.concatenate(halves, axis=1).astype(BF16)
        st_ref[gi, a:a + TQ, :] = st


def _banded_attention(q, k, v):
    g, sub_len, _ = q.shape
    tl = min(TL_ATTN, sub_len)
    gb = min(TL_ATTN // tl, g)
    hs = HALF_SPAN
    per = tl // hs
    last = sub_len // hs - 1
    cur = pl.BlockSpec((gb, tl, WIDTH), lambda b, i: (b, i, 0))
    prev = pl.BlockSpec((gb, hs, WIDTH), lambda b, i: (b, jnp.maximum(i * per - 1, 0), 0))
    nxt = pl.BlockSpec((gb, hs, WIDTH), lambda b, i: (b, jnp.minimum((i + 1) * per, last), 0))
    return pl.pallas_call(
        functools.partial(_attn_kernel, gb=gb, tl=tl, sub_len=sub_len),
        grid=(g // gb, sub_len // tl),
        in_specs=[cur, prev, cur, nxt, prev, cur, nxt],
        out_specs=[cur, pl.BlockSpec((gb, tl, LANES), lambda b, i: (b, i, 0))],
        out_shape=[jax.ShapeDtypeStruct((g, sub_len, WIDTH), BF16),
                   jax.ShapeDtypeStruct((g, sub_len, LANES), F32)],
        scratch_shapes=[pltpu.VMEM((gb, tl + 2 * hs, WIDTH), BF16), pltpu.VMEM((gb, tl + 2 * hs, WIDTH), BF16)],
        compiler_params=_cparams("arbitrary", "arbitrary"),
        name="banded_attention",
    )(q, k, k, k, v, v, v)


def _ret_kernel(q_ref, k_ref, v_ref, gate_ref, dmat_ref, wq_ref, wk_ref, dec_ref, gn_ref, o_ref, sf_scr, sb_scr, *, c, n_chunks):
    pair = 2 * HEAD_DIM
    lane = lax.broadcasted_iota(jnp.int32, (c, pair), 1)
    head0 = lane < HEAD_DIM
    blk_r = lax.broadcasted_iota(jnp.int32, (pair, pair), 0) // HEAD_DIM
    blk_c = lax.broadcasted_iota(jnp.int32, (pair, pair), 1) // HEAD_DIM
    same_head = blk_r == blk_c
    dec_f = dec_ref[0:1, :]
    dec_b = dec_ref[1:2, :]
    tn = (((0,), (0,)), ((), ()))
    nt = (((1,), (1,)), ((), ()))

    def rows_of(n):
        return pl.ds(pl.multiple_of(n * c, c), c)

    def kv_body(n, carry):
        rows = rows_of(n)
        kf32 = k_ref[0, rows, :].astype(F32)
        kw = jnp.concatenate([(kf32 * wk_ref[:, :pair]).astype(BF16), (kf32 * wk_ref[:, pair:]).astype(BF16)], axis=1)
        kv = lax.dot_general(kw, v_ref[0, rows, :], tn, preferred_element_type=F32)
        sf_scr[n] = jnp.where(same_head, kv[:pair], 0.0)
        sb_scr[n] = jnp.where(same_head, kv[pair:], 0.0)
        return carry

    lax.fori_loop(0, n_chunks, kv_body, 0, unroll=RET_UNROLL)

    def scan_body(t, carry):
        sf, sb = carry
        nb = n_chunks - 1 - t
        kv_f = sf_scr[t]
        kv_b = sb_scr[nb]
        sf_scr[t] = sf
        sb_scr[nb] = sb
        return sf * dec_f + kv_f, sb * dec_b + kv_b

    zero = jnp.zeros((pair, pair), F32)
    lax.fori_loop(0, n_chunks, scan_body, (zero, zero))

    def out_body(n, carry):
        rows = rows_of(n)
        q = q_ref[0, rows, :]
        k = k_ref[0, rows, :]
        v = v_ref[0, rows, :]
        qf32 = q.astype(F32)
        intra = []
        for hh in range(2):
            mask = head0 if hh == 0 else jnp.logical_not(head0)
            kh = jnp.where(mask, k, jnp.zeros_like(k))
            s = lax.dot_general(q, kh, nt, preferred_element_type=F32)
            a = (s * dmat_ref[hh]).astype(BF16)
            intra.append(jnp.dot(a, v, preferred_element_type=F32))
        qw = jnp.concatenate([(qf32 * wq_ref[:, :pair]).astype(BF16), (qf32 * wq_ref[:, pair:]).astype(BF16)], axis=1)
        states = jnp.concatenate([sf_scr[n].astype(BF16), sb_scr[n].astype(BF16)], axis=0)
        tot = jnp.where(head0, intra[0], intra[1]) + jnp.dot(qw, states, preferred_element_type=F32)
        inv = 1.0 / HEAD_DIM
        s0 = jnp.sum(jnp.where(head0, tot, 0.0), axis=-1, keepdims=True)
        s1 = jnp.sum(jnp.where(head0, 0.0, tot), axis=-1, keepdims=True)
        xc = tot - jnp.where(head0, s0, s1) * inv
        sq = xc * xc
        v0 = jnp.sum(jnp.where(head0, sq, 0.0), axis=-1, keepdims=True)
        v1 = jnp.sum(jnp.where(head0, 0.0, sq), axis=-1, keepdims=True)
        y = xc * lax.rsqrt(jnp.where(head0, v0, v1) * inv + NORM_EPS)
        gt = gate_ref[0, rows, :].astype(F32)
        y = y * gn_ref[...] * (gt / (1.0 + jnp.exp(-gt)))
        o_ref[0, rows, :] = y.astype(BF16)
        return carry

    lax.fori_loop(0, n_chunks, out_body, 0, unroll=RET_UNROLL)


def _retention_tables(c):
    hidx = jnp.arange(N_HEADS, dtype=F32)
    lg_f = jnp.log1p(-jnp.exp2(-(RET_DECAY_BASE_FWD + hidx)))
    lg_b = jnp.log1p(-jnp.exp2(-(RET_DECAY_BASE_BWD + hidx)))
    pos = jnp.arange(c, dtype=F32)
    diff = pos[:, None] - pos[None, :]
    dm_f = jnp.exp(jnp.maximum(diff, 0.0)[None] * lg_f[:, None, None])
    dm_b = jnp.exp(jnp.maximum(-diff, 0.0)[None] * lg_b[:, None, None])
    dmat = jnp.where((diff >= 0)[None], dm_f, dm_b)

    def per_lane(tab):
        t = jnp.repeat(tab[:, :, None], HEAD_DIM, axis=2)
        return t.reshape(N_HEADS // 2, 2, c, HEAD_DIM).transpose(0, 2, 1, 3).reshape(N_HEADS // 2, c, 2 * HEAD_DIM)

    wq_f = per_lane(jnp.exp((pos + 1.0)[None, :] * lg_f[:, None]))
    wq_b = per_lane(jnp.exp((c - pos)[None, :] * lg_b[:, None]))
    wk_f = per_lane(jnp.exp((c - 1.0 - pos)[None, :] * lg_f[:, None]))
    wk_b = per_lane(jnp.exp(pos[None, :] * lg_b[:, None]))
    wq = jnp.concatenate([wq_f, wq_b], axis=2)
    wk = jnp.concatenate([wk_f, wk_b], axis=2)
    dec = jnp.stack([jnp.repeat(jnp.exp(c * lg_f), HEAD_DIM), jnp.repeat(jnp.exp(c * lg_b), HEAD_DIM)], axis=0)
    dec = dec.reshape(2, N_HEADS // 2, 2 * HEAD_DIM).transpose(1, 0, 2)
    return dmat, wq, wk, dec


def _retention(q, k, v, gate, ret_norm_g, b, s):
    c = min(RET_CHUNK, s)
    n_chunks = s // c
    pair = 2 * HEAD_DIM
    dmat, wq, wk, dec = _retention_tables(c)
    seq_blk = pl.BlockSpec((1, s, pair), lambda bi, hp: (bi, 0, hp))
    r3 = lambda z: z.reshape(b, s, WIDTH)
    return pl.pallas_call(
        functools.partial(_ret_kernel, c=c, n_chunks=n_chunks),
        grid=(b, N_HEADS // 2),
        in_specs=[seq_blk, seq_blk, seq_blk, seq_blk,
                  pl.BlockSpec((2, c, c), lambda bi, hp: (hp, 0, 0)),
                  pl.BlockSpec((None, c, 2 * pair), lambda bi, hp: (hp, 0, 0)),
                  pl.BlockSpec((None, c, 2 * pair), lambda bi, hp: (hp, 0, 0)),
                  pl.BlockSpec((None, 2, pair), lambda bi, hp: (hp, 0, 0)),
                  pl.BlockSpec((1, pair), lambda bi, hp: (0, hp))],
        out_specs=seq_blk,
        out_shape=jax.ShapeDtypeStruct((b, s, WIDTH), BF16),
        scratch_shapes=[pltpu.VMEM((n_chunks, pair, pair), F32), pltpu.VMEM((n_chunks, pair, pair), F32)],
        compiler_params=_cparams("arbitrary", "arbitrary"),
        name="retention",
    )(r3(q), r3(k), r3(v), r3(gate), dmat, wq, wk, dec, ret_norm_g.reshape(1, WIDTH)).reshape(b * s, WIDTH)


N_PIECES = 4
HALF_D = D_MODEL // 2
assert N_PIECES * LANES == HALF_D


def _pack_bf16_pair(a, b):
    ua = pltpu.bitcast(a.astype(BF16).astype(F32), jnp.uint32)
    ub = pltpu.bitcast(b.astype(BF16).astype(F32), jnp.uint32)
    return ua | (ub >> 16)


def _unpack_bf16_pair(u):
    return pltpu.bitcast(u & jnp.uint32(0xFFFF0000), F32), pltpu.bitcast(u << 16, F32)


def _store_pieces(ref, rows, val):
    for s in range(N_PIECES):
        lo = slice(s * LANES, (s + 1) * LANES)
        hi = slice(HALF_D + s * LANES, HALF_D + (s + 1) * LANES)
        ref[s, rows, :] = _pack_bf16_pair(val[:, lo], val[:, hi])


def _load_pieces(ref, rows):
    parts = [_unpack_bf16_pair(ref[s, rows, :]) for s in range(N_PIECES)]
    return [p[0] for p in parts] + [p[1] for p in parts]


def _split_bf16(x):
    hi = x.astype(BF16)
    return hi, (x - hi.astype(F32)).astype(BF16)


def _outproj_kernel(x_ref, o1_ref, o2_ref, o3_ref, s1_ref, s2_ref, s3_ref, ret_ref, wout_ref, expand_ref,
                    gffn_ref, wr_ref, br_ref, tri_ref,
                    x1_ref, h2t_ref, ri_ref, gates_ref, cnt_ref, base_scr, nat_o, nat_s, grp_o):
    i = pl.program_id(0)
    tm = x_ref.shape[0]

    @pl.when(i == 0)
    def _():
        base_scr[...] = jnp.zeros_like(base_scr)

    n_col = WIDTH // LANES
    for bb in range(REGROUP):
        rows4 = pl.ds(bb, tm // REGROUP, stride=REGROUP)
        for c in range(n_col):
            nat_o[0, c, rows4, :] = o2_ref[0, bb, :, c * LANES:(c + 1) * LANES].astype(F32)
        nat_s[0, rows4, :] = s2_ref[0, bb]
        for aa in range(REGROUP):
            rows16 = pl.ds(aa, tm // REGROUP ** 2, stride=REGROUP)
            for c in range(n_col):
                grp_o[c, bb, rows16, :] = o3_ref[0, bb * REGROUP + aa, :, c * LANES:(c + 1) * LANES].astype(F32)
            grp_o[n_col, bb, rows16, :] = s3_ref[0, bb * REGROUP + aa]
        for c in range(n_col):
            nat_o[1, c, rows4, :] = grp_o[c, bb]
        nat_s[1, rows4, :] = grp_o[n_col, bb]
    hm = tm // OUT_SPLIT
    lane = lax.broadcasted_iota(jnp.int32, (hm, LANES), 1).astype(F32)
    base = base_scr[...]
    for r0 in range(0, tm, hm):
        rows = slice(r0, r0 + hm)
        o_nat = [[o1_ref[rows, c * LANES:(c + 1) * LANES].astype(F32) for c in range(n_col)]]
        sts = [s1_ref[rows, :]]
        for di in range(len(RESIDUE_DILATIONS)):
            o_nat.append([nat_o[di, c, rows, :] for c in range(n_col)])
            sts.append(nat_s[di, rows, :])

        mx = jnp.maximum(jnp.maximum(sts[0], sts[1]), sts[2])
        es = [jnp.exp(st - mx) for st in sts]
        den = es[0] + es[1] + es[2]
        attn = jnp.concatenate(o_nat[-1], axis=1)
        for e, o_cols in zip(es[:-1], o_nat[:-1]):
            hi, lo = _split_bf16(e / den)
            w_full = jnp.dot(jnp.concatenate([hi, lo], axis=1), expand_ref[...], preferred_element_type=F32)
            attn = attn + w_full * (jnp.concatenate(o_cols, axis=1) - jnp.concatenate(o_nat[-1], axis=1))
        mixed = jnp.concatenate([attn.astype(BF16), ret_ref[rows, :]], axis=1)
        x1 = x_ref[rows, :] + jnp.dot(mixed, wout_ref[...], preferred_element_type=F32)
        x1_ref[rows, :] = x1

        h2 = x1 * lax.rsqrt(jnp.mean(x1 * x1, axis=-1, keepdims=True) + NORM_EPS) * gffn_ref[...]
        _store_pieces(h2t_ref, rows, h2)

        hi, lo = _split_bf16(h2)
        prod = jnp.dot(jnp.concatenate([hi, lo], axis=1), wr_ref[...], preferred_element_type=F32)
        logits = prod + pltpu.roll(prod, LANES - N_EXPERTS, 1) + br_ref[...]
        work = logits
        vals, idxs = [], []
        onehot = jnp.zeros((hm, LANES), F32)
        for _k in range(TOP_K):
            mk = jnp.max(work, axis=-1, keepdims=True)
            ik = jnp.min(jnp.where(work == mk, lane, float(LANES)), axis=-1, keepdims=True)
            sel = lane == ik
            onehot = jnp.where(sel, 1.0, onehot)
            work = jnp.where(sel, -jnp.inf, work)
            vals.append(mk)
            idxs.append(ik)
        ex = [jnp.exp(vk - vals[0]) for vk in vals]
        tot = ex[0] + ex[1] + ex[2] + ex[3]
        before = jnp.dot(tri_ref[...], onehot.astype(BF16), preferred_element_type=F32) + base
        ri = jnp.zeros((hm, LANES), F32)
        gt = jnp.zeros((hm, LANES), F32)
        for kk in range(TOP_K):
            rank = jnp.sum(jnp.where(lane == idxs[kk], before, 0.0), axis=-1, keepdims=True)
            ri = jnp.where(lane == float(kk), idxs[kk], ri)
            ri = jnp.where(lane == float(TOP_K + kk), rank, ri)
            gt = jnp.where(lane == float(kk), ex[kk] / tot, gt)
        ri_ref[:, rows] = jnp.transpose(ri)[:2 * TOP_K].astype(jnp.int32)
        gates_ref[rows, :] = gt
        base = base + jnp.sum(onehot, axis=0, keepdims=True)
    base_scr[...] = base
    cnt_ref[...] = base


def _outproj_router(x2d, outs, stats, ret, wout_bf16, norm_ffn_g, w_router, b_router, seq):
    t = x2d.shape[0]
    tm = TM_OUT
    per_seq = seq // tm
    row = lambda i: (i, 0)

    def res_spec(d, width):
        return pl.BlockSpec((1, d, tm // d, width), lambda i: (i // per_seq, 0, i % per_seq, 0))

    const = lambda i: (0, 0)
    head_of_col = jnp.arange(WIDTH) // HEAD_DIM
    expand = (jnp.arange(LANES)[:, None] == head_of_col[None, :]).astype(BF16)
    expand2 = jnp.concatenate([expand, expand], axis=0)
    wr_hi, wr_lo = _split_bf16(w_router)
    zeros = jnp.zeros((D_MODEL, LANES), BF16)
    wr2 = jnp.concatenate([zeros.at[:, :N_EXPERTS].set(wr_hi).at[:, N_EXPERTS:2 * N_EXPERTS].set(wr_lo),
                           zeros.at[:, :N_EXPERTS].set(wr_hi)], axis=0)
    br = jnp.full((1, LANES), NEG_INF, F32).at[0, :N_EXPERTS].set(b_router)
    hm = tm // OUT_SPLIT
    tri = (jnp.arange(hm)[:, None] > jnp.arange(hm)[None, :]).astype(BF16)
    o_spec = pl.BlockSpec((tm, WIDTH), row)
    s_spec = pl.BlockSpec((tm, LANES), row)
    return pl.pallas_call(
        _outproj_kernel,
        grid=(t // tm,),
        in_specs=[pl.BlockSpec((tm, D_MODEL), row),
                  o_spec, *[res_spec(d, WIDTH) for d in RESIDUE_DILATIONS],
                  s_spec, *[res_spec(d, LANES) for d in RESIDUE_DILATIONS], o_spec,
                  pl.BlockSpec((D_MODEL, D_MODEL), const), pl.BlockSpec((2 * LANES, WIDTH), const),
                  pl.BlockSpec((1, D_MODEL), const), pl.BlockSpec((2 * D_MODEL, LANES), const),
                  pl.BlockSpec((1, LANES), const), pl.BlockSpec((hm, hm), const)],
        out_specs=[pl.BlockSpec((tm, D_MODEL), row), pl.BlockSpec((N_PIECES, tm, LANES), lambda i: (0, i, 0)),
                   pl.BlockSpec((2 * TOP_K, tm), lambda i: (0, i)), s_spec, pl.BlockSpec((1, LANES), const)],
        out_shape=[jax.ShapeDtypeStruct((t, D_MODEL), F32), jax.ShapeDtypeStruct((N_PIECES, t, LANES), jnp.uint32),
                   jax.ShapeDtypeStruct((2 * TOP_K, t), jnp.int32), jax.ShapeDtypeStruct((t, LANES), F32),
                   jax.ShapeDtypeStruct((1, LANES), F32)],
        scratch_shapes=[pltpu.VMEM((1, LANES), F32),
                        pltpu.VMEM((len(RESIDUE_DILATIONS), WIDTH // LANES, tm, LANES), F32),
                        pltpu.VMEM((len(RESIDUE_DILATIONS), tm, LANES), F32),
                        pltpu.VMEM((WIDTH // LANES + 1, REGROUP, tm // REGROUP, LANES), F32)],
        compiler_params=_cparams("arbitrary"),
        name="outproj_router",
    )(x2d, *outs, *stats, ret, wout_bf16, expand2, norm_ffn_g.reshape(1, D_MODEL), wr2, br, tri)


SC_CORES = 2
SC_SUBCORES = 16
SC_WINDOW = 128


def _sc_mesh():
    return plsc.VectorSubcoreMesh(core_axis_name="c", subcore_axis_name="s")


def _sc_scatter_rows(src, idx, n_out_rows):
    n_rows = src.shape[0]
    workers = SC_CORES * SC_SUBCORES
    per_worker = n_rows // workers
    n_win = per_worker // SC_WINDOW
    assert per_worker * workers == n_rows and n_win * SC_WINDOW == per_worker and idx.shape == (TOP_K * n_rows,)

    @functools.partial(
        pl.kernel, mesh=_sc_mesh(),
        out_type=jax.ShapeDtypeStruct((n_out_rows, LANES), src.dtype),
        scratch_types=[pltpu.VMEM((SC_WINDOW,), jnp.int32), pltpu.VMEM((SC_WINDOW, LANES), src.dtype)],
        name="sc_scatter_rows",
    )
    def scatter(src_hbm, idx_hbm, out_hbm, idx_v, rows_v):
        wid = lax.axis_index("s") * SC_CORES + lax.axis_index("c")
        base = wid * per_worker

        @pl.loop(0, n_win)
        def _(j):
            off = pl.multiple_of(base + j * SC_WINDOW, SC_WINDOW)
            pltpu.sync_copy(src_hbm.at[pl.ds(off, SC_WINDOW)], rows_v)
            for kk in range(TOP_K):
                pltpu.sync_copy(idx_hbm.at[pl.ds(pl.multiple_of(kk * n_rows + off, SC_WINDOW), SC_WINDOW)], idx_v)
                pltpu.sync_copy(rows_v, out_hbm.at[idx_v])

    return scatter(src, idx)


def _expert_kernel(blk_e_ref, nvalid_ref, rows_ref, xs_ref, wgu_ref, bgu_ref, wd_ref, bd_ref, ys_ref, wgu_bf, wd_bf):
    i = pl.program_id(0)

    @pl.when((i == 0) | (blk_e_ref[i] != blk_e_ref[jnp.maximum(i - 1, 0)]))
    def _():
        wgu_bf[...] = wgu_ref[0].astype(BF16)
        wd_bf[...] = wd_ref[0].astype(BF16)

    def mlp(m):
        live = lax.broadcasted_iota(jnp.int32, (m, LANES), 0) < rows_ref[i]
        x = jnp.concatenate([jnp.where(live, p, 0.0).astype(BF16) for p in _load_pieces(xs_ref, slice(0, m))], axis=1)
        gu = jnp.dot(x, wgu_bf[...], preferred_element_type=F32) + bgu_ref[0]
        gate = jnp.minimum(gu[:, :EXPERT_FF], SWIGLU_LIMIT)
        up = jnp.clip(gu[:, EXPERT_FF:], -SWIGLU_LIMIT, SWIGLU_LIMIT)
        act = gate * (1.0 / (1.0 + jnp.exp(-SWIGLU_ALPHA * gate))) * (up + 1.0)
        y = jnp.dot(act.astype(BF16), wd_bf[...], preferred_element_type=F32) + bd_ref[0]
        _store_pieces(ys_ref, slice(0, m), y)

    n_sub = lax.shift_right_logical(rows_ref[i] + (EXPERT_SUB - 1), EXPERT_SUB.bit_length() - 1)
    for v in range(1, BM // EXPERT_SUB + 1):
        pl.when((i < nvalid_ref[0]) & (n_sub == v))(functools.partial(mlp, v * EXPERT_SUB))


def _experts(xs, blk_e, nvalid, live_rows, wgu_bf16, bgu, wd_bf16, bd):
    cap = xs.shape[1]
    nblk = cap // BM

    def blk(i, be, nv, lr):
        return (0, jnp.minimum(i, nv[0] - 1), 0)

    def by_expert(i, be, nv, lr):
        return (be[i], 0, 0)

    return pl.pallas_call(
        _expert_kernel,
        grid_spec=pltpu.PrefetchScalarGridSpec(
            num_scalar_prefetch=3,
            grid=(nblk,),
            in_specs=[pl.BlockSpec((N_PIECES, BM, LANES), blk),
                      pl.BlockSpec((1, D_MODEL, 2 * EXPERT_FF), by_expert),
                      pl.BlockSpec((1, 1, 2 * EXPERT_FF), by_expert),
                      pl.BlockSpec((1, EXPERT_FF, D_MODEL), by_expert),
                      pl.BlockSpec((1, 1, D_MODEL), by_expert)],
            out_specs=pl.BlockSpec((N_PIECES, BM, LANES), blk),
            scratch_shapes=[pltpu.VMEM((D_MODEL, 2 * EXPERT_FF), BF16), pltpu.VMEM((EXPERT_FF, D_MODEL), BF16)],
        ),
        out_shape=jax.ShapeDtypeStruct((N_PIECES, cap, LANES), jnp.uint32),
        compiler_params=_cparams("arbitrary"),
        name="moe_experts",
    )(blk_e, nvalid, live_rows, xs, wgu_bf16, bgu.reshape(N_EXPERTS, 1, 2 * EXPERT_FF), wd_bf16,
      bd.reshape(N_EXPERTS, 1, D_MODEL))


def _sc_gather_rows(table, idx):
    n_rows = idx.shape[0]
    workers = SC_CORES * SC_SUBCORES
    per_worker = n_rows // workers
    n_win = per_worker // SC_WINDOW
    assert per_worker * workers == n_rows and n_win * SC_WINDOW == per_worker

    @functools.partial(
        pl.kernel, mesh=_sc_mesh(),
        out_type=jax.ShapeDtypeStruct((n_rows, LANES), table.dtype),
        scratch_types=[pltpu.VMEM((SC_WINDOW,), jnp.int32), pltpu.VMEM((SC_WINDOW, LANES), table.dtype),
                       pltpu.SemaphoreType.DMA],
        name="sc_gather_rows",
    )
    def gather(table_hbm, idx_hbm, out_hbm, idx_v, rows_v, sem):
        wid = lax.axis_index("s") * SC_CORES + lax.axis_index("c")
        base = wid * per_worker

        @pl.loop(0, n_win)
        def _(j):
            off = pl.multiple_of(base + j * SC_WINDOW, SC_WINDOW)
            pltpu.sync_copy(idx_hbm.at[pl.ds(off, SC_WINDOW)], idx_v)
            pltpu.async_copy(table_hbm.at[idx_v], rows_v, sem).wait()
            pltpu.sync_copy(rows_v, out_hbm.at[pl.ds(off, SC_WINDOW)])

    return gather(table, idx)


def _combine_kernel(x1_ref, gates_ref, gfin_ref, *refs, tc):
    piece_refs, o_ref = refs[:TOP_K * N_PIECES], refs[TOP_K * N_PIECES]
    g = gates_ref[...]
    gk = [jnp.broadcast_to(g[:, kk:kk + 1], (tc, LANES)) for kk in range(TOP_K)]
    zs = {}
    ssq = jnp.zeros((tc, 1), F32)
    for s in range(N_PIECES):
        c_lo, c_hi = s * LANES, HALF_D + s * LANES
        z_lo = x1_ref[:, c_lo:c_lo + LANES]
        z_hi = x1_ref[:, c_hi:c_hi + LANES]
        for kk in range(TOP_K):
            a, b = _unpack_bf16_pair(piece_refs[kk * N_PIECES + s][...])
            z_lo = z_lo + gk[kk] * a
            z_hi = z_hi + gk[kk] * b
        zs[c_lo], zs[c_hi] = z_lo, z_hi
        ssq = ssq + jnp.sum(z_lo * z_lo + z_hi * z_hi, axis=-1, keepdims=True)
    inv = lax.rsqrt(ssq * (1.0 / D_MODEL) + NORM_EPS)
    for c0, z in zs.items():
        o_ref[:, c0:c0 + LANES] = z * inv * gfin_ref[:, c0:c0 + LANES]


def _combine(x1, gates, gathered, norm_final_g):
    t = x1.shape[0]
    tc = TC
    row = lambda i: (i, 0)
    slot_specs = [pl.BlockSpec((None, tc, LANES), functools.partial(lambda i, j: (j, i, 0), j=j))
                  for j in range(TOP_K * N_PIECES)]
    return pl.pallas_call(
        functools.partial(_combine_kernel, tc=tc),
        grid=(t // tc,),
        in_specs=[pl.BlockSpec((tc, D_MODEL), row), pl.BlockSpec((tc, LANES), row),
                  pl.BlockSpec((1, D_MODEL), lambda i: (0, 0)), *slot_specs],
        out_specs=pl.BlockSpec((tc, D_MODEL), row),
        out_shape=jax.ShapeDtypeStruct((t, D_MODEL), F32),
        compiler_params=_cparams("arbitrary"),
        name="moe_combine",
    )(x1, gates, norm_final_g.reshape(1, D_MODEL), *([gathered] * (TOP_K * N_PIECES)))


def _moe(x1, h2t, ri, gates, counts_f, wgu_bf16, bgu, wd_bf16, bd, norm_final_g):
    t = x1.shape[0]
    a = t * TOP_K
    cap = a + N_EXPERTS * BM
    nblk = cap // BM
    counts = counts_f[0, :N_EXPERTS].astype(jnp.int32)
    padded = ((counts + BM - 1) // BM) * BM
    pend = jnp.cumsum(padded)
    pstart = pend - padded
    nvalid = (pend[-1] // BM).reshape(1)
    first_row = jnp.minimum(jnp.arange(nblk, dtype=jnp.int32) * BM, pend[-1] - 1)
    blk_e = jnp.sum(pend[None, :] <= first_row[:, None], axis=1).astype(jnp.int32)
    idx, rank = ri[:TOP_K], ri[TOP_K:2 * TOP_K]
    onehot = idx[None, :, :] == jnp.arange(N_EXPERTS, dtype=jnp.int32)[:, None, None]
    dest = rank + jnp.sum(jnp.where(onehot, pstart[:, None, None], 0), axis=0)
    seg_end = (pstart + counts)[blk_e]
    live_rows = jnp.clip(seg_end - jnp.arange(nblk, dtype=jnp.int32) * BM, 0, BM).astype(jnp.int32)
    piece = dest[:, None, :] + (jnp.arange(N_PIECES, dtype=jnp.int32) * cap)[None, :, None]
    piece = piece.reshape(TOP_K * N_PIECES * t)
    xs = _sc_scatter_rows(h2t.reshape(N_PIECES * t, LANES), piece, N_PIECES * cap)
    ys = _experts(xs.reshape(N_PIECES, cap, LANES), blk_e, nvalid, live_rows, wgu_bf16, bgu, wd_bf16, bd)
    gathered = _sc_gather_rows(ys.reshape(N_PIECES * cap, LANES), piece)
    return _combine(x1, gates, gathered.reshape(TOP_K * N_PIECES, t, LANES), norm_final_g)


def _encoder(x, p):
    b, s, d_model = x.shape
    assert d_model == D_MODEL and s % (max(DILATIONS) * TQ) == 0 and (b * s) % max(TM_OUT, TC) == 0, x.shape
    x2d = x.reshape(b * s, D_MODEL)
    cos_t, sin_t = _rope_tables(s)
    (qa, ka, va, qr, kr, vr, gr), by_residue = _inproj(x2d, p["norm_mix_g"], p["w_in"], cos_t, sin_t, s)
    outs, stats = [], []
    for d in DILATIONS:
        if d == 1:
            o, st = _banded_attention(*[z.reshape(b, s, WIDTH) for z in (qa, ka, va)])
            outs.append(o.reshape(b * s, WIDTH))
            stats.append(st.reshape(b * s, LANES))
        else:
            o, st = _banded_attention(*[z.reshape(b * d, s // d, WIDTH) for z in by_residue[d]])
            outs.append(o.reshape(b, d, s // d, WIDTH))
            stats.append(st.reshape(b, d, s // d, LANES))
    ret = _retention(qr, kr, vr, gr, p["ret_norm_g"], b, s)
    x1, h2t, ri, gates, counts = _outproj_router(x2d, outs, stats, ret, p["w_out"], p["norm_ffn_g"],
                                                 p["w_router"], p["b_router"], s)
    y = _moe(x1, h2t, ri, gates, counts, p["w_gate_up"], p["b_gate_up"], p["w_down"], p["b_down"], p["norm_final_g"])
    return y.reshape(b, s, D_MODEL)


def kernel(x_prompt, x_sample, norm_mix_g, w_in, ret_norm_g, w_out, norm_ffn_g, w_router, b_router, w_gate_up, b_gate_up, w_down, b_down, norm_final_g):
    assert norm_mix_g.shape[0] == 1, "single layer"
    p = dict(norm_mix_g=norm_mix_g[0], w_in=w_in[0].astype(BF16), ret_norm_g=ret_norm_g[0],
             w_out=w_out[0].astype(BF16), norm_ffn_g=norm_ffn_g[0], w_router=w_router[0], b_router=b_router[0],
             w_gate_up=w_gate_up[0], b_gate_up=b_gate_up[0], w_down=w_down[0],
             b_down=b_down[0], norm_final_g=norm_final_g)
    y_sample = _encoder(x_sample, p)
    y_prompt = _encoder(x_prompt, p)
    return (y_prompt, y_sample)
```

```python
import functools

import jax
import jax.numpy as jnp
from jax import lax
from jax.experimental import pallas as pl
from jax.experimental.pallas import tpu as pltpu
from jax.experimental.pallas import tpu_sc as plsc

D_MODEL = 1024
HEAD_DIM = 64
N_HEADS = 8
WIDTH = N_HEADS * HEAD_DIM
N_SLABS = 7
DILATIONS = (1, 4, 16)
HALF_SPAN = 64
ROPE_THETA = 10000.0
RET_DECAY_BASE_FWD = 5.0
RET_DECAY_BASE_BWD = 5.5
N_EXPERTS = 32
TOP_K = 4
EXPERT_FF = D_MODEL
SWIGLU_LIMIT = 7.0
SWIGLU_ALPHA = 1.702
NORM_EPS = 1e-6
NEG_INF = -1e30

LANES = 128
VMEM_LIMIT_BYTES = 56 * 1024 * 1024

TM_INPROJ = 1024
TQ = 128
TL_ATTN = 2048
HEAD_GROUP = 4
RET_CHUNK = 256
TM_OUT = 1024
OUT_SPLIT = 4
RET_UNROLL = 8
BM = 1024
EXPERT_SUB = 256
TC = 1024

F32 = jnp.float32
BF16 = jnp.bfloat16


def _cparams(*sem):
    return pltpu.CompilerParams(dimension_semantics=sem, vmem_limit_bytes=VMEM_LIMIT_BYTES)


_ROTATE = (True, True, False, True, True, False, False)
ATTN_SCALE = HEAD_DIM ** -0.5
LOG2E = 1.4426950408889634
_SCALE = (ATTN_SCALE * LOG2E, 1.0, 1.0, 1.0, ATTN_SCALE, 1.0, 1.0)


N_ATTN_SLABS = 3
RESIDUE_DILATIONS = tuple(d for d in DILATIONS if d > 1)
REGROUP = 4
STAGE_SLOTS = 2
assert RESIDUE_DILATIONS == (REGROUP, REGROUP ** 2)


def _inproj_kernel(x_ref, g_ref, w_ref, cos_ref, sin_ref, *refs):
    out_refs = refs[:N_SLABS]
    res_refs = refs[N_SLABS:N_SLABS + N_ATTN_SLABS * len(RESIDUE_DILATIONS)]
    stage, stage4 = refs[-2:]
    x = x_ref[...]
    tm = x.shape[0]
    h = (x * lax.rsqrt(jnp.mean(x * x, axis=-1, keepdims=True) + NORM_EPS) * g_ref[...]).astype(BF16)
    cos = cos_ref[...]
    sin = sin_ref[...]
    tables = {sc: (cos, sin) if sc == 1.0 else (cos * sc, sin * sc) for sc in sorted(set(_SCALE))}
    lane = lax.broadcasted_iota(jnp.int32, cos.shape, 1)
    first_half = (lane & (HEAD_DIM - 1)) < HEAD_DIM // 2
    for j, o_ref in enumerate(out_refs):
        p = jnp.dot(h, w_ref[:, j * WIDTH:(j + 1) * WIDTH], preferred_element_type=F32)
        for c in range(WIDTH // LANES):
            cols = slice(c * LANES, (c + 1) * LANES)
            r = p[:, cols]
            if _ROTATE[j]:
                partner = jnp.where(first_half, pltpu.roll(r, LANES - HEAD_DIM // 2, 1), pltpu.roll(r, HEAD_DIM // 2, 1))
                cs, sn = tables[_SCALE[j]]
                r = r * cs + partner * sn
            o_ref[:, cols] = r.astype(BF16)
            if j < N_ATTN_SLABS:
                js = j % STAGE_SLOTS
                stage[js, c] = r
                dst4 = res_refs[j]
                dst16 = res_refs[N_ATTN_SLABS + j]
                for bb in range(REGROUP):
                    grp = stage[js, c, pl.ds(bb, tm // REGROUP, stride=REGROUP), :]
                    dst4[0, bb, :, cols] = grp.astype(BF16)
                    stage4[js, c, bb] = grp
                    for aa in range(REGROUP):
                        sub = stage4[js, c, bb, pl.ds(aa, tm // REGROUP ** 2, stride=REGROUP), :]
                        dst16[0, bb * REGROUP + aa, :, cols] = sub.astype(BF16)


def _inproj(x2d, g, w_bf16, cos_t, sin_t, seq):
    t = x2d.shape[0]
    tm = min(TM_INPROJ, seq)
    pos_blocks = seq // tm
    b = t // seq
    out = jax.ShapeDtypeStruct((t, WIDTH), BF16)
    row = lambda i: (i, 0)
    out_specs = [pl.BlockSpec((tm, WIDTH), row)] * N_SLABS
    out_shape = [out] * N_SLABS
    for d in RESIDUE_DILATIONS:
        out_specs += [pl.BlockSpec((1, d, tm // d, WIDTH), lambda i: (i // pos_blocks, 0, i % pos_blocks, 0))] * N_ATTN_SLABS
        out_shape += [jax.ShapeDtypeStruct((b, d, seq // d, WIDTH), BF16)] * N_ATTN_SLABS
    res = pl.pallas_call(
        _inproj_kernel,
        grid=(t // tm,),
        in_specs=[
            pl.BlockSpec((tm, D_MODEL), row),
            pl.BlockSpec((1, D_MODEL), lambda i: (0, 0)),
            pl.BlockSpec((D_MODEL, N_SLABS * WIDTH), lambda i: (0, 0)),
            pl.BlockSpec((tm, LANES), lambda i: (i % pos_blocks, 0)),
            pl.BlockSpec((tm, LANES), lambda i: (i % pos_blocks, 0)),
        ],
        out_specs=out_specs,
        out_shape=out_shape,
        scratch_shapes=[pltpu.VMEM((STAGE_SLOTS, WIDTH // LANES, tm, LANES), F32),
                        pltpu.VMEM((STAGE_SLOTS, WIDTH // LANES, REGROUP, tm // REGROUP, LANES), F32)],
        compiler_params=_cparams("arbitrary"),
        name="inproj",
    )(x2d, g.reshape(1, D_MODEL), w_bf16, cos_t, sin_t)
    natural = res[:N_SLABS]
    by_residue = {d: res[N_SLABS + di * N_ATTN_SLABS:N_SLABS + (di + 1) * N_ATTN_SLABS]
                  for di, d in enumerate(RESIDUE_DILATIONS)}
    return natural, by_residue


def _rope_tables(seq):
    half = HEAD_DIM // 2
    inv_freq = ROPE_THETA ** (-jnp.arange(0, HEAD_DIM, 2, dtype=F32) / HEAD_DIM)
    ang = jnp.arange(seq, dtype=F32)[:, None] * inv_freq[None, :]
    cos, sin = jnp.cos(ang), jnp.sin(ang)
    reps = LANES // HEAD_DIM
    cos_t = jnp.tile(jnp.concatenate([cos, cos], axis=1), (1, reps))
    sin_t = jnp.tile(jnp.concatenate([-sin, sin], axis=1), (1, reps))
    assert cos_t.shape == (seq, LANES) and half * 2 == HEAD_DIM
    return cos_t, sin_t


def _attn_kernel(q_ref, kp_ref, kc_ref, kn_ref, vp_ref, vc_ref, vn_ref, o_ref, st_ref, kbuf, vbuf, *, gb, tl, sub_len):
    i = pl.program_id(1)
    hs = HALF_SPAN
    for gi in range(gb):
        kbuf[gi, 0:hs] = kp_ref[gi]
        kbuf[gi, hs:hs + tl] = kc_ref[gi]
        kbuf[gi, hs + tl:hs + tl + hs] = kn_ref[gi]
        vbuf[gi, 0:hs] = vp_ref[gi]
        vbuf[gi, hs:hs + tl] = vc_ref[gi]
        vbuf[gi, hs + tl:hs + tl + hs] = vn_ref[gi]
    tk = TQ + 2 * hs
    gw = HEAD_GROUP * HEAD_DIM
    qi = lax.broadcasted_iota(jnp.int32, (TQ, tk), 0)
    kj = lax.broadcasted_iota(jnp.int32, (TQ, tk), 1)
    band_bias = jnp.where((kj >= qi) & (kj - qi <= 2 * hs), 0.0, NEG_INF).astype(F32)
    key_col = lax.broadcasted_iota(jnp.int32, (1, tk), 1)
    lane = lax.broadcasted_iota(jnp.int32, (TQ, LANES), 1)
    head_of_lane = lax.broadcasted_iota(jnp.int32, (TQ, gw), 1) // HEAD_DIM
    n_sub = tl // TQ
    for gi, sub in [(gi, sub) for gi in range(gb) for sub in range(n_sub)]:
        a = sub * TQ
        bias = band_bias
        if sub == 0 or sub == n_sub - 1:
            kpos = key_col + (i * tl + a - hs)
            bias = bias + jnp.where((kpos >= 0) & (kpos < sub_len), 0.0, NEG_INF).astype(F32)
        bias = jnp.concatenate([bias] * HEAD_GROUP, axis=0)
        st = jnp.zeros((TQ, LANES), F32)
        for g in range(N_HEADS // HEAD_GROUP):
            cols = slice(g * gw, (g + 1) * gw)
            q4 = q_ref[gi, a:a + TQ, cols]
            k4 = kbuf[gi, a:a + tk, cols]
            v4 = vbuf[gi, a:a + tk, cols]
            lhs = jnp.concatenate([jnp.where(head_of_lane == h, q4, jnp.zeros_like(q4)) for h in range(HEAD_GROUP)], axis=0)
            s = lax.dot_general(lhs, k4, (((1,), (1,)), ((), ())), preferred_element_type=F32) + bias
            m = jnp.max(s, axis=-1, keepdims=True)
            p = jnp.exp2(s - m)
            l = jnp.sum(p, axis=-1, keepdims=True)
            o_all = jnp.dot(p.astype(BF16), v4, preferred_element_type=F32)
            inv = 1.0 / l
            lse = m + jnp.log2(l)
            halves = []
            for hp in range(HEAD_GROUP // 2):
                blk = slice(hp * LANES, (hp + 1) * LANES)
                ra, rb = slice(2 * hp * TQ, (2 * hp + 1) * TQ), slice((2 * hp + 1) * TQ, (2 * hp + 2) * TQ)
                halves.append(jnp.where(lane < HEAD_DIM, o_all[ra, blk] * inv[ra], o_all[rb, blk] * inv[rb]))
            for h in range(HEAD_GROUP):
                st = jnp.where(lane == g * HEAD_GROUP + h, lse[h * TQ:(h + 1) * TQ], st)
            o_ref[gi, a:a + TQ, cols] = jnp.concatenate(halves, axis=1).astype(BF16)
        st_ref[gi, a:a + TQ, :] = st


def _banded_attention(q, k, v):
    g, sub_len, _ = q.shape
    tl = min(TL_ATTN, sub_len)
    gb = min(TL_ATTN // tl, g)
    hs = HALF_SPAN
    per = tl // hs
    last = sub_len // hs - 1
    cur = pl.BlockSpec((gb, tl, WIDTH), lambda b, i: (b, i, 0))
    prev = pl.BlockSpec((gb, hs, WIDTH), lambda b, i: (b, jnp.maximum(i * per - 1, 0), 0))
    nxt = pl.BlockSpec((gb, hs, WIDTH), lambda b, i: (b, jnp.minimum((i + 1) * per, last), 0))
    return pl.pallas_call(
        functools.partial(_attn_kernel, gb=gb, tl=tl, sub_len=sub_len),
        grid=(g // gb, sub_len // tl),
        in_specs=[cur, prev, cur, nxt, prev, cur, nxt],
        out_specs=[cur, pl.BlockSpec((gb, tl, LANES), lambda b, i: (b, i, 0))],
        out_shape=[jax.ShapeDtypeStruct((g, sub_len, WIDTH), BF16),
                   jax.ShapeDtypeStruct((g, sub_len, LANES), F32)],
        scratch_shapes=[pltpu.VMEM((gb, tl + 2 * hs, WIDTH), BF16), pltpu.VMEM((gb, tl + 2 * hs, WIDTH), BF16)],
        compiler_params=_cparams("arbitrary", "arbitrary"),
        name="banded_attention",
    )(q, k, k, k, v, v, v)


def _ret_kernel(q_ref, k_ref, v_ref, gate_ref, dmat_ref, wq_ref, wk_ref, dec_ref, gn_ref, o_ref, sf_scr, sb_scr, *, c, n_chunks):
    pair = 2 * HEAD_DIM
    lane = lax.broadcasted_iota(jnp.int32, (c, pair), 1)
    head0 = lane < HEAD_DIM
    blk_r = lax.broadcasted_iota(jnp.int32, (pair, pair), 0) // HEAD_DIM
    blk_c = lax.broadcasted_iota(jnp.int32, (pair, pair), 1) // HEAD_DIM
    same_head = blk_r == blk_c
    dec_f = dec_ref[0:1, :]
    dec_b = dec_ref[1:2, :]
    tn = (((0,), (0,)), ((), ()))
    nt = (((1,), (1,)), ((), ()))

    def rows_of(n):
        return pl.ds(pl.multiple_of(n * c, c), c)

    def kv_body(n, carry):
        rows = rows_of(n)
        kf32 = k_ref[0, rows, :].astype(F32)
        kw = jnp.concatenate([(kf32 * wk_ref[:, :pair]).astype(BF16), (kf32 * wk_ref[:, pair:]).astype(BF16)], axis=1)
        kv = lax.dot_general(kw, v_ref[0, rows, :], tn, preferred_element_type=F32)
        sf_scr[n] = jnp.where(same_head, kv[:pair], 0.0)
        sb_scr[n] = jnp.where(same_head, kv[pair:], 0.0)
        return carry

    lax.fori_loop(0, n_chunks, kv_body, 0, unroll=RET_UNROLL)

    def scan_body(t, carry):
        sf, sb = carry
        nb = n_chunks - 1 - t
        kv_f = sf_scr[t]
        kv_b = sb_scr[nb]
        sf_scr[t] = sf
        sb_scr[nb] = sb
        return sf * dec_f + kv_f, sb * dec_b + kv_b

    zero = jnp.zeros((pair, pair), F32)
    lax.fori_loop(0, n_chunks, scan_body, (zero, zero))

    def out_body(n, carry):
        rows = rows_of(n)
        q = q_ref[0, rows, :]
        k = k_ref[0, rows, :]
        v = v_ref[0, rows, :]
        qf32 = q.astype(F32)
        intra = []
        for hh in range(2):
            mask = head0 if hh == 0 else jnp.logical_not(head0)
            kh = jnp.where(mask, k, jnp.zeros_like(k))
            s = lax.dot_general(q, kh, nt, preferred_element_type=F32)
            a = (s * dmat_ref[hh]).astype(BF16)
            intra.append(jnp.dot(a, v, preferred_element_type=F32))
        qw = jnp.concatenate([(qf32 * wq_ref[:, :pair]).astype(BF16), (qf32 * wq_ref[:, pair:]).astype(BF16)], axis=1)
        states = jnp.concatenate([sf_scr[n].astype(BF16), sb_scr[n].astype(BF16)], axis=0)
        tot = jnp.where(head0, intra[0], intra[1]) + jnp.dot(qw, states, preferred_element_type=F32)
        inv = 1.0 / HEAD_DIM
        s0 = jnp.sum(jnp.where(head0, tot, 0.0), axis=-1, keepdims=True)
        s1 = jnp.sum(jnp.where(head0, 0.0, tot), axis=-1, keepdims=True)
        xc = tot - jnp.where(head0, s0, s1) * inv
        sq = xc * xc
        v0 = jnp.sum(jnp.where(head0, sq, 0.0), axis=-1, keepdims=True)
        v1 = jnp.sum(jnp.where(head0, 0.0, sq), axis=-1, keepdims=True)
        y = xc * lax.rsqrt(jnp.where(head0, v0, v1) * inv + NORM_EPS)
        gt = gate_ref[0, rows, :].astype(F32)
        y = y * gn_ref[...] * (gt / (1.0 + jnp.exp(-gt)))
        o_ref[0, rows, :] = y.astype(BF16)
        return carry

    lax.fori_loop(0, n_chunks, out_body, 0, unroll=RET_UNROLL)


def _retention_tables(c):
    hidx = jnp.arange(N_HEADS, dtype=F32)
    lg_f = jnp.log1p(-jnp.exp2(-(RET_DECAY_BASE_FWD + hidx)))
    lg_b = jnp.log1p(-jnp.exp2(-(RET_DECAY_BASE_BWD + hidx)))
    pos = jnp.arange(c, dtype=F32)
    diff = pos[:, None] - pos[None, :]
    dm_f = jnp.exp(jnp.maximum(diff, 0.0)[None] * lg_f[:, None, None])
    dm_b = jnp.exp(jnp.maximum(-diff, 0.0)[None] * lg_b[:, None, None])
    dmat = jnp.where((diff >= 0)[None], dm_f, dm_b)

    def per_lane(tab):
        t = jnp.repeat(tab[:, :, None], HEAD_DIM, axis=2)
        return t.reshape(N_HEADS // 2, 2, c, HEAD_DIM).transpose(0, 2, 1, 3).reshape(N_HEADS // 2, c, 2 * HEAD_DIM)

    wq_f = per_lane(jnp.exp((pos + 1.0)[None, :] * lg_f[:, None]))
    wq_b = per_lane(jnp.exp((c - pos)[None, :] * lg_b[:, None]))
    wk_f = per_lane(jnp.exp((c - 1.0 - pos)[None, :] * lg_f[:, None]))
    wk_b = per_lane(jnp.exp(pos[None, :] * lg_b[:, None]))
    wq = jnp.concatenate([wq_f, wq_b], axis=2)
    wk = jnp.concatenate([wk_f, wk_b], axis=2)
    dec = jnp.stack([jnp.repeat(jnp.exp(c * lg_f), HEAD_DIM), jnp.repeat(jnp.exp(c * lg_b), HEAD_DIM)], axis=0)
    dec = dec.reshape(2, N_HEADS // 2, 2 * HEAD_DIM).transpose(1, 0, 2)
    return dmat, wq, wk, dec


def _retention(q, k, v, gate, ret_norm_g, b, s):
    c = min(RET_CHUNK, s)
    n_chunks = s // c
    pair = 2 * HEAD_DIM
    dmat, wq, wk, dec = _retention_tables(c)
    seq_blk = pl.BlockSpec((1, s, pair), lambda bi, hp: (bi, 0, hp))
    r3 = lambda z: z.reshape(b, s, WIDTH)
    return pl.pallas_call(
        functools.partial(_ret_kernel, c=c, n_chunks=n_chunks),
        grid=(b, N_HEADS // 2),
        in_specs=[seq_blk, seq_blk, seq_blk, seq_blk,
                  pl.BlockSpec((2, c, c), lambda bi, hp: (hp, 0, 0)),
                  pl.BlockSpec((None, c, 2 * pair), lambda bi, hp: (hp, 0, 0)),
                  pl.BlockSpec((None, c, 2 * pair), lambda bi, hp: (hp, 0, 0)),
                  pl.BlockSpec((None, 2, pair), lambda bi, hp: (hp, 0, 0)),
                  pl.BlockSpec((1, pair), lambda bi, hp: (0, hp))],
        out_specs=seq_blk,
        out_shape=jax.ShapeDtypeStruct((b, s, WIDTH), BF16),
        scratch_shapes=[pltpu.VMEM((n_chunks, pair, pair), F32), pltpu.VMEM((n_chunks, pair, pair), F32)],
        compiler_params=_cparams("arbitrary", "arbitrary"),
        name="retention",
    )(r3(q), r3(k), r3(v), r3(gate), dmat, wq, wk, dec, ret_norm_g.reshape(1, WIDTH)).reshape(b * s, WIDTH)


N_PIECES = 4
HALF_D = D_MODEL // 2
assert N_PIECES * LANES == HALF_D


def _pack_bf16_pair(a, b):
    ua = pltpu.bitcast(a.astype(BF16).astype(F32), jnp.uint32)
    ub = pltpu.bitcast(b.astype(BF16).astype(F32), jnp.uint32)
    return ua | (ub >> 16)


def _unpack_bf16_pair(u):
    return pltpu.bitcast(u & jnp.uint32(0xFFFF0000), F32), pltpu.bitcast(u << 16, F32)


def _store_pieces(ref, rows, val):
    for s in range(N_PIECES):
        lo = slice(s * LANES, (s + 1) * LANES)
        hi = slice(HALF_D + s * LANES, HALF_D + (s + 1) * LANES)
        ref[s, rows, :] = _pack_bf16_pair(val[:, lo], val[:, hi])


def _load_pieces(ref, rows):
    parts = [_unpack_bf16_pair(ref[s, rows, :]) for s in range(N_PIECES)]
    return [p[0] for p in parts] + [p[1] for p in parts]


def _split_bf16(x):
    hi = x.astype(BF16)
    return hi, (x - hi.astype(F32)).astype(BF16)


def _outproj_kernel(x_ref, o1_ref, o2_ref, o3_ref, s1_ref, s2_ref, s3_ref, ret_ref, wout_ref, expand_ref,
                    gffn_ref, wr_ref, br_ref, tri_ref,
                    x1_ref, h2t_ref, ri_ref, gates_ref, cnt_ref, base_scr, nat_o, nat_s, grp_o):
    i = pl.program_id(0)
    tm = x_ref.shape[0]

    @pl.when(i == 0)
    def _():
        base_scr[...] = jnp.zeros_like(base_scr)

    n_col = WIDTH // LANES
    for bb in range(REGROUP):
        rows4 = pl.ds(bb, tm // REGROUP, stride=REGROUP)
        for c in range(n_col):
            nat_o[0, c, rows4, :] = o2_ref[0, bb, :, c * LANES:(c + 1) * LANES].astype(F32)
        nat_s[0, rows4, :] = s2_ref[0, bb]
        for aa in range(REGROUP):
            rows16 = pl.ds(aa, tm // REGROUP ** 2, stride=REGROUP)
            for c in range(n_col):
                grp_o[c, bb, rows16, :] = o3_ref[0, bb * REGROUP + aa, :, c * LANES:(c + 1) * LANES].astype(F32)
            grp_o[n_col, bb, rows16, :] = s3_ref[0, bb * REGROUP + aa]
        for c in range(n_col):
            nat_o[1, c, rows4, :] = grp_o[c, bb]
        nat_s[1, rows4, :] = grp_o[n_col, bb]
    hm = tm // OUT_SPLIT
    lane = lax.broadcasted_iota(jnp.int32, (hm, LANES), 1).astype(F32)
    base = base_scr[...]
    for r0 in range(0, tm, hm):
        rows = slice(r0, r0 + hm)
        o_nat = [[o1_ref[rows, c * LANES:(c + 1) * LANES].astype(F32) for c in range(n_col)]]
        sts = [s1_ref[rows, :]]
        for di in range(len(RESIDUE_DILATIONS)):
            o_nat.append([nat_o[di, c, rows, :] for c in range(n_col)])
            sts.append(nat_s[di, rows, :])

        mx = jnp.maximum(jnp.maximum(sts[0], sts[1]), sts[2])
        es = [jnp.exp2(st - mx) for st in sts]
        den = es[0] + es[1] + es[2]
        attn = jnp.concatenate(o_nat[-1], axis=1)
        for e, o_cols in zip(es[:-1], o_nat[:-1]):
            hi, lo = _split_bf16(e / den)
            w_full = jnp.dot(jnp.concatenate([hi, lo], axis=1), expand_ref[...], preferred_element_type=F32)
            attn = attn + w_full * (jnp.concatenate(o_cols, axis=1) - jnp.concatenate(o_nat[-1], axis=1))
        mixed = jnp.concatenate([attn.astype(BF16), ret_ref[rows, :]], axis=1)
        x1 = x_ref[rows, :] + jnp.dot(mixed, wout_ref[...], preferred_element_type=F32)
        x1_ref[rows, :] = x1

        h2 = x1 * lax.rsqrt(jnp.mean(x1 * x1, axis=-1, keepdims=True) + NORM_EPS) * gffn_ref[...]
        _store_pieces(h2t_ref, rows, h2)

        hi, lo = _split_bf16(h2)
        prod = jnp.dot(jnp.concatenate([hi, lo], axis=1), wr_ref[...], preferred_element_type=F32)
        logits = prod + pltpu.roll(prod, LANES - N_EXPERTS, 1) + br_ref[...]
        work = logits
        vals, idxs = [], []
        onehot = jnp.zeros((hm, LANES), F32)
        for _k in range(TOP_K):
            mk = jnp.max(work, axis=-1, keepdims=True)
            ik = jnp.min(jnp.where(work == mk, lane, float(LANES)), axis=-1, keepdims=True)
            sel = lane == ik
            onehot = jnp.where(sel, 1.0, onehot)
            work = jnp.where(sel, -jnp.inf, work)
            vals.append(mk)
            idxs.append(ik)
        ex = [jnp.exp(vk - vals[0]) for vk in vals]
        tot = ex[0] + ex[1] + ex[2] + ex[3]
        before = jnp.dot(tri_ref[...], onehot.astype(BF16), preferred_element_type=F32) + base
        ri = jnp.zeros((hm, LANES), F32)
        gt = jnp.zeros((hm, LANES), F32)
        for kk in range(TOP_K):
            rank = jnp.sum(jnp.where(lane == idxs[kk], before, 0.0), axis=-1, keepdims=True)
            ri = jnp.where(lane == float(kk), idxs[kk], ri)
            ri = jnp.where(lane == float(TOP_K + kk), rank, ri)
            gt = jnp.where(lane == float(kk), ex[kk] / tot, gt)
        ri_ref[:, rows] = jnp.transpose(ri)[:2 * TOP_K].astype(jnp.int32)
        gates_ref[rows, :] = gt
        base = base + jnp.sum(onehot, axis=0, keepdims=True)
    base_scr[...] = base
    cnt_ref[...] = base


def _outproj_router(x2d, outs, stats, ret, wout_bf16, norm_ffn_g, w_router, b_router, seq):
    t = x2d.shape[0]
    tm = TM_OUT
    per_seq = seq // tm
    row = lambda i: (i, 0)

    def res_spec(d, width):
        return pl.BlockSpec((1, d, tm // d, width), lambda i: (i // per_seq, 0, i % per_seq, 0))

    const = lambda i: (0, 0)
    head_of_col = jnp.arange(WIDTH) // HEAD_DIM
    expand = (jnp.arange(LANES)[:, None] == head_of_col[None, :]).astype(BF16)
    expand2 = jnp.concatenate([expand, expand], axis=0)
    wr_hi, wr_lo = _split_bf16(w_router)
    zeros = jnp.zeros((D_MODEL, LANES), BF16)
    wr2 = jnp.concatenate([zeros.at[:, :N_EXPERTS].set(wr_hi).at[:, N_EXPERTS:2 * N_EXPERTS].set(wr_lo),
                           zeros.at[:, :N_EXPERTS].set(wr_hi)], axis=0)
    br = jnp.full((1, LANES), NEG_INF, F32).at[0, :N_EXPERTS].set(b_router)
    hm = tm // OUT_SPLIT
    tri = (jnp.arange(hm)[:, None] > jnp.arange(hm)[None, :]).astype(BF16)
    o_spec = pl.BlockSpec((tm, WIDTH), row)
    s_spec = pl.BlockSpec((tm, LANES), row)
    return pl.pallas_call(
        _outproj_kernel,
        grid=(t // tm,),
        in_specs=[pl.BlockSpec((tm, D_MODEL), row),
                  o_spec, *[res_spec(d, WIDTH) for d in RESIDUE_DILATIONS],
                  s_spec, *[res_spec(d, LANES) for d in RESIDUE_DILATIONS], o_spec,
                  pl.BlockSpec((D_MODEL, D_MODEL), const), pl.BlockSpec((2 * LANES, WIDTH), const),
                  pl.BlockSpec((1, D_MODEL), const), pl.BlockSpec((2 * D_MODEL, LANES), const),
                  pl.BlockSpec((1, LANES), const), pl.BlockSpec((hm, hm), const)],
        out_specs=[pl.BlockSpec((tm, D_MODEL), row), pl.BlockSpec((N_PIECES, tm, LANES), lambda i: (0, i, 0)),
                   pl.BlockSpec((2 * TOP_K, tm), lambda i: (0, i)), s_spec, pl.BlockSpec((1, LANES), const)],
        out_shape=[jax.ShapeDtypeStruct((t, D_MODEL), F32), jax.ShapeDtypeStruct((N_PIECES, t, LANES), jnp.uint32),
                   jax.ShapeDtypeStruct((2 * TOP_K, t), jnp.int32), jax.ShapeDtypeStruct((t, LANES), F32),
                   jax.ShapeDtypeStruct((1, LANES), F32)],
        scratch_shapes=[pltpu.VMEM((1, LANES), F32),
                        pltpu.VMEM((len(RESIDUE_DILATIONS), WIDTH // LANES, tm, LANES), F32),
                        pltpu.VMEM((len(RESIDUE_DILATIONS), tm, LANES), F32),
                        pltpu.VMEM((WIDTH // LANES + 1, REGROUP, tm // REGROUP, LANES), F32)],
        compiler_params=_cparams("arbitrary"),
        name="outproj_router",
    )(x2d, *outs, *stats, ret, wout_bf16, expand2, norm_ffn_g.reshape(1, D_MODEL), wr2, br, tri)


SC_CORES = 2
SC_SUBCORES = 16
SC_WINDOW = 128


def _sc_mesh():
    return plsc.VectorSubcoreMesh(core_axis_name="c", subcore_axis_name="s")


def _sc_scatter_rows(src, idx, n_out_rows):
    n_rows = src.shape[0]
    workers = SC_CORES * SC_SUBCORES
    per_worker = n_rows // workers
    n_win = per_worker // SC_WINDOW
    assert per_worker * workers == n_rows and n_win * SC_WINDOW == per_worker and idx.shape == (TOP_K * n_rows,)

    @functools.partial(
        pl.kernel, mesh=_sc_mesh(),
        out_type=jax.ShapeDtypeStruct((n_out_rows, LANES), src.dtype),
        scratch_types=[pltpu.VMEM((SC_WINDOW,), jnp.int32), pltpu.VMEM((SC_WINDOW, LANES), src.dtype)],
        name="sc_scatter_rows",
    )
    def scatter(src_hbm, idx_hbm, out_hbm, idx_v, rows_v):
        wid = lax.axis_index("s") * SC_CORES + lax.axis_index("c")
        base = wid * per_worker

        @pl.loop(0, n_win)
        def _(j):
            off = pl.multiple_of(base + j * SC_WINDOW, SC_WINDOW)
            pltpu.sync_copy(src_hbm.at[pl.ds(off, SC_WINDOW)], rows_v)
            for kk in range(TOP_K):
                pltpu.sync_copy(idx_hbm.at[pl.ds(pl.multiple_of(kk * n_rows + off, SC_WINDOW), SC_WINDOW)], idx_v)
                pltpu.sync_copy(rows_v, out_hbm.at[idx_v])

    return scatter(src, idx)


def _expert_kernel(blk_e_ref, nvalid_ref, rows_ref, xs_ref, wgu_ref, bgu_ref, wd_ref, bd_ref, ys_ref, wgu_bf, wd_bf):
    i = pl.program_id(0)

    @pl.when((i == 0) | (blk_e_ref[i] != blk_e_ref[jnp.maximum(i - 1, 0)]))
    def _():
        wgu_bf[...] = wgu_ref[0].astype(BF16)
        wd_bf[...] = wd_ref[0].astype(BF16)

    def mlp(m):
        live = lax.broadcasted_iota(jnp.int32, (m, LANES), 0) < rows_ref[i]
        x = jnp.concatenate([jnp.where(live, p, 0.0).astype(BF16) for p in _load_pieces(xs_ref, slice(0, m))], axis=1)
        gu = jnp.dot(x, wgu_bf[...], preferred_element_type=F32) + bgu_ref[0]
        gate = jnp.minimum(gu[:, :EXPERT_FF], SWIGLU_LIMIT)
        up = jnp.clip(gu[:, EXPERT_FF:], -SWIGLU_LIMIT, SWIGLU_LIMIT)
        act = gate * (1.0 / (1.0 + jnp.exp(-SWIGLU_ALPHA * gate))) * (up + 1.0)
        y = jnp.dot(act.astype(BF16), wd_bf[...], preferred_element_type=F32) + bd_ref[0]
        _store_pieces(ys_ref, slice(0, m), y)

    n_sub = lax.shift_right_logical(rows_ref[i] + (EXPERT_SUB - 1), EXPERT_SUB.bit_length() - 1)
    for v in range(1, BM // EXPERT_SUB + 1):
        pl.when((i < nvalid_ref[0]) & (n_sub == v))(functools.partial(mlp, v * EXPERT_SUB))


def _experts(xs, blk_e, nvalid, live_rows, wgu_bf16, bgu, wd_bf16, bd):
    cap = xs.shape[1]
    nblk = cap // BM

    def blk(i, be, nv, lr):
        return (0, jnp.minimum(i, nv[0] - 1), 0)

    def by_expert(i, be, nv, lr):
        return (be[i], 0, 0)

    return pl.pallas_call(
        _expert_kernel,
        grid_spec=pltpu.PrefetchScalarGridSpec(
            num_scalar_prefetch=3,
            grid=(nblk,),
            in_specs=[pl.BlockSpec((N_PIECES, BM, LANES), blk),
                      pl.BlockSpec((1, D_MODEL, 2 * EXPERT_FF), by_expert),
                      pl.BlockSpec((1, 1, 2 * EXPERT_FF), by_expert),
                      pl.BlockSpec((1, EXPERT_FF, D_MODEL), by_expert),
                      pl.BlockSpec((1, 1, D_MODEL), by_expert)],
            out_specs=pl.BlockSpec((N_PIECES, BM, LANES), blk),
            scratch_shapes=[pltpu.VMEM((D_MODEL, 2 * EXPERT_FF), BF16), pltpu.VMEM((EXPERT_FF, D_MODEL), BF16)],
        ),
        out_shape=jax.ShapeDtypeStruct((N_PIECES, cap, LANES), jnp.uint32),
        compiler_params=_cparams("arbitrary"),
        name="moe_experts",
    )(blk_e, nvalid, live_rows, xs, wgu_bf16, bgu.reshape(N_EXPERTS, 1, 2 * EXPERT_FF), wd_bf16,
      bd.reshape(N_EXPERTS, 1, D_MODEL))


def _sc_gather_rows(table, idx):
    n_rows = idx.shape[0]
    workers = SC_CORES * SC_SUBCORES
    per_worker = n_rows // workers
    n_win = per_worker // SC_WINDOW
    assert per_worker * workers == n_rows and n_win * SC_WINDOW == per_worker

    @functools.partial(
        pl.kernel, mesh=_sc_mesh(),
        out_type=jax.ShapeDtypeStruct((n_rows, LANES), table.dtype),
        scratch_types=[pltpu.VMEM((SC_WINDOW,), jnp.int32), pltpu.VMEM((SC_WINDOW, LANES), table.dtype),
                       pltpu.SemaphoreType.DMA],
        name="sc_gather_rows",
    )
    def gather(table_hbm, idx_hbm, out_hbm, idx_v, rows_v, sem):
        wid = lax.axis_index("s") * SC_CORES + lax.axis_index("c")
        base = wid * per_worker

        @pl.loop(0, n_win)
        def _(j):
            off = pl.multiple_of(base + j * SC_WINDOW, SC_WINDOW)
            pltpu.sync_copy(idx_hbm.at[pl.ds(off, SC_WINDOW)], idx_v)
            pltpu.async_copy(table_hbm.at[idx_v], rows_v, sem).wait()
            pltpu.sync_copy(rows_v, out_hbm.at[pl.ds(off, SC_WINDOW)])

    return gather(table, idx)


def _combine_kernel(x1_ref, gates_ref, gfin_ref, *refs, tc):
    piece_refs, o_ref = refs[:TOP_K * N_PIECES], refs[TOP_K * N_PIECES]
    g = gates_ref[...]
    gk = [jnp.broadcast_to(g[:, kk:kk + 1], (tc, LANES)) for kk in range(TOP_K)]
    zs = {}
    ssq = jnp.zeros((tc, 1), F32)
    for s in range(N_PIECES):
        c_lo, c_hi = s * LANES, HALF_D + s * LANES
        z_lo = x1_ref[:, c_lo:c_lo + LANES]
        z_hi = x1_ref[:, c_hi:c_hi + LANES]
        for kk in range(TOP_K):
            a, b = _unpack_bf16_pair(piece_refs[kk * N_PIECES + s][...])
            z_lo = z_lo + gk[kk] * a
            z_hi = z_hi + gk[kk] * b
        zs[c_lo], zs[c_hi] = z_lo, z_hi
        ssq = ssq + jnp.sum(z_lo * z_lo + z_hi * z_hi, axis=-1, keepdims=True)
    inv = lax.rsqrt(ssq * (1.0 / D_MODEL) + NORM_EPS)
    for c0, z in zs.items():
        o_ref[:, c0:c0 + LANES] = z * inv * gfin_ref[:, c0:c0 + LANES]


def _combine(x1, gates, gathered, norm_final_g):
    t = x1.shape[0]
    tc = TC
    row = lambda i: (i, 0)
    slot_specs = [pl.BlockSpec((None, tc, LANES), functools.partial(lambda i, j: (j, i, 0), j=j))
                  for j in range(TOP_K * N_PIECES)]
    return pl.pallas_call(
        functools.partial(_combine_kernel, tc=tc),
        grid=(t // tc,),
        in_specs=[pl.BlockSpec((tc, D_MODEL), row), pl.BlockSpec((tc, LANES), row),
                  pl.BlockSpec((1, D_MODEL), lambda i: (0, 0)), *slot_specs],
        out_specs=pl.BlockSpec((tc, D_MODEL), row),
        out_shape=jax.ShapeDtypeStruct((t, D_MODEL), F32),
        compiler_params=_cparams("arbitrary"),
        name="moe_combine",
    )(x1, gates, norm_final_g.reshape(1, D_MODEL), *([gathered] * (TOP_K * N_PIECES)))


def _moe(x1, h2t, ri, gates, counts_f, wgu_bf16, bgu, wd_bf16, bd, norm_final_g):
    t = x1.shape[0]
    a = t * TOP_K
    cap = a + N_EXPERTS * BM
    nblk = cap // BM
    counts = counts_f[0, :N_EXPERTS].astype(jnp.int32)
    padded = ((counts + BM - 1) // BM) * BM
    pend = jnp.cumsum(padded)
    pstart = pend - padded
    nvalid = (pend[-1] // BM).reshape(1)
    first_row = jnp.minimum(jnp.arange(nblk, dtype=jnp.int32) * BM, pend[-1] - 1)
    blk_e = jnp.sum(pend[None, :] <= first_row[:, None], axis=1).astype(jnp.int32)
    idx, rank = ri[:TOP_K], ri[TOP_K:2 * TOP_K]
    onehot = idx[None, :, :] == jnp.arange(N_EXPERTS, dtype=jnp.int32)[:, None, None]
    dest = rank + jnp.sum(jnp.where(onehot, pstart[:, None, None], 0), axis=0)
    seg_end = (pstart + counts)[blk_e]
    live_rows = jnp.clip(seg_end - jnp.arange(nblk, dtype=jnp.int32) * BM, 0, BM).astype(jnp.int32)
    piece = dest[:, None, :] + (jnp.arange(N_PIECES, dtype=jnp.int32) * cap)[None, :, None]
    piece = piece.reshape(TOP_K * N_PIECES * t)
    xs = _sc_scatter_rows(h2t.reshape(N_PIECES * t, LANES), piece, N_PIECES * cap)
    ys = _experts(xs.reshape(N_PIECES, cap, LANES), blk_e, nvalid, live_rows, wgu_bf16, bgu, wd_bf16, bd)
    gathered = _sc_gather_rows(ys.reshape(N_PIECES * cap, LANES), piece)
    return _combine(x1, gates, gathered.reshape(TOP_K * N_PIECES, t, LANES), norm_final_g)


def _encoder(x, p):
    b, s, d_model = x.shape
    assert d_model == D_MODEL and s % (max(DILATIONS) * TQ) == 0 and (b * s) % max(TM_OUT, TC) == 0, x.shape
    x2d = x.reshape(b * s, D_MODEL)
    cos_t, sin_t = _rope_tables(s)
    (qa, ka, va, qr, kr, vr, gr), by_residue = _inproj(x2d, p["norm_mix_g"], p["w_in"], cos_t, sin_t, s)
    outs, stats = [], []
    for d in DILATIONS:
        if d == 1:
            o, st = _banded_attention(*[z.reshape(b, s, WIDTH) for z in (qa, ka, va)])
            outs.append(o.reshape(b * s, WIDTH))
            stats.append(st.reshape(b * s, LANES))
        else:
            o, st = _banded_attention(*[z.reshape(b * d, s // d, WIDTH) for z in by_residue[d]])
            outs.append(o.reshape(b, d, s // d, WIDTH))
            stats.append(st.reshape(b, d, s // d, LANES))
    ret = _retention(qr, kr, vr, gr, p["ret_norm_g"], b, s)
    x1, h2t, ri, gates, counts = _outproj_router(x2d, outs, stats, ret, p["w_out"], p["norm_ffn_g"],
                                                 p["w_router"], p["b_router"], s)
    y = _moe(x1, h2t, ri, gates, counts, p["w_gate_up"], p["b_gate_up"], p["w_down"], p["b_down"], p["norm_final_g"])
    return y.reshape(b, s, D_MODEL)


def kernel(x_prompt, x_sample, norm_mix_g, w_in, ret_norm_g, w_out, norm_ffn_g, w_router, b_router, w_gate_up, b_gate_up, w_down, b_down, norm_final_g):
    assert norm_mix_g.shape[0] == 1, "single layer"
    p = dict(norm_mix_g=norm_mix_g[0], w_in=w_in[0].astype(BF16), ret_norm_g=ret_norm_g[0],
             w_out=w_out[0].astype(BF16), norm_ffn_g=norm_ffn_g[0], w_router=w_router[0], b_router=b_router[0],
             w_gate_up=w_gate_up[0], b_gate_up=b_gate_up[0], w_down=w_down[0],
             b_down=b_down[0], norm_final_g=norm_final_g)
    y_sample = _encoder(x_sample, p)
    y_prompt = _encoder(x_prompt, p)
    return (y_prompt, y_sample)
```

```python
import functools

import jax
import jax.numpy as jnp
from jax import lax
from jax.experimental import pallas as pl
from jax.experimental.pallas import tpu as pltpu
from jax.experimental.pallas import tpu_sc as plsc

D_MODEL = 1024
HEAD_DIM = 64
N_HEADS = 8
WIDTH = N_HEADS * HEAD_DIM
N_SLABS = 7
DILATIONS = (1, 4, 16)
HALF_SPAN = 64
ROPE_THETA = 10000.0
RET_DECAY_BASE_FWD = 5.0
RET_DECAY_BASE_BWD = 5.5
N_EXPERTS = 32
TOP_K = 4
EXPERT_FF = D_MODEL
SWIGLU_LIMIT = 7.0
SWIGLU_ALPHA = 1.702
NORM_EPS = 1e-6
NEG_INF = -1e30

LANES = 128
VMEM_LIMIT_BYTES = 56 * 1024 * 1024

TM_INPROJ = 1024
TQ = 128
TL_ATTN = 2048
HEAD_GROUP = 4
RET_CHUNK = 256
TM_OUT = 1024
OUT_SPLIT = 4
RET_UNROLL = 8
BM = 1024
EXPERT_SUB = 256
TC = 1024

F32 = jnp.float32
BF16 = jnp.bfloat16


def _cparams(*sem):
    return pltpu.CompilerParams(dimension_semantics=sem, vmem_limit_bytes=VMEM_LIMIT_BYTES)


_ROTATE = (True, True, False, True, True, False, False)
ATTN_SCALE = HEAD_DIM ** -0.5
LOG2E = 1.4426950408889634
_SCALE = (ATTN_SCALE * LOG2E, 1.0, 1.0, 1.0, ATTN_SCALE, 1.0, 1.0)


N_ATTN_SLABS = 3
RESIDUE_DILATIONS = tuple(d for d in DILATIONS if d > 1)
REGROUP = 4
STAGE_SLOTS = 2
assert RESIDUE_DILATIONS == (REGROUP, REGROUP ** 2)


def _inproj_kernel(x_ref, g_ref, w_ref, cos_ref, sin_ref, *refs):
    out_refs = refs[:N_SLABS]
    res_refs = refs[N_SLABS:N_SLABS + N_ATTN_SLABS * len(RESIDUE_DILATIONS)]
    stage, stage4 = refs[-2:]
    x = x_ref[...]
    tm = x.shape[0]
    h = (x * lax.rsqrt(jnp.mean(x * x, axis=-1, keepdims=True) + NORM_EPS) * g_ref[...]).astype(BF16)
    cos = cos_ref[...]
    sin = sin_ref[...]
    tables = {sc: (cos, sin) if sc == 1.0 else (cos * sc, sin * sc) for sc in sorted(set(_SCALE))}
    lane = lax.broadcasted_iota(jnp.int32, cos.shape, 1)
    first_half = (lane & (HEAD_DIM - 1)) < HEAD_DIM // 2
    for j, o_ref in enumerate(out_refs):
        p = jnp.dot(h, w_ref[:, j * WIDTH:(j + 1) * WIDTH], preferred_element_type=F32)
        for c in range(WIDTH // LANES):
            cols = slice(c * LANES, (c + 1) * LANES)
            r = p[:, cols]
            if _ROTATE[j]:
                partner = jnp.where(first_half, pltpu.roll(r, LANES - HEAD_DIM // 2, 1), pltpu.roll(r, HEAD_DIM // 2, 1))
                cs, sn = tables[_SCALE[j]]
                r = r * cs + partner * sn
            o_ref[:, cols] = r.astype(BF16)
            if j < N_ATTN_SLABS:
                js = j % STAGE_SLOTS
                stage[js, c] = r
                dst4 = res_refs[j]
                dst16 = res_refs[N_ATTN_SLABS + j]
                for bb in range(REGROUP):
                    grp = stage[js, c, pl.ds(bb, tm // REGROUP, stride=REGROUP), :]
                    dst4[0, bb, :, cols] = grp.astype(BF16)
                    stage4[js, c, bb] = grp
                    for aa in range(REGROUP):
                        sub = stage4[js, c, bb, pl.ds(aa, tm // REGROUP ** 2, stride=REGROUP), :]
                        dst16[0, bb * REGROUP + aa, :, cols] = sub.astype(BF16)


def _inproj(x2d, g, w_bf16, cos_t, sin_t, seq):
    t = x2d.shape[0]
    tm = min(TM_INPROJ, seq)
    pos_blocks = seq // tm
    b = t // seq
    out = jax.ShapeDtypeStruct((t, WIDTH), BF16)
    row = lambda i: (i, 0)
    out_specs = [pl.BlockSpec((tm, WIDTH), row)] * N_SLABS
    out_shape = [out] * N_SLABS
    for d in RESIDUE_DILATIONS:
        out_specs += [pl.BlockSpec((1, d, tm // d, WIDTH), lambda i: (i // pos_blocks, 0, i % pos_blocks, 0))] * N_ATTN_SLABS
        out_shape += [jax.ShapeDtypeStruct((b, d, seq // d, WIDTH), BF16)] * N_ATTN_SLABS
    res = pl.pallas_call(
        _inproj_kernel,
        grid=(t // tm,),
        in_specs=[
            pl.BlockSpec((tm, D_MODEL), row),
            pl.BlockSpec((1, D_MODEL), lambda i: (0, 0)),
            pl.BlockSpec((D_MODEL, N_SLABS * WIDTH), lambda i: (0, 0)),
            pl.BlockSpec((tm, LANES), lambda i: (i % pos_blocks, 0)),
            pl.BlockSpec((tm, LANES), lambda i: (i % pos_blocks, 0)),
        ],
        out_specs=out_specs,
        out_shape=out_shape,
        scratch_shapes=[pltpu.VMEM((STAGE_SLOTS, WIDTH // LANES, tm, LANES), F32),
                        pltpu.VMEM((STAGE_SLOTS, WIDTH // LANES, REGROUP, tm // REGROUP, LANES), F32)],
        compiler_params=_cparams("arbitrary"),
        name="inproj",
    )(x2d, g.reshape(1, D_MODEL), w_bf16, cos_t, sin_t)
    natural = res[:N_SLABS]
    by_residue = {d: res[N_SLABS + di * N_ATTN_SLABS:N_SLABS + (di + 1) * N_ATTN_SLABS]
                  for di, d in enumerate(RESIDUE_DILATIONS)}
    return natural, by_residue


def _rope_tables(seq):
    half = HEAD_DIM // 2
    inv_freq = ROPE_THETA ** (-jnp.arange(0, HEAD_DIM, 2, dtype=F32) / HEAD_DIM)
    ang = jnp.arange(seq, dtype=F32)[:, None] * inv_freq[None, :]
    cos, sin = jnp.cos(ang), jnp.sin(ang)
    reps = LANES // HEAD_DIM
    cos_t = jnp.tile(jnp.concatenate([cos, cos], axis=1), (1, reps))
    sin_t = jnp.tile(jnp.concatenate([-sin, sin], axis=1), (1, reps))
    assert cos_t.shape == (seq, LANES) and half * 2 == HEAD_DIM
    return cos_t, sin_t


def _attn_kernel(q_ref, kp_ref, kc_ref, kn_ref, vp_ref, vc_ref, vn_ref, o_ref, st_ref, kbuf, vbuf, *, gb, tl, sub_len):
    i = pl.program_id(1)
    hs = HALF_SPAN
    for gi in range(gb):
        kbuf[gi, 0:hs] = kp_ref[gi]
        kbuf[gi, hs:hs + tl] = kc_ref[gi]
        kbuf[gi, hs + tl:hs + tl + hs] = kn_ref[gi]
        vbuf[gi, 0:hs] = vp_ref[gi]
        vbuf[gi, hs:hs + tl] = vc_ref[gi]
        vbuf[gi, hs + tl:hs + tl + hs] = vn_ref[gi]
    tk = TQ + 2 * hs
    gw = HEAD_GROUP * HEAD_DIM
    qi = lax.broadcasted_iota(jnp.int32, (TQ, tk), 0)
    kj = lax.broadcasted_iota(jnp.int32, (TQ, tk), 1)
    band_bias = jnp.where((kj >= qi) & (kj - qi <= 2 * hs), 0.0, NEG_INF).astype(F32)
    key_col = lax.broadcasted_iota(jnp.int32, (1, tk), 1)
    lane = lax.broadcasted_iota(jnp.int32, (TQ, LANES), 1)
    head_of_lane = lax.broadcasted_iota(jnp.int32, (TQ, gw), 1) // HEAD_DIM
    n_sub = tl // TQ
    for gi, sub in [(gi, sub) for gi in range(gb) for sub in range(n_sub)]:
        a = sub * TQ
        bias = band_bias
        if sub == 0 or sub == n_sub - 1:
            kpos = key_col + (i * tl + a - hs)
            bias = bias + jnp.where((kpos >= 0) & (kpos < sub_len), 0.0, NEG_INF).astype(F32)
        bias = jnp.concatenate([bias] * HEAD_GROUP, axis=0)
        st_ref[gi, a:a + TQ, :] = jnp.zeros((TQ, LANES), F32)
        for g in range(N_HEADS // HEAD_GROUP):
            cols = slice(g * gw, (g + 1) * gw)
            q4 = q_ref[gi, a:a + TQ, cols]
            k4 = kbuf[gi, a:a + tk, cols]
            v4 = vbuf[gi, a:a + tk, cols]
            lhs = jnp.concatenate([jnp.where(head_of_lane == h, q4, jnp.zeros_like(q4)) for h in range(HEAD_GROUP)], axis=0)
            s = lax.dot_general(lhs, k4, (((1,), (1,)), ((), ())), preferred_element_type=F32) + bias
            m = jnp.max(s, axis=-1, keepdims=True)
            p = jnp.exp2(s - m)
            l = jnp.sum(p, axis=-1, keepdims=True)
            o_all = jnp.dot(p.astype(BF16), v4, preferred_element_type=F32)
            inv = 1.0 / l
            lse = m + jnp.log2(l)
            halves = []
            for hp in range(HEAD_GROUP // 2):
                blk = slice(hp * LANES, (hp + 1) * LANES)
                ra, rb = slice(2 * hp * TQ, (2 * hp + 1) * TQ), slice((2 * hp + 1) * TQ, (2 * hp + 2) * TQ)
                halves.append(jnp.where(lane < HEAD_DIM, o_all[ra, blk] * inv[ra], o_all[rb, blk] * inv[rb]))
            for h in range(HEAD_GROUP):
                gh = g * HEAD_GROUP + h
                st_ref[gi, a:a + TQ, gh:gh + 1] = lse[h * TQ:(h + 1) * TQ]
            o_ref[gi, a:a + TQ, cols] = jnp.concatenate(halves, axis=1).astype(BF16)


def _banded_attention(q, k, v):
    g, sub_len, _ = q.shape
    tl = min(TL_ATTN, sub_len)
    gb = min(TL_ATTN // tl, g)
    hs = HALF_SPAN
    per = tl // hs
    last = sub_len // hs - 1
    cur = pl.BlockSpec((gb, tl, WIDTH), lambda b, i: (b, i, 0))
    prev = pl.BlockSpec((gb, hs, WIDTH), lambda b, i: (b, jnp.maximum(i * per - 1, 0), 0))
    nxt = pl.BlockSpec((gb, hs, WIDTH), lambda b, i: (b, jnp.minimum((i + 1) * per, last), 0))
    return pl.pallas_call(
        functools.partial(_attn_kernel, gb=gb, tl=tl, sub_len=sub_len),
        grid=(g // gb, sub_len // tl),
        in_specs=[cur, prev, cur, nxt, prev, cur, nxt],
        out_specs=[cur, pl.BlockSpec((gb, tl, LANES), lambda b, i: (b, i, 0))],
        out_shape=[jax.ShapeDtypeStruct((g, sub_len, WIDTH), BF16),
                   jax.ShapeDtypeStruct((g, sub_len, LANES), F32)],
        scratch_shapes=[pltpu.VMEM((gb, tl + 2 * hs, WIDTH), BF16), pltpu.VMEM((gb, tl + 2 * hs, WIDTH), BF16)],
        compiler_params=_cparams("arbitrary", "arbitrary"),
        name="banded_attention",
    )(q, k, k, k, v, v, v)


def _ret_kernel(q_ref, k_ref, v_ref, gate_ref, dmat_ref, wq_ref, wk_ref, dec_ref, gn_ref, o_ref, sf_scr, sb_scr, *, c, n_chunks):
    pair = 2 * HEAD_DIM
    lane = lax.broadcasted_iota(jnp.int32, (c, pair), 1)
    head0 = lane < HEAD_DIM
    blk_r = lax.broadcasted_iota(jnp.int32, (pair, pair), 0) // HEAD_DIM
    blk_c = lax.broadcasted_iota(jnp.int32, (pair, pair), 1) // HEAD_DIM
    same_head = blk_r == blk_c
    dec_f = dec_ref[0:1, :]
    dec_b = dec_ref[1:2, :]
    tn = (((0,), (0,)), ((), ()))
    nt = (((1,), (1,)), ((), ()))

    def rows_of(n):
        return pl.ds(pl.multiple_of(n * c, c), c)

    def kv_body(n, carry):
        rows = rows_of(n)
        kf32 = k_ref[0, rows, :].astype(F32)
        kw = jnp.concatenate([(kf32 * wk_ref[:, :pair]).astype(BF16), (kf32 * wk_ref[:, pair:]).astype(BF16)], axis=1)
        kv = lax.dot_general(kw, v_ref[0, rows, :], tn, preferred_element_type=F32)
        sf_scr[n] = jnp.where(same_head, kv[:pair], 0.0)
        sb_scr[n] = jnp.where(same_head, kv[pair:], 0.0)
        return carry

    lax.fori_loop(0, n_chunks, kv_body, 0, unroll=RET_UNROLL)

    def scan_body(t, carry):
        sf, sb = carry
        nb = n_chunks - 1 - t
        kv_f = sf_scr[t]
        kv_b = sb_scr[nb]
        sf_scr[t] = sf
        sb_scr[nb] = sb
        return sf * dec_f + kv_f, sb * dec_b + kv_b

    zero = jnp.zeros((pair, pair), F32)
    lax.fori_loop(0, n_chunks, scan_body, (zero, zero))

    def out_body(n, carry):
        rows = rows_of(n)
        q = q_ref[0, rows, :]
        k = k_ref[0, rows, :]
        v = v_ref[0, rows, :]
        qf32 = q.astype(F32)
        intra = []
        for hh in range(2):
            mask = head0 if hh == 0 else jnp.logical_not(head0)
            kh = jnp.where(mask, k, jnp.zeros_like(k))
            s = lax.dot_general(q, kh, nt, preferred_element_type=F32)
            a = (s * dmat_ref[hh]).astype(BF16)
            intra.append(jnp.dot(a, v, preferred_element_type=F32))
        qw = jnp.concatenate([(qf32 * wq_ref[:, :pair]).astype(BF16), (qf32 * wq_ref[:, pair:]).astype(BF16)], axis=1)
        states = jnp.concatenate([sf_scr[n].astype(BF16), sb_scr[n].astype(BF16)], axis=0)
        tot = jnp.where(head0, intra[0], intra[1]) + jnp.dot(qw, states, preferred_element_type=F32)
        inv = 1.0 / HEAD_DIM
        s0 = jnp.sum(jnp.where(head0, tot, 0.0), axis=-1, keepdims=True)
        s1 = jnp.sum(jnp.where(head0, 0.0, tot), axis=-1, keepdims=True)
        xc = tot - jnp.where(head0, s0, s1) * inv
        sq = xc * xc
        v0 = jnp.sum(jnp.where(head0, sq, 0.0), axis=-1, keepdims=True)
        v1 = jnp.sum(jnp.where(head0, 0.0, sq), axis=-1, keepdims=True)
        y = xc * lax.rsqrt(jnp.where(head0, v0, v1) * inv + NORM_EPS)
        gt = gate_ref[0, rows, :].astype(F32)
        y = y * gn_ref[...] * (gt / (1.0 + jnp.exp(-gt)))
        o_ref[0, rows, :] = y.astype(BF16)
        return carry

    lax.fori_loop(0, n_chunks, out_body, 0, unroll=RET_UNROLL)


def _retention_tables(c):
    hidx = jnp.arange(N_HEADS, dtype=F32)
    lg_f = jnp.log1p(-jnp.exp2(-(RET_DECAY_BASE_FWD + hidx)))
    lg_b = jnp.log1p(-jnp.exp2(-(RET_DECAY_BASE_BWD + hidx)))
    pos = jnp.arange(c, dtype=F32)
    diff = pos[:, None] - pos[None, :]
    dm_f = jnp.exp(jnp.maximum(diff, 0.0)[None] * lg_f[:, None, None])
    dm_b = jnp.exp(jnp.maximum(-diff, 0.0)[None] * lg_b[:, None, None])
    dmat = jnp.where((diff >= 0)[None], dm_f, dm_b)

    def per_lane(tab):
        t = jnp.repeat(tab[:, :, None], HEAD_DIM, axis=2)
        return t.reshape(N_HEADS // 2, 2, c, HEAD_DIM).transpose(0, 2, 1, 3).reshape(N_HEADS // 2, c, 2 * HEAD_DIM)

    wq_f = per_lane(jnp.exp((pos + 1.0)[None, :] * lg_f[:, None]))
    wq_b = per_lane(jnp.exp((c - pos)[None, :] * lg_b[:, None]))
    wk_f = per_lane(jnp.exp((c - 1.0 - pos)[None, :] * lg_f[:, None]))
    wk_b = per_lane(jnp.exp(pos[None, :] * lg_b[:, None]))
    wq = jnp.concatenate([wq_f, wq_b], axis=2)
    wk = jnp.concatenate([wk_f, wk_b], axis=2)
    dec = jnp.stack([jnp.repeat(jnp.exp(c * lg_f), HEAD_DIM), jnp.repeat(jnp.exp(c * lg_b), HEAD_DIM)], axis=0)
    dec = dec.reshape(2, N_HEADS // 2, 2 * HEAD_DIM).transpose(1, 0, 2)
    return dmat, wq, wk, dec


def _retention(q, k, v, gate, ret_norm_g, b, s):
    c = min(RET_CHUNK, s)
    n_chunks = s // c
    pair = 2 * HEAD_DIM
    dmat, wq, wk, dec = _retention_tables(c)
    seq_blk = pl.BlockSpec((1, s, pair), lambda bi, hp: (bi, 0, hp))
    r3 = lambda z: z.reshape(b, s, WIDTH)
    return pl.pallas_call(
        functools.partial(_ret_kernel, c=c, n_chunks=n_chunks),
        grid=(b, N_HEADS // 2),
        in_specs=[seq_blk, seq_blk, seq_blk, seq_blk,
                  pl.BlockSpec((2, c, c), lambda bi, hp: (hp, 0, 0)),
                  pl.BlockSpec((None, c, 2 * pair), lambda bi, hp: (hp, 0, 0)),
                  pl.BlockSpec((None, c, 2 * pair), lambda bi, hp: (hp, 0, 0)),
                  pl.BlockSpec((None, 2, pair), lambda bi, hp: (hp, 0, 0)),
                  pl.BlockSpec((1, pair), lambda bi, hp: (0, hp))],
        out_specs=seq_blk,
        out_shape=jax.ShapeDtypeStruct((b, s, WIDTH), BF16),
        scratch_shapes=[pltpu.VMEM((n_chunks, pair, pair), F32), pltpu.VMEM((n_chunks, pair, pair), F32)],
        compiler_params=_cparams("arbitrary", "arbitrary"),
        name="retention",
    )(r3(q), r3(k), r3(v), r3(gate), dmat, wq, wk, dec, ret_norm_g.reshape(1, WIDTH)).reshape(b * s, WIDTH)


N_PIECES = 4
HALF_D = D_MODEL // 2
assert N_PIECES * LANES == HALF_D


def _pack_bf16_pair(a, b):
    ua = pltpu.bitcast(a.astype(BF16).astype(F32), jnp.uint32)
    ub = pltpu.bitcast(b.astype(BF16).astype(F32), jnp.uint32)
    return ua | (ub >> 16)


def _unpack_bf16_pair(u):
    return pltpu.bitcast(u & jnp.uint32(0xFFFF0000), F32), pltpu.bitcast(u << 16, F32)


def _store_pieces(ref, rows, val):
    for s in range(N_PIECES):
        lo = slice(s * LANES, (s + 1) * LANES)
        hi = slice(HALF_D + s * LANES, HALF_D + (s + 1) * LANES)
        ref[s, rows, :] = _pack_bf16_pair(val[:, lo], val[:, hi])


def _load_pieces(ref, rows):
    parts = [_unpack_bf16_pair(ref[s, rows, :]) for s in range(N_PIECES)]
    return [p[0] for p in parts] + [p[1] for p in parts]


def _split_bf16(x):
    hi = x.astype(BF16)
    return hi, (x - hi.astype(F32)).astype(BF16)


def _outproj_kernel(x_ref, o1_ref, o2_ref, o3_ref, s1_ref, s2_ref, s3_ref, ret_ref, wout_ref, expand_ref,
                    gffn_ref, wr_ref, br_ref, tri_ref,
                    x1_ref, h2t_ref, ri_ref, gates_ref, cnt_ref, base_scr, nat_o, nat_s, grp_o):
    i = pl.program_id(0)
    tm = x_ref.shape[0]

    @pl.when(i == 0)
    def _():
        base_scr[...] = jnp.zeros_like(base_scr)

    n_col = WIDTH // LANES
    for bb in range(REGROUP):
        rows4 = pl.ds(bb, tm // REGROUP, stride=REGROUP)
        for c in range(n_col):
            nat_o[0, c, rows4, :] = o2_ref[0, bb, :, c * LANES:(c + 1) * LANES].astype(F32)
        nat_s[0, rows4, :] = s2_ref[0, bb]
        for aa in range(REGROUP):
            rows16 = pl.ds(aa, tm // REGROUP ** 2, stride=REGROUP)
            for c in range(n_col):
                grp_o[c, bb, rows16, :] = o3_ref[0, bb * REGROUP + aa, :, c * LANES:(c + 1) * LANES].astype(F32)
            grp_o[n_col, bb, rows16, :] = s3_ref[0, bb * REGROUP + aa]
        for c in range(n_col):
            nat_o[1, c, rows4, :] = grp_o[c, bb]
        nat_s[1, rows4, :] = grp_o[n_col, bb]
    hm = tm // OUT_SPLIT
    lane = lax.broadcasted_iota(jnp.int32, (hm, LANES), 1).astype(F32)
    base = base_scr[...]
    for r0 in range(0, tm, hm):
        rows = slice(r0, r0 + hm)
        o_nat = [[o1_ref[rows, c * LANES:(c + 1) * LANES].astype(F32) for c in range(n_col)]]
        sts = [s1_ref[rows, :]]
        for di in range(len(RESIDUE_DILATIONS)):
            o_nat.append([nat_o[di, c, rows, :] for c in range(n_col)])
            sts.append(nat_s[di, rows, :])

        mx = jnp.maximum(jnp.maximum(sts[0], sts[1]), sts[2])
        es = [jnp.exp2(st - mx) for st in sts]
        den = es[0] + es[1] + es[2]
        attn = jnp.concatenate(o_nat[-1], axis=1)
        for e, o_cols in zip(es[:-1], o_nat[:-1]):
            hi, lo = _split_bf16(e / den)
            w_full = jnp.dot(jnp.concatenate([hi, lo], axis=1), expand_ref[...], preferred_element_type=F32)
            attn = attn + w_full * (jnp.concatenate(o_cols, axis=1) - jnp.concatenate(o_nat[-1], axis=1))
        mixed = jnp.concatenate([attn.astype(BF16), ret_ref[rows, :]], axis=1)
        x1 = x_ref[rows, :] + jnp.dot(mixed, wout_ref[...], preferred_element_type=F32)
        x1_ref[rows, :] = x1

        h2 = x1 * lax.rsqrt(jnp.mean(x1 * x1, axis=-1, keepdims=True) + NORM_EPS) * gffn_ref[...]
        _store_pieces(h2t_ref, rows, h2)

        hi, lo = _split_bf16(h2)
        prod = jnp.dot(jnp.concatenate([hi, lo], axis=1), wr_ref[...], preferred_element_type=F32)
        logits = prod + pltpu.roll(prod, LANES - N_EXPERTS, 1) + br_ref[...]
        work = logits
        vals, idxs = [], []
        onehot = jnp.zeros((hm, LANES), F32)
        for _k in range(TOP_K):
            mk = jnp.max(work, axis=-1, keepdims=True)
            ik = jnp.min(jnp.where(work == mk, lane, float(LANES)), axis=-1, keepdims=True)
            sel = lane == ik
            onehot = jnp.where(sel, 1.0, onehot)
            work = jnp.where(sel, -jnp.inf, work)
            vals.append(mk)
            idxs.append(ik)
        ex = [jnp.exp(vk - vals[0]) for vk in vals]
        tot = ex[0] + ex[1] + ex[2] + ex[3]
        before = jnp.dot(tri_ref[...], onehot.astype(BF16), preferred_element_type=F32) + base
        ri = jnp.zeros((hm, LANES), F32)
        gt = jnp.zeros((hm, LANES), F32)
        for kk in range(TOP_K):
            rank = jnp.sum(jnp.where(lane == idxs[kk], before, 0.0), axis=-1, keepdims=True)
            ri = jnp.where(lane == float(kk), idxs[kk], ri)
            ri = jnp.where(lane == float(TOP_K + kk), rank, ri)
            gt = jnp.where(lane == float(kk), ex[kk] / tot, gt)
        ri_ref[:, rows] = jnp.transpose(ri)[:2 * TOP_K].astype(jnp.int32)
        gates_ref[rows, :] = gt
        base = base + jnp.sum(onehot, axis=0, keepdims=True)
    base_scr[...] = base
    cnt_ref[...] = base


def _outproj_router(x2d, outs, stats, ret, wout_bf16, norm_ffn_g, w_router, b_router, seq):
    t = x2d.shape[0]
    tm = TM_OUT
    per_seq = seq // tm
    row = lambda i: (i, 0)

    def res_spec(d, width):
        return pl.BlockSpec((1, d, tm // d, width), lambda i: (i // per_seq, 0, i % per_seq, 0))

    const = lambda i: (0, 0)
    head_of_col = jnp.arange(WIDTH) // HEAD_DIM
    expand = (jnp.arange(LANES)[:, None] == head_of_col[None, :]).astype(BF16)
    expand2 = jnp.concatenate([expand, expand], axis=0)
    wr_hi, wr_lo = _split_bf16(w_router)
    zeros = jnp.zeros((D_MODEL, LANES), BF16)
    wr2 = jnp.concatenate([zeros.at[:, :N_EXPERTS].set(wr_hi).at[:, N_EXPERTS:2 * N_EXPERTS].set(wr_lo),
                           zeros.at[:, :N_EXPERTS].set(wr_hi)], axis=0)
    br = jnp.full((1, LANES), NEG_INF, F32).at[0, :N_EXPERTS].set(b_router)
    hm = tm // OUT_SPLIT
    tri = (jnp.arange(hm)[:, None] > jnp.arange(hm)[None, :]).astype(BF16)
    o_spec = pl.BlockSpec((tm, WIDTH), row)
    s_spec = pl.BlockSpec((tm, LANES), row)
    return pl.pallas_call(
        _outproj_kernel,
        grid=(t // tm,),
        in_specs=[pl.BlockSpec((tm, D_MODEL), row),
                  o_spec, *[res_spec(d, WIDTH) for d in RESIDUE_DILATIONS],
                  s_spec, *[res_spec(d, LANES) for d in RESIDUE_DILATIONS], o_spec,
                  pl.BlockSpec((D_MODEL, D_MODEL), const), pl.BlockSpec((2 * LANES, WIDTH), const),
                  pl.BlockSpec((1, D_MODEL), const), pl.BlockSpec((2 * D_MODEL, LANES), const),
                  pl.BlockSpec((1, LANES), const), pl.BlockSpec((hm, hm), const)],
        out_specs=[pl.BlockSpec((tm, D_MODEL), row), pl.BlockSpec((N_PIECES, tm, LANES), lambda i: (0, i, 0)),
                   pl.BlockSpec((2 * TOP_K, tm), lambda i: (0, i)), s_spec, pl.BlockSpec((1, LANES), const)],
        out_shape=[jax.ShapeDtypeStruct((t, D_MODEL), F32), jax.ShapeDtypeStruct((N_PIECES, t, LANES), jnp.uint32),
                   jax.ShapeDtypeStruct((2 * TOP_K, t), jnp.int32), jax.ShapeDtypeStruct((t, LANES), F32),
                   jax.ShapeDtypeStruct((1, LANES), F32)],
        scratch_shapes=[pltpu.VMEM((1, LANES), F32),
                        pltpu.VMEM((len(RESIDUE_DILATIONS), WIDTH // LANES, tm, LANES), F32),
                        pltpu.VMEM((len(RESIDUE_DILATIONS), tm, LANES), F32),
                        pltpu.VMEM((WIDTH // LANES + 1, REGROUP, tm // REGROUP, LANES), F32)],
        compiler_params=_cparams("arbitrary"),
        name="outproj_router",
    )(x2d, *outs, *stats, ret, wout_bf16, expand2, norm_ffn_g.reshape(1, D_MODEL), wr2, br, tri)


SC_CORES = 2
SC_SUBCORES = 16
SC_WINDOW = 128


def _sc_mesh():
    return plsc.VectorSubcoreMesh(core_axis_name="c", subcore_axis_name="s")


def _sc_scatter_rows(src, idx, n_out_rows):
    n_rows = src.shape[0]
    workers = SC_CORES * SC_SUBCORES
    per_worker = n_rows // workers
    n_win = per_worker // SC_WINDOW
    assert per_worker * workers == n_rows and n_win * SC_WINDOW == per_worker and idx.shape == (TOP_K * n_rows,)

    @functools.partial(
        pl.kernel, mesh=_sc_mesh(),
        out_type=jax.ShapeDtypeStruct((n_out_rows, LANES), src.dtype),
        scratch_types=[pltpu.VMEM((SC_WINDOW,), jnp.int32), pltpu.VMEM((SC_WINDOW, LANES), src.dtype)],
        name="sc_scatter_rows",
    )
    def scatter(src_hbm, idx_hbm, out_hbm, idx_v, rows_v):
        wid = lax.axis_index("s") * SC_CORES + lax.axis_index("c")
        base = wid * per_worker

        @pl.loop(0, n_win)
        def _(j):
            off = pl.multiple_of(base + j * SC_WINDOW, SC_WINDOW)
            pltpu.sync_copy(src_hbm.at[pl.ds(off, SC_WINDOW)], rows_v)
            for kk in range(TOP_K):
                pltpu.sync_copy(idx_hbm.at[pl.ds(pl.multiple_of(kk * n_rows + off, SC_WINDOW), SC_WINDOW)], idx_v)
                pltpu.sync_copy(rows_v, out_hbm.at[idx_v])

    return scatter(src, idx)


def _expert_kernel(blk_e_ref, nvalid_ref, rows_ref, xs_ref, wgu_ref, bgu_ref, wd_ref, bd_ref, ys_ref, wgu_bf, wd_bf):
    i = pl.program_id(0)

    @pl.when((i == 0) | (blk_e_ref[i] != blk_e_ref[jnp.maximum(i - 1, 0)]))
    def _():
        wgu_bf[...] = wgu_ref[0].astype(BF16)
        wd_bf[...] = wd_ref[0].astype(BF16)

    def mlp(m):
        live = lax.broadcasted_iota(jnp.int32, (m, LANES), 0) < rows_ref[i]
        x = jnp.concatenate([jnp.where(live, p, 0.0).astype(BF16) for p in _load_pieces(xs_ref, slice(0, m))], axis=1)
        gu = jnp.dot(x, wgu_bf[...], preferred_element_type=F32) + bgu_ref[0]
        gate = jnp.minimum(gu[:, :EXPERT_FF], SWIGLU_LIMIT)
        up = jnp.clip(gu[:, EXPERT_FF:], -SWIGLU_LIMIT, SWIGLU_LIMIT)
        act = gate * (1.0 / (1.0 + jnp.exp(-SWIGLU_ALPHA * gate))) * (up + 1.0)
        y = jnp.dot(act.astype(BF16), wd_bf[...], preferred_element_type=F32) + bd_ref[0]
        _store_pieces(ys_ref, slice(0, m), y)

    n_sub = lax.shift_right_logical(rows_ref[i] + (EXPERT_SUB - 1), EXPERT_SUB.bit_length() - 1)
    for v in range(1, BM // EXPERT_SUB + 1):
        pl.when((i < nvalid_ref[0]) & (n_sub == v))(functools.partial(mlp, v * EXPERT_SUB))


def _experts(xs, blk_e, nvalid, live_rows, wgu_bf16, bgu, wd_bf16, bd):
    cap = xs.shape[1]
    nblk = cap // BM

    def blk(i, be, nv, lr):
        return (0, jnp.minimum(i, nv[0] - 1), 0)

    def by_expert(i, be, nv, lr):
        return (be[i], 0, 0)

    return pl.pallas_call(
        _expert_kernel,
        grid_spec=pltpu.PrefetchScalarGridSpec(
            num_scalar_prefetch=3,
            grid=(nblk,),
            in_specs=[pl.BlockSpec((N_PIECES, BM, LANES), blk),
                      pl.BlockSpec((1, D_MODEL, 2 * EXPERT_FF), by_expert),
                      pl.BlockSpec((1, 1, 2 * EXPERT_FF), by_expert),
                      pl.BlockSpec((1, EXPERT_FF, D_MODEL), by_expert),
                      pl.BlockSpec((1, 1, D_MODEL), by_expert)],
            out_specs=pl.BlockSpec((N_PIECES, BM, LANES), blk),
            scratch_shapes=[pltpu.VMEM((D_MODEL, 2 * EXPERT_FF), BF16), pltpu.VMEM((EXPERT_FF, D_MODEL), BF16)],
        ),
        out_shape=jax.ShapeDtypeStruct((N_PIECES, cap, LANES), jnp.uint32),
        compiler_params=_cparams("arbitrary"),
        name="moe_experts",
    )(blk_e, nvalid, live_rows, xs, wgu_bf16, bgu.reshape(N_EXPERTS, 1, 2 * EXPERT_FF), wd_bf16,
      bd.reshape(N_EXPERTS, 1, D_MODEL))


def _sc_gather_rows(table, idx):
    n_rows = idx.shape[0]
    workers = SC_CORES * SC_SUBCORES
    per_worker = n_rows // workers
    n_win = per_worker // SC_WINDOW
    assert per_worker * workers == n_rows and n_win * SC_WINDOW == per_worker

    @functools.partial(
        pl.kernel, mesh=_sc_mesh(),
        out_type=jax.ShapeDtypeStruct((n_rows, LANES), table.dtype),
        scratch_types=[pltpu.VMEM((SC_WINDOW,), jnp.int32), pltpu.VMEM((SC_WINDOW, LANES), table.dtype),
                       pltpu.SemaphoreType.DMA],
        name="sc_gather_rows",
    )
    def gather(table_hbm, idx_hbm, out_hbm, idx_v, rows_v, sem):
        wid = lax.axis_index("s") * SC_CORES + lax.axis_index("c")
        base = wid * per_worker

        @pl.loop(0, n_win)
        def _(j):
            off = pl.multiple_of(base + j * SC_WINDOW, SC_WINDOW)
            pltpu.sync_copy(idx_hbm.at[pl.ds(off, SC_WINDOW)], idx_v)
            pltpu.async_copy(table_hbm.at[idx_v], rows_v, sem).wait()
            pltpu.sync_copy(rows_v, out_hbm.at[pl.ds(off, SC_WINDOW)])

    return gather(table, idx)


def _combine_kernel(x1_ref, gates_ref, gfin_ref, *refs, tc):
    piece_refs, o_ref = refs[:TOP_K * N_PIECES], refs[TOP_K * N_PIECES]
    g = gates_ref[...]
    gk = [jnp.broadcast_to(g[:, kk:kk + 1], (tc, LANES)) for kk in range(TOP_K)]
    zs = {}
    ssq = jnp.zeros((tc, 1), F32)
    for s in range(N_PIECES):
        c_lo, c_hi = s * LANES, HALF_D + s * LANES
        z_lo = x1_ref[:, c_lo:c_lo + LANES]
        z_hi = x1_ref[:, c_hi:c_hi + LANES]
        for kk in range(TOP_K):
            a, b = _unpack_bf16_pair(piece_refs[kk * N_PIECES + s][...])
            z_lo = z_lo + gk[kk] * a
            z_hi = z_hi + gk[kk] * b
        zs[c_lo], zs[c_hi] = z_lo, z_hi
        ssq = ssq + jnp.sum(z_lo * z_lo + z_hi * z_hi, axis=-1, keepdims=True)
    inv = lax.rsqrt(ssq * (1.0 / D_MODEL) + NORM_EPS)
    for c0, z in zs.items():
        o_ref[:, c0:c0 + LANES] = z * inv * gfin_ref[:, c0:c0 + LANES]


def _combine(x1, gates, gathered, norm_final_g):
    t = x1.shape[0]
    tc = TC
    row = lambda i: (i, 0)
    slot_specs = [pl.BlockSpec((None, tc, LANES), functools.partial(lambda i, j: (j, i, 0), j=j))
                  for j in range(TOP_K * N_PIECES)]
    return pl.pallas_call(
        functools.partial(_combine_kernel, tc=tc),
        grid=(t // tc,),
        in_specs=[pl.BlockSpec((tc, D_MODEL), row), pl.BlockSpec((tc, LANES), row),
                  pl.BlockSpec((1, D_MODEL), lambda i: (0, 0)), *slot_specs],
        out_specs=pl.BlockSpec((tc, D_MODEL), row),
        out_shape=jax.ShapeDtypeStruct((t, D_MODEL), F32),
        compiler_params=_cparams("arbitrary"),
        name="moe_combine",
    )(x1, gates, norm_final_g.reshape(1, D_MODEL), *([gathered] * (TOP_K * N_PIECES)))


def _moe(x1, h2t, ri, gates, counts_f, wgu_bf16, bgu, wd_bf16, bd, norm_final_g):
    t = x1.shape[0]
    a = t * TOP_K
    cap = a + N_EXPERTS * BM
    nblk = cap // BM
    counts = counts_f[0, :N_EXPERTS].astype(jnp.int32)
    padded = ((counts + BM - 1) // BM) * BM
    pend = jnp.cumsum(padded)
    pstart = pend - padded
    nvalid = (pend[-1] // BM).reshape(1)
    first_row = jnp.minimum(jnp.arange(nblk, dtype=jnp.int32) * BM, pend[-1] - 1)
    blk_e = jnp.sum(pend[None, :] <= first_row[:, None], axis=1).astype(jnp.int32)
    idx, rank = ri[:TOP_K], ri[TOP_K:2 * TOP_K]
    onehot = idx[None, :, :] == jnp.arange(N_EXPERTS, dtype=jnp.int32)[:, None, None]
    dest = rank + jnp.sum(jnp.where(onehot, pstart[:, None, None], 0), axis=0)
    seg_end = (pstart + counts)[blk_e]
    live_rows = jnp.clip(seg_end - jnp.arange(nblk, dtype=jnp.int32) * BM, 0, BM).astype(jnp.int32)
    piece = dest[:, None, :] + (jnp.arange(N_PIECES, dtype=jnp.int32) * cap)[None, :, None]
    piece = piece.reshape(TOP_K * N_PIECES * t)
    xs = _sc_scatter_rows(h2t.reshape(N_PIECES * t, LANES), piece, N_PIECES * cap)
    ys = _experts(xs.reshape(N_PIECES, cap, LANES), blk_e, nvalid, live_rows, wgu_bf16, bgu, wd_bf16, bd)
    gathered = _sc_gather_rows(ys.reshape(N_PIECES * cap, LANES), piece)
    return _combine(x1, gates, gathered.reshape(TOP_K * N_PIECES, t, LANES), norm_final_g)


def _encoder(x, p):
    b, s, d_model = x.shape
    assert d_model == D_MODEL and s % (max(DILATIONS) * TQ) == 0 and (b * s) % max(TM_OUT, TC) == 0, x.shape
    x2d = x.reshape(b * s, D_MODEL)
    cos_t, sin_t = _rope_tables(s)
    (qa, ka, va, qr, kr, vr, gr), by_residue = _inproj(x2d, p["norm_mix_g"], p["w_in"], cos_t, sin_t, s)
    outs, stats = [], []
    for d in DILATIONS:
        if d == 1:
            o, st = _banded_attention(*[z.reshape(b, s, WIDTH) for z in (qa, ka, va)])
            outs.append(o.reshape(b * s, WIDTH))
            stats.append(st.reshape(b * s, LANES))
        else:
            o, st = _banded_attention(*[z.reshape(b * d, s // d, WIDTH) for z in by_residue[d]])
            outs.append(o.reshape(b, d, s // d, WIDTH))
            stats.append(st.reshape(b, d, s // d, LANES))
    ret = _retention(qr, kr, vr, gr, p["ret_norm_g"], b, s)
    x1, h2t, ri, gates, counts = _outproj_router(x2d, outs, stats, ret, p["w_out"], p["norm_ffn_g"],
                                                 p["w_router"], p["b_router"], s)
    y = _moe(x1, h2t, ri, gates, counts, p["w_gate_up"], p["b_gate_up"], p["w_down"], p["b_down"], p["norm_final_g"])
    return y.reshape(b, s, D_MODEL)


def kernel(x_prompt, x_sample, norm_mix_g, w_in, ret_norm_g, w_out, norm_ffn_g, w_router, b_router, w_gate_up, b_gate_up, w_down, b_down, norm_final_g):
    assert norm_mix_g.shape[0] == 1, "single layer"
    p = dict(norm_mix_g=norm_mix_g[0], w_in=w_in[0].astype(BF16), ret_norm_g=ret_norm_g[0],
             w_out=w_out[0].astype(BF16), norm_ffn_g=norm_ffn_g[0], w_router=w_router[0], b_router=b_router[0],
             w_gate_up=w_gate_up[0], b_gate_up=b_gate_up[0], w_down=w_down[0],
             b_down=b_down[0], norm_final_g=norm_final_g)
    y_sample = _encoder(x_sample, p)
    y_prompt = _encoder(x_prompt, p)
    return (y_prompt, y_sample)
```

```python
import functools

import jax
import jax.numpy as jnp
from jax import lax
from jax.experimental import pallas as pl
from jax.experimental.pallas import tpu as pltpu
from jax.experimental.pallas import tpu_sc as plsc

D_MODEL = 1024
HEAD_DIM = 64
N_HEADS = 8
WIDTH = N_HEADS * HEAD_DIM
N_SLABS = 7
DILATIONS = (1, 4, 16)
HALF_SPAN = 64
ROPE_THETA = 10000.0
RET_DECAY_BASE_FWD = 5.0
RET_DECAY_BASE_BWD = 5.5
N_EXPERTS = 32
TOP_K = 4
EXPERT_FF = D_MODEL
SWIGLU_LIMIT = 7.0
SWIGLU_ALPHA = 1.702
NORM_EPS = 1e-6
NEG_INF = -1e30

LANES = 128
VMEM_LIMIT_BYTES = 56 * 1024 * 1024

TM_INPROJ = 1024
TQ = 128
TL_ATTN = 2048
HEAD_GROUP = 4
RET_CHUNK = 256
TM_OUT = 1024
OUT_SPLIT = 4
RET_UNROLL = 8
BM = 1024
EXPERT_SUB = 256
TC = 1024

F32 = jnp.float32
BF16 = jnp.bfloat16


def _cparams(*sem):
    return pltpu.CompilerParams(dimension_semantics=sem, vmem_limit_bytes=VMEM_LIMIT_BYTES)


_ROTATE = (True, True, False, True, True, False, False)
ATTN_SCALE = HEAD_DIM ** -0.5
LOG2E = 1.4426950408889634
_SCALE = (ATTN_SCALE * LOG2E, 1.0, 1.0, 1.0, ATTN_SCALE, 1.0, 1.0)


N_ATTN_SLABS = 3
RESIDUE_DILATIONS = tuple(d for d in DILATIONS if d > 1)
REGROUP = 4
STAGE_SLOTS = 2
assert RESIDUE_DILATIONS == (REGROUP, REGROUP ** 2)


def _inproj_kernel(x_ref, g_ref, w_ref, cos_ref, sin_ref, *refs):
    out_refs = refs[:N_SLABS]
    res_refs = refs[N_SLABS:N_SLABS + N_ATTN_SLABS * len(RESIDUE_DILATIONS)]
    stage, stage4 = refs[-2:]
    x = x_ref[...]
    tm = x.shape[0]
    h = (x * lax.rsqrt(jnp.mean(x * x, axis=-1, keepdims=True) + NORM_EPS) * g_ref[...]).astype(BF16)
    cos = cos_ref[...]
    sin = sin_ref[...]
    tables = {sc: (cos, sin) if sc == 1.0 else (cos * sc, sin * sc) for sc in sorted(set(_SCALE))}
    lane = lax.broadcasted_iota(jnp.int32, cos.shape, 1)
    first_half = (lane & (HEAD_DIM - 1)) < HEAD_DIM // 2
    for j, o_ref in enumerate(out_refs):
        p = jnp.dot(h, w_ref[:, j * WIDTH:(j + 1) * WIDTH], preferred_element_type=F32)
        for c in range(WIDTH // LANES):
            cols = slice(c * LANES, (c + 1) * LANES)
            r = p[:, cols]
            if _ROTATE[j]:
                partner = jnp.where(first_half, pltpu.roll(r, LANES - HEAD_DIM // 2, 1), pltpu.roll(r, HEAD_DIM // 2, 1))
                cs, sn = tables[_SCALE[j]]
                r = r * cs + partner * sn
            o_ref[:, cols] = r.astype(BF16)
            if j < N_ATTN_SLABS:
                js = j % STAGE_SLOTS
                stage[js, c] = r
                dst4 = res_refs[j]
                dst16 = res_refs[N_ATTN_SLABS + j]
                for bb in range(REGROUP):
                    grp = stage[js, c, pl.ds(bb, tm // REGROUP, stride=REGROUP), :]
                    dst4[0, bb, :, cols] = grp.astype(BF16)
                    stage4[js, c, bb] = grp
                    for aa in range(REGROUP):
                        sub = stage4[js, c, bb, pl.ds(aa, tm // REGROUP ** 2, stride=REGROUP), :]
                        dst16[0, bb * REGROUP + aa, :, cols] = sub.astype(BF16)


def _inproj(x2d, g, w_bf16, cos_t, sin_t, seq):
    t = x2d.shape[0]
    tm = min(TM_INPROJ, seq)
    pos_blocks = seq // tm
    b = t // seq
    out = jax.ShapeDtypeStruct((t, WIDTH), BF16)
    row = lambda i: (i, 0)
    out_specs = [pl.BlockSpec((tm, WIDTH), row)] * N_SLABS
    out_shape = [out] * N_SLABS
    for d in RESIDUE_DILATIONS:
        out_specs += [pl.BlockSpec((1, d, tm // d, WIDTH), lambda i: (i // pos_blocks, 0, i % pos_blocks, 0))] * N_ATTN_SLABS
        out_shape += [jax.ShapeDtypeStruct((b, d, seq // d, WIDTH), BF16)] * N_ATTN_SLABS
    res = pl.pallas_call(
        _inproj_kernel,
        grid=(t // tm,),
        in_specs=[
            pl.BlockSpec((tm, D_MODEL), row),
            pl.BlockSpec((1, D_MODEL), lambda i: (0, 0)),
            pl.BlockSpec((D_MODEL, N_SLABS * WIDTH), lambda i: (0, 0)),
            pl.BlockSpec((tm, LANES), lambda i: (i % pos_blocks, 0)),
            pl.BlockSpec((tm, LANES), lambda i: (i % pos_blocks, 0)),
        ],
        out_specs=out_specs,
        out_shape=out_shape,
        scratch_shapes=[pltpu.VMEM((STAGE_SLOTS, WIDTH // LANES, tm, LANES), F32),
                        pltpu.VMEM((STAGE_SLOTS, WIDTH // LANES, REGROUP, tm // REGROUP, LANES), F32)],
        compiler_params=_cparams("arbitrary"),
        name="inproj",
    )(x2d, g.reshape(1, D_MODEL), w_bf16, cos_t, sin_t)
    natural = res[:N_SLABS]
    by_residue = {d: res[N_SLABS + di * N_ATTN_SLABS:N_SLABS + (di + 1) * N_ATTN_SLABS]
                  for di, d in enumerate(RESIDUE_DILATIONS)}
    return natural, by_residue


def _rope_tables(seq):
    half = HEAD_DIM // 2
    inv_freq = ROPE_THETA ** (-jnp.arange(0, HEAD_DIM, 2, dtype=F32) / HEAD_DIM)
    ang = jnp.arange(seq, dtype=F32)[:, None] * inv_freq[None, :]
    cos, sin = jnp.cos(ang), jnp.sin(ang)
    reps = LANES // HEAD_DIM
    cos_t = jnp.tile(jnp.concatenate([cos, cos], axis=1), (1, reps))
    sin_t = jnp.tile(jnp.concatenate([-sin, sin], axis=1), (1, reps))
    assert cos_t.shape == (seq, LANES) and half * 2 == HEAD_DIM
    return cos_t, sin_t


def _attn_kernel(q_ref, kp_ref, kc_ref, kn_ref, vp_ref, vc_ref, vn_ref, o_ref, st_ref, kbuf, vbuf, *, gb, tl, sub_len):
    i = pl.program_id(1)
    hs = HALF_SPAN
    for gi in range(gb):
        kbuf[gi, 0:hs] = kp_ref[gi]
        kbuf[gi, hs:hs + tl] = kc_ref[gi]
        kbuf[gi, hs + tl:hs + tl + hs] = kn_ref[gi]
        vbuf[gi, 0:hs] = vp_ref[gi]
        vbuf[gi, hs:hs + tl] = vc_ref[gi]
        vbuf[gi, hs + tl:hs + tl + hs] = vn_ref[gi]
    tk = TQ + 2 * hs
    gw = HEAD_GROUP * HEAD_DIM
    qi = lax.broadcasted_iota(jnp.int32, (TQ, tk), 0)
    kj = lax.broadcasted_iota(jnp.int32, (TQ, tk), 1)
    band_bias = jnp.where((kj >= qi) & (kj - qi <= 2 * hs), 0.0, NEG_INF).astype(F32)
    key_col = lax.broadcasted_iota(jnp.int32, (1, tk), 1)
    lane = lax.broadcasted_iota(jnp.int32, (TQ, LANES), 1)
    head_of_lane = lax.broadcasted_iota(jnp.int32, (TQ, gw), 1) // HEAD_DIM
    n_sub = tl // TQ
    for gi, sub in [(gi, sub) for gi in range(gb) for sub in range(n_sub)]:
        a = sub * TQ
        bias = band_bias
        if sub == 0 or sub == n_sub - 1:
            kpos = key_col + (i * tl + a - hs)
            bias = bias + jnp.where((kpos >= 0) & (kpos < sub_len), 0.0, NEG_INF).astype(F32)
        bias = jnp.concatenate([bias] * HEAD_GROUP, axis=0)
        st_ref[gi, a:a + TQ, :] = jnp.zeros((TQ, LANES), F32)
        for g in range(N_HEADS // HEAD_GROUP):
            cols = slice(g * gw, (g + 1) * gw)
            q4 = q_ref[gi, a:a + TQ, cols]
            k4 = kbuf[gi, a:a + tk, cols]
            v4 = vbuf[gi, a:a + tk, cols]
            lhs = jnp.concatenate([jnp.where(head_of_lane == h, q4, jnp.zeros_like(q4)) for h in range(HEAD_GROUP)], axis=0)
            s = lax.dot_general(lhs, k4, (((1,), (1,)), ((), ())), preferred_element_type=F32) + bias
            m = jnp.max(s, axis=-1, keepdims=True)
            p = jnp.exp2(s - m)
            l = jnp.sum(p, axis=-1, keepdims=True)
            o_all = jnp.dot(p.astype(BF16), v4, preferred_element_type=F32)
            inv = 1.0 / l
            lse = m + jnp.log2(l)
            halves = []
            for hp in range(HEAD_GROUP // 2):
                blk = slice(hp * LANES, (hp + 1) * LANES)
                ra, rb = slice(2 * hp * TQ, (2 * hp + 1) * TQ), slice((2 * hp + 1) * TQ, (2 * hp + 2) * TQ)
                halves.append(jnp.where(lane < HEAD_DIM, o_all[ra, blk] * inv[ra], o_all[rb, blk] * inv[rb]))
            for h in range(HEAD_GROUP):
                gh = g * HEAD_GROUP + h
                st_ref[gi, a:a + TQ, gh:gh + 1] = lse[h * TQ:(h + 1) * TQ]
            o_ref[gi, a:a + TQ, cols] = jnp.concatenate(halves, axis=1).astype(BF16)


def _banded_attention(q, k, v):
    g, sub_len, _ = q.shape
    tl = min(TL_ATTN, sub_len)
    gb = min(TL_ATTN // tl, g)
    hs = HALF_SPAN
    per = tl // hs
    last = sub_len // hs - 1
    cur = pl.BlockSpec((gb, tl, WIDTH), lambda b, i: (b, i, 0))
    prev = pl.BlockSpec((gb, hs, WIDTH), lambda b, i: (b, jnp.maximum(i * per - 1, 0), 0))
    nxt = pl.BlockSpec((gb, hs, WIDTH), lambda b, i: (b, jnp.minimum((i + 1) * per, last), 0))
    return pl.pallas_call(
        functools.partial(_attn_kernel, gb=gb, tl=tl, sub_len=sub_len),
        grid=(g // gb, sub_len // tl),
        in_specs=[cur, prev, cur, nxt, prev, cur, nxt],
        out_specs=[cur, pl.BlockSpec((gb, tl, LANES), lambda b, i: (b, i, 0))],
        out_shape=[jax.ShapeDtypeStruct((g, sub_len, WIDTH), BF16),
                   jax.ShapeDtypeStruct((g, sub_len, LANES), F32)],
        scratch_shapes=[pltpu.VMEM((gb, tl + 2 * hs, WIDTH), BF16), pltpu.VMEM((gb, tl + 2 * hs, WIDTH), BF16)],
        compiler_params=_cparams("arbitrary", "arbitrary"),
        name="banded_attention",
    )(q, k, k, k, v, v, v)


def _ret_kernel(q_ref, k_ref, v_ref, gate_ref, dmat_ref, wq_ref, wk_ref, dec_ref, gn_ref, o_ref, sf_scr, sb_scr, *, c, n_chunks):
    pair = 2 * HEAD_DIM
    lane = lax.broadcasted_iota(jnp.int32, (c, pair), 1)
    head0 = lane < HEAD_DIM
    blk_r = lax.broadcasted_iota(jnp.int32, (pair, pair), 0) // HEAD_DIM
    blk_c = lax.broadcasted_iota(jnp.int32, (pair, pair), 1) // HEAD_DIM
    same_head = blk_r == blk_c
    dec_f = dec_ref[0:1, :]
    dec_b = dec_ref[1:2, :]
    tn = (((0,), (0,)), ((), ()))
    nt = (((1,), (1,)), ((), ()))

    def rows_of(n):
        return pl.ds(pl.multiple_of(n * c, c), c)

    def kv_body(n, carry):
        rows = rows_of(n)
        kf32 = k_ref[0, rows, :].astype(F32)
        kw = jnp.concatenate([(kf32 * wk_ref[:, :pair]).astype(BF16), (kf32 * wk_ref[:, pair:]).astype(BF16)], axis=1)
        kv = lax.dot_general(kw, v_ref[0, rows, :], tn, preferred_element_type=F32)
        sf_scr[n] = jnp.where(same_head, kv[:pair], 0.0)
        sb_scr[n] = jnp.where(same_head, kv[pair:], 0.0)
        return carry

    lax.fori_loop(0, n_chunks, kv_body, 0, unroll=RET_UNROLL)

    def scan_body(t, carry):
        sf, sb = carry
        nb = n_chunks - 1 - t
        kv_f = sf_scr[t]
        kv_b = sb_scr[nb]
        sf_scr[t] = sf
        sb_scr[nb] = sb
        return sf * dec_f + kv_f, sb * dec_b + kv_b

    zero = jnp.zeros((pair, pair), F32)
    lax.fori_loop(0, n_chunks, scan_body, (zero, zero))

    def out_body(n, carry):
        rows = rows_of(n)
        q = q_ref[0, rows, :]
        k = k_ref[0, rows, :]
        v = v_ref[0, rows, :]
        qf32 = q.astype(F32)
        intra = []
        for hh in range(2):
            mask = head0 if hh == 0 else jnp.logical_not(head0)
            kh = jnp.where(mask, k, jnp.zeros_like(k))
            s = lax.dot_general(q, kh, nt, preferred_element_type=F32)
            a = (s * dmat_ref[hh]).astype(BF16)
            intra.append(jnp.dot(a, v, preferred_element_type=F32))
        qw = jnp.concatenate([(qf32 * wq_ref[:, :pair]).astype(BF16), (qf32 * wq_ref[:, pair:]).astype(BF16)], axis=1)
        states = jnp.concatenate([sf_scr[n].astype(BF16), sb_scr[n].astype(BF16)], axis=0)
        tot = jnp.where(head0, intra[0], intra[1]) + jnp.dot(qw, states, preferred_element_type=F32)
        inv = 1.0 / HEAD_DIM
        s0 = jnp.sum(jnp.where(head0, tot, 0.0), axis=-1, keepdims=True)
        s1 = jnp.sum(jnp.where(head0, 0.0, tot), axis=-1, keepdims=True)
        xc = tot - jnp.where(head0, s0, s1) * inv
        sq = xc * xc
        v0 = jnp.sum(jnp.where(head0, sq, 0.0), axis=-1, keepdims=True)
        v1 = jnp.sum(jnp.where(head0, 0.0, sq), axis=-1, keepdims=True)
        y = xc * lax.rsqrt(jnp.where(head0, v0, v1) * inv + NORM_EPS)
        gt = gate_ref[0, rows, :].astype(F32)
        y = y * gn_ref[...] * (gt / (1.0 + jnp.exp(-gt)))
        o_ref[0, rows, :] = y.astype(BF16)
        return carry

    lax.fori_loop(0, n_chunks, out_body, 0, unroll=RET_UNROLL)


def _retention_tables(c):
    hidx = jnp.arange(N_HEADS, dtype=F32)
    lg_f = jnp.log1p(-jnp.exp2(-(RET_DECAY_BASE_FWD + hidx)))
    lg_b = jnp.log1p(-jnp.exp2(-(RET_DECAY_BASE_BWD + hidx)))
    pos = jnp.arange(c, dtype=F32)
    diff = pos[:, None] - pos[None, :]
    dm_f = jnp.exp(jnp.maximum(diff, 0.0)[None] * lg_f[:, None, None])
    dm_b = jnp.exp(jnp.maximum(-diff, 0.0)[None] * lg_b[:, None, None])
    dmat = jnp.where((diff >= 0)[None], dm_f, dm_b)

    def per_lane(tab):
        t = jnp.repeat(tab[:, :, None], HEAD_DIM, axis=2)
        return t.reshape(N_HEADS // 2, 2, c, HEAD_DIM).transpose(0, 2, 1, 3).reshape(N_HEADS // 2, c, 2 * HEAD_DIM)

    wq_f = per_lane(jnp.exp((pos + 1.0)[None, :] * lg_f[:, None]))
    wq_b = per_lane(jnp.exp((c - pos)[None, :] * lg_b[:, None]))
    wk_f = per_lane(jnp.exp((c - 1.0 - pos)[None, :] * lg_f[:, None]))
    wk_b = per_lane(jnp.exp(pos[None, :] * lg_b[:, None]))
    wq = jnp.concatenate([wq_f, wq_b], axis=2)
    wk = jnp.concatenate([wk_f, wk_b], axis=2)
    dec = jnp.stack([jnp.repeat(jnp.exp(c * lg_f), HEAD_DIM), jnp.repeat(jnp.exp(c * lg_b), HEAD_DIM)], axis=0)
    dec = dec.reshape(2, N_HEADS // 2, 2 * HEAD_DIM).transpose(1, 0, 2)
    return dmat, wq, wk, dec


def _retention(q, k, v, gate, ret_norm_g, b, s):
    c = min(RET_CHUNK, s)
    n_chunks = s // c
    pair = 2 * HEAD_DIM
    dmat, wq, wk, dec = _retention_tables(c)
    seq_blk = pl.BlockSpec((1, s, pair), lambda bi, hp: (bi, 0, hp))
    r3 = lambda z: z.reshape(b, s, WIDTH)
    return pl.pallas_call(
        functools.partial(_ret_kernel, c=c, n_chunks=n_chunks),
        grid=(b, N_HEADS // 2),
        in_specs=[seq_blk, seq_blk, seq_blk, seq_blk,
                  pl.BlockSpec((2, c, c), lambda bi, hp: (hp, 0, 0)),
                  pl.BlockSpec((None, c, 2 * pair), lambda bi, hp: (hp, 0, 0)),
                  pl.BlockSpec((None, c, 2 * pair), lambda bi, hp: (hp, 0, 0)),
                  pl.BlockSpec((None, 2, pair), lambda bi, hp: (hp, 0, 0)),
                  pl.BlockSpec((1, pair), lambda bi, hp: (0, hp))],
        out_specs=seq_blk,
        out_shape=jax.ShapeDtypeStruct((b, s, WIDTH), BF16),
        scratch_shapes=[pltpu.VMEM((n_chunks, pair, pair), F32), pltpu.VMEM((n_chunks, pair, pair), F32)],
        compiler_params=_cparams("arbitrary", "arbitrary"),
        name="retention",
    )(r3(q), r3(k), r3(v), r3(gate), dmat, wq, wk, dec, ret_norm_g.reshape(1, WIDTH)).reshape(b * s, WIDTH)


N_PIECES = 4
HALF_D = D_MODEL // 2
assert N_PIECES * LANES == HALF_D


def _pack_bf16_pair(a, b):
    ua = pltpu.bitcast(a.astype(BF16).astype(F32), jnp.uint32)
    ub = pltpu.bitcast(b.astype(BF16).astype(F32), jnp.uint32)
    return ua | (ub >> 16)


def _unpack_bf16_pair(u):
    return pltpu.bitcast(u & jnp.uint32(0xFFFF0000), F32), pltpu.bitcast(u << 16, F32)


def _store_pieces(ref, rows, val):
    for s in range(N_PIECES):
        lo = slice(s * LANES, (s + 1) * LANES)
        hi = slice(HALF_D + s * LANES, HALF_D + (s + 1) * LANES)
        ref[s, rows, :] = _pack_bf16_pair(val[:, lo], val[:, hi])


def _load_pieces(ref, rows):
    parts = [_unpack_bf16_pair(ref[s, rows, :]) for s in range(N_PIECES)]
    return [p[0] for p in parts] + [p[1] for p in parts]


def _split_bf16(x):
    hi = x.astype(BF16)
    return hi, (x - hi.astype(F32)).astype(BF16)


def _outproj_kernel(x_ref, o1_ref, o2_ref, o3_ref, s1_ref, s2_ref, s3_ref, ret_ref, wout_ref, expand_ref,
                    gffn_ref, wr_ref, br_ref, tri_ref,
                    x1_ref, h2t_ref, ri_ref, gates_ref, cnt_ref, base_scr, nat_o, nat_s, grp_o):
    i = pl.program_id(0)
    tm = x_ref.shape[0]

    @pl.when(i == 0)
    def _():
        base_scr[...] = jnp.zeros_like(base_scr)

    n_col = WIDTH // LANES
    for bb in range(REGROUP):
        rows4 = pl.ds(bb, tm // REGROUP, stride=REGROUP)
        for c in range(n_col):
            nat_o[0, c, rows4, :] = o2_ref[0, bb, :, c * LANES:(c + 1) * LANES].astype(F32)
        nat_s[0, rows4, :] = s2_ref[0, bb]
        for aa in range(REGROUP):
            rows16 = pl.ds(aa, tm // REGROUP ** 2, stride=REGROUP)
            for c in range(n_col):
                grp_o[c, bb, rows16, :] = o3_ref[0, bb * REGROUP + aa, :, c * LANES:(c + 1) * LANES].astype(F32)
            grp_o[n_col, bb, rows16, :] = s3_ref[0, bb * REGROUP + aa]
        for c in range(n_col):
            nat_o[1, c, rows4, :] = grp_o[c, bb]
        nat_s[1, rows4, :] = grp_o[n_col, bb]
    hm = tm // OUT_SPLIT
    lane = lax.broadcasted_iota(jnp.int32, (hm, LANES), 1).astype(F32)
    base = base_scr[...]
    for r0 in range(0, tm, hm):
        rows = slice(r0, r0 + hm)
        o_nat = [[o1_ref[rows, c * LANES:(c + 1) * LANES].astype(F32) for c in range(n_col)]]
        sts = [s1_ref[rows, :]]
        for di in range(len(RESIDUE_DILATIONS)):
            o_nat.append([nat_o[di, c, rows, :] for c in range(n_col)])
            sts.append(nat_s[di, rows, :])

        mx = jnp.maximum(jnp.maximum(sts[0], sts[1]), sts[2])
        es = [jnp.exp2(st - mx) for st in sts]
        den = es[0] + es[1] + es[2]
        attn = jnp.concatenate(o_nat[-1], axis=1)
        for e, o_cols in zip(es[:-1], o_nat[:-1]):
            hi, lo = _split_bf16(e / den)
            w_full = jnp.dot(jnp.concatenate([hi, lo], axis=1), expand_ref[...], preferred_element_type=F32)
            attn = attn + w_full * (jnp.concatenate(o_cols, axis=1) - jnp.concatenate(o_nat[-1], axis=1))
        mixed = jnp.concatenate([attn.astype(BF16), ret_ref[rows, :]], axis=1)
        x1 = x_ref[rows, :] + jnp.dot(mixed, wout_ref[...], preferred_element_type=F32)
        x1_ref[rows, :] = x1

        h2 = x1 * lax.rsqrt(jnp.mean(x1 * x1, axis=-1, keepdims=True) + NORM_EPS) * gffn_ref[...]
        _store_pieces(h2t_ref, rows, h2)

        hi, lo = _split_bf16(h2)
        prod = jnp.dot(jnp.concatenate([hi, lo], axis=1), wr_ref[...], preferred_element_type=F32)
        logits = prod + pltpu.roll(prod, LANES - N_EXPERTS, 1) + br_ref[...]
        work = logits
        vals, idxs = [], []
        onehot = jnp.zeros((hm, LANES), F32)
        for _k in range(TOP_K):
            mk = jnp.max(work, axis=-1, keepdims=True)
            ik = jnp.min(jnp.where(work == mk, lane, float(LANES)), axis=-1, keepdims=True)
            sel = lane == ik
            onehot = jnp.where(sel, 1.0, onehot)
            work = jnp.where(sel, -jnp.inf, work)
            vals.append(mk)
            idxs.append(ik)
        ex = [jnp.exp(vk - vals[0]) for vk in vals]
        tot = ex[0] + ex[1] + ex[2] + ex[3]
        before = jnp.dot(tri_ref[...], onehot.astype(BF16), preferred_element_type=F32) + base
        ri = jnp.zeros((hm, LANES), F32)
        gt = jnp.zeros((hm, LANES), F32)
        for kk in range(TOP_K):
            rank = jnp.sum(jnp.where(lane == idxs[kk], before, 0.0), axis=-1, keepdims=True)
            ri = jnp.where(lane == float(kk), idxs[kk], ri)
            ri = jnp.where(lane == float(TOP_K + kk), rank, ri)
            gt = jnp.where(lane == float(kk), ex[kk] / tot, gt)
        ri_ref[:, rows] = jnp.transpose(ri)[:2 * TOP_K].astype(jnp.int32)
        gates_ref[rows, :] = gt
        base = base + jnp.sum(onehot, axis=0, keepdims=True)
    base_scr[...] = base
    cnt_ref[...] = base


def _outproj_router(x2d, outs, stats, ret, wout_bf16, norm_ffn_g, w_router, b_router, seq):
    t = x2d.shape[0]
    tm = TM_OUT
    per_seq = seq // tm
    row = lambda i: (i, 0)

    def res_spec(d, width):
        return pl.BlockSpec((1, d, tm // d, width), lambda i: (i // per_seq, 0, i % per_seq, 0))

    const = lambda i: (0, 0)
    head_of_col = jnp.arange(WIDTH) // HEAD_DIM
    expand = (jnp.arange(LANES)[:, None] == head_of_col[None, :]).astype(BF16)
    expand2 = jnp.concatenate([expand, expand], axis=0)
    wr_hi, wr_lo = _split_bf16(w_router)
    zeros = jnp.zeros((D_MODEL, LANES), BF16)
    wr2 = jnp.concatenate([zeros.at[:, :N_EXPERTS].set(wr_hi).at[:, N_EXPERTS:2 * N_EXPERTS].set(wr_lo),
                           zeros.at[:, :N_EXPERTS].set(wr_hi)], axis=0)
    br = jnp.full((1, LANES), NEG_INF, F32).at[0, :N_EXPERTS].set(b_router)
    hm = tm // OUT_SPLIT
    tri = (jnp.arange(hm)[:, None] > jnp.arange(hm)[None, :]).astype(BF16)
    o_spec = pl.BlockSpec((tm, WIDTH), row)
    s_spec = pl.BlockSpec((tm, LANES), row)
    return pl.pallas_call(
        _outproj_kernel,
        grid=(t // tm,),
        in_specs=[pl.BlockSpec((tm, D_MODEL), row),
                  o_spec, *[res_spec(d, WIDTH) for d in RESIDUE_DILATIONS],
                  s_spec, *[res_spec(d, LANES) for d in RESIDUE_DILATIONS], o_spec,
                  pl.BlockSpec((D_MODEL, D_MODEL), const), pl.BlockSpec((2 * LANES, WIDTH), const),
                  pl.BlockSpec((1, D_MODEL), const), pl.BlockSpec((2 * D_MODEL, LANES), const),
                  pl.BlockSpec((1, LANES), const), pl.BlockSpec((hm, hm), const)],
        out_specs=[pl.BlockSpec((tm, D_MODEL), row), pl.BlockSpec((N_PIECES, tm, LANES), lambda i: (0, i, 0)),
                   pl.BlockSpec((2 * TOP_K, tm), lambda i: (0, i)), s_spec, pl.BlockSpec((1, LANES), const)],
        out_shape=[jax.ShapeDtypeStruct((t, D_MODEL), F32), jax.ShapeDtypeStruct((N_PIECES, t, LANES), jnp.uint32),
                   jax.ShapeDtypeStruct((2 * TOP_K, t), jnp.int32), jax.ShapeDtypeStruct((t, LANES), F32),
                   jax.ShapeDtypeStruct((1, LANES), F32)],
        scratch_shapes=[pltpu.VMEM((1, LANES), F32),
                        pltpu.VMEM((len(RESIDUE_DILATIONS), WIDTH // LANES, tm, LANES), F32),
                        pltpu.VMEM((len(RESIDUE_DILATIONS), tm, LANES), F32),
                        pltpu.VMEM((WIDTH // LANES + 1, REGROUP, tm // REGROUP, LANES), F32)],
        compiler_params=_cparams("arbitrary"),
        name="outproj_router",
    )(x2d, *outs, *stats, ret, wout_bf16, expand2, norm_ffn_g.reshape(1, D_MODEL), wr2, br, tri)


SC_CORES = 2
SC_SUBCORES = 16
SC_WINDOW = 128


def _sc_mesh():
    return plsc.VectorSubcoreMesh(core_axis_name="c", subcore_axis_name="s")


def _sc_scatter_rows(src, idx, n_out_rows):
    n_rows = src.shape[0]
    workers = SC_CORES * SC_SUBCORES
    per_worker = n_rows // workers
    n_win = per_worker // SC_WINDOW
    assert per_worker * workers == n_rows and n_win * SC_WINDOW == per_worker and idx.shape == (TOP_K * n_rows,)

    @functools.partial(
        pl.kernel, mesh=_sc_mesh(),
        out_type=jax.ShapeDtypeStruct((n_out_rows, LANES), src.dtype),
        scratch_types=[pltpu.VMEM((SC_WINDOW,), jnp.int32), pltpu.VMEM((SC_WINDOW, LANES), src.dtype)],
        name="sc_scatter_rows",
    )
    def scatter(src_hbm, idx_hbm, out_hbm, idx_v, rows_v):
        wid = lax.axis_index("s") * SC_CORES + lax.axis_index("c")
        base = wid * per_worker

        @pl.loop(0, n_win)
        def _(j):
            off = pl.multiple_of(base + j * SC_WINDOW, SC_WINDOW)
            pltpu.sync_copy(src_hbm.at[pl.ds(off, SC_WINDOW)], rows_v)
            for kk in range(TOP_K):
                pltpu.sync_copy(idx_hbm.at[pl.ds(pl.multiple_of(kk * n_rows + off, SC_WINDOW), SC_WINDOW)], idx_v)
                pltpu.sync_copy(rows_v, out_hbm.at[idx_v])

    return scatter(src, idx)


def _expert_kernel(blk_e_ref, nvalid_ref, rows_ref, xs_ref, wgu_ref, bgu_ref, wd_ref, bd_ref, ys_ref, wgu_bf, wd_bf):
    i = pl.program_id(0)

    @pl.when((i == 0) | (blk_e_ref[i] != blk_e_ref[jnp.maximum(i - 1, 0)]))
    def _():
        wgu_bf[...] = wgu_ref[0].astype(BF16)
        wd_bf[...] = wd_ref[0].astype(BF16)

    def mlp(m):
        live = lax.broadcasted_iota(jnp.int32, (m, LANES), 0) < rows_ref[i]
        x = jnp.concatenate([jnp.where(live, p, 0.0).astype(BF16) for p in _load_pieces(xs_ref, slice(0, m))], axis=1)
        gu = jnp.dot(x, wgu_bf[...], preferred_element_type=F32) + bgu_ref[0]
        gate = jnp.minimum(gu[:, :EXPERT_FF], SWIGLU_LIMIT)
        up = jnp.clip(gu[:, EXPERT_FF:], -SWIGLU_LIMIT, SWIGLU_LIMIT)
        act = gate * (1.0 / (1.0 + jnp.exp(-SWIGLU_ALPHA * gate))) * (up + 1.0)
        y = jnp.dot(act.astype(BF16), wd_bf[...], preferred_element_type=F32) + bd_ref[0]
        _store_pieces(ys_ref, slice(0, m), y)

    n_sub = lax.shift_right_logical(rows_ref[i] + (EXPERT_SUB - 1), EXPERT_SUB.bit_length() - 1)
    for v in range(1, BM // EXPERT_SUB + 1):
        pl.when((i < nvalid_ref[0]) & (n_sub == v))(functools.partial(mlp, v * EXPERT_SUB))


def _experts(xs, blk_e, nvalid, live_rows, wgu_bf16, bgu, wd_bf16, bd):
    cap = xs.shape[1]
    nblk = cap // BM

    def blk(i, be, nv, lr):
        return (0, jnp.minimum(i, nv[0] - 1), 0)

    def by_expert(i, be, nv, lr):
        return (be[i], 0, 0)

    return pl.pallas_call(
        _expert_kernel,
        grid_spec=pltpu.PrefetchScalarGridSpec(
            num_scalar_prefetch=3,
            grid=(nblk,),
            in_specs=[pl.BlockSpec((N_PIECES, BM, LANES), blk),
                      pl.BlockSpec((1, D_MODEL, 2 * EXPERT_FF), by_expert),
                      pl.BlockSpec((1, 1, 2 * EXPERT_FF), by_expert),
                      pl.BlockSpec((1, EXPERT_FF, D_MODEL), by_expert),
                      pl.BlockSpec((1, 1, D_MODEL), by_expert)],
            out_specs=pl.BlockSpec((N_PIECES, BM, LANES), blk),
            scratch_shapes=[pltpu.VMEM((D_MODEL, 2 * EXPERT_FF), BF16), pltpu.VMEM((EXPERT_FF, D_MODEL), BF16)],
        ),
        out_shape=jax.ShapeDtypeStruct((N_PIECES, cap, LANES), jnp.uint32),
        compiler_params=_cparams("arbitrary"),
        name="moe_experts",
    )(blk_e, nvalid, live_rows, xs, wgu_bf16, bgu.reshape(N_EXPERTS, 1, 2 * EXPERT_FF), wd_bf16,
      bd.reshape(N_EXPERTS, 1, D_MODEL))


def _sc_gather_rows(table, idx):
    n_rows = idx.shape[0]
    workers = SC_CORES * SC_SUBCORES
    per_worker = n_rows // workers
    n_win = per_worker // SC_WINDOW
    assert per_worker * workers == n_rows and n_win * SC_WINDOW == per_worker

    @functools.partial(
        pl.kernel, mesh=_sc_mesh(),
        out_type=jax.ShapeDtypeStruct((n_rows, LANES), table.dtype),
        scratch_types=[pltpu.VMEM((SC_WINDOW,), jnp.int32), pltpu.VMEM((SC_WINDOW, LANES), table.dtype),
                       pltpu.SemaphoreType.DMA],
        name="sc_gather_rows",
    )
    def gather(table_hbm, idx_hbm, out_hbm, idx_v, rows_v, sem):
        wid = lax.axis_index("s") * SC_CORES + lax.axis_index("c")
        base = wid * per_worker

        @pl.loop(0, n_win)
        def _(j):
            off = pl.multiple_of(base + j * SC_WINDOW, SC_WINDOW)
            pltpu.sync_copy(idx_hbm.at[pl.ds(off, SC_WINDOW)], idx_v)
            pltpu.async_copy(table_hbm.at[idx_v], rows_v, sem).wait()
            pltpu.sync_copy(rows_v, out_hbm.at[pl.ds(off, SC_WINDOW)])

    return gather(table, idx)


def _combine_kernel(x1_ref, gates_ref, gfin_ref, *refs, tc):
    pieces_ref, o_ref = refs
    g = gates_ref[...]
    gk = [jnp.broadcast_to(g[:, kk:kk + 1], (tc, LANES)) for kk in range(TOP_K)]
    zs = {}
    ssq = jnp.zeros((tc, 1), F32)
    for s in range(N_PIECES):
        c_lo, c_hi = s * LANES, HALF_D + s * LANES
        z_lo = x1_ref[:, c_lo:c_lo + LANES]
        z_hi = x1_ref[:, c_hi:c_hi + LANES]
        for kk in range(TOP_K):
            a, b = _unpack_bf16_pair(pieces_ref[kk * N_PIECES + s])
            z_lo = z_lo + gk[kk] * a
            z_hi = z_hi + gk[kk] * b
        zs[c_lo], zs[c_hi] = z_lo, z_hi
        ssq = ssq + jnp.sum(z_lo * z_lo + z_hi * z_hi, axis=-1, keepdims=True)
    inv = lax.rsqrt(ssq * (1.0 / D_MODEL) + NORM_EPS)
    for c0, z in zs.items():
        o_ref[:, c0:c0 + LANES] = z * inv * gfin_ref[:, c0:c0 + LANES]


def _combine(x1, gates, gathered, norm_final_g):
    t = x1.shape[0]
    tc = TC
    row = lambda i: (i, 0)
    slot_specs = [pl.BlockSpec((TOP_K * N_PIECES, tc, LANES), lambda i: (0, i, 0))]
    return pl.pallas_call(
        functools.partial(_combine_kernel, tc=tc),
        grid=(t // tc,),
        in_specs=[pl.BlockSpec((tc, D_MODEL), row), pl.BlockSpec((tc, LANES), row),
                  pl.BlockSpec((1, D_MODEL), lambda i: (0, 0)), *slot_specs],
        out_specs=pl.BlockSpec((tc, D_MODEL), row),
        out_shape=jax.ShapeDtypeStruct((t, D_MODEL), F32),
        compiler_params=_cparams("arbitrary"),
        name="moe_combine",
    )(x1, gates, norm_final_g.reshape(1, D_MODEL), gathered)


def _moe(x1, h2t, ri, gates, counts_f, wgu_bf16, bgu, wd_bf16, bd, norm_final_g):
    t = x1.shape[0]
    a = t * TOP_K
    cap = a + N_EXPERTS * BM
    nblk = cap // BM
    counts = counts_f[0, :N_EXPERTS].astype(jnp.int32)
    padded = ((counts + BM - 1) // BM) * BM
    pend = jnp.cumsum(padded)
    pstart = pend - padded
    nvalid = (pend[-1] // BM).reshape(1)
    first_row = jnp.minimum(jnp.arange(nblk, dtype=jnp.int32) * BM, pend[-1] - 1)
    blk_e = jnp.sum(pend[None, :] <= first_row[:, None], axis=1).astype(jnp.int32)
    idx, rank = ri[:TOP_K], ri[TOP_K:2 * TOP_K]
    onehot = idx[None, :, :] == jnp.arange(N_EXPERTS, dtype=jnp.int32)[:, None, None]
    dest = rank + jnp.sum(jnp.where(onehot, pstart[:, None, None], 0), axis=0)
    seg_end = (pstart + counts)[blk_e]
    live_rows = jnp.clip(seg_end - jnp.arange(nblk, dtype=jnp.int32) * BM, 0, BM).astype(jnp.int32)
    piece = dest[:, None, :] + (jnp.arange(N_PIECES, dtype=jnp.int32) * cap)[None, :, None]
    piece = piece.reshape(TOP_K * N_PIECES * t)
    xs = _sc_scatter_rows(h2t.reshape(N_PIECES * t, LANES), piece, N_PIECES * cap)
    ys = _experts(xs.reshape(N_PIECES, cap, LANES), blk_e, nvalid, live_rows, wgu_bf16, bgu, wd_bf16, bd)
    gathered = _sc_gather_rows(ys.reshape(N_PIECES * cap, LANES), piece)
    return _combine(x1, gates, gathered.reshape(TOP_K * N_PIECES, t, LANES), norm_final_g)


def _encoder(x, p):
    b, s, d_model = x.shape
    assert d_model == D_MODEL and s % (max(DILATIONS) * TQ) == 0 and (b * s) % max(TM_OUT, TC) == 0, x.shape
    x2d = x.reshape(b * s, D_MODEL)
    cos_t, sin_t = _rope_tables(s)
    (qa, ka, va, qr, kr, vr, gr), by_residue = _inproj(x2d, p["norm_mix_g"], p["w_in"], cos_t, sin_t, s)
    outs, stats = [], []
    for d in DILATIONS:
        if d == 1:
            o, st = _banded_attention(*[z.reshape(b, s, WIDTH) for z in (qa, ka, va)])
            outs.append(o.reshape(b * s, WIDTH))
            stats.append(st.reshape(b * s, LANES))
        else:
            o, st = _banded_attention(*[z.reshape(b * d, s // d, WIDTH) for z in by_residue[d]])
            outs.append(o.reshape(b, d, s // d, WIDTH))
            stats.append(st.reshape(b, d, s // d, LANES))
    ret = _retention(qr, kr, vr, gr, p["ret_norm_g"], b, s)
    x1, h2t, ri, gates, counts = _outproj_router(x2d, outs, stats, ret, p["w_out"], p["norm_ffn_g"],
                                                 p["w_router"], p["b_router"], s)
    y = _moe(x1, h2t, ri, gates, counts, p["w_gate_up"], p["b_gate_up"], p["w_down"], p["b_down"], p["norm_final_g"])
    return y.reshape(b, s, D_MODEL)


def kernel(x_prompt, x_sample, norm_mix_g, w_in, ret_norm_g, w_out, norm_ffn_g, w_router, b_router, w_gate_up, b_gate_up, w_down, b_down, norm_final_g):
    assert norm_mix_g.shape[0] == 1, "single layer"
    p = dict(norm_mix_g=norm_mix_g[0], w_in=w_in[0].astype(BF16), ret_norm_g=ret_norm_g[0],
             w_out=w_out[0].astype(BF16), norm_ffn_g=norm_ffn_g[0], w_router=w_router[0], b_router=b_router[0],
             w_gate_up=w_gate_up[0], b_gate_up=b_gate_up[0], w_down=w_down[0],
             b_down=b_down[0], norm_final_g=norm_final_g)
    y_sample = _encoder(x_sample, p)
    y_prompt = _encoder(x_prompt, p)
    return (y_prompt, y_sample)
```
